```python
import jax, jax.numpy as jnp
from jax import lax
import numpy as np

D_MODEL = 1024
BATCH = 16
SEQ = 256
DEPTH = 4
DEC_BATCH = 4
DEC_SEQ = 2048
PAST_LEN = 512

GRID_W = 64
H_A = 4
DH_A = 96
D_A = H_A * DH_A
MLSTM_CHUNK = 64
H_B = 6
KV_B = 2
G_B = H_B // KV_B
HD_B = 64
D_B = H_B * HD_B
D_KVB = KV_B * HD_B
WIN = 128
WBLK = 128
ROPE_BASE = 10000.0
H_C = 4
HD_C = 64
D_C = H_C * HD_C
NA_WIN_H = 8
NA_WIN_W = 16
NA_QCOL = 16
NA_KCOL = NA_QCOL + NA_WIN_W
SPLIT_SIZES = (D_A, D_A, D_A, D_A, 2 * H_A, 2 * H_A, D_B, D_KVB, D_KVB, D_C, D_C, D_C)
D_IN = 4 * D_A + 4 * H_A + D_B + 2 * D_KVB + 3 * D_C
D_MIX = D_A + D_B + D_C
QBLK = 128
N_GROUPS = 4
E_PER_GROUP = 8
N_EXPERTS = N_GROUPS * E_PER_GROUP
TOP_K = 2
D_EXPERT = D_MODEL // 4
MOE_BLK = 128
ALPHA = (2 * DEPTH) ** 0.25
BETA = (8 * DEPTH) ** -0.25
LN_EPS = 1e-5

kernel_name = 'hybrid_mlstm_swa_natten_hmoe_diffusion_step'


def layer_norm(x, g, b):
    xf = x.astype(jnp.float32)
    mu = jnp.mean(xf, -1, keepdims=True)
    var = jnp.mean(jnp.square(xf - mu), -1, keepdims=True)
    y = (xf - mu) * lax.rsqrt(var + LN_EPS) * g.astype(jnp.float32) + b.astype(jnp.float32)
    return y.astype(x.dtype)


def adaln(cvec, w, b):
    mod = (jax.nn.silu(cvec) @ w + b)[:, None, :]
    return jnp.split(mod, 6, axis=-1)


def post_norm(x, y, gate, g, b):
    return layer_norm(ALPHA * x + gate * y, g, b)


def split_proj(z):
    pts, acc = [], 0
    for s in SPLIT_SIZES[:-1]:
        acc += s
        pts.append(acc)
    return jnp.split(z, pts, axis=-1)


def axial_rope(n_tok, hd):
    t = jnp.arange(n_tok)
    nf = hd // 4
    inv = ROPE_BASE ** (-jnp.arange(nf, dtype=jnp.float32) / nf)
    ar = (t // GRID_W).astype(jnp.float32)[:, None] * inv
    ac = (t % GRID_W).astype(jnp.float32)[:, None] * inv
    ang = jnp.concatenate([ar, ar, ac, ac], -1)
    return jnp.cos(ang), jnp.sin(ang)


def apply_rope(x, cos, sin):
    x1, x2, x3, x4 = jnp.split(x, 4, axis=-1)
    rot = jnp.concatenate([-x2, x1, -x4, x3], -1)
    return x * cos[:, None].astype(x.dtype) + rot * sin[:, None].astype(x.dtype)


def mlstm_chunkwise(q, k, v, ig, lf, C0, n0, m0):
    B, H, L, d = q.shape
    lc = min(MLSTM_CHUNK, L)
    nc = L // lc

    def to_chunks(a):
        return jnp.moveaxis(a.reshape(a.shape[:2] + (nc, lc) + a.shape[3:]), 2, 0)

    xs = tuple(to_chunks(a) for a in (q, k, v, ig, lf))
    causal = jnp.tril(jnp.ones((lc, lc), bool))

    def step(carry, inp):
        C, n, m = carry
        qc, kc, vc, ic, fc = inp
        b = jnp.cumsum(fc, -1)
        dmat = jnp.where(causal, b[..., :, None] - b[..., None, :] + ic[..., None, :], -jnp.inf)
        inter = b + m[..., None]
        m_out = jnp.maximum(inter, jnp.max(dmat, -1))
        wmat = jnp.exp(dmat - m_out[..., None])
        sw = jnp.einsum('bhtd,bhsd->bhts', qc, kc) * wmat
        sc_in = jnp.exp(inter - m_out)
        num = jnp.einsum('bhts,bhsd->bhtd', sw, vc) + sc_in[..., None] * jnp.einsum('bhvk,bhtk->bhtv', C, qc)
        den = jnp.sum(sw, -1) + sc_in * jnp.einsum('bhk,bhtk->bht', n, qc)
        h = num / jnp.maximum(jnp.abs(den), jnp.exp(-m_out))[..., None]
        bl = b[..., -1]
        dec = bl[..., None] - b + ic
        m_new = jnp.maximum(bl + m, jnp.max(dec, -1))
        wk = jnp.exp(dec - m_new[..., None])
        sc_st = jnp.exp(bl + m - m_new)
        C_new = sc_st[..., None, None] * C + jnp.einsum('bhs,bhsv,bhsk->bhvk', wk, vc, kc)
        n_new = sc_st[..., None] * n + jnp.einsum('bhs,bhsk->bhk', wk, kc)
        return (C_new, n_new, m_new), h

    (C, n, m), hs = lax.scan(step, (C0, n0, m0), xs)
    h = jnp.moveaxis(hs, 0, 2).reshape(B, H, L, d)
    return h, C, n, m


def mlstm_mixer(q, k, v, o_pre, i_pre, f_pre, b_i, b_f, w_norm, C0, n0, m0):
    B, L, _ = q.shape

    def heads(a):
        return jnp.moveaxis(a.reshape(B, L, H_A, DH_A), 1, 2).astype(jnp.float32)

    qh, kh, vh = heads(q), heads(k) * DH_A ** -0.5, heads(v)
    ig = jnp.moveaxis(i_pre.reshape(B, L, 2, H_A).astype(jnp.float32) + b_i.astype(jnp.float32), 1, -1)
    lf = jnp.moveaxis(jax.nn.log_sigmoid(f_pre.reshape(B, L, 2, H_A).astype(jnp.float32) + b_f.astype(jnp.float32)), 1, -1)
    C0, n0, m0 = C0.astype(jnp.float32), n0.astype(jnp.float32), m0.astype(jnp.float32)
    h_f, Cf, nf, mf = mlstm_chunkwise(qh, kh, vh, ig[:, 0], lf[:, 0], C0[:, 0], n0[:, 0], m0[:, 0])
    flip = lambda a: jnp.flip(a, axis=2)
    h_b, Cb, nb, mb = mlstm_chunkwise(flip(qh), flip(kh), flip(vh), flip(ig[:, 1]), flip(lf[:, 1]), C0[:, 1], n0[:, 1], m0[:, 1])
    h = jnp.moveaxis(h_f + flip(h_b), 1, 2)
    mu = jnp.mean(h, -1, keepdims=True)
    var = jnp.mean(jnp.square(h - mu), -1, keepdims=True)
    h = ((h - mu) * lax.rsqrt(var + LN_EPS)).reshape(B, L, D_A) * w_norm.astype(jnp.float32)
    out = (jax.nn.sigmoid(o_pre.astype(jnp.float32)) * h).astype(q.dtype)
    return out, jnp.stack([Cf, Cb], 1), jnp.stack([nf, nb], 1), jnp.stack([mf, mb], 1)


def ctx_attention(q, k, v, sink):
    B, L, KV, G, d = q.shape
    nq = L // QBLK
    qb = jnp.moveaxis(q.reshape(B, nq, QBLK, KV, G, d), 1, 0)

    def one(qc):
        s = jnp.einsum('bqkgd,blkd->bkgql', qc, k).astype(jnp.float32) * d ** -0.5
        if sink is not None:
            snk = jnp.broadcast_to(sink.astype(jnp.float32)[None, :, :, None, None], s.shape[:-1] + (1,))
            p = jax.nn.softmax(jnp.concatenate([s, snk], -1), -1)[..., :-1]
        else:
            p = jax.nn.softmax(s, -1)
        return jnp.einsum('bkgql,blkd->bqkgd', p.astype(v.dtype), v)

    o = lax.map(one, qb)
    return jnp.moveaxis(o, 0, 1).reshape(B, L, KV * G * d)


def window_attention_latent(q, k, v, k_ctx, v_ctx, sink):
    B, N, _, d = q.shape
    nb = N // WBLK
    lc = k_ctx.shape[1]
    qb = q.reshape(B, nb, WBLK, KV_B, G_B, d)
    pad = ((0, 0), (WBLK, WBLK), (0, 0), (0, 0))
    kp = jnp.pad(k, pad).reshape(B, nb + 2, WBLK, KV_B, d)
    vp = jnp.pad(v, pad).reshape(B, nb + 2, WBLK, KV_B, d)
    kband = jnp.concatenate([kp[:, :-2], kp[:, 1:-1], kp[:, 2:]], axis=2)
    vband = jnp.concatenate([vp[:, :-2], vp[:, 1:-1], vp[:, 2:]], axis=2)
    blk = jnp.arange(nb)
    kpos = blk[:, None] * WBLK - WBLK + jnp.arange(3 * WBLK)[None, :]
    qpos = blk[:, None] * WBLK + jnp.arange(WBLK)[None, :]
    mask = (jnp.abs(kpos[:, None, :] - qpos[:, :, None]) <= WIN) & (kpos[:, None, :] >= 0) & (kpos[:, None, :] < N)
    sink_l = sink.reshape(KV_B, G_B).astype(jnp.float32)
    scale = d ** -0.5

    def one(args):
        qc, kc, vc, mc = args
        s_loc = jnp.einsum('bqkgd,bskd->bkgqs', qc, kc).astype(jnp.float32) * scale
        s_loc = jnp.where(mc, s_loc, -jnp.inf)
        s_ctx = jnp.einsum('bqkgd,blkd->bkgql', qc, k_ctx).astype(jnp.float32) * scale
        snk = jnp.broadcast_to(sink_l[None, :, :, None, None], s_loc.shape[:-1] + (1,))
        p = jax.nn.softmax(jnp.concatenate([s_ctx, s_loc, snk], -1), -1).astype(v.dtype)
        return (jnp.einsum('bkgql,blkd->bqkgd', p[..., :lc], v_ctx)
                + jnp.einsum('bkgqs,bskd->bqkgd', p[..., lc:lc + 3 * WBLK], vc))

    o = lax.map(one, (jnp.moveaxis(qb, 1, 0), jnp.moveaxis(kband, 1, 0), jnp.moveaxis(vband, 1, 0), mask))
    return jnp.moveaxis(o, 0, 1).reshape(B, N, H_B * d)


def neighborhood_attention_latent(q, k, v, k_ctx, v_ctx, rpb_l):
    B, N, H, d = q.shape
    rows = N // GRID_W
    wh = min(NA_WIN_H, rows)
    ncb = GRID_W // NA_QCOL
    lc = k_ctx.shape[1]
    r = jnp.arange(rows)
    key_rows = jnp.clip(r - wh // 2, 0, rows - wh)[:, None] + jnp.arange(wh)[None, :]
    cb = jnp.arange(ncb)
    key_cols = jnp.clip(cb * NA_QCOL - NA_WIN_W // 2, 0, GRID_W - NA_KCOL)[:, None] + jnp.arange(NA_KCOL)[None, :]
    ri = key_rows[:, None, :, None]
    ci = key_cols[None, :, None, :]
    kg = k.reshape(B, rows, GRID_W, H, d)[:, ri, ci]
    vg = v.reshape(B, rows, GRID_W, H, d)[:, ri, ci]
    qg = q.reshape(B, rows, ncb, NA_QCOL, H, d)
    qcol = cb[:, None] * NA_QCOL + jnp.arange(NA_QCOL)[None, :]
    wstart = jnp.clip(qcol - NA_WIN_W // 2, 0, GRID_W - NA_WIN_W)
    kc3 = key_cols[:, None, :]
    in_win = (kc3 >= wstart[:, :, None]) & (kc3 < wstart[:, :, None] + NA_WIN_W)
    dr_idx = key_rows - r[:, None] + NA_WIN_H - 1
    dc_idx = jnp.clip(kc3 - qcol[:, :, None], 1 - NA_WIN_W, NA_WIN_W - 1) + NA_WIN_W - 1
    bias = rpb_l[:, dr_idx[:, None, None, :, None], dc_idx[None, :, :, None, :]].astype(jnp.float32)
    scale = d ** -0.5
    s_loc = jnp.einsum('brcqhd,brcwshd->bhrcqws', qg, kg).astype(jnp.float32) * scale + bias[None]
    s_loc = jnp.where(in_win[:, :, None, :], s_loc, -jnp.inf).reshape(B, H, rows, ncb, NA_QCOL, wh * NA_KCOL)
    s_ctx = jnp.einsum('brcqhd,blhd->bhrcql', qg, k_ctx).astype(jnp.float32) * scale
    p = jax.nn.softmax(jnp.concatenate([s_ctx, s_loc], -1), -1).astype(v.dtype)
    o = (jnp.einsum('bhrcql,blhd->brcqhd', p[..., :lc], v_ctx)
         + jnp.einsum('bhrcqws,brcwshd->brcqhd', p[..., lc:].reshape(B, H, rows, ncb, NA_QCOL, wh, NA_KCOL), vg))
    return o.reshape(B, N, H * d)


def moe_ffn(h, w_rg, w_re, w_g, w_u, w_d):
    shp = h.shape
    x = h.reshape(-1, shp[-1])
    T = x.shape[0]
    pg = jax.nn.softmax((x @ w_rg).astype(jnp.float32), -1)
    grp = jnp.argmax(pg, -1)
    g_w = jnp.max(pg, -1)
    le = (x @ w_re).astype(jnp.float32).reshape(T, N_GROUPS, E_PER_GROUP)
    le = jnp.take_along_axis(le, grp[:, None, None], axis=1)[:, 0]
    top_l, top_i = lax.top_k(le, TOP_K)
    top_w = jax.nn.softmax(top_l, -1) * g_w[:, None]
    e_flat = (grp[:, None] * E_PER_GROUP + top_i).reshape(-1)
    w_flat = top_w.reshape(-1)
    t_flat = jnp.repeat(jnp.arange(T), TOP_K)
    A = T * TOP_K
    order = jnp.argsort(e_flat)
    e_s, t_s, w_s = e_flat[order], t_flat[order], w_flat[order]
    counts = jnp.bincount(e_flat, length=N_EXPERTS)
    start = jnp.cumsum(counts) - counts
    padded = (counts + MOE_BLK - 1) // MOE_BLK * MOE_BLK
    p_end = jnp.cumsum(padded)
    p_start = p_end - padded
    pos = p_start[e_s] + jnp.arange(A) - start[e_s]
    nblk = -(-A // MOE_BLK) + N_EXPERTS
    buf = jnp.zeros((nblk * MOE_BLK, x.shape[-1]), x.dtype).at[pos].set(x[t_s])
    blk_e = jnp.minimum(jnp.searchsorted(p_end, jnp.arange(nblk) * MOE_BLK, side='right'), N_EXPERTS - 1)

    def expert_block(args):
        xb, e = args
        return (jax.nn.silu(xb @ w_g[e]) * (xb @ w_u[e])) @ w_d[e]

    out = lax.map(expert_block, (buf.reshape(nblk, MOE_BLK, -1), blk_e)).reshape(nblk * MOE_BLK, -1)
    y = jax.ops.segment_sum(out[pos] * w_s[:, None].astype(out.dtype), t_s, num_segments=T)
    return y.reshape(shp)


def setup_inputs(seed: int = 0) -> dict:
    key = jax.random.key(seed)
    ks = jax.random.split(key, 28)

    def nrm(k, shape, s):
        return jax.random.normal(k, shape, jnp.float32) * s

    return {
        'x_prompt': nrm(ks[0], (BATCH, SEQ, D_MODEL), 1.0),
        'x_sample': nrm(ks[1], (DEC_BATCH, DEC_SEQ, D_MODEL), 1.0),
        'state_a_C': nrm(ks[2], (DEC_BATCH, DEPTH, 2, H_A, DH_A, DH_A), 0.5),
        'state_a_n': nrm(ks[3], (DEC_BATCH, DEPTH, 2, H_A, DH_A), 0.5),
        'state_a_m': nrm(ks[4], (DEC_BATCH, DEPTH, 2, H_A), 0.5),
        'cache_b_k': nrm(ks[5], (DEC_BATCH, DEPTH, PAST_LEN, KV_B, HD_B), 1.0),
        'cache_b_v': nrm(ks[6], (DEC_BATCH, DEPTH, PAST_LEN, KV_B, HD_B), 1.0),
        'cache_c_k': nrm(ks[7], (DEC_BATCH, DEPTH, PAST_LEN, H_C, HD_C), 1.0),
        'cache_c_v': nrm(ks[8], (DEC_BATCH, DEPTH, PAST_LEN, H_C, HD_C), 1.0),
        'c': nrm(ks[9], (DEC_BATCH, D_MODEL), 1.0),
        'c_ctx': nrm(ks[10], (D_MODEL,), 1.0),
        'w_in': nrm(ks[11], (DEPTH, D_MODEL, D_IN), D_MODEL ** -0.5),
        'b_a_i': nrm(ks[12], (DEPTH, 2, H_A), 0.1),
        'b_a_f': jnp.linspace(3.0, 6.0, H_A, dtype=jnp.float32)[None, None, :] + nrm(ks[13], (DEPTH, 2, H_A), 0.1),
        'w_a_hnorm': 1.0 + nrm(ks[14], (DEPTH, D_A), 0.02),
        'b_sink': nrm(ks[15], (DEPTH, H_B), 0.5),
        'rpb': nrm(ks[16], (DEPTH, H_C, 2 * NA_WIN_H - 1, 2 * NA_WIN_W - 1), 0.1),
        'w_out': nrm(ks[17], (DEPTH, D_MIX, D_MODEL), D_MIX ** -0.5 * BETA),
        'w_ada': nrm(ks[18], (DEPTH, D_MODEL, 6 * D_MODEL), 0.5 * D_MODEL ** -0.5),
        'b_ada': nrm(ks[19], (DEPTH, 6 * D_MODEL), 0.02),
        'ln_g': 1.0 + nrm(ks[20], (DEPTH, 2, D_MODEL), 0.02),
        'ln_b': nrm(ks[21], (DEPTH, 2, D_MODEL), 0.02),
        'w_router_grp': nrm(ks[22], (DEPTH, D_MODEL, N_GROUPS), D_MODEL ** -0.5),
        'w_router_exp': nrm(ks[23], (DEPTH, D_MODEL, N_EXPERTS), D_MODEL ** -0.5),
        'w_e_gate': nrm(ks[24], (DEPTH, N_EXPERTS, D_MODEL, D_EXPERT), D_MODEL ** -0.5),
        'w_e_up': nrm(ks[25], (DEPTH, N_EXPERTS, D_MODEL, D_EXPERT), D_MODEL ** -0.5),
        'w_e_down': nrm(ks[26], (DEPTH, N_EXPERTS, D_EXPERT, D_MODEL), D_EXPERT ** -0.5 * BETA),
    }


def reference(x_prompt, x_sample, state_a_C, state_a_n, state_a_m, cache_b_k, cache_b_v, cache_c_k, cache_c_v,
              c, c_ctx, w_in, b_a_i, b_a_f, w_a_hnorm, b_sink, rpb, w_out, w_ada, b_ada, ln_g, ln_b,
              w_router_grp, w_router_exp, w_e_gate, w_e_up, w_e_down):
    x = x_prompt
    B, L, _ = x.shape
    zC = jnp.zeros((B, 2, H_A, DH_A, DH_A), jnp.float32)
    zn = jnp.zeros((B, 2, H_A, DH_A), jnp.float32)
    zm = jnp.zeros((B, 2, H_A), jnp.float32)
    Cs, ns, ms, kbs, vbs, kcs, vcs = [], [], [], [], [], [], []
    for l in range(DEPTH):
        sh_a, sc_a, g_a, sh_f, sc_f, g_f = adaln(c_ctx[None, :], w_ada[l], b_ada[l])
        h = x * (1.0 + sc_a) + sh_a
        qa, ka, va, oa, ia, fa, qb, kb, vb, qc, kc, vc = split_proj(h @ w_in[l])
        ya, Cl, nl, ml = mlstm_mixer(qa, ka, va, oa, ia, fa, b_a_i[l], b_a_f[l], w_a_hnorm[l], zC, zn, zm)
        kb4 = kb.reshape(B, L, KV_B, HD_B)
        vb4 = vb.reshape(B, L, KV_B, HD_B)
        yb = ctx_attention(qb.reshape(B, L, KV_B, G_B, HD_B), kb4, vb4, b_sink[l].reshape(KV_B, G_B))
        kc4 = kc.reshape(B, L, H_C, HD_C)
        vc4 = vc.reshape(B, L, H_C, HD_C)
        yc = ctx_attention(qc.reshape(B, L, H_C, 1, HD_C), kc4, vc4, None)
        x = post_norm(x, jnp.concatenate([ya, yb, yc], -1) @ w_out[l], g_a, ln_g[l, 0], ln_b[l, 0])
        h = x * (1.0 + sc_f) + sh_f
        x = post_norm(x, moe_ffn(h, w_router_grp[l], w_router_exp[l], w_e_gate[l], w_e_up[l], w_e_down[l]), g_f, ln_g[l, 1], ln_b[l, 1])
        Cs.append(Cl); ns.append(nl); ms.append(ml)
        kbs.append(kb4); vbs.append(vb4); kcs.append(kc4); vcs.append(vc4)
    y_prompt = x

    x = x_sample
    Bd, N, _ = x.shape
    cos, sin = axial_rope(N, HD_B)
    for l in range(DEPTH):
        sh_a, sc_a, g_a, sh_f, sc_f, g_f = adaln(c, w_ada[l], b_ada[l])
        h = x * (1.0 + sc_a) + sh_a
        qa, ka, va, oa, ia, fa, qb, kb, vb, qc, kc, vc = split_proj(h @ w_in[l])
        ya, _, _, _ = mlstm_mixer(qa, ka, va, oa, ia, fa, b_a_i[l], b_a_f[l], w_a_hnorm[l],
                                  state_a_C[:, l], state_a_n[:, l], state_a_m[:, l])
        qb4 = apply_rope(qb.reshape(Bd, N, H_B, HD_B), cos, sin)
        kb4 = apply_rope(kb.reshape(Bd, N, KV_B, HD_B), cos, sin)
        yb = window_attention_latent(qb4, kb4, vb.reshape(Bd, N, KV_B, HD_B), cache_b_k[:, l], cache_b_v[:, l], b_sink[l])
        yc = neighborhood_attention_latent(qc.reshape(Bd, N, H_C, HD_C), kc.reshape(Bd, N, H_C, HD_C),
                                           vc.reshape(Bd, N, H_C, HD_C), cache_c_k[:, l], cache_c_v[:, l], rpb[l])
        x = post_norm(x, jnp.concatenate([ya, yb, yc], -1) @ w_out[l], g_a, ln_g[l, 0], ln_b[l, 0])
        h = x * (1.0 + sc_f) + sh_f
        x = post_norm(x, moe_ffn(h, w_router_grp[l], w_router_exp[l], w_e_gate[l], w_e_up[l], w_e_down[l]), g_f, ln_g[l, 1], ln_b[l, 1])
    y_sample = x

    new_state_a_C = jnp.stack(Cs, 1)
    new_state_a_n = jnp.stack(ns, 1)
    new_state_a_m = jnp.stack(ms, 1)
    new_cache_b_k = jnp.stack(kbs, 1)
    new_cache_b_v = jnp.stack(vbs, 1)
    new_cache_c_k = jnp.stack(kcs, 1)
    new_cache_c_v = jnp.stack(vcs, 1)
    return (y_prompt, y_sample, new_state_a_C, new_state_a_n, new_state_a_m, new_cache_b_k, new_cache_b_v, new_cache_c_k, new_cache_c_v)
```

```python
import functools

import numpy as np
import jax
import jax.numpy as jnp
from jax import lax
from jax.experimental import pallas as pl
from jax.experimental.pallas import tpu as pltpu

F32 = jnp.float32
BF16 = jnp.bfloat16
NEG_INF = float("-inf")

D_MODEL = 1024
DEPTH = 4
B_CTX, L_CTX = 16, 256
B_LAT, N_LAT = 4, 2048
PAST_LEN = 512
GRID_W = 64
H_A, DH_A, LC_A = 4, 96, 64
D_A = H_A * DH_A
H_B, KV_B, HD = 6, 2, 64
G_B = H_B // KV_B
D_B, D_KVB = H_B * HD, KV_B * HD
WIN = 128
ROPE_BASE = 10000.0
H_C = 4
D_C = H_C * HD
NA_WIN_H, NA_WIN_W = 8, 16
N_GROUPS, E_PER_GROUP = 4, 8
N_EXPERTS = N_GROUPS * E_PER_GROUP
D_EXPERT = D_MODEL // 4
ALPHA = (2 * DEPTH) ** 0.25
LN_EPS = 1e-5

LANES = 128
DP_A = LANES
TM = 256
MOE_BLK = 128
VMEM_LIMIT = 48 * 1024 * 1024

T_CTX = B_CTX * L_CTX
T_LAT = B_LAT * N_LAT
T_ALL = T_CTX + T_LAT
N_TILES = T_ALL // TM
ZA_COLS = 4 * H_A * DP_A
Y_A_COLS = H_A * DP_A
IN_SPLITS = (ZA_COLS, D_B, D_KVB, D_KVB, D_C, D_C, D_C, LANES)
IN_COLS = sum(IN_SPLITS)


def _cparams(*sem):
    return pltpu.CompilerParams(dimension_semantics=sem, vmem_limit_bytes=VMEM_LIMIT)


def _dot(a, b):
    return jnp.dot(a.astype(BF16), b.astype(BF16), preferred_element_type=F32)


def _dot_nt(a, b):
    return lax.dot_general(a.astype(BF16), b.astype(BF16), (((1,), (1,)), ((), ())), preferred_element_type=F32)


def _dot_tn(a, b):
    return lax.dot_general(a.astype(BF16), b.astype(BF16), (((0,), (0,)), ((), ())), preferred_element_type=F32)


def _layer_norm(v, g, b):
    mu = jnp.mean(v, -1, keepdims=True)
    var = jnp.mean(jnp.square(v - mu), -1, keepdims=True)
    return (v - mu) * lax.rsqrt(var + LN_EPS) * g + b


def _ada_kernel(c_ref, w_ref, b_ref, o_ref):
    s = jax.nn.silu(c_ref[...])
    o_ref[0] = _dot(s, w_ref[0]) + b_ref[0]


def _ada_call(cvec, w_ada, b_ada):
    nb = 6
    return pl.pallas_call(
        _ada_kernel,
        grid=(DEPTH, nb),
        in_specs=[
            pl.BlockSpec((8, D_MODEL), lambda l, j: (0, 0)),
            pl.BlockSpec((1, D_MODEL, D_MODEL), lambda l, j: (l, 0, j)),
            pl.BlockSpec((1, 1, D_MODEL), lambda l, j: (l, 0, j)),
        ],
        out_specs=pl.BlockSpec((1, 8, D_MODEL), lambda l, j: (l, 0, j)),
        out_shape=jax.ShapeDtypeStruct((DEPTH, 8, 6 * D_MODEL), F32),
        compiler_params=_cparams("arbitrary", "arbitrary"),
        name="adaln",
    )(cvec, w_ada, b_ada.reshape(DEPTH, 1, 6 * D_MODEL))


def _inproj_kernel(x_ref, sc_ref, sh_ref, w_ref, *out_refs):
    h = (x_ref[...] * (1.0 + sc_ref[0]) + sh_ref[0]).astype(BF16)
    off = 0
    for ref in out_refs:
        n = ref.shape[-1]
        ref[...] = jnp.dot(h, w_ref[:, off:off + n], preferred_element_type=F32)
        off += n


def _inproj_call(x, sc, sh, w):
    tile_vec = pl.BlockSpec((1, 1, D_MODEL), lambda i: (i, 0, 0))
    return pl.pallas_call(
        _inproj_kernel,
        grid=(N_TILES,),
        in_specs=[
            pl.BlockSpec((TM, D_MODEL), lambda i: (i, 0)),
            tile_vec, tile_vec,
            pl.BlockSpec((D_MODEL, IN_COLS), lambda i: (0, 0)),
        ],
        out_specs=[pl.BlockSpec((TM, n), lambda i: (i, 0)) for n in IN_SPLITS],
        out_shape=[jax.ShapeDtypeStruct((T_ALL, n), F32) for n in IN_SPLITS],
        compiler_params=_cparams("arbitrary"),
        name="inproj",
    )(x, sc, sh, w)


def _mlstm_kernel(q_ref, k_ref, v_ref, o_ref, gc_ref, gr_ref, bc_ref, br_ref, wn_ref, c0_ref, n0_ref, m0_ref,
                  y_ref, co_ref, no_ref, mo_ref, hf_scr, *, nc):
    lc = LC_A
    scale = DH_A ** -0.5
    ti = lax.broadcasted_iota(jnp.int32, (lc, lc), 0)
    si = lax.broadcasted_iota(jnp.int32, (lc, lc), 1)
    lane = lax.broadcasted_iota(jnp.int32, (1, DP_A), 1)
    lane_ok = lane < DH_A
    bc = bc_ref[0]
    br = br_ref[0]
    wn = wn_ref[0]

    def chunk(c, carry, d):
        cmat, nvec, m = carry
        r0 = pl.multiple_of(c * lc, lc)
        qc = q_ref[pl.ds(r0, lc), :]
        kc = k_ref[pl.ds(r0, lc), :] * scale
        vc = v_ref[pl.ds(r0, lc), :]
        gcol = gc_ref[0, 0, c] + bc
        grow = gr_ref[0, 0, c] + br
        i_col = gcol[:, d:d + 1]
        f_col = jax.nn.log_sigmoid(gcol[:, 2 + d:3 + d])
        i_row = grow[d:d + 1, :]
        f_row = jax.nn.log_sigmoid(grow[2 + d:3 + d, :])
        if d == 0:
            mask, mask_t = si <= ti, si >= ti
        else:
            mask, mask_t = si >= ti, si <= ti
        b_col = jnp.sum(jnp.where(mask, f_row, 0.0), axis=1, keepdims=True)
        b_row = jnp.sum(jnp.where(mask_t, f_col, 0.0), axis=0, keepdims=True)
        dmat = jnp.where(mask, b_col + (i_row - b_row), NEG_INF)
        inter = b_col + m
        m_out = jnp.maximum(inter, jnp.max(dmat, axis=1, keepdims=True))
        wmat = jnp.exp(dmat - m_out)
        sw = _dot_nt(qc, kc) * wmat
        sc_in = jnp.exp(inter - m_out)
        num = _dot(sw, vc) + sc_in * _dot_nt(qc, cmat)
        den = jnp.sum(sw, axis=1, keepdims=True) + sc_in * jnp.sum(qc * nvec, axis=1, keepdims=True)
        h = num / jnp.maximum(jnp.abs(den), jnp.exp(-m_out))
        bl = jnp.sum(f_row, axis=1, keepdims=True)
        dec = bl - b_col + i_col
        m_new = jnp.maximum(bl + m, jnp.max(dec, axis=0, keepdims=True))
        wk = jnp.exp(dec - m_new)
        sc_st = jnp.exp(bl + m - m_new)
        c_new = sc_st * cmat + _dot_tn(wk * vc, kc)
        n_new = sc_st * nvec + jnp.sum(wk * kc, axis=0, keepdims=True)
        return r0, h, (c_new, n_new, m_new)

    def init(d):
        return (c0_ref[0, d, 0], n0_ref[0, d, 0], m0_ref[0, d, 0])

    def fwd_body(c, carry):
        r0, h, carry = chunk(c, carry, 0)
        hf_scr[pl.ds(r0, lc), :] = h
        return carry

    def bwd_body(i, carry):
        c = nc - 1 - i
        r0, h, carry = chunk(c, carry, 1)
        h = h + hf_scr[pl.ds(r0, lc), :]
        mu = jnp.sum(h, axis=1, keepdims=True) * (1.0 / DH_A)
        dv = jnp.where(lane_ok, h - mu, 0.0)
        var = jnp.sum(dv * dv, axis=1, keepdims=True) * (1.0 / DH_A)
        hn = dv * lax.rsqrt(var + LN_EPS) * wn
        y_ref[pl.ds(r0, lc), :] = jax.nn.sigmoid(o_ref[pl.ds(r0, lc), :]) * hn
        return carry

    cf, nf, mf = lax.fori_loop(0, nc, fwd_body, init(0))
    cb, nb, mb = lax.fori_loop(0, nc, bwd_body, init(1))
    co_ref[0, 0, 0] = cf[:DH_A, :DH_A]
    co_ref[0, 1, 0] = cb[:DH_A, :DH_A]
    no_ref[0, 0, 0] = nf[:, :DH_A]
    no_ref[0, 1, 0] = nb[:, :DH_A]
    mo_ref[0, 0, 0] = mf
    mo_ref[0, 1, 0] = mb


def _mlstm_call(za, gcol, grow, bcol, brow, wn, c0, n0, m0, *, nb, seq, row_blk0):
    nc = seq // LC_A

    def zspec(part):
        return pl.BlockSpec((seq, DP_A), lambda b, h: (row_blk0 + b, part * H_A + h))

    gate_c = pl.BlockSpec((1, 1, nc, LC_A, 4), lambda b, h: (b, h, 0, 0, 0))
    gate_r = pl.BlockSpec((1, 1, nc, 4, LC_A), lambda b, h: (b, h, 0, 0, 0))
    return pl.pallas_call(
        functools.partial(_mlstm_kernel, nc=nc),
        grid=(nb, H_A),
        in_specs=[
            zspec(0), zspec(1), zspec(2), zspec(3), gate_c, gate_r,
            pl.BlockSpec((1, 1, 4), lambda b, h: (h, 0, 0)),
            pl.BlockSpec((1, 4, 1), lambda b, h: (h, 0, 0)),
            pl.BlockSpec((1, 1, DP_A), lambda b, h: (h, 0, 0)),
            pl.BlockSpec((1, 2, 1, DP_A, DP_A), lambda b, h: (b, 0, h, 0, 0)),
            pl.BlockSpec((1, 2, 1, 1, DP_A), lambda b, h: (b, 0, h, 0, 0)),
            pl.BlockSpec((1, 2, 1, 1, 1), lambda b, h: (b, 0, h, 0, 0)),
        ],
        out_specs=[
            pl.BlockSpec((seq, DP_A), lambda b, h: (b, h)),
            pl.BlockSpec((1, 2, 1, DH_A, DH_A), lambda b, h: (b, 0, h, 0, 0)),
            pl.BlockSpec((1, 2, 1, 1, DH_A), lambda b, h: (b, 0, h, 0, 0)),
            pl.BlockSpec((1, 2, 1, 1, 1), lambda b, h: (b, 0, h, 0, 0)),
        ],
        out_shape=[
            jax.ShapeDtypeStruct((nb * seq, Y_A_COLS), F32),
            jax.ShapeDtypeStruct((nb, 2, H_A, DH_A, DH_A), F32),
            jax.ShapeDtypeStruct((nb, 2, H_A, 1, DH_A), F32),
            jax.ShapeDtypeStruct((nb, 2, H_A, 1, 1), F32),
        ],
        scratch_shapes=[pltpu.VMEM((seq, DP_A), F32)],
        compiler_params=_cparams("arbitrary", "arbitrary"),
        name="mlstm",
    )(za, za, za, za, gcol, grow, bcol, brow, wn, c0, n0, m0)


def _softmax_pv(s_list, v_list, sink):
    m = s_list[0].max(axis=1, keepdims=True)
    for s in s_list[1:]:
        m = jnp.maximum(m, s.max(axis=1, keepdims=True))
    if sink is not None:
        m = jnp.maximum(m, sink)
    den = None
    acc = None
    for s, v in zip(s_list, v_list):
        e = jnp.exp(s - m)
        t = jnp.sum(e, axis=1, keepdims=True)
        den = t if den is None else den + t
        pv = _dot(e, v)
        acc = pv if acc is None else acc + pv
    if sink is not None:
        den = den + jnp.exp(sink - m)
    return acc / den


def _ctx_attn_kernel(bq_ref, bk_ref, bv_ref, cq_ref, ck_ref, cv_ref, sink_ref, yb_ref, yc_ref):
    scale = HD ** -0.5
    for g in range(KV_B):
        kg = bk_ref[:, g * HD:(g + 1) * HD]
        vg = bv_ref[:, g * HD:(g + 1) * HD]
        for j in range(G_B):
            hq = g * G_B + j
            qh = bq_ref[:, hq * HD:(hq + 1) * HD]
            s = _dot_nt(qh, kg) * scale
            yb_ref[:, hq * HD:(hq + 1) * HD] = _softmax_pv([s], [vg], sink_ref[0:1, hq:hq + 1])
    for h in range(H_C):
        sl = slice(h * HD, (h + 1) * HD)
        s = _dot_nt(cq_ref[:, sl], ck_ref[:, sl]) * scale
        yc_ref[:, sl] = _softmax_pv([s], [cv_ref[:, sl]], None)


def _ctx_attn_call(bq, bk, bv, cq, ck, cv, sink):
    def spec(n):
        return pl.BlockSpec((L_CTX, n), lambda b: (b, 0))

    return pl.pallas_call(
        _ctx_attn_kernel,
        grid=(B_CTX,),
        in_specs=[spec(D_B), spec(D_KVB), spec(D_KVB), spec(D_C), spec(D_C), spec(D_C),
                  pl.BlockSpec((1, 8), lambda b: (0, 0))],
        out_specs=[spec(D_B), spec(D_C)],
        out_shape=[jax.ShapeDtypeStruct((T_CTX, D_B), F32), jax.ShapeDtypeStruct((T_CTX, D_C), F32)],
        compiler_params=_cparams("arbitrary"),
        name="ctx_attn",
    )(bq, bk, bv, cq, ck, cv, sink)


def _rope(x, cos, sin_signed, first):
    rot = jnp.where(first, pltpu.roll(x, LANES - 16, 1), pltpu.roll(x, 16, 1))
    return x * cos + rot * sin_signed


def _win_attn_kernel(q_ref, k_ref, v_ref, kc_ref, vc_ref, cos_ref, sin_ref, sink_ref, y_ref, kr_scr):
    scale = HD ** -0.5
    qb_rows = WIN
    band = 3 * WIN
    nblk = N_LAT // qb_rows
    lane = lax.broadcasted_iota(jnp.int32, (1, LANES), 1)
    first = (lane % 32) < 16
    kr_scr[...] = _rope(k_ref[...], cos_ref[...], sin_ref[...], first)
    kctx = kc_ref[0, 0]
    vctx = vc_ref[0, 0]

    def body(blk, _):
        q0 = pl.multiple_of(blk * qb_rows, qb_rows)
        start = pl.multiple_of(jnp.clip(q0 - WIN, 0, N_LAT - band), WIN)
        cos = cos_ref[pl.ds(q0, qb_rows), :]
        sin = sin_ref[pl.ds(q0, qb_rows), :]
        kband = kr_scr[pl.ds(start, band), :]
        vband = v_ref[pl.ds(start, band), :]
        qpos = q0 + lax.broadcasted_iota(jnp.int32, (qb_rows, band), 0)
        kpos = start + lax.broadcasted_iota(jnp.int32, (qb_rows, band), 1)
        mask = jnp.abs(kpos - qpos) <= WIN
        for p in range(D_B // LANES):
            qp = _rope(q_ref[pl.ds(q0, qb_rows), p * LANES:(p + 1) * LANES], cos, sin, first)
            for u in range(LANES // HD):
                hq = p * (LANES // HD) + u
                g = hq // G_B
                qh = qp[:, u * HD:(u + 1) * HD]
                sl = slice(g * HD, (g + 1) * HD)
                s_loc = jnp.where(mask, _dot_nt(qh, kband[:, sl]) * scale, NEG_INF)
                s_ctx = _dot_nt(qh, kctx[:, sl]) * scale
                o = _softmax_pv([s_ctx, s_loc], [vctx[:, sl], vband[:, sl]], sink_ref[0:1, hq:hq + 1])
                y_ref[pl.ds(q0, qb_rows), hq * HD:(hq + 1) * HD] = o
        return 0

    lax.fori_loop(0, nblk, body, 0)


def _win_attn_call(bq, bk, bv, cache_k, cache_v, cos, sin, sink, layer):
    rb0 = T_CTX // N_LAT

    def spec(n):
        return pl.BlockSpec((N_LAT, n), lambda b: (rb0 + b, 0))

    cache = pl.BlockSpec((1, 1, PAST_LEN, D_KVB), lambda b: (b, layer, 0, 0))
    tab = pl.BlockSpec((N_LAT, LANES), lambda b: (0, 0))
    return pl.pallas_call(
        _win_attn_kernel,
        grid=(B_LAT,),
        in_specs=[spec(D_B), spec(D_KVB), spec(D_KVB), cache, cache, tab, tab,
                  pl.BlockSpec((1, 8), lambda b: (0, 0))],
        out_specs=pl.BlockSpec((N_LAT, D_B), lambda b: (b, 0)),
        out_shape=jax.ShapeDtypeStruct((T_LAT, D_B), F32),
        scratch_shapes=[pltpu.VMEM((N_LAT, D_KVB), F32)],
        compiler_params=_cparams("arbitrary"),
        name="win_attn",
    )(bq, bk, bv, cache_k, cache_v, cos, sin, sink)


def _na_attn_kernel(q_ref, k_ref, v_ref, kc_ref, vc_ref, bias_ref, y_ref):
    scale = HD ** -0.5
    rows = N_LAT // GRID_W
    nkeys = NA_WIN_H * GRID_W
    kctx = kc_ref[0, 0]
    vctx = vc_ref[0, 0]

    def body(r, _):
        kr0 = jnp.clip(r - NA_WIN_H // 2, 0, rows - NA_WIN_H)
        var = kr0 - r + NA_WIN_H - 1
        q0 = pl.multiple_of(r * GRID_W, GRID_W)
        k0 = pl.multiple_of(kr0 * GRID_W, GRID_W)
        for h in range(H_C):
            sl = slice(h * HD, (h + 1) * HD)
            qh = q_ref[pl.ds(q0, GRID_W), sl]
            kb = k_ref[pl.ds(k0, nkeys), sl]
            vb = v_ref[pl.ds(k0, nkeys), sl]
            s_loc = _dot_nt(qh, kb) * scale + bias_ref[h, var]
            s_ctx = _dot_nt(qh, kctx[:, sl]) * scale
            y_ref[pl.ds(q0, GRID_W), sl] = _softmax_pv([s_ctx, s_loc], [vctx[:, sl], vb], None)
        return 0

    lax.fori_loop(0, rows, body, 0)


def _na_attn_call(cq, ck, cv, cache_k, cache_v, bias_tab, layer):
    rb0 = T_CTX // N_LAT
    spec = pl.BlockSpec((N_LAT, D_C), lambda b: (rb0 + b, 0))
    cache = pl.BlockSpec((1, 1, PAST_LEN, D_C), lambda b: (b, layer, 0, 0))
    return pl.pallas_call(
        _na_attn_kernel,
        grid=(B_LAT,),
        in_specs=[spec, spec, spec, cache, cache,
                  pl.BlockSpec((H_C, NA_WIN_H, GRID_W, NA_WIN_H * GRID_W), lambda b: (0, 0, 0, 0))],
        out_specs=pl.BlockSpec((N_LAT, D_C), lambda b: (b, 0)),
        out_shape=jax.ShapeDtypeStruct((T_LAT, D_C), F32),
        compiler_params=_cparams("arbitrary"),
        name="na_attn",
    )(cq, ck, cv, cache_k, cache_v, bias_tab)


def _outproj_kernel(ya_ref, yb_ref, yc_ref, x_ref, ga_ref, scf_ref, shf_ref, lng_ref, lnb_ref, wo_ref, wr_ref,
                    x1_ref, h2_ref, route_ref):
    y = (_dot(ya_ref[...], wo_ref[0:Y_A_COLS, :])
         + _dot(yb_ref[...], wo_ref[Y_A_COLS:Y_A_COLS + D_B, :])
         + _dot(yc_ref[...], wo_ref[Y_A_COLS + D_B:, :]))
    x1 = _layer_norm(ALPHA * x_ref[...] + ga_ref[0] * y, lng_ref[...], lnb_ref[...])
    x1_ref[...] = x1
    h2 = x1 * (1.0 + scf_ref[0]) + shf_ref[0]
    h2_ref[...] = h2
    logits = jnp.dot(h2, wr_ref[...], preferred_element_type=F32, precision=lax.Precision.HIGHEST)
    lane = lax.broadcasted_iota(jnp.int32, logits.shape, 1)
    lanef = lane.astype(F32)
    big = float(LANES)
    lg = jnp.where((lane >= N_EXPERTS) & (lane < N_EXPERTS + N_GROUPS), logits, NEG_INF)
    mg = jnp.max(lg, axis=1, keepdims=True)
    grp = jnp.min(jnp.where(lg == mg, lanef, big), axis=1, keepdims=True) - float(N_EXPERTS)
    g_w = 1.0 / jnp.sum(jnp.exp(lg - mg), axis=1, keepdims=True)
    in_grp = (lane < N_EXPERTS) & ((lane // E_PER_GROUP).astype(F32) == grp)
    le = jnp.where(in_grp, logits, NEG_INF)
    l1 = jnp.max(le, axis=1, keepdims=True)
    i1 = jnp.min(jnp.where(le == l1, lanef, big), axis=1, keepdims=True)
    le2 = jnp.where(lanef == i1, NEG_INF, le)
    l2 = jnp.max(le2, axis=1, keepdims=True)
    i2 = jnp.min(jnp.where(le2 == l2, lanef, big), axis=1, keepdims=True)
    e2 = jnp.exp(l2 - l1)
    w1 = g_w / (1.0 + e2)
    w2 = g_w * e2 / (1.0 + e2)
    out = jnp.where(lane == 0, i1, jnp.where(lane == 1, i2, jnp.where(lane == 2, w1, jnp.where(lane == 3, w2, 0.0))))
    route_ref[...] = out[:, :8]


def _outproj_call(ya, yb, yc, x, ga, scf, shf, lng, lnb, wo, wr):
    tile_vec = pl.BlockSpec((1, 1, D_MODEL), lambda i: (i, 0, 0))
    row_vec = pl.BlockSpec((1, D_MODEL), lambda i: (0, 0))

    def tok(n):
        return pl.BlockSpec((TM, n), lambda i: (i, 0))

    return pl.pallas_call(
        _outproj_kernel,
        grid=(N_TILES,),
        in_specs=[tok(Y_A_COLS), tok(D_B), tok(D_C), tok(D_MODEL), tile_vec, tile_vec, tile_vec, row_vec, row_vec,
                  pl.BlockSpec((Y_A_COLS + D_B + D_C, D_MODEL), lambda i: (0, 0)),
                  pl.BlockSpec((D_MODEL, LANES), lambda i: (0, 0))],
        out_specs=[tok(D_MODEL), tok(D_MODEL), tok(8)],
        out_shape=[jax.ShapeDtypeStruct((T_ALL, D_MODEL), F32), jax.ShapeDtypeStruct((T_ALL, D_MODEL), F32),
                   jax.ShapeDtypeStruct((T_ALL, 8), F32)],
        compiler_params=_cparams("arbitrary"),
        name="outproj_router",
    )(ya, yb, yc, x, ga, scf, shf, lng, lnb, wo, wr)


def _row_copy(src_hbm, row, dst, slot, sem):
    return pltpu.make_async_copy(src_hbm.at[pl.ds(row, 1), :], dst.at[pl.ds(slot, 1), :], sem)


def _expert_kernel(blk_e_ref, src_ref, nused_ref, h_hbm, wg_ref, wu_ref, wd_ref, out_ref, xbuf, sem):
    j = pl.program_id(0)

    @pl.when(j < nused_ref[0])
    def _():
        base = j * MOE_BLK

        def issue(r, c):
            _row_copy(h_hbm, src_ref[base + r], xbuf, r, sem).start()
            return c

        lax.fori_loop(0, MOE_BLK, issue, 0)

        def drain(r, c):
            _row_copy(h_hbm, 0, xbuf, r, sem).wait()
            return c

        lax.fori_loop(0, MOE_BLK, drain, 0)
        xb = xbuf[...].astype(BF16)
        g = jnp.dot(xb, wg_ref[0].astype(BF16), preferred_element_type=F32)
        u = jnp.dot(xb, wu_ref[0].astype(BF16), preferred_element_type=F32)
        out_ref[...] = _dot(jax.nn.silu(g) * u, wd_ref[0])

    @pl.when(j >= nused_ref[0])
    def _():
        out_ref[...] = jnp.zeros_like(out_ref)


def _expert_call(blk_e, src_tok, nused, h2, wg, wu, wd, nblk):
    grid_spec = pltpu.PrefetchScalarGridSpec(
        num_scalar_prefetch=3,
        grid=(nblk,),
        in_specs=[
            pl.BlockSpec(memory_space=pl.ANY),
            pl.BlockSpec((1, D_MODEL, D_EXPERT), lambda j, be, st, nu: (be[j], 0, 0)),
            pl.BlockSpec((1, D_MODEL, D_EXPERT), lambda j, be, st, nu: (be[j], 0, 0)),
            pl.BlockSpec((1, D_EXPERT, D_MODEL), lambda j, be, st, nu: (be[j], 0, 0)),
        ],
        out_specs=pl.BlockSpec((MOE_BLK, D_MODEL), lambda j, be, st, nu: (j, 0)),
        scratch_shapes=[pltpu.VMEM((MOE_BLK, D_MODEL), F32), pltpu.SemaphoreType.DMA(())],
    )
    return pl.pallas_call(
        _expert_kernel,
        grid_spec=grid_spec,
        out_shape=jax.ShapeDtypeStruct((nblk * MOE_BLK, D_MODEL), F32),
        compiler_params=_cparams("arbitrary"),
        name="experts",
    )(blk_e, src_tok, nused, h2, wg, wu, wd)


def _combine_kernel(pos_ref, eo_hbm, x1_ref, route_ref, gf_ref, lng_ref, lnb_ref, x2_ref, buf, sem):
    i = pl.program_id(0)
    base = i * TM

    def issue(t, c):
        _row_copy(eo_hbm, pos_ref[base + t], buf.at[0], t, sem).start()
        _row_copy(eo_hbm, pos_ref[T_ALL + base + t], buf.at[1], t, sem).start()
        return c

    lax.fori_loop(0, TM, issue, 0)

    def drain(t, c):
        _row_copy(eo_hbm, 0, buf.at[0], t, sem).wait()
        _row_copy(eo_hbm, 0, buf.at[1], t, sem).wait()
        return c

    lax.fori_loop(0, TM, drain, 0)
    route = route_ref[...]
    y = route[:, 2:3] * buf[0] + route[:, 3:4] * buf[1]
    x2_ref[...] = _layer_norm(ALPHA * x1_ref[...] + gf_ref[0] * y, lng_ref[...], lnb_ref[...])


def _combine_call(pos2, eo, x1, route, gf, lng, lnb):
    grid_spec = pltpu.PrefetchScalarGridSpec(
        num_scalar_prefetch=1,
        grid=(N_TILES,),
        in_specs=[
            pl.BlockSpec(memory_space=pl.ANY),
            pl.BlockSpec((TM, D_MODEL), lambda i, p: (i, 0)),
            pl.BlockSpec((TM, 8), lambda i, p: (i, 0)),
            pl.BlockSpec((1, 1, D_MODEL), lambda i, p: (i, 0, 0)),
            pl.BlockSpec((1, D_MODEL), lambda i, p: (0, 0)),
            pl.BlockSpec((1, D_MODEL), lambda i, p: (0, 0)),
        ],
        out_specs=pl.BlockSpec((TM, D_MODEL), lambda i, p: (i, 0)),
        scratch_shapes=[pltpu.VMEM((2, TM, D_MODEL), F32), pltpu.SemaphoreType.DMA(())],
    )
    return pl.pallas_call(
        _combine_kernel,
        grid_spec=grid_spec,
        out_shape=jax.ShapeDtypeStruct((T_ALL, D_MODEL), F32),
        compiler_params=_cparams("arbitrary"),
        name="combine_norm",
    )(pos2, eo, x1, route, gf, lng, lnb)


def _prep_w_in(w_in):
    a = w_in[..., :4 * D_A].reshape(DEPTH, D_MODEL, 4, H_A, DH_A)
    a = jnp.pad(a, ((0, 0), (0, 0), (0, 0), (0, 0), (0, DP_A - DH_A))).reshape(DEPTH, D_MODEL, ZA_COLS)
    g = jnp.pad(w_in[..., 4 * D_A:4 * D_A + 4 * H_A], ((0, 0), (0, 0), (0, LANES - 4 * H_A)))
    rest = w_in[..., 4 * D_A + 4 * H_A:]
    return jnp.concatenate([a, rest, g], -1).astype(BF16)


def _prep_w_out(w_out):
    a = w_out[:, :D_A].reshape(DEPTH, H_A, DH_A, D_MODEL)
    a = jnp.pad(a, ((0, 0), (0, 0), (0, DP_A - DH_A), (0, 0))).reshape(DEPTH, Y_A_COLS, D_MODEL)
    return jnp.concatenate([a, w_out[:, D_A:]], 1).astype(BF16)


def _rope_tables():
    t = np.arange(N_LAT)
    nf = HD // 4
    inv = ROPE_BASE ** (-np.arange(nf, dtype=np.float32) / nf)
    ar = (t // GRID_W).astype(np.float32)[:, None] * inv
    ac = (t % GRID_W).astype(np.float32)[:, None] * inv
    ang = jnp.asarray(np.concatenate([ar, ar, ac, ac], -1), F32)
    cos, sin = jnp.cos(ang), jnp.sin(ang)
    sign = np.where((np.arange(HD) % 32) < 16, -1.0, 1.0).astype(np.float32)
    reps = LANES // HD
    return jnp.tile(cos, (1, reps)), jnp.tile(sin * sign, (1, reps))


def _na_bias_table(rpb_l):
    qcol = np.arange(GRID_W)[:, None]
    kcol = np.arange(GRID_W)[None, :]
    dc = np.clip(kcol - qcol, 1 - NA_WIN_W, NA_WIN_W - 1) + NA_WIN_W - 1
    wstart = np.clip(qcol - NA_WIN_W // 2, 0, GRID_W - NA_WIN_W)
    in_win = (kcol >= wstart) & (kcol < wstart + NA_WIN_W)
    dr = np.arange(NA_WIN_H)[:, None] + np.arange(NA_WIN_H)[None, :]
    tab = rpb_l[:, dr][..., dc]
    tab = jnp.where(in_win[None, None, None], tab, NEG_INF)
    return tab.transpose(0, 1, 3, 2, 4).reshape(H_C, NA_WIN_H, GRID_W, NA_WIN_H * GRID_W)


def _gate_layouts(zg, nb, seq, row0):
    nc = seq // LC_A
    g = zg[row0:row0 + nb * seq, :4 * H_A].reshape(nb, seq, 2, 2, H_A)
    g = g.transpose(0, 4, 1, 2, 3).reshape(nb, H_A, nc, LC_A, 4)
    return g, g.transpose(0, 1, 2, 4, 3)


def _dispatch(route):
    a_tot = 2 * T_ALL
    nblk = a_tot // MOE_BLK + N_EXPERTS
    e_flat = route[:, :2].astype(jnp.int32).reshape(-1)
    oh = (e_flat[:, None] == jnp.arange(N_EXPERTS, dtype=jnp.int32)[None, :]).astype(jnp.int32)
    cs = jnp.cumsum(oh, axis=0)
    rank = jnp.sum(oh * cs, axis=1) - 1
    counts = cs[-1]
    padded = (counts + MOE_BLK - 1) // MOE_BLK * MOE_BLK
    p_end = jnp.cumsum(padded)
    pos = (p_end - padded)[e_flat] + rank
    src_tok = jnp.zeros((nblk * MOE_BLK,), jnp.int32).at[pos].set(jnp.arange(a_tot, dtype=jnp.int32) // 2)
    blk_e = jnp.minimum(jnp.searchsorted(p_end, jnp.arange(nblk, dtype=jnp.int32) * MOE_BLK, side="right"),
                        N_EXPERTS - 1).astype(jnp.int32)
    nused = (p_end[-1:] // MOE_BLK).astype(jnp.int32)
    pos2 = pos.reshape(T_ALL, 2).T.reshape(-1).astype(jnp.int32)
    return blk_e, src_tok, nused, pos2, nblk


def kernel(x_prompt, x_sample, state_a_C, state_a_n, state_a_m, cache_b_k, cache_b_v, cache_c_k, cache_c_v, c, c_ctx, w_in, b_a_i, b_a_f, w_a_hnorm, b_sink, rpb, w_out, w_ada, b_ada, ln_g, ln_b, w_router_grp, w_router_exp, w_e_gate, w_e_up, w_e_down):
    w_in_p = _prep_w_in(w_in)
    w_out_p = _prep_w_out(w_out)
    w_r = jnp.pad(jnp.concatenate([w_router_exp, w_router_grp], -1),
                  ((0, 0), (0, 0), (0, LANES - N_EXPERTS - N_GROUPS)))
    wn_p = jnp.pad(w_a_hnorm.reshape(DEPTH, H_A, 1, DH_A), ((0, 0), (0, 0), (0, 0), (0, DP_A - DH_A)))
    gate_b = jnp.concatenate([b_a_i, b_a_f], 1).transpose(0, 2, 1)
    sink_p = jnp.pad(b_sink, ((0, 0), (0, 8 - H_B))).reshape(DEPTH, 1, 8)
    cos_t, sin_t = _rope_tables()
    cb_k = cache_b_k.reshape(B_LAT, DEPTH, PAST_LEN, D_KVB)
    cb_v = cache_b_v.reshape(B_LAT, DEPTH, PAST_LEN, D_KVB)
    cc_k = cache_c_k.reshape(B_LAT, DEPTH, PAST_LEN, D_C)
    cc_v = cache_c_v.reshape(B_LAT, DEPTH, PAST_LEN, D_C)
    pad_c = ((0, 0), (0, 0), (0, 0), (0, 0), (0, DP_A - DH_A), (0, DP_A - DH_A))
    st_c = jnp.pad(state_a_C, pad_c)
    st_n = jnp.pad(state_a_n, pad_c[:-1])[..., None, :]
    st_m = state_a_m[..., None, None]
    z_c = jnp.zeros((B_CTX, 2, H_A, DP_A, DP_A), F32)
    z_n = jnp.zeros((B_CTX, 2, H_A, 1, DP_A), F32)
    z_m = jnp.zeros((B_CTX, 2, H_A, 1, 1), F32)

    cvec = jnp.concatenate([c, c_ctx[None, :], jnp.zeros((3, D_MODEL), F32)], 0)
    mod = _ada_call(cvec, w_ada, b_ada)
    tile_row = np.concatenate([np.full(T_CTX // TM, B_LAT), np.repeat(np.arange(B_LAT), N_LAT // TM)])
    mod_t = mod[:, tile_row].reshape(DEPTH, N_TILES, 1, 6, D_MODEL)

    x = jnp.concatenate([x_prompt.reshape(T_CTX, D_MODEL), x_sample.reshape(T_LAT, D_MODEL)], 0)
    cs_, ns_, ms_, kbs, vbs, kcs, vcs = [], [], [], [], [], [], []
    for l in range(DEPTH):
        sh_a, sc_a, g_a, sh_f, sc_f, g_f = (mod_t[l, :, :, j] for j in range(6))
        za, bq, bk, bv, cq, ck, cv, zg = _inproj_call(x, sc_a, sh_a, w_in_p[l])
        bcol = gate_b[l].reshape(H_A, 1, 4)
        brow = gate_b[l].reshape(H_A, 4, 1)
        gc, gr = _gate_layouts(zg, B_CTX, L_CTX, 0)
        ya_c, c_l, n_l, m_l = _mlstm_call(za, gc, gr, bcol, brow, wn_p[l], z_c, z_n, z_m,
                                          nb=B_CTX, seq=L_CTX, row_blk0=0)
        yb_c, yc_c = _ctx_attn_call(bq, bk, bv, cq, ck, cv, sink_p[l])
        gc, gr = _gate_layouts(zg, B_LAT, N_LAT, T_CTX)
        ya_l, _, _, _ = _mlstm_call(za, gc, gr, bcol, brow, wn_p[l], st_c[:, l], st_n[:, l], st_m[:, l],
                                    nb=B_LAT, seq=N_LAT, row_blk0=T_CTX // N_LAT)
        yb_l = _win_attn_call(bq, bk, bv, cb_k, cb_v, cos_t, sin_t, sink_p[l], l)
        yc_l = _na_attn_call(cq, ck, cv, cc_k, cc_v, _na_bias_table(rpb[l]), l)
        ya = jnp.concatenate([ya_c, ya_l], 0)
        yb = jnp.concatenate([yb_c, yb_l], 0)
        yc = jnp.concatenate([yc_c, yc_l], 0)
        x1, h2, route = _outproj_call(ya, yb, yc, x, g_a, sc_f, sh_f, ln_g[l, 0:1], ln_b[l, 0:1], w_out_p[l], w_r[l])
        blk_e, src_tok, nused, pos2, nblk = _dispatch(route)
        eo = _expert_call(blk_e, src_tok, nused, h2, w_e_gate[l], w_e_up[l], w_e_down[l], nblk)
        x = _combine_call(pos2, eo, x1, route, g_f, ln_g[l, 1:2], ln_b[l, 1:2])
        cs_.append(c_l)
        ns_.append(n_l.reshape(B_CTX, 2, H_A, DH_A))
        ms_.append(m_l.reshape(B_CTX, 2, H_A))
        kbs.append(bk[:T_CTX].reshape(B_CTX, L_CTX, KV_B, HD))
        vbs.append(bv[:T_CTX].reshape(B_CTX, L_CTX, KV_B, HD))
        kcs.append(ck[:T_CTX].reshape(B_CTX, L_CTX, H_C, HD))
        vcs.append(cv[:T_CTX].reshape(B_CTX, L_CTX, H_C, HD))
    y_prompt = x[:T_CTX].reshape(B_CTX, L_CTX, D_MODEL)
    y_sample = x[T_CTX:].reshape(B_LAT, N_LAT, D_MODEL)
    return (y_prompt, y_sample, jnp.stack(cs_, 1), jnp.stack(ns_, 1), jnp.stack(ms_, 1),
            jnp.stack(kbs, 1), jnp.stack(vbs, 1), jnp.stack(kcs, 1), jnp.stack(vcs, 1))
```

```python
import functools

import numpy as np
import jax
import jax.numpy as jnp
from jax import lax
from jax.experimental import pallas as pl
from jax.experimental.pallas import tpu as pltpu

F32 = jnp.float32
BF16 = jnp.bfloat16
NEG_INF = float("-inf")

D_MODEL = 1024
DEPTH = 4
B_CTX, L_CTX = 16, 256
B_LAT, N_LAT = 4, 2048
PAST_LEN = 512
GRID_W = 64
H_A, DH_A = 4, 96
D_A = H_A * DH_A
H_B, KV_B, HD = 6, 2, 64
G_B = H_B // KV_B
D_B, D_KVB = H_B * HD, KV_B * HD
WIN = 128
ROPE_BASE = 10000.0
H_C = 4
D_C = H_C * HD
NA_WIN_H, NA_WIN_W = 8, 16
N_GROUPS, E_PER_GROUP = 4, 8
N_EXPERTS = N_GROUPS * E_PER_GROUP
D_EXPERT = D_MODEL // 4
ALPHA = (2 * DEPTH) ** 0.25
LN_EPS = 1e-5

LANES = 128
DP_A = LANES
LC_K = LANES
TM = 256
MOE_BLK = 128
VMEM_LIMIT = 48 * 1024 * 1024

T_CTX = B_CTX * L_CTX
T_LAT = B_LAT * N_LAT
T_ALL = T_CTX + T_LAT
N_TILES = T_ALL // TM
ZA_COLS = 4 * H_A * DP_A
Y_A_COLS = H_A * DP_A
IN_SPLITS = (ZA_COLS, D_B, D_KVB, D_KVB, D_C, D_C, D_C, LANES)
IN_COLS = sum(IN_SPLITS)


def _cparams(*sem):
    return pltpu.CompilerParams(dimension_semantics=sem, vmem_limit_bytes=VMEM_LIMIT)


def _dot(a, b):
    return jnp.dot(a.astype(BF16), b.astype(BF16), preferred_element_type=F32)


def _dot_nt(a, b):
    return lax.dot_general(a.astype(BF16), b.astype(BF16), (((1,), (1,)), ((), ())), preferred_element_type=F32)


def _dot_tn(a, b):
    return lax.dot_general(a.astype(BF16), b.astype(BF16), (((0,), (0,)), ((), ())), preferred_element_type=F32)


def _layer_norm(v, g, b):
    mu = jnp.mean(v, -1, keepdims=True)
    var = jnp.mean(jnp.square(v - mu), -1, keepdims=True)
    return (v - mu) * lax.rsqrt(var + LN_EPS) * g + b


def _ada_kernel(c_ref, w_ref, b_ref, o_ref):
    s = jax.nn.silu(c_ref[...])
    o_ref[0] = _dot(s, w_ref[0]) + b_ref[0]


def _ada_call(cvec, w_ada, b_ada):
    nb = 6
    return pl.pallas_call(
        _ada_kernel,
        grid=(DEPTH, nb),
        in_specs=[
            pl.BlockSpec((8, D_MODEL), lambda l, j: (0, 0)),
            pl.BlockSpec((1, D_MODEL, D_MODEL), lambda l, j: (l, 0, j)),
            pl.BlockSpec((1, 1, D_MODEL), lambda l, j: (l, 0, j)),
        ],
        out_specs=pl.BlockSpec((1, 8, D_MODEL), lambda l, j: (l, 0, j)),
        out_shape=jax.ShapeDtypeStruct((DEPTH, 8, 6 * D_MODEL), F32),
        compiler_params=_cparams("arbitrary", "arbitrary"),
        name="adaln",
    )(cvec, w_ada, b_ada.reshape(DEPTH, 1, 6 * D_MODEL))


def _inproj_kernel(x_ref, sc_ref, sh_ref, w_ref, *out_refs):
    h = (x_ref[...] * (1.0 + sc_ref[0]) + sh_ref[0]).astype(BF16)
    off = 0
    for ref in out_refs:
        n = ref.shape[-1]
        ref[...] = jnp.dot(h, w_ref[:, off:off + n], preferred_element_type=F32)
        off += n


def _inproj_call(x, sc, sh, w):
    tile_vec = pl.BlockSpec((1, 1, D_MODEL), lambda i: (i, 0, 0))
    return pl.pallas_call(
        _inproj_kernel,
        grid=(N_TILES,),
        in_specs=[
            pl.BlockSpec((TM, D_MODEL), lambda i: (i, 0)),
            tile_vec, tile_vec,
            pl.BlockSpec((D_MODEL, IN_COLS), lambda i: (0, 0)),
        ],
        out_specs=[pl.BlockSpec((TM, n), lambda i: (i, 0)) for n in IN_SPLITS],
        out_shape=[jax.ShapeDtypeStruct((T_ALL, n), F32) for n in IN_SPLITS],
        compiler_params=_cparams("arbitrary"),
        name="inproj",
    )(x, sc, sh, w)


def _mlstm_kernel(q_ref, kt_ref, v_ref, o_ref, gc_ref, gr_ref, bc_ref, br_ref, wn_ref, s0_ref, m0_ref,
                  y_ref, so_ref, mo_ref, hf_scr, hb_scr, *, nc, hp):
    lc = LC_K
    scale = DH_A ** -0.5
    ti = lax.broadcasted_iota(jnp.int32, (lc, lc), 0)
    si = lax.broadcasted_iota(jnp.int32, (lc, lc), 1)
    lane = lax.broadcasted_iota(jnp.int32, (1, DP_A), 1)
    lane_ok = lane < DH_A
    ones = jnp.ones((lc, DP_A), F32)

    def chunk(c, carry, d, j):
        smat, m = carry
        r0 = pl.multiple_of(c * lc, lc)
        cols = slice(j * DP_A, (j + 1) * DP_A)
        qc = q_ref[pl.ds(r0, lc), cols].astype(BF16)
        kt = (kt_ref[0, j, c] * scale).astype(BF16)
        v1 = jnp.concatenate([v_ref[pl.ds(r0, lc), cols], ones], axis=1)
        gcol = gc_ref[0, j, c] + bc_ref[j]
        grow = gr_ref[0, j, c] + br_ref[j]
        i_col = gcol[:, d:d + 1]
        f_col = jax.nn.log_sigmoid(gcol[:, 2 + d:3 + d])
        i_row = grow[d:d + 1, :]
        f_row = jax.nn.log_sigmoid(grow[2 + d:3 + d, :])
        if d == 0:
            mask, mask_t = si <= ti, si >= ti
        else:
            mask, mask_t = si >= ti, si <= ti
        b_col = jnp.sum(jnp.where(mask, f_row, 0.0), axis=1, keepdims=True)
        b_row = jnp.sum(jnp.where(mask_t, f_col, 0.0), axis=0, keepdims=True)
        dmat = jnp.where(mask, b_col + (i_row - b_row), NEG_INF)
        inter = b_col + m
        m_out = jnp.maximum(inter, jnp.max(dmat, axis=1, keepdims=True))
        wmat = jnp.exp(dmat - m_out)
        sw = jnp.dot(qc, kt, preferred_element_type=F32) * wmat
        sc_in = jnp.exp(inter - m_out)
        tot = _dot(sw, v1) + sc_in * _dot(qc, smat)
        h = tot[:, :DP_A] / jnp.maximum(jnp.abs(tot[:, DP_A:]), jnp.exp(-m_out))
        bl = jnp.sum(f_row, axis=1, keepdims=True)
        dec = bl - b_col + i_col
        m_new = jnp.maximum(bl + m, jnp.max(dec, axis=0, keepdims=True))
        wk = jnp.exp(dec - m_new)
        sc_st = jnp.exp(bl + m - m_new)
        s_new = sc_st * smat + _dot(kt, wk * v1)
        return r0, cols, h, (s_new, m_new)

    def body(i, carry):
        out = []
        for j in range(hp):
            r0, cols, h, st = chunk(i, carry[2 * j], 0, j)
            hf_scr[pl.ds(r0, lc), cols] = h
            out.append(st)
            r0, cols, h, st = chunk(nc - 1 - i, carry[2 * j + 1], 1, j)
            hb_scr[pl.ds(r0, lc), cols] = h
            out.append(st)
        return tuple(out)

    init = tuple((s0_ref[0, d, j], m0_ref[0, d, j]) for j in range(hp) for d in range(2))
    final = lax.fori_loop(0, nc, body, init)
    for j in range(hp):
        for d in range(2):
            smat, m = final[2 * j + d]
            so_ref[0, d, j] = smat
            mo_ref[0, d, j] = m

    def finish(c, _):
        r0 = pl.multiple_of(c * lc, lc)
        for j in range(hp):
            cols = slice(j * DP_A, (j + 1) * DP_A)
            h = hf_scr[pl.ds(r0, lc), cols] + hb_scr[pl.ds(r0, lc), cols]
            mu = jnp.sum(h, axis=1, keepdims=True) * (1.0 / DH_A)
            dv = jnp.where(lane_ok, h - mu, 0.0)
            var = jnp.sum(dv * dv, axis=1, keepdims=True) * (1.0 / DH_A)
            hn = dv * lax.rsqrt(var + LN_EPS) * wn_ref[j]
            y_ref[pl.ds(r0, lc), cols] = jax.nn.sigmoid(o_ref[pl.ds(r0, lc), cols]) * hn
        return 0

    lax.fori_loop(0, nc, finish, 0)


def _mlstm_call(za, kt, gcol, grow, bcol, brow, wn, s0, m0, *, nb, seq, row_blk0, hp):
    nc = seq // LC_K
    ng = H_A // hp
    w = hp * DP_A

    def zspec(part):
        return pl.BlockSpec((seq, w), lambda b, g: (row_blk0 + b, part * ng + g))

    def head_vec(*tail):
        return pl.BlockSpec((hp,) + tail, lambda b, g: (g,) + (0,) * len(tail))

    def state(*tail):
        return pl.BlockSpec((1, 2, hp) + tail, lambda b, g: (b, 0, g) + (0,) * len(tail))

    def per_chunk(r, c):
        return pl.BlockSpec((1, hp, nc, r, c), lambda b, g: (b, g, 0, 0, 0))

    return pl.pallas_call(
        functools.partial(_mlstm_kernel, nc=nc, hp=hp),
        grid=(nb, ng),
        in_specs=[
            zspec(0), per_chunk(DP_A, LC_K), zspec(2), zspec(3), per_chunk(LC_K, 4), per_chunk(4, LC_K),
            head_vec(1, 4), head_vec(4, 1), head_vec(1, DP_A),
            state(DP_A, 2 * DP_A), state(1, 1),
        ],
        out_specs=[pl.BlockSpec((seq, w), lambda b, g: (b, g)), state(DP_A, 2 * DP_A), state(1, 1)],
        out_shape=[
            jax.ShapeDtypeStruct((nb * seq, Y_A_COLS), F32),
            jax.ShapeDtypeStruct((nb, 2, H_A, DP_A, 2 * DP_A), F32),
            jax.ShapeDtypeStruct((nb, 2, H_A, 1, 1), F32),
        ],
        scratch_shapes=[pltpu.VMEM((seq, w), F32), pltpu.VMEM((seq, w), F32)],
        compiler_params=_cparams("arbitrary", "arbitrary"),
        name="mlstm",
    )(za, kt, za, za, gcol, grow, bcol, brow, wn, s0, m0)


def _softmax_pv(s_list, v_list, sink):
    m = s_list[0].max(axis=1, keepdims=True)
    for s in s_list[1:]:
        m = jnp.maximum(m, s.max(axis=1, keepdims=True))
    if sink is not None:
        m = jnp.maximum(m, sink)
    den = None
    acc = None
    for s, v in zip(s_list, v_list):
        e = jnp.exp(s - m)
        t = jnp.sum(e, axis=1, keepdims=True)
        den = t if den is None else den + t
        pv = _dot(e, v)
        acc = pv if acc is None else acc + pv
    if sink is not None:
        den = den + jnp.exp(sink - m)
    return acc / den


def _ctx_attn_kernel(bq_ref, bk_ref, bv_ref, cq_ref, ck_ref, cv_ref, sink_ref, yb_ref, yc_ref):
    scale = HD ** -0.5
    for g in range(KV_B):
        kg = bk_ref[:, g * HD:(g + 1) * HD]
        vg = bv_ref[:, g * HD:(g + 1) * HD]
        for j in range(G_B):
            hq = g * G_B + j
            qh = bq_ref[:, hq * HD:(hq + 1) * HD]
            s = _dot_nt(qh, kg) * scale
            yb_ref[:, hq * HD:(hq + 1) * HD] = _softmax_pv([s], [vg], sink_ref[0:1, hq:hq + 1])
    for h in range(H_C):
        sl = slice(h * HD, (h + 1) * HD)
        s = _dot_nt(cq_ref[:, sl], ck_ref[:, sl]) * scale
        yc_ref[:, sl] = _softmax_pv([s], [cv_ref[:, sl]], None)


def _ctx_attn_call(bq, bk, bv, cq, ck, cv, sink):
    def spec(n):
        return pl.BlockSpec((L_CTX, n), lambda b: (b, 0))

    return pl.pallas_call(
        _ctx_attn_kernel,
        grid=(B_CTX,),
        in_specs=[spec(D_B), spec(D_KVB), spec(D_KVB), spec(D_C), spec(D_C), spec(D_C),
                  pl.BlockSpec((1, 8), lambda b: (0, 0))],
        out_specs=[spec(D_B), spec(D_C)],
        out_shape=[jax.ShapeDtypeStruct((T_CTX, D_B), F32), jax.ShapeDtypeStruct((T_CTX, D_C), F32)],
        compiler_params=_cparams("arbitrary"),
        name="ctx_attn",
    )(bq, bk, bv, cq, ck, cv, sink)


def _rope(x, cos, sin_signed, first):
    rot = jnp.where(first, pltpu.roll(x, LANES - 16, 1), pltpu.roll(x, 16, 1))
    return x * cos + rot * sin_signed


def _win_attn_kernel(q_ref, k_ref, v_ref, kc_ref, vc_ref, cos_ref, sin_ref, sink_ref, y_ref, kr_scr):
    scale = HD ** -0.5
    qb_rows = WIN
    band = 3 * WIN
    nblk = N_LAT // qb_rows
    lane = lax.broadcasted_iota(jnp.int32, (1, LANES), 1)
    first = (lane % 32) < 16
    kr_scr[...] = _rope(k_ref[...], cos_ref[...], sin_ref[...], first)
    kctx = kc_ref[0, 0]
    vctx = vc_ref[0, 0]

    def body(blk, _):
        q0 = pl.multiple_of(blk * qb_rows, qb_rows)
        start = pl.multiple_of(jnp.clip(q0 - WIN, 0, N_LAT - band), WIN)
        cos = cos_ref[pl.ds(q0, qb_rows), :]
        sin = sin_ref[pl.ds(q0, qb_rows), :]
        kband = kr_scr[pl.ds(start, band), :]
        vband = v_ref[pl.ds(start, band), :]
        qpos = q0 + lax.broadcasted_iota(jnp.int32, (qb_rows, band), 0)
        kpos = start + lax.broadcasted_iota(jnp.int32, (qb_rows, band), 1)
        mask = jnp.abs(kpos - qpos) <= WIN
        for p in range(D_B // LANES):
            qp = _rope(q_ref[pl.ds(q0, qb_rows), p * LANES:(p + 1) * LANES], cos, sin, first)
            for u in range(LANES // HD):
                hq = p * (LANES // HD) + u
                g = hq // G_B
                qh = qp[:, u * HD:(u + 1) * HD]
                sl = slice(g * HD, (g + 1) * HD)
                s_loc = jnp.where(mask, _dot_nt(qh, kband[:, sl]) * scale, NEG_INF)
                s_ctx = _dot_nt(qh, kctx[:, sl]) * scale
                o = _softmax_pv([s_ctx, s_loc], [vctx[:, sl], vband[:, sl]], sink_ref[0:1, hq:hq + 1])
                y_ref[pl.ds(q0, qb_rows), hq * HD:(hq + 1) * HD] = o
        return 0

    lax.fori_loop(0, nblk, body, 0)


def _win_attn_call(bq, bk, bv, cache_k, cache_v, cos, sin, sink, layer):
    rb0 = T_CTX // N_LAT

    def spec(n):
        return pl.BlockSpec((N_LAT, n), lambda b: (rb0 + b, 0))

    cache = pl.BlockSpec((1, 1, PAST_LEN, D_KVB), lambda b: (b, layer, 0, 0))
    tab = pl.BlockSpec((N_LAT, LANES), lambda b: (0, 0))
    return pl.pallas_call(
        _win_attn_kernel,
        grid=(B_LAT,),
        in_specs=[spec(D_B), spec(D_KVB), spec(D_KVB), cache, cache, tab, tab,
                  pl.BlockSpec((1, 8), lambda b: (0, 0))],
        out_specs=pl.BlockSpec((N_LAT, D_B), lambda b: (b, 0)),
        out_shape=jax.ShapeDtypeStruct((T_LAT, D_B), F32),
        scratch_shapes=[pltpu.VMEM((N_LAT, D_KVB), F32)],
        compiler_params=_cparams("arbitrary"),
        name="win_attn",
    )(bq, bk, bv, cache_k, cache_v, cos, sin, sink)


def _na_attn_kernel(q_ref, k_ref, v_ref, kc_ref, vc_ref, bias_ref, y_ref):
    scale = HD ** -0.5
    rows = N_LAT // GRID_W
    nkeys = NA_WIN_H * GRID_W
    kctx = kc_ref[0, 0]
    vctx = vc_ref[0, 0]

    def body(r, _):
        kr0 = jnp.clip(r - NA_WIN_H // 2, 0, rows - NA_WIN_H)
        var = kr0 - r + NA_WIN_H - 1
        q0 = pl.multiple_of(r * GRID_W, GRID_W)
        k0 = pl.multiple_of(kr0 * GRID_W, GRID_W)
        for h in range(H_C):
            sl = slice(h * HD, (h + 1) * HD)
            qh = q_ref[pl.ds(q0, GRID_W), sl]
            kb = k_ref[pl.ds(k0, nkeys), sl]
            vb = v_ref[pl.ds(k0, nkeys), sl]
            s_loc = _dot_nt(qh, kb) * scale + bias_ref[h, var]
            s_ctx = _dot_nt(qh, kctx[:, sl]) * scale
            y_ref[pl.ds(q0, GRID_W), sl] = _softmax_pv([s_ctx, s_loc], [vctx[:, sl], vb], None)
        return 0

    lax.fori_loop(0, rows, body, 0)


def _na_attn_call(cq, ck, cv, cache_k, cache_v, bias_tab, layer):
    rb0 = T_CTX // N_LAT
    spec = pl.BlockSpec((N_LAT, D_C), lambda b: (rb0 + b, 0))
    cache = pl.BlockSpec((1, 1, PAST_LEN, D_C), lambda b: (b, layer, 0, 0))
    return pl.pallas_call(
        _na_attn_kernel,
        grid=(B_LAT,),
        in_specs=[spec, spec, spec, cache, cache,
                  pl.BlockSpec((H_C, NA_WIN_H, GRID_W, NA_WIN_H * GRID_W), lambda b: (0, 0, 0, 0))],
        out_specs=pl.BlockSpec((N_LAT, D_C), lambda b: (b, 0)),
        out_shape=jax.ShapeDtypeStruct((T_LAT, D_C), F32),
        compiler_params=_cparams("arbitrary"),
        name="na_attn",
    )(cq, ck, cv, cache_k, cache_v, bias_tab)


def _outproj_kernel(ya_ref, yb_ref, yc_ref, x_ref, ga_ref, scf_ref, shf_ref, lng_ref, lnb_ref, wo_ref, wr_ref,
                    x1_ref, h2_ref, route_ref, cnt_ref, run_scr):
    @pl.when(pl.program_id(0) == 0)
    def _():
        run_scr[...] = jnp.zeros_like(run_scr)

    y = (_dot(ya_ref[...], wo_ref[0:Y_A_COLS, :])
         + _dot(yb_ref[...], wo_ref[Y_A_COLS:Y_A_COLS + D_B, :])
         + _dot(yc_ref[...], wo_ref[Y_A_COLS + D_B:, :]))
    x1 = _layer_norm(ALPHA * x_ref[...] + ga_ref[0] * y, lng_ref[...], lnb_ref[...])
    x1_ref[...] = x1
    h2 = x1 * (1.0 + scf_ref[0]) + shf_ref[0]
    h2_ref[...] = h2
    logits = jnp.dot(h2, wr_ref[...], preferred_element_type=F32, precision=lax.Precision.HIGHEST)
    lane = lax.broadcasted_iota(jnp.int32, logits.shape, 1)
    lanef = lane.astype(F32)
    big = float(LANES)
    lg = jnp.where((lane >= N_EXPERTS) & (lane < N_EXPERTS + N_GROUPS), logits, NEG_INF)
    mg = jnp.max(lg, axis=1, keepdims=True)
    grp = jnp.min(jnp.where(lg == mg, lanef, big), axis=1, keepdims=True) - float(N_EXPERTS)
    g_w = 1.0 / jnp.sum(jnp.exp(lg - mg), axis=1, keepdims=True)
    in_grp = (lane < N_EXPERTS) & ((lane // E_PER_GROUP).astype(F32) == grp)
    le = jnp.where(in_grp, logits, NEG_INF)
    l1 = jnp.max(le, axis=1, keepdims=True)
    i1 = jnp.min(jnp.where(le == l1, lanef, big), axis=1, keepdims=True)
    le2 = jnp.where(lanef == i1, NEG_INF, le)
    l2 = jnp.max(le2, axis=1, keepdims=True)
    i2 = jnp.min(jnp.where(le2 == l2, lanef, big), axis=1, keepdims=True)
    e2 = jnp.exp(l2 - l1)
    w1 = g_w / (1.0 + e2)
    w2 = g_w * e2 / (1.0 + e2)
    oh1 = jnp.where(lanef == i1, 1.0, 0.0)
    oh2 = jnp.where(lanef == i2, 1.0, 0.0)
    rt = lax.broadcasted_iota(jnp.int32, (TM, TM), 0)
    ct = lax.broadcasted_iota(jnp.int32, (TM, TM), 1)
    before = jnp.where(ct < rt, 1.0, 0.0)
    run = run_scr[...]
    tot1 = jnp.sum(oh1, axis=0, keepdims=True)
    r1 = jnp.sum(oh1 * (run + _dot(before, oh1)), axis=1, keepdims=True)
    r2 = jnp.sum(oh2 * (run + tot1 + _dot(before, oh2)), axis=1, keepdims=True)
    run = run + tot1 + jnp.sum(oh2, axis=0, keepdims=True)
    run_scr[...] = run
    cnt_ref[...] = run
    vals = (i1, i2, w1, w2, r1, r2)
    out = jnp.zeros_like(logits)
    for n, v in enumerate(vals):
        out = jnp.where(lane == n, v, out)
    route_ref[...] = out[:, :8]


def _outproj_call(ya, yb, yc, x, ga, scf, shf, lng, lnb, wo, wr):
    tile_vec = pl.BlockSpec((1, 1, D_MODEL), lambda i: (i, 0, 0))
    row_vec = pl.BlockSpec((1, D_MODEL), lambda i: (0, 0))

    def tok(n):
        return pl.BlockSpec((TM, n), lambda i: (i, 0))

    return pl.pallas_call(
        _outproj_kernel,
        grid=(N_TILES,),
        in_specs=[tok(Y_A_COLS), tok(D_B), tok(D_C), tok(D_MODEL), tile_vec, tile_vec, tile_vec, row_vec, row_vec,
                  pl.BlockSpec((Y_A_COLS + D_B + D_C, D_MODEL), lambda i: (0, 0)),
                  pl.BlockSpec((D_MODEL, LANES), lambda i: (0, 0))],
        out_specs=[tok(D_MODEL), tok(D_MODEL), tok(8), pl.BlockSpec((1, LANES), lambda i: (0, 0))],
        out_shape=[jax.ShapeDtypeStruct((T_ALL, D_MODEL), F32), jax.ShapeDtypeStruct((T_ALL, D_MODEL), F32),
                   jax.ShapeDtypeStruct((T_ALL, 8), F32), jax.ShapeDtypeStruct((1, LANES), F32)],
        scratch_shapes=[pltpu.VMEM((1, LANES), F32)],
        compiler_params=_cparams("arbitrary"),
        name="outproj_router",
    )(ya, yb, yc, x, ga, scf, shf, lng, lnb, wo, wr)


def _dispatch_kernel(pos_ref, h_ref, xs_in, xs_hbm, sem):
    del xs_in
    base = pl.program_id(0) * TM

    def row(t, p):
        return pltpu.make_async_copy(h_ref.at[pl.ds(t, 1), :], xs_hbm.at[pl.ds(p, 1), :], sem)

    def issue(t, c):
        row(t, pos_ref[base + t]).start()
        row(t, pos_ref[T_ALL + base + t]).start()
        return c

    lax.fori_loop(0, TM, issue, 0, unroll=8)
    whole = pltpu.make_async_copy(h_ref, xs_hbm.at[pl.ds(0, TM), :], sem)
    whole.wait()
    whole.wait()


def _dispatch_call(pos2, h2, nblk):
    rows = nblk * MOE_BLK
    grid_spec = pltpu.PrefetchScalarGridSpec(
        num_scalar_prefetch=1,
        grid=(N_TILES,),
        in_specs=[pl.BlockSpec((TM, D_MODEL), lambda i, p: (i, 0)), pl.BlockSpec(memory_space=pl.ANY)],
        out_specs=pl.BlockSpec(memory_space=pl.ANY),
        scratch_shapes=[pltpu.SemaphoreType.DMA(())],
    )
    return pl.pallas_call(
        _dispatch_kernel,
        grid_spec=grid_spec,
        out_shape=jax.ShapeDtypeStruct((rows, D_MODEL), F32),
        input_output_aliases={2: 0},
        compiler_params=_cparams("arbitrary"),
        name="dispatch",
    )(pos2, h2, jnp.zeros((rows, D_MODEL), F32))


def _expert_kernel(blk_e_ref, nused_ref, xs_ref, wg_ref, wu_ref, wd_ref, out_ref):
    j = pl.program_id(0)

    @pl.when(j < nused_ref[0])
    def _():
        xb = xs_ref[...].astype(BF16)
        g = jnp.dot(xb, wg_ref[0].astype(BF16), preferred_element_type=F32)
        u = jnp.dot(xb, wu_ref[0].astype(BF16), preferred_element_type=F32)
        out_ref[...] = _dot(jax.nn.silu(g) * u, wd_ref[0])

    @pl.when(j >= nused_ref[0])
    def _():
        out_ref[...] = jnp.zeros_like(out_ref)


def _expert_call(blk_e, nused, xs, wg, wu, wd, nblk):
    def wspec(r, c):
        return pl.BlockSpec((1, r, c), lambda j, be, nu: (be[j], 0, 0))

    grid_spec = pltpu.PrefetchScalarGridSpec(
        num_scalar_prefetch=2,
        grid=(nblk,),
        in_specs=[
            pl.BlockSpec((MOE_BLK, D_MODEL), lambda j, be, nu: (jnp.minimum(j, nu[0] - 1), 0)),
            wspec(D_MODEL, D_EXPERT), wspec(D_MODEL, D_EXPERT), wspec(D_EXPERT, D_MODEL),
        ],
        out_specs=pl.BlockSpec((MOE_BLK, D_MODEL), lambda j, be, nu: (j, 0)),
    )
    return pl.pallas_call(
        _expert_kernel,
        grid_spec=grid_spec,
        out_shape=jax.ShapeDtypeStruct((nblk * MOE_BLK, D_MODEL), F32),
        compiler_params=_cparams("arbitrary"),
        name="experts",
    )(blk_e, nused, xs, wg, wu, wd)


def _combine_kernel(pos_ref, eo_hbm, x1_ref, route_ref, gf_ref, lng_ref, lnb_ref, x2_ref, buf, sem):
    base = pl.program_id(0) * TM

    def row(p, r, t):
        return pltpu.make_async_copy(eo_hbm.at[pl.ds(p, 1), :], buf.at[r, pl.ds(t, 1), :], sem)

    def issue(t, c):
        row(pos_ref[base + t], 0, t).start()
        row(pos_ref[T_ALL + base + t], 1, t).start()
        return c

    lax.fori_loop(0, TM, issue, 0, unroll=8)
    for r in range(2):
        pltpu.make_async_copy(eo_hbm.at[pl.ds(0, TM), :], buf.at[r], sem).wait()
    route = route_ref[...]
    y = route[:, 2:3] * buf[0] + route[:, 3:4] * buf[1]
    x2_ref[...] = _layer_norm(ALPHA * x1_ref[...] + gf_ref[0] * y, lng_ref[...], lnb_ref[...])


def _combine_call(pos2, eo, x1, route, gf, lng, lnb):
    grid_spec = pltpu.PrefetchScalarGridSpec(
        num_scalar_prefetch=1,
        grid=(N_TILES,),
        in_specs=[
            pl.BlockSpec(memory_space=pl.ANY),
            pl.BlockSpec((TM, D_MODEL), lambda i, p: (i, 0)),
            pl.BlockSpec((TM, 8), lambda i, p: (i, 0)),
            pl.BlockSpec((1, 1, D_MODEL), lambda i, p: (i, 0, 0)),
            pl.BlockSpec((1, D_MODEL), lambda i, p: (0, 0)),
            pl.BlockSpec((1, D_MODEL), lambda i, p: (0, 0)),
        ],
        out_specs=pl.BlockSpec((TM, D_MODEL), lambda i, p: (i, 0)),
        scratch_shapes=[pltpu.VMEM((2, TM, D_MODEL), F32), pltpu.SemaphoreType.DMA(())],
    )
    return pl.pallas_call(
        _combine_kernel,
        grid_spec=grid_spec,
        out_shape=jax.ShapeDtypeStruct((T_ALL, D_MODEL), F32),
        compiler_params=_cparams("arbitrary"),
        name="combine_norm",
    )(pos2, eo, x1, route, gf, lng, lnb)


def _prep_w_in(w_in):
    a = w_in[..., :4 * D_A].reshape(DEPTH, D_MODEL, 4, H_A, DH_A)
    a = jnp.pad(a, ((0, 0), (0, 0), (0, 0), (0, 0), (0, DP_A - DH_A))).reshape(DEPTH, D_MODEL, ZA_COLS)
    g = jnp.pad(w_in[..., 4 * D_A:4 * D_A + 4 * H_A], ((0, 0), (0, 0), (0, LANES - 4 * H_A)))
    rest = w_in[..., 4 * D_A + 4 * H_A:]
    return jnp.concatenate([a, rest, g], -1).astype(BF16)


def _prep_w_out(w_out):
    a = w_out[:, :D_A].reshape(DEPTH, H_A, DH_A, D_MODEL)
    a = jnp.pad(a, ((0, 0), (0, 0), (0, DP_A - DH_A), (0, 0))).reshape(DEPTH, Y_A_COLS, D_MODEL)
    return jnp.concatenate([a, w_out[:, D_A:]], 1).astype(BF16)


def _rope_tables():
    t = np.arange(N_LAT)
    nf = HD // 4
    inv = ROPE_BASE ** (-np.arange(nf, dtype=np.float32) / nf)
    ar = (t // GRID_W).astype(np.float32)[:, None] * inv
    ac = (t % GRID_W).astype(np.float32)[:, None] * inv
    ang = jnp.asarray(np.concatenate([ar, ar, ac, ac], -1), F32)
    cos, sin = jnp.cos(ang), jnp.sin(ang)
    sign = np.where((np.arange(HD) % 32) < 16, -1.0, 1.0).astype(np.float32)
    reps = LANES // HD
    return jnp.tile(cos, (1, reps)), jnp.tile(sin * sign, (1, reps))


def _na_bias_table(rpb_l):
    qcol = np.arange(GRID_W)[:, None]
    kcol = np.arange(GRID_W)[None, :]
    dc = np.clip(kcol - qcol, 1 - NA_WIN_W, NA_WIN_W - 1) + NA_WIN_W - 1
    wstart = np.clip(qcol - NA_WIN_W // 2, 0, GRID_W - NA_WIN_W)
    in_win = (kcol >= wstart) & (kcol < wstart + NA_WIN_W)
    dr = np.arange(NA_WIN_H)[:, None] + np.arange(NA_WIN_H)[None, :]
    tab = rpb_l[:, dr][..., dc]
    tab = jnp.where(in_win[None, None, None], tab, NEG_INF)
    return tab.transpose(0, 1, 3, 2, 4).reshape(H_C, NA_WIN_H, GRID_W, NA_WIN_H * GRID_W)


def _gate_layouts(zg, nb, seq, row0):
    nc = seq // LC_K
    g = zg[row0:row0 + nb * seq, :4 * H_A].reshape(nb, seq, 2, 2, H_A)
    g = g.transpose(0, 4, 1, 2, 3).reshape(nb, H_A, nc, LC_K, 4)
    return g, g.transpose(0, 1, 2, 4, 3)


def _keys_t(za, nb, seq, row0):
    k = za[row0:row0 + nb * seq, H_A * DP_A:2 * H_A * DP_A].reshape(nb, seq // LC_K, LC_K, H_A, DP_A)
    return k.transpose(0, 3, 1, 4, 2)


def _dispatch_plan(route, counts):
    nblk = 2 * T_ALL // MOE_BLK + N_EXPERTS
    cnt = counts[0, :N_EXPERTS].astype(jnp.int32)
    padded = (cnt + MOE_BLK - 1) // MOE_BLK * MOE_BLK
    p_end = jnp.cumsum(padded)
    p_start = p_end - padded
    e = route[:, 0:2].astype(jnp.int32)
    pos = p_start[e] + route[:, 4:6].astype(jnp.int32)
    pos2 = pos.T.reshape(-1)
    blk_first = jnp.arange(nblk, dtype=jnp.int32) * MOE_BLK
    blk_e = jnp.minimum(jnp.sum((p_end[None, :] <= blk_first[:, None]).astype(jnp.int32), axis=1), N_EXPERTS - 1)
    nused = p_end[-1:] // MOE_BLK
    return blk_e, nused, pos2, nblk


def kernel(x_prompt, x_sample, state_a_C, state_a_n, state_a_m, cache_b_k, cache_b_v, cache_c_k, cache_c_v, c, c_ctx, w_in, b_a_i, b_a_f, w_a_hnorm, b_sink, rpb, w_out, w_ada, b_ada, ln_g, ln_b, w_router_grp, w_router_exp, w_e_gate, w_e_up, w_e_down):
    w_in_p = _prep_w_in(w_in)
    w_out_p = _prep_w_out(w_out)
    w_r = jnp.pad(jnp.concatenate([w_router_exp, w_router_grp], -1),
                  ((0, 0), (0, 0), (0, LANES - N_EXPERTS - N_GROUPS)))
    wn_p = jnp.pad(w_a_hnorm.reshape(DEPTH, H_A, 1, DH_A), ((0, 0), (0, 0), (0, 0), (0, DP_A - DH_A)))
    gate_b = jnp.concatenate([b_a_i, b_a_f], 1).transpose(0, 2, 1)
    sink_p = jnp.pad(b_sink, ((0, 0), (0, 8 - H_B))).reshape(DEPTH, 1, 8)
    cos_t, sin_t = _rope_tables()
    cb_k = cache_b_k.reshape(B_LAT, DEPTH, PAST_LEN, D_KVB)
    cb_v = cache_b_v.reshape(B_LAT, DEPTH, PAST_LEN, D_KVB)
    cc_k = cache_c_k.reshape(B_LAT, DEPTH, PAST_LEN, D_C)
    cc_v = cache_c_v.reshape(B_LAT, DEPTH, PAST_LEN, D_C)
    pad_c = ((0, 0), (0, 0), (0, 0), (0, 0), (0, DP_A - DH_A), (0, DP_A - DH_A))
    st_ct = jnp.swapaxes(jnp.pad(state_a_C, pad_c), -1, -2)
    st_nr = jnp.broadcast_to(jnp.pad(state_a_n, pad_c[:-1])[..., None], st_ct.shape)
    st_s = jnp.concatenate([st_ct, st_nr], -1)
    st_m = state_a_m[..., None, None]
    z_s = jnp.zeros((B_CTX, 2, H_A, DP_A, 2 * DP_A), F32)
    z_m = jnp.zeros((B_CTX, 2, H_A, 1, 1), F32)

    cvec = jnp.concatenate([c, c_ctx[None, :], jnp.zeros((3, D_MODEL), F32)], 0)
    mod = _ada_call(cvec, w_ada, b_ada)
    tile_row = np.concatenate([np.full(T_CTX // TM, B_LAT), np.repeat(np.arange(B_LAT), N_LAT // TM)])
    mod_t = mod[:, tile_row].reshape(DEPTH, N_TILES, 1, 6, D_MODEL)

    x = jnp.concatenate([x_prompt.reshape(T_CTX, D_MODEL), x_sample.reshape(T_LAT, D_MODEL)], 0)
    cs_, ns_, ms_, kbs, vbs, kcs, vcs = [], [], [], [], [], [], []
    for l in range(DEPTH):
        sh_a, sc_a, g_a, sh_f, sc_f, g_f = (mod_t[l, :, :, j] for j in range(6))
        za, bq, bk, bv, cq, ck, cv, zg = _inproj_call(x, sc_a, sh_a, w_in_p[l])
        bcol = gate_b[l].reshape(H_A, 1, 4)
        brow = gate_b[l].reshape(H_A, 4, 1)
        gc, gr = _gate_layouts(zg, B_CTX, L_CTX, 0)
        ya_c, s_l, m_l = _mlstm_call(za, _keys_t(za, B_CTX, L_CTX, 0), gc, gr, bcol, brow, wn_p[l], z_s, z_m,
                                     nb=B_CTX, seq=L_CTX, row_blk0=0, hp=H_A)
        yb_c, yc_c = _ctx_attn_call(bq, bk, bv, cq, ck, cv, sink_p[l])
        gc, gr = _gate_layouts(zg, B_LAT, N_LAT, T_CTX)
        ya_l, _, _ = _mlstm_call(za, _keys_t(za, B_LAT, N_LAT, T_CTX), gc, gr, bcol, brow, wn_p[l],
                                 st_s[:, l], st_m[:, l],
                                 nb=B_LAT, seq=N_LAT, row_blk0=T_CTX // N_LAT, hp=H_A // 2)
        yb_l = _win_attn_call(bq, bk, bv, cb_k, cb_v, cos_t, sin_t, sink_p[l], l)
        yc_l = _na_attn_call(cq, ck, cv, cc_k, cc_v, _na_bias_table(rpb[l]), l)
        ya = jnp.concatenate([ya_c, ya_l], 0)
        yb = jnp.concatenate([yb_c, yb_l], 0)
        yc = jnp.concatenate([yc_c, yc_l], 0)
        x1, h2, route, counts = _outproj_call(ya, yb, yc, x, g_a, sc_f, sh_f, ln_g[l, 0:1], ln_b[l, 0:1],
                                              w_out_p[l], w_r[l])
        blk_e, nused, pos2, nblk = _dispatch_plan(route, counts)
        xs = _dispatch_call(pos2, h2, nblk)
        eo = _expert_call(blk_e, nused, xs, w_e_gate[l], w_e_up[l], w_e_down[l], nblk)
        x = _combine_call(pos2, eo, x1, route, g_f, ln_g[l, 1:2], ln_b[l, 1:2])
        cs_.append(jnp.swapaxes(s_l[..., :DH_A, :DH_A], -1, -2))
        ns_.append(s_l[..., :DH_A, DP_A])
        ms_.append(m_l.reshape(B_CTX, 2, H_A))
        kbs.append(bk[:T_CTX].reshape(B_CTX, L_CTX, KV_B, HD))
        vbs.append(bv[:T_CTX].reshape(B_CTX, L_CTX, KV_B, HD))
        kcs.append(ck[:T_CTX].reshape(B_CTX, L_CTX, H_C, HD))
        vcs.append(cv[:T_CTX].reshape(B_CTX, L_CTX, H_C, HD))
    y_prompt = x[:T_CTX].reshape(B_CTX, L_CTX, D_MODEL)
    y_sample = x[T_CTX:].reshape(B_LAT, N_LAT, D_MODEL)
    return (y_prompt, y_sample, jnp.stack(cs_, 1), jnp.stack(ns_, 1), jnp.stack(ms_, 1),
            jnp.stack(kbs, 1), jnp.stack(vbs, 1), jnp.stack(kcs, 1), jnp.stack(vcs, 1))
```

```python
import functools

import numpy as np
import jax
import jax.numpy as jnp
from jax import lax
from jax.experimental import pallas as pl
from jax.experimental.pallas import tpu as pltpu

F32 = jnp.float32
BF16 = jnp.bfloat16
NEG_INF = float("-inf")

D_MODEL = 1024
DEPTH = 4
B_CTX, L_CTX = 16, 256
B_LAT, N_LAT = 4, 2048
PAST_LEN = 512
GRID_W = 64
H_A, DH_A = 4, 96
D_A = H_A * DH_A
H_B, KV_B, HD = 6, 2, 64
G_B = H_B // KV_B
D_B, D_KVB = H_B * HD, KV_B * HD
WIN = 128
ROPE_BASE = 10000.0
H_C = 4
D_C = H_C * HD
NA_WIN_H, NA_WIN_W = 8, 16
N_GROUPS, E_PER_GROUP = 4, 8
N_EXPERTS = N_GROUPS * E_PER_GROUP
D_EXPERT = D_MODEL // 4
ALPHA = (2 * DEPTH) ** 0.25
LN_EPS = 1e-5

LANES = 128
DP_A = LANES
LC_K = LANES
TM = 256
MOE_BLK = 256
VMEM_LIMIT = 48 * 1024 * 1024

T_CTX = B_CTX * L_CTX
T_LAT = B_LAT * N_LAT
T_ALL = T_CTX + T_LAT
N_TILES = T_ALL // TM
ZA_COLS = 4 * H_A * DP_A
Y_A_COLS = H_A * DP_A
IN_SPLITS = (ZA_COLS, D_B, D_KVB, D_KVB, D_C, D_C, D_C, LANES)
IN_COLS = sum(IN_SPLITS)


def _cparams(*sem):
    return pltpu.CompilerParams(dimension_semantics=sem, vmem_limit_bytes=VMEM_LIMIT)


def _dot(a, b):
    return jnp.dot(a.astype(BF16), b.astype(BF16), preferred_element_type=F32)


def _dot_nt(a, b):
    return lax.dot_general(a.astype(BF16), b.astype(BF16), (((1,), (1,)), ((), ())), preferred_element_type=F32)


def _dot_tn(a, b):
    return lax.dot_general(a.astype(BF16), b.astype(BF16), (((0,), (0,)), ((), ())), preferred_element_type=F32)


def _layer_norm(v, g, b):
    mu = jnp.mean(v, -1, keepdims=True)
    var = jnp.mean(jnp.square(v - mu), -1, keepdims=True)
    return (v - mu) * lax.rsqrt(var + LN_EPS) * g + b


def _ada_kernel(c_ref, w_ref, b_ref, o_ref):
    s = jax.nn.silu(c_ref[...])
    o_ref[0] = _dot(s, w_ref[0]) + b_ref[0]


def _ada_call(cvec, w_ada, b_ada):
    nb = 6
    return pl.pallas_call(
        _ada_kernel,
        grid=(DEPTH, nb),
        in_specs=[
            pl.BlockSpec((8, D_MODEL), lambda l, j: (0, 0)),
            pl.BlockSpec((1, D_MODEL, D_MODEL), lambda l, j: (l, 0, j)),
            pl.BlockSpec((1, 1, D_MODEL), lambda l, j: (l, 0, j)),
        ],
        out_specs=pl.BlockSpec((1, 8, D_MODEL), lambda l, j: (l, 0, j)),
        out_shape=jax.ShapeDtypeStruct((DEPTH, 8, 6 * D_MODEL), F32),
        compiler_params=_cparams("arbitrary", "arbitrary"),
        name="adaln",
    )(cvec, w_ada, b_ada.reshape(DEPTH, 1, 6 * D_MODEL))


def _inproj_kernel(x_ref, sc_ref, sh_ref, w_ref, *out_refs):
    h = (x_ref[...] * (1.0 + sc_ref[0]) + sh_ref[0]).astype(BF16)
    off = 0
    for ref in out_refs:
        n = ref.shape[-1]
        ref[...] = jnp.dot(h, w_ref[:, off:off + n], preferred_element_type=F32)
        off += n


def _inproj_call(x, sc, sh, w):
    tile_vec = pl.BlockSpec((1, 1, D_MODEL), lambda i: (i, 0, 0))
    return pl.pallas_call(
        _inproj_kernel,
        grid=(N_TILES,),
        in_specs=[
            pl.BlockSpec((TM, D_MODEL), lambda i: (i, 0)),
            tile_vec, tile_vec,
            pl.BlockSpec((D_MODEL, IN_COLS), lambda i: (0, 0)),
        ],
        out_specs=[pl.BlockSpec((TM, n), lambda i: (i, 0)) for n in IN_SPLITS],
        out_shape=[jax.ShapeDtypeStruct((T_ALL, n), F32) for n in IN_SPLITS],
        compiler_params=_cparams("arbitrary"),
        name="inproj",
    )(x, sc, sh, w)


def _mlstm_kernel(q_ref, kt_ref, v_ref, o_ref, gc_ref, gr_ref, bc_ref, br_ref, wn_ref, s0_ref, m0_ref,
                  y_ref, so_ref, mo_ref, hf_scr, hb_scr, *, nc, hp):
    lc = LC_K
    scale = DH_A ** -0.5
    ti = lax.broadcasted_iota(jnp.int32, (lc, lc), 0)
    si = lax.broadcasted_iota(jnp.int32, (lc, lc), 1)
    lane = lax.broadcasted_iota(jnp.int32, (1, DP_A), 1)
    lane_ok = lane < DH_A
    ones = jnp.ones((lc, DP_A), F32)

    def chunk(c, carry, d, j):
        smat, m = carry
        r0 = pl.multiple_of(c * lc, lc)
        cols = slice(j * DP_A, (j + 1) * DP_A)
        qc = q_ref[pl.ds(r0, lc), cols].astype(BF16)
        kt = (kt_ref[0, j, c] * scale).astype(BF16)
        v1 = jnp.concatenate([v_ref[pl.ds(r0, lc), cols], ones], axis=1)
        gcol = gc_ref[0, j, c] + bc_ref[j]
        grow = gr_ref[0, j, c] + br_ref[j]
        i_col = gcol[:, d:d + 1]
        f_col = jax.nn.log_sigmoid(gcol[:, 2 + d:3 + d])
        i_row = grow[d:d + 1, :]
        f_row = jax.nn.log_sigmoid(grow[2 + d:3 + d, :])
        if d == 0:
            mask, mask_t = si <= ti, si >= ti
        else:
            mask, mask_t = si >= ti, si <= ti
        b_col = jnp.sum(jnp.where(mask, f_row, 0.0), axis=1, keepdims=True)
        b_row = jnp.sum(jnp.where(mask_t, f_col, 0.0), axis=0, keepdims=True)
        dmat = jnp.where(mask, b_col + (i_row - b_row), NEG_INF)
        inter = b_col + m
        m_out = jnp.maximum(inter, jnp.max(dmat, axis=1, keepdims=True))
        wmat = jnp.exp(dmat - m_out)
        sw = jnp.dot(qc, kt, preferred_element_type=F32) * wmat
        sc_in = jnp.exp(inter - m_out)
        tot = _dot(sw, v1) + sc_in * _dot(qc, smat)
        h = tot[:, :DP_A] / jnp.maximum(jnp.abs(tot[:, DP_A:]), jnp.exp(-m_out))
        bl = jnp.sum(f_row, axis=1, keepdims=True)
        dec = bl - b_col + i_col
        m_new = jnp.maximum(bl + m, jnp.max(dec, axis=0, keepdims=True))
        wk = jnp.exp(dec - m_new)
        sc_st = jnp.exp(bl + m - m_new)
        s_new = sc_st * smat + _dot(kt, wk * v1)
        return r0, cols, h, (s_new, m_new)

    def body(i, carry):
        out = []
        for j in range(hp):
            r0, cols, h, st = chunk(i, carry[2 * j], 0, j)
            hf_scr[pl.ds(r0, lc), cols] = h
            out.append(st)
            r0, cols, h, st = chunk(nc - 1 - i, carry[2 * j + 1], 1, j)
            hb_scr[pl.ds(r0, lc), cols] = h
            out.append(st)
        return tuple(out)

    init = tuple((s0_ref[0, d, j], m0_ref[0, d, j]) for j in range(hp) for d in range(2))
    final = lax.fori_loop(0, nc, body, init)
    for j in range(hp):
        for d in range(2):
            smat, m = final[2 * j + d]
            so_ref[0, d, j] = smat
            mo_ref[0, d, j] = m

    def finish(c, _):
        r0 = pl.multiple_of(c * lc, lc)
        for j in range(hp):
            cols = slice(j * DP_A, (j + 1) * DP_A)
            h = hf_scr[pl.ds(r0, lc), cols] + hb_scr[pl.ds(r0, lc), cols]
            mu = jnp.sum(h, axis=1, keepdims=True) * (1.0 / DH_A)
            dv = jnp.where(lane_ok, h - mu, 0.0)
            var = jnp.sum(dv * dv, axis=1, keepdims=True) * (1.0 / DH_A)
            hn = dv * lax.rsqrt(var + LN_EPS) * wn_ref[j]
            y_ref[pl.ds(r0, lc), cols] = jax.nn.sigmoid(o_ref[pl.ds(r0, lc), cols]) * hn
        return 0

    lax.fori_loop(0, nc, finish, 0)


def _mlstm_call(za, kt, gcol, grow, bcol, brow, wn, s0, m0, *, nb, seq, row_blk0, hp):
    nc = seq // LC_K
    ng = H_A // hp
    w = hp * DP_A

    def zspec(part):
        return pl.BlockSpec((seq, w), lambda b, g: (row_blk0 + b, part * ng + g))

    def head_vec(*tail):
        return pl.BlockSpec((hp,) + tail, lambda b, g: (g,) + (0,) * len(tail))

    def state(*tail):
        return pl.BlockSpec((1, 2, hp) + tail, lambda b, g: (b, 0, g) + (0,) * len(tail))

    def per_chunk(r, c):
        return pl.BlockSpec((1, hp, nc, r, c), lambda b, g: (b, g, 0, 0, 0))

    return pl.pallas_call(
        functools.partial(_mlstm_kernel, nc=nc, hp=hp),
        grid=(nb, ng),
        in_specs=[
            zspec(0), per_chunk(DP_A, LC_K), zspec(2), zspec(3), per_chunk(LC_K, 4), per_chunk(4, LC_K),
            head_vec(1, 4), head_vec(4, 1), head_vec(1, DP_A),
            state(DP_A, 2 * DP_A), state(1, 1),
        ],
        out_specs=[pl.BlockSpec((seq, w), lambda b, g: (b, g)), state(DP_A, 2 * DP_A), state(1, 1)],
        out_shape=[
            jax.ShapeDtypeStruct((nb * seq, Y_A_COLS), F32),
            jax.ShapeDtypeStruct((nb, 2, H_A, DP_A, 2 * DP_A), F32),
            jax.ShapeDtypeStruct((nb, 2, H_A, 1, 1), F32),
        ],
        scratch_shapes=[pltpu.VMEM((seq, w), F32), pltpu.VMEM((seq, w), F32)],
        compiler_params=_cparams("arbitrary", "arbitrary"),
        name="mlstm",
    )(za, kt, za, za, gcol, grow, bcol, brow, wn, s0, m0)


def _softmax_pv(s_list, v_list, sink):
    m = s_list[0].max(axis=1, keepdims=True)
    for s in s_list[1:]:
        m = jnp.maximum(m, s.max(axis=1, keepdims=True))
    if sink is not None:
        m = jnp.maximum(m, sink)
    den = None
    acc = None
    for s, v in zip(s_list, v_list):
        e = jnp.exp(s - m)
        t = jnp.sum(e, axis=1, keepdims=True)
        den = t if den is None else den + t
        pv = _dot(e, v)
        acc = pv if acc is None else acc + pv
    if sink is not None:
        den = den + jnp.exp(sink - m)
    return acc / den


def _ctx_attn_kernel(bq_ref, bk_ref, bv_ref, cq_ref, ck_ref, cv_ref, sink_ref, yb_ref, yc_ref):
    scale = HD ** -0.5
    for g in range(KV_B):
        kg = bk_ref[:, g * HD:(g + 1) * HD]
        vg = bv_ref[:, g * HD:(g + 1) * HD]
        for j in range(G_B):
            hq = g * G_B + j
            qh = bq_ref[:, hq * HD:(hq + 1) * HD]
            s = _dot_nt(qh, kg) * scale
            yb_ref[:, hq * HD:(hq + 1) * HD] = _softmax_pv([s], [vg], sink_ref[0:1, hq:hq + 1])
    for h in range(H_C):
        sl = slice(h * HD, (h + 1) * HD)
        s = _dot_nt(cq_ref[:, sl], ck_ref[:, sl]) * scale
        yc_ref[:, sl] = _softmax_pv([s], [cv_ref[:, sl]], None)


def _ctx_attn_call(bq, bk, bv, cq, ck, cv, sink):
    def spec(n):
        return pl.BlockSpec((L_CTX, n), lambda b: (b, 0))

    return pl.pallas_call(
        _ctx_attn_kernel,
        grid=(B_CTX,),
        in_specs=[spec(D_B), spec(D_KVB), spec(D_KVB), spec(D_C), spec(D_C), spec(D_C),
                  pl.BlockSpec((1, 8), lambda b: (0, 0))],
        out_specs=[spec(D_B), spec(D_C)],
        out_shape=[jax.ShapeDtypeStruct((T_CTX, D_B), F32), jax.ShapeDtypeStruct((T_CTX, D_C), F32)],
        compiler_params=_cparams("arbitrary"),
        name="ctx_attn",
    )(bq, bk, bv, cq, ck, cv, sink)


def _rope(x, cos, sin_signed, first):
    rot = jnp.where(first, pltpu.roll(x, LANES - 16, 1), pltpu.roll(x, 16, 1))
    return x * cos + rot * sin_signed


def _win_attn_kernel(q_ref, k_ref, v_ref, kc_ref, vc_ref, cos_ref, sin_ref, sink_ref, y_ref, kr_scr):
    scale = HD ** -0.5
    qb_rows = WIN
    band = 3 * WIN
    nblk = N_LAT // qb_rows
    lane = lax.broadcasted_iota(jnp.int32, (1, LANES), 1)
    first = (lane % 32) < 16
    kr_scr[...] = _rope(k_ref[...], cos_ref[...], sin_ref[...], first)
    kctx = kc_ref[0, 0]
    vctx = vc_ref[0, 0]

    def body(blk, _):
        q0 = pl.multiple_of(blk * qb_rows, qb_rows)
        start = pl.multiple_of(jnp.clip(q0 - WIN, 0, N_LAT - band), WIN)
        cos = cos_ref[pl.ds(q0, qb_rows), :]
        sin = sin_ref[pl.ds(q0, qb_rows), :]
        kband = kr_scr[pl.ds(start, band), :]
        vband = v_ref[pl.ds(start, band), :]
        qpos = q0 + lax.broadcasted_iota(jnp.int32, (qb_rows, band), 0)
        kpos = start + lax.broadcasted_iota(jnp.int32, (qb_rows, band), 1)
        mask = jnp.abs(kpos - qpos) <= WIN
        for p in range(D_B // LANES):
            qp = _rope(q_ref[pl.ds(q0, qb_rows), p * LANES:(p + 1) * LANES], cos, sin, first)
            for u in range(LANES // HD):
                hq = p * (LANES // HD) + u
                g = hq // G_B
                qh = qp[:, u * HD:(u + 1) * HD]
                sl = slice(g * HD, (g + 1) * HD)
                s_loc = jnp.where(mask, _dot_nt(qh, kband[:, sl]) * scale, NEG_INF)
                s_ctx = _dot_nt(qh, kctx[:, sl]) * scale
                o = _softmax_pv([s_ctx, s_loc], [vctx[:, sl], vband[:, sl]], sink_ref[0:1, hq:hq + 1])
                y_ref[pl.ds(q0, qb_rows), hq * HD:(hq + 1) * HD] = o
        return 0

    lax.fori_loop(0, nblk, body, 0)


def _win_attn_call(bq, bk, bv, cache_k, cache_v, cos, sin, sink, layer):
    rb0 = T_CTX // N_LAT

    def spec(n):
        return pl.BlockSpec((N_LAT, n), lambda b: (rb0 + b, 0))

    cache = pl.BlockSpec((1, 1, PAST_LEN, D_KVB), lambda b: (b, layer, 0, 0))
    tab = pl.BlockSpec((N_LAT, LANES), lambda b: (0, 0))
    return pl.pallas_call(
        _win_attn_kernel,
        grid=(B_LAT,),
        in_specs=[spec(D_B), spec(D_KVB), spec(D_KVB), cache, cache, tab, tab,
                  pl.BlockSpec((1, 8), lambda b: (0, 0))],
        out_specs=pl.BlockSpec((N_LAT, D_B), lambda b: (b, 0)),
        out_shape=jax.ShapeDtypeStruct((T_LAT, D_B), F32),
        scratch_shapes=[pltpu.VMEM((N_LAT, D_KVB), F32)],
        compiler_params=_cparams("arbitrary"),
        name="win_attn",
    )(bq, bk, bv, cache_k, cache_v, cos, sin, sink)


def _na_attn_kernel(q_ref, k_ref, v_ref, kc_ref, vc_ref, bias_ref, y_ref):
    scale = HD ** -0.5
    rows = N_LAT // GRID_W
    nkeys = NA_WIN_H * GRID_W
    kctx = kc_ref[0, 0]
    vctx = vc_ref[0, 0]

    def body(r, _):
        kr0 = jnp.clip(r - NA_WIN_H // 2, 0, rows - NA_WIN_H)
        var = kr0 - r + NA_WIN_H - 1
        q0 = pl.multiple_of(r * GRID_W, GRID_W)
        k0 = pl.multiple_of(kr0 * GRID_W, GRID_W)
        for h in range(H_C):
            sl = slice(h * HD, (h + 1) * HD)
            qh = q_ref[pl.ds(q0, GRID_W), sl]
            kb = k_ref[pl.ds(k0, nkeys), sl]
            vb = v_ref[pl.ds(k0, nkeys), sl]
            s_loc = _dot_nt(qh, kb) * scale + bias_ref[h, var]
            s_ctx = _dot_nt(qh, kctx[:, sl]) * scale
            y_ref[pl.ds(q0, GRID_W), sl] = _softmax_pv([s_ctx, s_loc], [vctx[:, sl], vb], None)
        return 0

    lax.fori_loop(0, rows, body, 0)


def _na_attn_call(cq, ck, cv, cache_k, cache_v, bias_tab, layer):
    rb0 = T_CTX // N_LAT
    spec = pl.BlockSpec((N_LAT, D_C), lambda b: (rb0 + b, 0))
    cache = pl.BlockSpec((1, 1, PAST_LEN, D_C), lambda b: (b, layer, 0, 0))
    return pl.pallas_call(
        _na_attn_kernel,
        grid=(B_LAT,),
        in_specs=[spec, spec, spec, cache, cache,
                  pl.BlockSpec((H_C, NA_WIN_H, GRID_W, NA_WIN_H * GRID_W), lambda b: (0, 0, 0, 0))],
        out_specs=pl.BlockSpec((N_LAT, D_C), lambda b: (b, 0)),
        out_shape=jax.ShapeDtypeStruct((T_LAT, D_C), F32),
        compiler_params=_cparams("arbitrary"),
        name="na_attn",
    )(cq, ck, cv, cache_k, cache_v, bias_tab)


def _outproj_kernel(yac_ref, ybc_ref, ycc_ref, yal_ref, ybl_ref, ycl_ref, x_ref, ga_ref, scf_ref, shf_ref,
                    lng_ref, lnb_ref, wo_ref, wr_ref, x1_ref, h2_ref, route_ref, cnt_ref, run_scr):
    @pl.when(pl.program_id(0) == 0)
    def _():
        run_scr[...] = jnp.zeros_like(run_scr)

    is_ctx = pl.program_id(0) < T_CTX // TM

    def pick(c_ref, l_ref):
        return jnp.where(is_ctx, c_ref[...], l_ref[...])

    y = (_dot(pick(yac_ref, yal_ref), wo_ref[0:Y_A_COLS, :])
         + _dot(pick(ybc_ref, ybl_ref), wo_ref[Y_A_COLS:Y_A_COLS + D_B, :])
         + _dot(pick(ycc_ref, ycl_ref), wo_ref[Y_A_COLS + D_B:, :]))
    x1 = _layer_norm(ALPHA * x_ref[...] + ga_ref[0] * y, lng_ref[...], lnb_ref[...])
    x1_ref[...] = x1
    h2 = x1 * (1.0 + scf_ref[0]) + shf_ref[0]
    h2_ref[...] = h2
    w_r = wr_ref[...]
    w_hi = w_r.astype(BF16)
    w_lo = (w_r - w_hi.astype(F32)).astype(BF16)
    h_hi = h2.astype(BF16)
    h_lo = (h2 - h_hi.astype(F32)).astype(BF16)
    p_hi = jnp.dot(h_hi, jnp.concatenate([w_hi, w_lo], axis=1), preferred_element_type=F32)
    logits = p_hi[:, :LANES] + p_hi[:, LANES:] + jnp.dot(h_lo, w_hi, preferred_element_type=F32)
    lane = lax.broadcasted_iota(jnp.int32, logits.shape, 1)
    lanef = lane.astype(F32)
    big = float(LANES)
    lg = jnp.where((lane >= N_EXPERTS) & (lane < N_EXPERTS + N_GROUPS), logits, NEG_INF)
    mg = jnp.max(lg, axis=1, keepdims=True)
    grp = jnp.min(jnp.where(lg == mg, lanef, big), axis=1, keepdims=True) - float(N_EXPERTS)
    g_w = 1.0 / jnp.sum(jnp.exp(lg - mg), axis=1, keepdims=True)
    in_grp = (lane < N_EXPERTS) & ((lane // E_PER_GROUP).astype(F32) == grp)
    le = jnp.where(in_grp, logits, NEG_INF)
    l1 = jnp.max(le, axis=1, keepdims=True)
    i1 = jnp.min(jnp.where(le == l1, lanef, big), axis=1, keepdims=True)
    le2 = jnp.where(lanef == i1, NEG_INF, le)
    l2 = jnp.max(le2, axis=1, keepdims=True)
    i2 = jnp.min(jnp.where(le2 == l2, lanef, big), axis=1, keepdims=True)
    e2 = jnp.exp(l2 - l1)
    w1 = g_w / (1.0 + e2)
    w2 = g_w * e2 / (1.0 + e2)
    oh1 = jnp.where(lanef == i1, 1.0, 0.0)
    oh2 = jnp.where(lanef == i2, 1.0, 0.0)
    rt = lax.broadcasted_iota(jnp.int32, (TM, TM), 0)
    ct = lax.broadcasted_iota(jnp.int32, (TM, TM), 1)
    before = jnp.where(ct < rt, 1.0, 0.0)
    run = run_scr[...]
    tot1 = jnp.sum(oh1, axis=0, keepdims=True)
    r1 = jnp.sum(oh1 * (run + _dot(before, oh1)), axis=1, keepdims=True)
    r2 = jnp.sum(oh2 * (run + tot1 + _dot(before, oh2)), axis=1, keepdims=True)
    run = run + tot1 + jnp.sum(oh2, axis=0, keepdims=True)
    run_scr[...] = run
    cnt_ref[...] = run
    vals = (i1, i2, w1, w2, r1, r2)
    out = jnp.zeros_like(logits)
    for n, v in enumerate(vals):
        out = jnp.where(lane == n, v, out)
    route_ref[...] = out[:, :8]


def _outproj_call(y_ctx, y_lat, x, ga, scf, shf, lng, lnb, wo, wr):
    tile_vec = pl.BlockSpec((1, 1, D_MODEL), lambda i: (i, 0, 0))
    row_vec = pl.BlockSpec((1, D_MODEL), lambda i: (0, 0))
    n_ctx = T_CTX // TM

    def tok(n):
        return pl.BlockSpec((TM, n), lambda i: (i, 0))

    def tok_ctx(n):
        return pl.BlockSpec((TM, n), lambda i: (jnp.minimum(i, n_ctx - 1), 0))

    def tok_lat(n):
        return pl.BlockSpec((TM, n), lambda i: (jnp.maximum(i - n_ctx, 0), 0))

    return pl.pallas_call(
        _outproj_kernel,
        grid=(N_TILES,),
        in_specs=[tok_ctx(Y_A_COLS), tok_ctx(D_B), tok_ctx(D_C), tok_lat(Y_A_COLS), tok_lat(D_B), tok_lat(D_C),
                  tok(D_MODEL), tile_vec, tile_vec, tile_vec, row_vec, row_vec,
                  pl.BlockSpec((Y_A_COLS + D_B + D_C, D_MODEL), lambda i: (0, 0)),
                  pl.BlockSpec((D_MODEL, LANES), lambda i: (0, 0))],
        out_specs=[tok(D_MODEL), tok(D_MODEL), tok(8), pl.BlockSpec((1, LANES), lambda i: (0, 0))],
        out_shape=[jax.ShapeDtypeStruct((T_ALL, D_MODEL), F32), jax.ShapeDtypeStruct((T_ALL, D_MODEL), F32),
                   jax.ShapeDtypeStruct((T_ALL, 8), F32), jax.ShapeDtypeStruct((1, LANES), F32)],
        scratch_shapes=[pltpu.VMEM((1, LANES), F32)],
        compiler_params=_cparams("arbitrary"),
        name="outproj_router",
    )(*y_ctx, *y_lat, x, ga, scf, shf, lng, lnb, wo, wr)


def _dispatch_kernel(pos_ref, pend_ref, h_ref, xs_hbm, zero_scr, sem):
    base = pl.program_id(0) * TM

    @pl.when(pl.program_id(0) == 0)
    def _():
        zero_scr[...] = jnp.zeros_like(zero_scr)

        def fill(e, op):
            prev = pend_ref[e - 1] if e else 0

            @pl.when(pend_ref[e] > prev)
            def _():
                first = pl.multiple_of(pend_ref[e] - MOE_BLK, MOE_BLK)
                op(pltpu.make_async_copy(zero_scr, xs_hbm.at[pl.ds(first, MOE_BLK), :], sem))

        def tail_copy(b):
            first = pl.multiple_of(b * MOE_BLK, MOE_BLK)
            return pltpu.make_async_copy(zero_scr, xs_hbm.at[pl.ds(first, MOE_BLK), :], sem)

        def tail_start(b, c):
            tail_copy(b).start()
            return c

        def tail_wait(b, c):
            tail_copy(b).wait()
            return c

        n_blocks = xs_hbm.shape[0] // MOE_BLK
        first_free = pend_ref[N_EXPERTS - 1] // MOE_BLK
        for e in range(N_EXPERTS):
            fill(e, lambda cp: cp.start())
        lax.fori_loop(first_free, n_blocks, tail_start, 0)
        for e in range(N_EXPERTS):
            fill(e, lambda cp: cp.wait())
        lax.fori_loop(first_free, n_blocks, tail_wait, 0)

    def row(t, p):
        return pltpu.make_async_copy(h_ref.at[pl.ds(t, 1), :], xs_hbm.at[pl.ds(p, 1), :], sem)

    def issue(t, c):
        row(t, pos_ref[base + t]).start()
        row(t, pos_ref[T_ALL + base + t]).start()
        return c

    lax.fori_loop(0, TM, issue, 0, unroll=8)
    whole = pltpu.make_async_copy(h_ref, xs_hbm.at[pl.ds(0, TM), :], sem)
    whole.wait()
    whole.wait()


def _dispatch_call(pos2, p_end, h2, nblk):
    grid_spec = pltpu.PrefetchScalarGridSpec(
        num_scalar_prefetch=2,
        grid=(N_TILES,),
        in_specs=[pl.BlockSpec((TM, D_MODEL), lambda i, p, pe: (i, 0))],
        out_specs=pl.BlockSpec(memory_space=pl.ANY),
        scratch_shapes=[pltpu.VMEM((MOE_BLK, D_MODEL), F32), pltpu.SemaphoreType.DMA(())],
    )
    return pl.pallas_call(
        _dispatch_kernel,
        grid_spec=grid_spec,
        out_shape=jax.ShapeDtypeStruct((nblk * MOE_BLK, D_MODEL), F32),
        compiler_params=_cparams("arbitrary"),
        name="dispatch",
    )(pos2, p_end, h2)


def _expert_kernel(blk_e_ref, nused_ref, xs_ref, wg_ref, wu_ref, wd_ref, out_ref):
    j = pl.program_id(0)

    @pl.when(j < nused_ref[0])
    def _():
        xb = xs_ref[...].astype(BF16)
        g = jnp.dot(xb, wg_ref[0].astype(BF16), preferred_element_type=F32)
        u = jnp.dot(xb, wu_ref[0].astype(BF16), preferred_element_type=F32)
        out_ref[...] = _dot(jax.nn.silu(g) * u, wd_ref[0])

    @pl.when(j >= nused_ref[0])
    def _():
        out_ref[...] = jnp.zeros_like(out_ref)


def _expert_call(blk_e, nused, xs, wg, wu, wd, nblk):
    def wspec(r, c):
        return pl.BlockSpec((1, r, c), lambda j, be, nu: (be[j], 0, 0))

    grid_spec = pltpu.PrefetchScalarGridSpec(
        num_scalar_prefetch=2,
        grid=(nblk,),
        in_specs=[
            pl.BlockSpec((MOE_BLK, D_MODEL), lambda j, be, nu: (jnp.clip(j, 0, jnp.maximum(nu[0] - 1, 0)), 0)),
            wspec(D_MODEL, D_EXPERT), wspec(D_MODEL, D_EXPERT), wspec(D_EXPERT, D_MODEL),
        ],
        out_specs=pl.BlockSpec((MOE_BLK, D_MODEL), lambda j, be, nu: (j, 0)),
    )
    return pl.pallas_call(
        _expert_kernel,
        grid_spec=grid_spec,
        out_shape=jax.ShapeDtypeStruct((nblk * MOE_BLK, D_MODEL), F32),
        compiler_params=_cparams("arbitrary"),
        name="experts",
    )(blk_e, nused, xs, wg, wu, wd)


def _combine_kernel(pos_ref, eo_hbm, x1_ref, route_ref, gf_ref, lng_ref, lnb_ref, x2_ref, buf, sem):
    base = pl.program_id(0) * TM

    def row(p, r, t):
        return pltpu.make_async_copy(eo_hbm.at[pl.ds(p, 1), :], buf.at[r, pl.ds(t, 1), :], sem)

    def issue(t, c):
        row(pos_ref[base + t], 0, t).start()
        row(pos_ref[T_ALL + base + t], 1, t).start()
        return c

    lax.fori_loop(0, TM, issue, 0, unroll=8)
    for r in range(2):
        pltpu.make_async_copy(eo_hbm.at[pl.ds(0, TM), :], buf.at[r], sem).wait()
    route = route_ref[...]
    y = route[:, 2:3] * buf[0] + route[:, 3:4] * buf[1]
    x2_ref[...] = _layer_norm(ALPHA * x1_ref[...] + gf_ref[0] * y, lng_ref[...], lnb_ref[...])


def _combine_call(pos2, eo, x1, route, gf, lng, lnb):
    grid_spec = pltpu.PrefetchScalarGridSpec(
        num_scalar_prefetch=1,
        grid=(N_TILES,),
        in_specs=[
            pl.BlockSpec(memory_space=pl.ANY),
            pl.BlockSpec((TM, D_MODEL), lambda i, p: (i, 0)),
            pl.BlockSpec((TM, 8), lambda i, p: (i, 0)),
            pl.BlockSpec((1, 1, D_MODEL), lambda i, p: (i, 0, 0)),
            pl.BlockSpec((1, D_MODEL), lambda i, p: (0, 0)),
            pl.BlockSpec((1, D_MODEL), lambda i, p: (0, 0)),
        ],
        out_specs=pl.BlockSpec((TM, D_MODEL), lambda i, p: (i, 0)),
        scratch_shapes=[pltpu.VMEM((2, TM, D_MODEL), F32), pltpu.SemaphoreType.DMA(())],
    )
    return pl.pallas_call(
        _combine_kernel,
        grid_spec=grid_spec,
        out_shape=jax.ShapeDtypeStruct((T_ALL, D_MODEL), F32),
        compiler_params=_cparams("arbitrary"),
        name="combine_norm",
    )(pos2, eo, x1, route, gf, lng, lnb)


def _prep_w_in(w_in):
    a = w_in[..., :4 * D_A].reshape(DEPTH, D_MODEL, 4, H_A, DH_A)
    a = jnp.pad(a, ((0, 0), (0, 0), (0, 0), (0, 0), (0, DP_A - DH_A))).reshape(DEPTH, D_MODEL, ZA_COLS)
    g = jnp.pad(w_in[..., 4 * D_A:4 * D_A + 4 * H_A], ((0, 0), (0, 0), (0, LANES - 4 * H_A)))
    rest = w_in[..., 4 * D_A + 4 * H_A:]
    return jnp.concatenate([a, rest, g], -1).astype(BF16)


def _prep_w_out(w_out):
    a = w_out[:, :D_A].reshape(DEPTH, H_A, DH_A, D_MODEL)
    a = jnp.pad(a, ((0, 0), (0, 0), (0, DP_A - DH_A), (0, 0))).reshape(DEPTH, Y_A_COLS, D_MODEL)
    return jnp.concatenate([a, w_out[:, D_A:]], 1).astype(BF16)


def _rope_tables():
    t = np.arange(N_LAT)
    nf = HD // 4
    inv = ROPE_BASE ** (-np.arange(nf, dtype=np.float32) / nf)
    ar = (t // GRID_W).astype(np.float32)[:, None] * inv
    ac = (t % GRID_W).astype(np.float32)[:, None] * inv
    ang = jnp.asarray(np.concatenate([ar, ar, ac, ac], -1), F32)
    cos, sin = jnp.cos(ang), jnp.sin(ang)
    sign = np.where((np.arange(HD) % 32) < 16, -1.0, 1.0).astype(np.float32)
    reps = LANES // HD
    return jnp.tile(cos, (1, reps)), jnp.tile(sin * sign, (1, reps))


def _na_bias_tables(rpb):
    qcol = np.arange(GRID_W)[:, None]
    kcol = np.arange(GRID_W)[None, :]
    dc = np.clip(kcol - qcol, 1 - NA_WIN_W, NA_WIN_W - 1) + NA_WIN_W - 1
    wstart = np.clip(qcol - NA_WIN_W // 2, 0, GRID_W - NA_WIN_W)
    in_win = (kcol >= wstart) & (kcol < wstart + NA_WIN_W)
    sel = (np.arange(2 * NA_WIN_W - 1)[:, None] == dc.reshape(1, -1)).astype(np.float32)
    cols = jnp.einsum("lhrd,dn->lhrn", rpb, jnp.asarray(sel), precision=lax.Precision.HIGHEST)
    cols = jnp.where(in_win.reshape(-1), cols, NEG_INF).reshape(DEPTH, H_C, 2 * NA_WIN_H - 1, GRID_W, GRID_W)
    tab = jnp.stack([cols[:, :, v:v + NA_WIN_H] for v in range(NA_WIN_H)], 2)
    return tab.transpose(0, 1, 2, 4, 3, 5).reshape(DEPTH, H_C, NA_WIN_H, GRID_W, NA_WIN_H * GRID_W)


def _gate_layouts(zg, nb, seq, row0):
    nc = seq // LC_K
    g = zg[row0:row0 + nb * seq, :4 * H_A].reshape(nb, seq, 2, 2, H_A)
    g = g.transpose(0, 4, 1, 2, 3).reshape(nb, H_A, nc, LC_K, 4)
    return g, g.transpose(0, 1, 2, 4, 3)


def _keys_t(za, nb, seq, row0):
    k = za[row0:row0 + nb * seq, H_A * DP_A:2 * H_A * DP_A].reshape(nb, seq // LC_K, LC_K, H_A, DP_A)
    return k.transpose(0, 3, 1, 4, 2)


def _dispatch_plan(route, counts):
    nblk = 2 * T_ALL // MOE_BLK + N_EXPERTS
    cnt = counts[0, :N_EXPERTS].astype(jnp.int32)
    padded = (cnt + MOE_BLK - 1) // MOE_BLK * MOE_BLK
    p_end = jnp.cumsum(padded)
    p_start = p_end - padded
    e = route[:, 0:2].astype(jnp.int32)
    pos = p_start[e] + route[:, 4:6].astype(jnp.int32)
    pos2 = pos.T.reshape(-1)
    blk_first = jnp.arange(nblk, dtype=jnp.int32) * MOE_BLK
    blk_e = jnp.minimum(jnp.sum((p_end[None, :] <= blk_first[:, None]).astype(jnp.int32), axis=1), N_EXPERTS - 1)
    nused = p_end[-1:] // MOE_BLK
    return blk_e, nused, pos2, p_end, nblk


def kernel(x_prompt, x_sample, state_a_C, state_a_n, state_a_m, cache_b_k, cache_b_v, cache_c_k, cache_c_v, c, c_ctx, w_in, b_a_i, b_a_f, w_a_hnorm, b_sink, rpb, w_out, w_ada, b_ada, ln_g, ln_b, w_router_grp, w_router_exp, w_e_gate, w_e_up, w_e_down):
    w_in_p = _prep_w_in(w_in)
    w_out_p = _prep_w_out(w_out)
    w_r = jnp.pad(jnp.concatenate([w_router_exp, w_router_grp], -1),
                  ((0, 0), (0, 0), (0, LANES - N_EXPERTS - N_GROUPS)))
    wn_p = jnp.pad(w_a_hnorm.reshape(DEPTH, H_A, 1, DH_A), ((0, 0), (0, 0), (0, 0), (0, DP_A - DH_A)))
    gate_b = jnp.concatenate([b_a_i, b_a_f], 1).transpose(0, 2, 1)
    sink_p = jnp.pad(b_sink, ((0, 0), (0, 8 - H_B))).reshape(DEPTH, 1, 8)
    cos_t, sin_t = _rope_tables()
    na_bias = _na_bias_tables(rpb)
    cb_k = cache_b_k.reshape(B_LAT, DEPTH, PAST_LEN, D_KVB)
    cb_v = cache_b_v.reshape(B_LAT, DEPTH, PAST_LEN, D_KVB)
    cc_k = cache_c_k.reshape(B_LAT, DEPTH, PAST_LEN, D_C)
    cc_v = cache_c_v.reshape(B_LAT, DEPTH, PAST_LEN, D_C)
    pad_c = ((0, 0), (0, 0), (0, 0), (0, 0), (0, DP_A - DH_A), (0, DP_A - DH_A))
    st_ct = jnp.swapaxes(jnp.pad(state_a_C, pad_c), -1, -2)
    st_nr = jnp.broadcast_to(jnp.pad(state_a_n, pad_c[:-1])[..., None], st_ct.shape)
    st_s = jnp.concatenate([st_ct, st_nr], -1)
    st_m = state_a_m[..., None, None]
    z_s = jnp.zeros((B_CTX, 2, H_A, DP_A, 2 * DP_A), F32)
    z_m = jnp.zeros((B_CTX, 2, H_A, 1, 1), F32)

    cvec = jnp.concatenate([c, c_ctx[None, :], jnp.zeros((3, D_MODEL), F32)], 0)
    mod = _ada_call(cvec, w_ada, b_ada)
    tile_row = np.concatenate([np.full(T_CTX // TM, B_LAT), np.repeat(np.arange(B_LAT), N_LAT // TM)])
    mod_t = mod[:, tile_row].reshape(DEPTH, N_TILES, 1, 6, D_MODEL)

    x = jnp.concatenate([x_prompt.reshape(T_CTX, D_MODEL), x_sample.reshape(T_LAT, D_MODEL)], 0)
    cs_, ns_, ms_, kbs, vbs, kcs, vcs = [], [], [], [], [], [], []
    for l in range(DEPTH):
        sh_a, sc_a, g_a, sh_f, sc_f, g_f = (mod_t[l, :, :, j] for j in range(6))
        za, bq, bk, bv, cq, ck, cv, zg = _inproj_call(x, sc_a, sh_a, w_in_p[l])
        bcol = gate_b[l].reshape(H_A, 1, 4)
        brow = gate_b[l].reshape(H_A, 4, 1)
        gc, gr = _gate_layouts(zg, B_CTX, L_CTX, 0)
        ya_c, s_l, m_l = _mlstm_call(za, _keys_t(za, B_CTX, L_CTX, 0), gc, gr, bcol, brow, wn_p[l], z_s, z_m,
                                     nb=B_CTX, seq=L_CTX, row_blk0=0, hp=H_A)
        yb_c, yc_c = _ctx_attn_call(bq, bk, bv, cq, ck, cv, sink_p[l])
        gc, gr = _gate_layouts(zg, B_LAT, N_LAT, T_CTX)
        ya_l, _, _ = _mlstm_call(za, _keys_t(za, B_LAT, N_LAT, T_CTX), gc, gr, bcol, brow, wn_p[l],
                                 st_s[:, l], st_m[:, l],
                                 nb=B_LAT, seq=N_LAT, row_blk0=T_CTX // N_LAT, hp=H_A // 2)
        yb_l = _win_attn_call(bq, bk, bv, cb_k, cb_v, cos_t, sin_t, sink_p[l], l)
        yc_l = _na_attn_call(cq, ck, cv, cc_k, cc_v, na_bias[l], l)
        x1, h2, route, counts = _outproj_call((ya_c, yb_c, yc_c), (ya_l, yb_l, yc_l), x, g_a, sc_f, sh_f,
                                              ln_g[l, 0:1], ln_b[l, 0:1], w_out_p[l], w_r[l])
        blk_e, nused, pos2, p_end, nblk = _dispatch_plan(route, counts)
        xs = _dispatch_call(pos2, p_end, h2, nblk)
        eo = _expert_call(blk_e, nused, xs, w_e_gate[l], w_e_up[l], w_e_down[l], nblk)
        x = _combine_call(pos2, eo, x1, route, g_f, ln_g[l, 1:2], ln_b[l, 1:2])
        cs_.append(jnp.swapaxes(s_l[..., :DH_A, :DH_A], -1, -2))
        ns_.append(s_l[..., :DH_A, DP_A])
        ms_.append(m_l.reshape(B_CTX, 2, H_A))
        kbs.append(bk[:T_CTX].reshape(B_CTX, L_CTX, KV_B, HD))
        vbs.append(bv[:T_CTX].reshape(B_CTX, L_CTX, KV_B, HD))
        kcs.append(ck[:T_CTX].reshape(B_CTX, L_CTX, H_C, HD))
        vcs.append(cv[:T_CTX].reshape(B_CTX, L_CTX, H_C, HD))
    y_prompt = x[:T_CTX].reshape(B_CTX, L_CTX, D_MODEL)
    y_sample = x[T_CTX:].reshape(B_LAT, N_LAT, D_MODEL)
    return (y_prompt, y_sample, jnp.stack(cs_, 1), jnp.stack(ns_, 1), jnp.stack(ms_, 1),
            jnp.stack(kbs, 1), jnp.stack(vbs, 1), jnp.stack(kcs, 1), jnp.stack(vcs, 1))
```

```python
import functools

import numpy as np
import jax
import jax.numpy as jnp
from jax import lax
from jax.experimental import pallas as pl
from jax.experimental.pallas import tpu as pltpu

F32 = jnp.float32
BF16 = jnp.bfloat16
NEG_INF = float("-inf")

D_MODEL = 1024
DEPTH = 4
B_CTX, L_CTX = 16, 256
B_LAT, N_LAT = 4, 2048
PAST_LEN = 512
GRID_W = 64
H_A, DH_A = 4, 96
D_A = H_A * DH_A
H_B, KV_B, HD = 6, 2, 64
G_B = H_B // KV_B
D_B, D_KVB = H_B * HD, KV_B * HD
WIN = 128
ROPE_BASE = 10000.0
H_C = 4
D_C = H_C * HD
NA_WIN_H, NA_WIN_W = 8, 16
N_GROUPS, E_PER_GROUP = 4, 8
N_EXPERTS = N_GROUPS * E_PER_GROUP
D_EXPERT = D_MODEL // 4
ALPHA = (2 * DEPTH) ** 0.25
LN_EPS = 1e-5

LANES = 128
DP_A = LANES
LC_K = LANES
TM = 256
MOE_BLK = 256
VMEM_LIMIT = 48 * 1024 * 1024

T_CTX = B_CTX * L_CTX
T_LAT = B_LAT * N_LAT
T_ALL = T_CTX + T_LAT
N_TILES = T_ALL // TM
ZA_COLS = 4 * H_A * DP_A
Y_A_COLS = H_A * DP_A
IN_SPLITS = (ZA_COLS, D_B, D_KVB, D_KVB, D_C, D_C, D_C, LANES)
IN_COLS = sum(IN_SPLITS)


def _cparams(*sem):
    return pltpu.CompilerParams(dimension_semantics=sem, vmem_limit_bytes=VMEM_LIMIT)


def _dot(a, b):
    return jnp.dot(a.astype(BF16), b.astype(BF16), preferred_element_type=F32)


def _dot_nt(a, b):
    return lax.dot_general(a.astype(BF16), b.astype(BF16), (((1,), (1,)), ((), ())), preferred_element_type=F32)


def _dot_tn(a, b):
    return lax.dot_general(a.astype(BF16), b.astype(BF16), (((0,), (0,)), ((), ())), preferred_element_type=F32)


def _layer_norm(v, g, b):
    mu = jnp.mean(v, -1, keepdims=True)
    var = jnp.mean(jnp.square(v - mu), -1, keepdims=True)
    return (v - mu) * lax.rsqrt(var + LN_EPS) * g + b


def _ada_kernel(c_ref, w_ref, b_ref, o_ref):
    s = jax.nn.silu(c_ref[...])
    o_ref[0] = _dot(s, w_ref[0]) + b_ref[0]


def _ada_call(cvec, w_ada, b_ada):
    nb = 6
    return pl.pallas_call(
        _ada_kernel,
        grid=(DEPTH, nb),
        in_specs=[
            pl.BlockSpec((8, D_MODEL), lambda l, j: (0, 0)),
            pl.BlockSpec((1, D_MODEL, D_MODEL), lambda l, j: (l, 0, j)),
            pl.BlockSpec((1, 1, D_MODEL), lambda l, j: (l, 0, j)),
        ],
        out_specs=pl.BlockSpec((1, 8, D_MODEL), lambda l, j: (l, 0, j)),
        out_shape=jax.ShapeDtypeStruct((DEPTH, 8, 6 * D_MODEL), F32),
        compiler_params=_cparams("arbitrary", "arbitrary"),
        name="adaln",
    )(cvec, w_ada, b_ada.reshape(DEPTH, 1, 6 * D_MODEL))


def _inproj_kernel(x_ref, sc_ref, sh_ref, w_ref, *out_refs):
    h = (x_ref[...] * (1.0 + sc_ref[0]) + sh_ref[0]).astype(BF16)
    off = 0
    for ref in out_refs:
        n = ref.shape[-1]
        ref[...] = jnp.dot(h, w_ref[:, off:off + n], preferred_element_type=F32)
        off += n


def _inproj_call(x, sc, sh, w):
    tile_vec = pl.BlockSpec((1, 1, D_MODEL), lambda i: (i, 0, 0))
    return pl.pallas_call(
        _inproj_kernel,
        grid=(N_TILES,),
        in_specs=[
            pl.BlockSpec((TM, D_MODEL), lambda i: (i, 0)),
            tile_vec, tile_vec,
            pl.BlockSpec((D_MODEL, IN_COLS), lambda i: (0, 0)),
        ],
        out_specs=[pl.BlockSpec((TM, n), lambda i: (i, 0)) for n in IN_SPLITS],
        out_shape=[jax.ShapeDtypeStruct((T_ALL, n), F32) for n in IN_SPLITS],
        compiler_params=_cparams("arbitrary"),
        name="inproj",
    )(x, sc, sh, w)


def _mlstm_kernel(q_ref, kt_ref, v_ref, o_ref, gc_ref, gr_ref, bc_ref, br_ref, wn_ref, s0_ref, m0_ref,
                  y_ref, so_ref, mo_ref, hf_scr, hb_scr, *, nc, hp):
    lc = LC_K
    scale = DH_A ** -0.5
    ti = lax.broadcasted_iota(jnp.int32, (lc, lc), 0)
    si = lax.broadcasted_iota(jnp.int32, (lc, lc), 1)
    lane = lax.broadcasted_iota(jnp.int32, (1, DP_A), 1)
    lane_ok = lane < DH_A
    ones = jnp.ones((lc, DP_A), F32)

    def chunk(c, carry, d, j):
        smat, m = carry
        r0 = pl.multiple_of(c * lc, lc)
        cols = slice(j * DP_A, (j + 1) * DP_A)
        qc = q_ref[pl.ds(r0, lc), cols].astype(BF16)
        kt = (kt_ref[0, j, c] * scale).astype(BF16)
        v1 = jnp.concatenate([v_ref[pl.ds(r0, lc), cols], ones], axis=1)
        gcol = gc_ref[0, j, c] + bc_ref[j]
        grow = gr_ref[0, j, c] + br_ref[j]
        i_col = gcol[:, d:d + 1]
        f_col = jax.nn.log_sigmoid(gcol[:, 2 + d:3 + d])
        i_row = grow[d:d + 1, :]
        f_row = jax.nn.log_sigmoid(grow[2 + d:3 + d, :])
        if d == 0:
            mask, mask_t = si <= ti, si >= ti
        else:
            mask, mask_t = si >= ti, si <= ti
        b_col = jnp.sum(jnp.where(mask, f_row, 0.0), axis=1, keepdims=True)
        b_row = jnp.sum(jnp.where(mask_t, f_col, 0.0), axis=0, keepdims=True)
        dmat = jnp.where(mask, b_col + (i_row - b_row), NEG_INF)
        inter = b_col + m
        m_out = jnp.maximum(inter, jnp.max(dmat, axis=1, keepdims=True))
        wmat = jnp.exp(dmat - m_out)
        sw = jnp.dot(qc, kt, preferred_element_type=F32) * wmat
        sc_in = jnp.exp(inter - m_out)
        tot = _dot(sw, v1) + sc_in * _dot(qc, smat)
        h = tot[:, :DP_A] / jnp.maximum(jnp.abs(tot[:, DP_A:]), jnp.exp(-m_out))
        bl = jnp.sum(f_row, axis=1, keepdims=True)
        dec = bl - b_col + i_col
        m_new = jnp.maximum(bl + m, jnp.max(dec, axis=0, keepdims=True))
        wk = jnp.exp(dec - m_new)
        sc_st = jnp.exp(bl + m - m_new)
        s_new = sc_st * smat + _dot(kt, wk * v1)
        return r0, cols, h, (s_new, m_new)

    def body(i, carry):
        out = []
        for j in range(hp):
            r0, cols, h, st = chunk(i, carry[2 * j], 0, j)
            hf_scr[pl.ds(r0, lc), cols] = h
            out.append(st)
            r0, cols, h, st = chunk(nc - 1 - i, carry[2 * j + 1], 1, j)
            hb_scr[pl.ds(r0, lc), cols] = h
            out.append(st)
        return tuple(out)

    init = tuple((s0_ref[0, d, j], m0_ref[0, d, j]) for j in range(hp) for d in range(2))
    final = lax.fori_loop(0, nc, body, init)
    for j in range(hp):
        for d in range(2):
            smat, m = final[2 * j + d]
            so_ref[0, d, j] = smat
            mo_ref[0, d, j] = m

    def finish(c, _):
        r0 = pl.multiple_of(c * lc, lc)
        for j in range(hp):
            cols = slice(j * DP_A, (j + 1) * DP_A)
            h = hf_scr[pl.ds(r0, lc), cols] + hb_scr[pl.ds(r0, lc), cols]
            mu = jnp.sum(h, axis=1, keepdims=True) * (1.0 / DH_A)
            dv = jnp.where(lane_ok, h - mu, 0.0)
            var = jnp.sum(dv * dv, axis=1, keepdims=True) * (1.0 / DH_A)
            hn = dv * lax.rsqrt(var + LN_EPS) * wn_ref[j]
            y_ref[pl.ds(r0, lc), cols] = jax.nn.sigmoid(o_ref[pl.ds(r0, lc), cols]) * hn
        return 0

    lax.fori_loop(0, nc, finish, 0)


def _mlstm_call(za, kt, gcol, grow, bcol, brow, wn, s0, m0, *, nb, seq, row_blk0, hp):
    nc = seq // LC_K
    ng = H_A // hp
    w = hp * DP_A

    def zspec(part):
        return pl.BlockSpec((seq, w), lambda b, g: (row_blk0 + b, part * ng + g))

    def head_vec(*tail):
        return pl.BlockSpec((hp,) + tail, lambda b, g: (g,) + (0,) * len(tail))

    def state(*tail):
        return pl.BlockSpec((1, 2, hp) + tail, lambda b, g: (b, 0, g) + (0,) * len(tail))

    def per_chunk(r, c):
        return pl.BlockSpec((1, hp, nc, r, c), lambda b, g: (b, g, 0, 0, 0))

    return pl.pallas_call(
        functools.partial(_mlstm_kernel, nc=nc, hp=hp),
        grid=(nb, ng),
        in_specs=[
            zspec(0), per_chunk(DP_A, LC_K), zspec(2), zspec(3), per_chunk(LC_K, 4), per_chunk(4, LC_K),
            head_vec(1, 4), head_vec(4, 1), head_vec(1, DP_A),
            state(DP_A, 2 * DP_A), state(1, 1),
        ],
        out_specs=[pl.BlockSpec((seq, w), lambda b, g: (b, g)), state(DP_A, 2 * DP_A), state(1, 1)],
        out_shape=[
            jax.ShapeDtypeStruct((nb * seq, Y_A_COLS), F32),
            jax.ShapeDtypeStruct((nb, 2, H_A, DP_A, 2 * DP_A), F32),
            jax.ShapeDtypeStruct((nb, 2, H_A, 1, 1), F32),
        ],
        scratch_shapes=[pltpu.VMEM((seq, w), F32), pltpu.VMEM((seq, w), F32)],
        compiler_params=_cparams("arbitrary", "arbitrary"),
        name="mlstm",
    )(za, kt, za, za, gcol, grow, bcol, brow, wn, s0, m0)


def _pair_attention(qp, kslabs, vaugs, masks, sink_col):
    m_rows = qp.shape[0]
    lo = lax.broadcasted_iota(jnp.int32, qp.shape, 1) < HD
    q2 = jnp.concatenate([jnp.where(lo, qp, 0.0), jnp.where(lo, 0.0, qp)], axis=0).astype(BF16)
    scores = []
    for ks, mk in zip(kslabs, masks):
        s = jnp.dot(q2, ks, preferred_element_type=F32)
        if mk is not None:
            s = jnp.where(mk, s, NEG_INF) if mk.dtype == jnp.bool_ else s + mk
        scores.append(s)
    mx = scores[0].max(axis=1, keepdims=True)
    for s in scores[1:]:
        mx = jnp.maximum(mx, s.max(axis=1, keepdims=True))
    if sink_col is not None:
        mx = jnp.maximum(mx, sink_col)
    acc = None
    for s, va in zip(scores, vaugs):
        pv = jnp.dot(jnp.exp(s - mx).astype(BF16), va, preferred_element_type=F32)
        acc = pv if acc is None else acc + pv
    den = acc[:, LANES:]
    if sink_col is not None:
        den = den + jnp.exp(sink_col - mx)
    o = acc[:, :LANES] / den
    return jnp.where(lo, o[:m_rows], o[m_rows:])


def _gqa_key_slabs(kt):
    a, b = kt[:HD], kt[HD:]
    return [jnp.concatenate([a, a], 0), kt, jnp.concatenate([b, b], 0)]


def _gqa_value_pairs(v):
    lo = lax.broadcasted_iota(jnp.int32, v.shape, 1) < HD
    sw = pltpu.roll(v, HD, 1)
    ones = jnp.ones_like(v)
    return [jnp.concatenate([x, ones], 1) for x in (jnp.where(lo, v, sw), v, jnp.where(lo, sw, v))]


def _sink_col(sink_ref, p, m_rows):
    row = lax.broadcasted_iota(jnp.int32, (2 * m_rows, 1), 0)
    return jnp.where(row < m_rows, sink_ref[0:1, 2 * p:2 * p + 1], sink_ref[0:1, 2 * p + 1:2 * p + 2])


def _ctx_attn_kernel(bq_ref, bk_ref, bv_ref, cq_ref, ck_ref, cv_ref, sink_ref, yb_ref, yc_ref):
    scale = HD ** -0.5
    kslabs = _gqa_key_slabs(bk_ref[...].T)
    vpairs = _gqa_value_pairs(bv_ref[...])
    for p in range(D_B // LANES):
        cols = slice(p * LANES, (p + 1) * LANES)
        yb_ref[:, cols] = _pair_attention(bq_ref[:, cols] * scale, [kslabs[p].astype(BF16)],
                                          [vpairs[p].astype(BF16)], [None], _sink_col(sink_ref, p, L_CTX))
    ckt = ck_ref[...].T
    ones = jnp.ones((L_CTX, LANES), F32)
    for p in range(D_C // LANES):
        cols = slice(p * LANES, (p + 1) * LANES)
        va = jnp.concatenate([cv_ref[:, cols], ones], 1)
        yc_ref[:, cols] = _pair_attention(cq_ref[:, cols] * scale, [ckt[cols].astype(BF16)], [va.astype(BF16)],
                                          [None], None)


def _ctx_attn_call(bq, bk, bv, cq, ck, cv, sink):
    def spec(n):
        return pl.BlockSpec((L_CTX, n), lambda b: (b, 0))

    return pl.pallas_call(
        _ctx_attn_kernel,
        grid=(B_CTX,),
        in_specs=[spec(D_B), spec(D_KVB), spec(D_KVB), spec(D_C), spec(D_C), spec(D_C),
                  pl.BlockSpec((1, 8), lambda b: (0, 0))],
        out_specs=[spec(D_B), spec(D_C)],
        out_shape=[jax.ShapeDtypeStruct((T_CTX, D_B), F32), jax.ShapeDtypeStruct((T_CTX, D_C), F32)],
        compiler_params=_cparams("arbitrary"),
        name="ctx_attn",
    )(bq, bk, bv, cq, ck, cv, sink)


def _rope(x, cos, sin_signed, first):
    rot = jnp.where(first, pltpu.roll(x, LANES - 16, 1), pltpu.roll(x, 16, 1))
    return x * cos + rot * sin_signed


def _win_attn_kernel(q_ref, k_ref, v_ref, kc_ref, vc_ref, cos_ref, sin_ref, sink_ref, y_ref,
                     kpt_scr, va_scr, kcp_scr, vca_scr):
    scale = HD ** -0.5
    n_pairs = D_B // LANES
    nblk = N_LAT // WIN
    nband = 3
    lane = lax.broadcasted_iota(jnp.int32, (1, LANES), 1)
    first = (lane % 32) < 16
    for p, (ks, va) in enumerate(zip(_gqa_key_slabs(kc_ref[0, 0].T), _gqa_value_pairs(vc_ref[0, 0]))):
        kcp_scr[p] = ks.astype(BF16)
        vca_scr[p] = va.astype(BF16)

    def prep(blk, c):
        r0 = pl.multiple_of(blk * WIN, WIN)
        kr = _rope(k_ref[pl.ds(r0, WIN), :], cos_ref[pl.ds(r0, WIN), :], sin_ref[pl.ds(r0, WIN), :], first)
        for p, (ks, va) in enumerate(zip(_gqa_key_slabs(kr.T), _gqa_value_pairs(v_ref[pl.ds(r0, WIN), :]))):
            kpt_scr[p, blk] = ks.astype(BF16)
            va_scr[p, pl.ds(r0, WIN), :] = va.astype(BF16)
        return c

    lax.fori_loop(0, nblk, prep, 0)

    def body(blk, c):
        q0 = pl.multiple_of(blk * WIN, WIN)
        sb = jnp.clip(blk - 1, 0, nblk - nband)
        k0 = pl.multiple_of(sb * WIN, WIN)
        cos = cos_ref[pl.ds(q0, WIN), :]
        sin = sin_ref[pl.ds(q0, WIN), :]
        row = lax.broadcasted_iota(jnp.int32, (2 * WIN, nband * WIN), 0)
        qpos = q0 + jnp.where(row < WIN, row, row - WIN)
        kpos = k0 + lax.broadcasted_iota(jnp.int32, (2 * WIN, nband * WIN), 1)
        mask = jnp.abs(kpos - qpos) <= WIN
        for p in range(n_pairs):
            cols = slice(p * LANES, (p + 1) * LANES)
            qp = _rope(q_ref[pl.ds(q0, WIN), cols], cos, sin, first) * scale
            k_loc = jnp.concatenate([kpt_scr[p, sb + j] for j in range(nband)], axis=1)
            v_loc = va_scr[p, pl.ds(k0, nband * WIN), :]
            y_ref[pl.ds(q0, WIN), cols] = _pair_attention(qp, [kcp_scr[p], k_loc], [vca_scr[p], v_loc],
                                                          [None, mask], _sink_col(sink_ref, p, WIN))
        return c

    lax.fori_loop(0, nblk, body, 0)


def _win_attn_call(bq, bk, bv, cache_k, cache_v, cos, sin, sink, layer):
    rb0 = T_CTX // N_LAT

    def spec(n):
        return pl.BlockSpec((N_LAT, n), lambda b: (rb0 + b, 0))

    cache = pl.BlockSpec((1, 1, PAST_LEN, D_KVB), lambda b: (b, layer, 0, 0))
    tab = pl.BlockSpec((N_LAT, LANES), lambda b: (0, 0))
    return pl.pallas_call(
        _win_attn_kernel,
        grid=(B_LAT,),
        in_specs=[spec(D_B), spec(D_KVB), spec(D_KVB), cache, cache, tab, tab,
                  pl.BlockSpec((1, 8), lambda b: (0, 0))],
        out_specs=pl.BlockSpec((N_LAT, D_B), lambda b: (b, 0)),
        out_shape=jax.ShapeDtypeStruct((T_LAT, D_B), F32),
        scratch_shapes=[pltpu.VMEM((D_B // LANES, N_LAT // WIN, LANES, WIN), BF16),
                        pltpu.VMEM((D_B // LANES, N_LAT, 2 * LANES), BF16),
                        pltpu.VMEM((D_B // LANES, LANES, PAST_LEN), BF16),
                        pltpu.VMEM((D_B // LANES, PAST_LEN, 2 * LANES), BF16)],
        compiler_params=_cparams("arbitrary"),
        name="win_attn",
    )(bq, bk, bv, cache_k, cache_v, cos, sin, sink)


NA_BLK = LANES
NA_SPAN_BLKS = NA_WIN_H * GRID_W // NA_BLK + 1
NA_SPAN_ROWS = NA_SPAN_BLKS * NA_BLK // GRID_W
NA_VARIANTS = ((7, 0), (6, 0), (5, 0), (4, 0), (3, 0), (3, 1), (3, 2), (2, 2), (1, 2), (0, 2))


def _na_attn_kernel(q_ref, k_ref, v_ref, kc_ref, vc_ref, bias_ref, y_ref, kpt_scr, va_scr, kcp_scr, vca_scr):
    scale = HD ** -0.5
    rows = N_LAT // GRID_W
    n_pairs = D_C // LANES
    nblk = N_LAT // NA_BLK
    half = NA_WIN_H // 2
    kct = kc_ref[0, 0].T
    for p in range(n_pairs):
        cols = slice(p * LANES, (p + 1) * LANES)
        kcp_scr[p] = kct[cols].astype(BF16)
        vca_scr[p] = jnp.concatenate([vc_ref[0, 0, :, cols], jnp.ones((PAST_LEN, LANES), F32)], 1).astype(BF16)

    def prep(blk, c):
        r0 = pl.multiple_of(blk * NA_BLK, NA_BLK)
        kt = k_ref[pl.ds(r0, NA_BLK), :].T
        for p in range(n_pairs):
            cols = slice(p * LANES, (p + 1) * LANES)
            kpt_scr[p, blk] = kt[cols].astype(BF16)
            va_scr[p, pl.ds(r0, NA_BLK), :] = jnp.concatenate(
                [v_ref[pl.ds(r0, NA_BLK), cols], jnp.ones((NA_BLK, LANES), F32)], 1).astype(BF16)
        return c

    lax.fori_loop(0, nblk, prep, 0)

    def body(r, c):
        kr0 = jnp.clip(r - half, 0, rows - NA_WIN_H)
        sb = jnp.minimum(kr0 // 2, nblk - NA_SPAN_BLKS)
        var = jnp.where(r <= half, r, jnp.where(r >= rows - half, r - (rows - 2 * half - 2), half + (kr0 & 1)))
        q0 = pl.multiple_of(r * GRID_W, GRID_W)
        k0 = pl.multiple_of(sb * NA_BLK, NA_BLK)
        for p in range(n_pairs):
            cols = slice(p * LANES, (p + 1) * LANES)
            k_loc = jnp.concatenate([kpt_scr[p, sb + j] for j in range(NA_SPAN_BLKS)], axis=1)
            v_loc = va_scr[p, pl.ds(k0, NA_SPAN_BLKS * NA_BLK), :]
            y_ref[pl.ds(q0, GRID_W), cols] = _pair_attention(
                q_ref[pl.ds(q0, GRID_W), cols] * scale, [kcp_scr[p], k_loc], [vca_scr[p], v_loc],
                [None, bias_ref[p, var]], None)
        return c

    lax.fori_loop(0, rows, body, 0)


def _na_attn_call(cq, ck, cv, cache_k, cache_v, bias_tab, layer):
    rb0 = T_CTX // N_LAT
    n_pairs = D_C // LANES
    spec = pl.BlockSpec((N_LAT, D_C), lambda b: (rb0 + b, 0))
    cache = pl.BlockSpec((1, 1, PAST_LEN, D_C), lambda b: (b, layer, 0, 0))
    span = NA_SPAN_BLKS * NA_BLK
    return pl.pallas_call(
        _na_attn_kernel,
        grid=(B_LAT,),
        in_specs=[spec, spec, spec, cache, cache,
                  pl.BlockSpec((n_pairs, len(NA_VARIANTS), 2 * GRID_W, span), lambda b: (0, 0, 0, 0))],
        out_specs=pl.BlockSpec((N_LAT, D_C), lambda b: (b, 0)),
        out_shape=jax.ShapeDtypeStruct((T_LAT, D_C), F32),
        scratch_shapes=[pltpu.VMEM((n_pairs, N_LAT // NA_BLK, LANES, NA_BLK), BF16),
                        pltpu.VMEM((n_pairs, N_LAT, 2 * LANES), BF16),
                        pltpu.VMEM((n_pairs, LANES, PAST_LEN), BF16),
                        pltpu.VMEM((n_pairs, PAST_LEN, 2 * LANES), BF16)],
        compiler_params=_cparams("arbitrary"),
        name="na_attn",
    )(cq, ck, cv, cache_k, cache_v, bias_tab)


def _outproj_kernel(yac_ref, ybc_ref, ycc_ref, yal_ref, ybl_ref, ycl_ref, x_ref, ga_ref, scf_ref, shf_ref,
                    lng_ref, lnb_ref, wo_ref, wr_ref, x1_ref, h2_ref, route_ref, cnt_ref, run_scr):
    @pl.when(pl.program_id(0) == 0)
    def _():
        run_scr[...] = jnp.zeros_like(run_scr)

    is_ctx = pl.program_id(0) < T_CTX // TM

    def pick(c_ref, l_ref):
        return jnp.where(is_ctx, c_ref[...], l_ref[...])

    y = (_dot(pick(yac_ref, yal_ref), wo_ref[0:Y_A_COLS, :])
         + _dot(pick(ybc_ref, ybl_ref), wo_ref[Y_A_COLS:Y_A_COLS + D_B, :])
         + _dot(pick(ycc_ref, ycl_ref), wo_ref[Y_A_COLS + D_B:, :]))
    x1 = _layer_norm(ALPHA * x_ref[...] + ga_ref[0] * y, lng_ref[...], lnb_ref[...])
    x1_ref[...] = x1
    h2 = x1 * (1.0 + scf_ref[0]) + shf_ref[0]
    h2_ref[...] = h2
    w_r = wr_ref[...]
    w_hi = w_r.astype(BF16)
    w_lo = (w_r - w_hi.astype(F32)).astype(BF16)
    h_hi = h2.astype(BF16)
    h_lo = (h2 - h_hi.astype(F32)).astype(BF16)
    p_hi = jnp.dot(h_hi, jnp.concatenate([w_hi, w_lo], axis=1), preferred_element_type=F32)
    logits = p_hi[:, :LANES] + p_hi[:, LANES:] + jnp.dot(h_lo, w_hi, preferred_element_type=F32)
    lane = lax.broadcasted_iota(jnp.int32, logits.shape, 1)
    lanef = lane.astype(F32)
    big = float(LANES)
    lg = jnp.where((lane >= N_EXPERTS) & (lane < N_EXPERTS + N_GROUPS), logits, NEG_INF)
    mg = jnp.max(lg, axis=1, keepdims=True)
    grp = jnp.min(jnp.where(lg == mg, lanef, big), axis=1, keepdims=True) - float(N_EXPERTS)
    g_w = 1.0 / jnp.sum(jnp.exp(lg - mg), axis=1, keepdims=True)
    in_grp = (lane < N_EXPERTS) & ((lane // E_PER_GROUP).astype(F32) == grp)
    le = jnp.where(in_grp, logits, NEG_INF)
    l1 = jnp.max(le, axis=1, keepdims=True)
    i1 = jnp.min(jnp.where(le == l1, lanef, big), axis=1, keepdims=True)
    le2 = jnp.where(lanef == i1, NEG_INF, le)
    l2 = jnp.max(le2, axis=1, keepdims=True)
    i2 = jnp.min(jnp.where(le2 == l2, lanef, big), axis=1, keepdims=True)
    e2 = jnp.exp(l2 - l1)
    w1 = g_w / (1.0 + e2)
    w2 = g_w * e2 / (1.0 + e2)
    oh1 = jnp.where(lanef == i1, 1.0, 0.0)
    oh2 = jnp.where(lanef == i2, 1.0, 0.0)
    rt = lax.broadcasted_iota(jnp.int32, (TM, TM), 0)
    ct = lax.broadcasted_iota(jnp.int32, (TM, TM), 1)
    before = jnp.where(ct < rt, 1.0, 0.0)
    run = run_scr[...]
    tot1 = jnp.sum(oh1, axis=0, keepdims=True)
    r1 = jnp.sum(oh1 * (run + _dot(before, oh1)), axis=1, keepdims=True)
    r2 = jnp.sum(oh2 * (run + tot1 + _dot(before, oh2)), axis=1, keepdims=True)
    run = run + tot1 + jnp.sum(oh2, axis=0, keepdims=True)
    run_scr[...] = run
    cnt_ref[...] = run
    vals = (i1, i2, w1, w2, r1, r2)
    out = jnp.zeros_like(logits)
    for n, v in enumerate(vals):
        out = jnp.where(lane == n, v, out)
    route_ref[...] = out[:, :8]


def _outproj_call(y_ctx, y_lat, x, ga, scf, shf, lng, lnb, wo, wr):
    tile_vec = pl.BlockSpec((1, 1, D_MODEL), lambda i: (i, 0, 0))
    row_vec = pl.BlockSpec((1, D_MODEL), lambda i: (0, 0))
    n_ctx = T_CTX // TM

    def tok(n):
        return pl.BlockSpec((TM, n), lambda i: (i, 0))

    def tok_ctx(n):
        return pl.BlockSpec((TM, n), lambda i: (jnp.minimum(i, n_ctx - 1), 0))

    def tok_lat(n):
        return pl.BlockSpec((TM, n), lambda i: (jnp.maximum(i - n_ctx, 0), 0))

    return pl.pallas_call(
        _outproj_kernel,
        grid=(N_TILES,),
        in_specs=[tok_ctx(Y_A_COLS), tok_ctx(D_B), tok_ctx(D_C), tok_lat(Y_A_COLS), tok_lat(D_B), tok_lat(D_C),
                  tok(D_MODEL), tile_vec, tile_vec, tile_vec, row_vec, row_vec,
                  pl.BlockSpec((Y_A_COLS + D_B + D_C, D_MODEL), lambda i: (0, 0)),
                  pl.BlockSpec((D_MODEL, LANES), lambda i: (0, 0))],
        out_specs=[tok(D_MODEL), tok(D_MODEL), tok(8), pl.BlockSpec((1, LANES), lambda i: (0, 0))],
        out_shape=[jax.ShapeDtypeStruct((T_ALL, D_MODEL), F32), jax.ShapeDtypeStruct((T_ALL, D_MODEL), F32),
                   jax.ShapeDtypeStruct((T_ALL, 8), F32), jax.ShapeDtypeStruct((1, LANES), F32)],
        scratch_shapes=[pltpu.VMEM((1, LANES), F32)],
        compiler_params=_cparams("arbitrary"),
        name="outproj_router",
    )(*y_ctx, *y_lat, x, ga, scf, shf, lng, lnb, wo, wr)


def _dispatch_kernel(pos_ref, pend_ref, h_ref, xs_hbm, zero_scr, sem):
    base = pl.program_id(0) * TM

    @pl.when(pl.program_id(0) == 0)
    def _():
        zero_scr[...] = jnp.zeros_like(zero_scr)

        def fill(e, op):
            prev = pend_ref[e - 1] if e else 0

            @pl.when(pend_ref[e] > prev)
            def _():
                first = pl.multiple_of(pend_ref[e] - MOE_BLK, MOE_BLK)
                op(pltpu.make_async_copy(zero_scr, xs_hbm.at[pl.ds(first, MOE_BLK), :], sem))

        def tail_copy(b):
            first = pl.multiple_of(b * MOE_BLK, MOE_BLK)
            return pltpu.make_async_copy(zero_scr, xs_hbm.at[pl.ds(first, MOE_BLK), :], sem)

        def tail_start(b, c):
            tail_copy(b).start()
            return c

        def tail_wait(b, c):
            tail_copy(b).wait()
            return c

        n_blocks = xs_hbm.shape[0] // MOE_BLK
        first_free = pend_ref[N_EXPERTS - 1] // MOE_BLK
        for e in range(N_EXPERTS):
            fill(e, lambda cp: cp.start())
        lax.fori_loop(first_free, n_blocks, tail_start, 0)
        for e in range(N_EXPERTS):
            fill(e, lambda cp: cp.wait())
        lax.fori_loop(first_free, n_blocks, tail_wait, 0)

    def row(t, p):
        return pltpu.make_async_copy(h_ref.at[pl.ds(t, 1), :], xs_hbm.at[pl.ds(p, 1), :], sem)

    def issue(t, c):
        row(t, pos_ref[base + t]).start()
        row(t, pos_ref[T_ALL + base + t]).start()
        return c

    lax.fori_loop(0, TM, issue, 0, unroll=8)
    whole = pltpu.make_async_copy(h_ref, xs_hbm.at[pl.ds(0, TM), :], sem)
    whole.wait()
    whole.wait()


def _dispatch_call(pos2, p_end, h2, nblk):
    grid_spec = pltpu.PrefetchScalarGridSpec(
        num_scalar_prefetch=2,
        grid=(N_TILES,),
        in_specs=[pl.BlockSpec((TM, D_MODEL), lambda i, p, pe: (i, 0))],
        out_specs=pl.BlockSpec(memory_space=pl.ANY),
        scratch_shapes=[pltpu.VMEM((MOE_BLK, D_MODEL), F32), pltpu.SemaphoreType.DMA(())],
    )
    return pl.pallas_call(
        _dispatch_kernel,
        grid_spec=grid_spec,
        out_shape=jax.ShapeDtypeStruct((nblk * MOE_BLK, D_MODEL), F32),
        compiler_params=_cparams("arbitrary"),
        name="dispatch",
    )(pos2, p_end, h2)


def _expert_kernel(blk_e_ref, nused_ref, xs_ref, wg_ref, wu_ref, wd_ref, out_ref):
    j = pl.program_id(0)

    @pl.when(j < nused_ref[0])
    def _():
        xb = xs_ref[...].astype(BF16)
        g = jnp.dot(xb, wg_ref[0].astype(BF16), preferred_element_type=F32)
        u = jnp.dot(xb, wu_ref[0].astype(BF16), preferred_element_type=F32)
        out_ref[...] = _dot(jax.nn.silu(g) * u, wd_ref[0])

    @pl.when(j >= nused_ref[0])
    def _():
        out_ref[...] = jnp.zeros_like(out_ref)


def _expert_call(blk_e, nused, xs, wg, wu, wd, nblk):
    def wspec(r, c):
        return pl.BlockSpec((1, r, c), lambda j, be, nu: (be[j], 0, 0))

    grid_spec = pltpu.PrefetchScalarGridSpec(
        num_scalar_prefetch=2,
        grid=(nblk,),
        in_specs=[
            pl.BlockSpec((MOE_BLK, D_MODEL), lambda j, be, nu: (jnp.clip(j, 0, jnp.maximum(nu[0] - 1, 0)), 0)),
            wspec(D_MODEL, D_EXPERT), wspec(D_MODEL, D_EXPERT), wspec(D_EXPERT, D_MODEL),
        ],
        out_specs=pl.BlockSpec((MOE_BLK, D_MODEL), lambda j, be, nu: (j, 0)),
    )
    return pl.pallas_call(
        _expert_kernel,
        grid_spec=grid_spec,
        out_shape=jax.ShapeDtypeStruct((nblk * MOE_BLK, D_MODEL), F32),
        compiler_params=_cparams("arbitrary"),
        name="experts",
    )(blk_e, nused, xs, wg, wu, wd)


def _combine_kernel(pos_ref, eo_hbm, x1_ref, route_ref, gf_ref, lng_ref, lnb_ref, x2_ref, buf, sem):
    base = pl.program_id(0) * TM

    def row(p, r, t):
        return pltpu.make_async_copy(eo_hbm.at[pl.ds(p, 1), :], buf.at[r, pl.ds(t, 1), :], sem)

    def issue(t, c):
        row(pos_ref[base + t], 0, t).start()
        row(pos_ref[T_ALL + base + t], 1, t).start()
        return c

    lax.fori_loop(0, TM, issue, 0, unroll=8)
    for r in range(2):
        pltpu.make_async_copy(eo_hbm.at[pl.ds(0, TM), :], buf.at[r], sem).wait()
    route = route_ref[...]
    y = route[:, 2:3] * buf[0] + route[:, 3:4] * buf[1]
    x2_ref[...] = _layer_norm(ALPHA * x1_ref[...] + gf_ref[0] * y, lng_ref[...], lnb_ref[...])


def _combine_call(pos2, eo, x1, route, gf, lng, lnb):
    grid_spec = pltpu.PrefetchScalarGridSpec(
        num_scalar_prefetch=1,
        grid=(N_TILES,),
        in_specs=[
            pl.BlockSpec(memory_space=pl.ANY),
            pl.BlockSpec((TM, D_MODEL), lambda i, p: (i, 0)),
            pl.BlockSpec((TM, 8), lambda i, p: (i, 0)),
            pl.BlockSpec((1, 1, D_MODEL), lambda i, p: (i, 0, 0)),
            pl.BlockSpec((1, D_MODEL), lambda i, p: (0, 0)),
            pl.BlockSpec((1, D_MODEL), lambda i, p: (0, 0)),
        ],
        out_specs=pl.BlockSpec((TM, D_MODEL), lambda i, p: (i, 0)),
        scratch_shapes=[pltpu.VMEM((2, TM, D_MODEL), F32), pltpu.SemaphoreType.DMA(())],
    )
    return pl.pallas_call(
        _combine_kernel,
        grid_spec=grid_spec,
        out_shape=jax.ShapeDtypeStruct((T_ALL, D_MODEL), F32),
        compiler_params=_cparams("arbitrary"),
        name="combine_norm",
    )(pos2, eo, x1, route, gf, lng, lnb)


def _prep_w_in(w_in):
    a = w_in[..., :4 * D_A].reshape(DEPTH, D_MODEL, 4, H_A, DH_A)
    a = jnp.pad(a, ((0, 0), (0, 0), (0, 0), (0, 0), (0, DP_A - DH_A))).reshape(DEPTH, D_MODEL, ZA_COLS)
    g = jnp.pad(w_in[..., 4 * D_A:4 * D_A + 4 * H_A], ((0, 0), (0, 0), (0, LANES - 4 * H_A)))
    rest = w_in[..., 4 * D_A + 4 * H_A:]
    return jnp.concatenate([a, rest, g], -1).astype(BF16)


def _prep_w_out(w_out):
    a = w_out[:, :D_A].reshape(DEPTH, H_A, DH_A, D_MODEL)
    a = jnp.pad(a, ((0, 0), (0, 0), (0, DP_A - DH_A), (0, 0))).reshape(DEPTH, Y_A_COLS, D_MODEL)
    return jnp.concatenate([a, w_out[:, D_A:]], 1).astype(BF16)


def _rope_tables():
    t = np.arange(N_LAT)
    nf = HD // 4
    inv = ROPE_BASE ** (-np.arange(nf, dtype=np.float32) / nf)
    ar = (t // GRID_W).astype(np.float32)[:, None] * inv
    ac = (t % GRID_W).astype(np.float32)[:, None] * inv
    ang = jnp.asarray(np.concatenate([ar, ar, ac, ac], -1), F32)
    cos, sin = jnp.cos(ang), jnp.sin(ang)
    sign = np.where((np.arange(HD) % 32) < 16, -1.0, 1.0).astype(np.float32)
    reps = LANES // HD
    return jnp.tile(cos, (1, reps)), jnp.tile(sin * sign, (1, reps))


def _na_bias_tables(rpb):
    qcol = np.arange(GRID_W)[:, None]
    kcol = np.arange(GRID_W)[None, :]
    dc = np.clip(kcol - qcol, 1 - NA_WIN_W, NA_WIN_W - 1) + NA_WIN_W - 1
    wstart = np.clip(qcol - NA_WIN_W // 2, 0, GRID_W - NA_WIN_W)
    in_win = (kcol >= wstart) & (kcol < wstart + NA_WIN_W)
    sel = (np.arange(2 * NA_WIN_W - 1)[:, None] == dc.reshape(1, -1)).astype(np.float32)
    cols = jnp.einsum("lhrd,dn->lhrn", rpb, jnp.asarray(sel), precision=lax.Precision.HIGHEST)
    cols = jnp.where(in_win.reshape(-1), cols, NEG_INF).reshape(DEPTH, H_C, 2 * NA_WIN_H - 1, GRID_W, GRID_W)
    outside = jnp.full((DEPTH, H_C, GRID_W, GRID_W), NEG_INF, F32)
    variants = []
    for dr0, off in NA_VARIANTS:
        span_rows = [cols[:, :, dr0 + i - off] if 0 <= i - off < NA_WIN_H else outside for i in range(NA_SPAN_ROWS)]
        variants.append(jnp.stack(span_rows, 3).reshape(DEPTH, H_C, GRID_W, NA_SPAN_ROWS * GRID_W))
    tab = jnp.stack(variants, 2).reshape(DEPTH, H_C // 2, 2, len(NA_VARIANTS), GRID_W, NA_SPAN_ROWS * GRID_W)
    return tab.transpose(0, 1, 3, 2, 4, 5).reshape(DEPTH, H_C // 2, len(NA_VARIANTS), 2 * GRID_W, NA_SPAN_ROWS * GRID_W)


def _gate_layouts(zg, nb, seq, row0):
    nc = seq // LC_K
    g = zg[row0:row0 + nb * seq, :4 * H_A].reshape(nb, seq, 2, 2, H_A)
    g = g.transpose(0, 4, 1, 2, 3).reshape(nb, H_A, nc, LC_K, 4)
    return g, g.transpose(0, 1, 2, 4, 3)


def _keys_t(za, nb, seq, row0):
    k = za[row0:row0 + nb * seq, H_A * DP_A:2 * H_A * DP_A].reshape(nb, seq // LC_K, LC_K, H_A, DP_A)
    return k.transpose(0, 3, 1, 4, 2)


def _dispatch_plan(route, counts):
    nblk = 2 * T_ALL // MOE_BLK + N_EXPERTS
    cnt = counts[0, :N_EXPERTS].astype(jnp.int32)
    padded = (cnt + MOE_BLK - 1) // MOE_BLK * MOE_BLK
    p_end = jnp.cumsum(padded)
    p_start = p_end - padded
    e = route[:, 0:2].astype(jnp.int32)
    pos = p_start[e] + route[:, 4:6].astype(jnp.int32)
    pos2 = pos.T.reshape(-1)
    blk_first = jnp.arange(nblk, dtype=jnp.int32) * MOE_BLK
    blk_e = jnp.minimum(jnp.sum((p_end[None, :] <= blk_first[:, None]).astype(jnp.int32), axis=1), N_EXPERTS - 1)
    nused = p_end[-1:] // MOE_BLK
    return blk_e, nused, pos2, p_end, nblk


def kernel(x_prompt, x_sample, state_a_C, state_a_n, state_a_m, cache_b_k, cache_b_v, cache_c_k, cache_c_v, c, c_ctx, w_in, b_a_i, b_a_f, w_a_hnorm, b_sink, rpb, w_out, w_ada, b_ada, ln_g, ln_b, w_router_grp, w_router_exp, w_e_gate, w_e_up, w_e_down):
    w_in_p = _prep_w_in(w_in)
    w_out_p = _prep_w_out(w_out)
    w_r = jnp.pad(jnp.concatenate([w_router_exp, w_router_grp], -1),
                  ((0, 0), (0, 0), (0, LANES - N_EXPERTS - N_GROUPS)))
    wn_p = jnp.pad(w_a_hnorm.reshape(DEPTH, H_A, 1, DH_A), ((0, 0), (0, 0), (0, 0), (0, DP_A - DH_A)))
    gate_b = jnp.concatenate([b_a_i, b_a_f], 1).transpose(0, 2, 1)
    sink_p = jnp.pad(b_sink, ((0, 0), (0, 8 - H_B))).reshape(DEPTH, 1, 8)
    cos_t, sin_t = _rope_tables()
    na_bias = _na_bias_tables(rpb)
    cb_k = cache_b_k.reshape(B_LAT, DEPTH, PAST_LEN, D_KVB)
    cb_v = cache_b_v.reshape(B_LAT, DEPTH, PAST_LEN, D_KVB)
    cc_k = cache_c_k.reshape(B_LAT, DEPTH, PAST_LEN, D_C)
    cc_v = cache_c_v.reshape(B_LAT, DEPTH, PAST_LEN, D_C)
    pad_c = ((0, 0), (0, 0), (0, 0), (0, 0), (0, DP_A - DH_A), (0, DP_A - DH_A))
    st_ct = jnp.swapaxes(jnp.pad(state_a_C, pad_c), -1, -2)
    st_nr = jnp.broadcast_to(jnp.pad(state_a_n, pad_c[:-1])[..., None], st_ct.shape)
    st_s = jnp.concatenate([st_ct, st_nr], -1)
    st_m = state_a_m[..., None, None]
    z_s = jnp.zeros((B_CTX, 2, H_A, DP_A, 2 * DP_A), F32)
    z_m = jnp.zeros((B_CTX, 2, H_A, 1, 1), F32)

    cvec = jnp.concatenate([c, c_ctx[None, :], jnp.zeros((3, D_MODEL), F32)], 0)
    mod = _ada_call(cvec, w_ada, b_ada)
    tile_row = np.concatenate([np.full(T_CTX // TM, B_LAT), np.repeat(np.arange(B_LAT), N_LAT // TM)])
    mod_t = mod[:, tile_row].reshape(DEPTH, N_TILES, 1, 6, D_MODEL)

    x = jnp.concatenate([x_prompt.reshape(T_CTX, D_MODEL), x_sample.reshape(T_LAT, D_MODEL)], 0)
    cs_, ns_, ms_, kbs, vbs, kcs, vcs = [], [], [], [], [], [], []
    for l in range(DEPTH):
        sh_a, sc_a, g_a, sh_f, sc_f, g_f = (mod_t[l, :, :, j] for j in range(6))
        za, bq, bk, bv, cq, ck, cv, zg = _inproj_call(x, sc_a, sh_a, w_in_p[l])
        bcol = gate_b[l].reshape(H_A, 1, 4)
        brow = gate_b[l].reshape(H_A, 4, 1)
        gc, gr = _gate_layouts(zg, B_CTX, L_CTX, 0)
        ya_c, s_l, m_l = _mlstm_call(za, _keys_t(za, B_CTX, L_CTX, 0), gc, gr, bcol, brow, wn_p[l], z_s, z_m,
                                     nb=B_CTX, seq=L_CTX, row_blk0=0, hp=H_A)
        yb_c, yc_c = _ctx_attn_call(bq, bk, bv, cq, ck, cv, sink_p[l])
        gc, gr = _gate_layouts(zg, B_LAT, N_LAT, T_CTX)
        ya_l, _, _ = _mlstm_call(za, _keys_t(za, B_LAT, N_LAT, T_CTX), gc, gr, bcol, brow, wn_p[l],
                                 st_s[:, l], st_m[:, l],
                                 nb=B_LAT, seq=N_LAT, row_blk0=T_CTX // N_LAT, hp=H_A // 2)
        yb_l = _win_attn_call(bq, bk, bv, cb_k, cb_v, cos_t, sin_t, sink_p[l], l)
        yc_l = _na_attn_call(cq, ck, cv, cc_k, cc_v, na_bias[l], l)
        x1, h2, route, counts = _outproj_call((ya_c, yb_c, yc_c), (ya_l, yb_l, yc_l), x, g_a, sc_f, sh_f,
                                              ln_g[l, 0:1], ln_b[l, 0:1], w_out_p[l], w_r[l])
        blk_e, nused, pos2, p_end, nblk = _dispatch_plan(route, counts)
        xs = _dispatch_call(pos2, p_end, h2, nblk)
        eo = _expert_call(blk_e, nused, xs, w_e_gate[l], w_e_up[l], w_e_down[l], nblk)
        x = _combine_call(pos2, eo, x1, route, g_f, ln_g[l, 1:2], ln_b[l, 1:2])
        cs_.append(jnp.swapaxes(s_l[..., :DH_A, :DH_A], -1, -2))
        ns_.append(s_l[..., :DH_A, DP_A])
        ms_.append(m_l.reshape(B_CTX, 2, H_A))
        kbs.append(bk[:T_CTX].reshape(B_CTX, L_CTX, KV_B, HD))
        vbs.append(bv[:T_CTX].reshape(B_CTX, L_CTX, KV_B, HD))
        kcs.append(ck[:T_CTX].reshape(B_CTX, L_CTX, H_C, HD))
        vcs.append(cv[:T_CTX].reshape(B_CTX, L_CTX, H_C, HD))
    y_prompt = x[:T_CTX].reshape(B_CTX, L_CTX, D_MODEL)
    y_sample = x[T_CTX:].reshape(B_LAT, N_LAT, D_MODEL)
    return (y_prompt, y_sample, jnp.stack(cs_, 1), jnp.stack(ns_, 1), jnp.stack(ms_, 1),
            jnp.stack(kbs, 1), jnp.stack(vbs, 1), jnp.stack(kcs, 1), jnp.stack(vcs, 1))
```

```python
import functools

import numpy as np
import jax
import jax.numpy as jnp
from jax import lax
from jax.experimental import pallas as pl
from jax.experimental.pallas import tpu as pltpu

F32 = jnp.float32
BF16 = jnp.bfloat16
NEG_INF = float("-inf")

D_MODEL = 1024
DEPTH = 4
B_CTX, L_CTX = 16, 256
B_LAT, N_LAT = 4, 2048
PAST_LEN = 512
GRID_W = 64
H_A, DH_A = 4, 96
D_A = H_A * DH_A
H_B, KV_B, HD = 6, 2, 64
G_B = H_B // KV_B
D_B, D_KVB = H_B * HD, KV_B * HD
WIN = 128
ROPE_BASE = 10000.0
H_C = 4
D_C = H_C * HD
NA_WIN_H, NA_WIN_W = 8, 16
N_GROUPS, E_PER_GROUP = 4, 8
N_EXPERTS = N_GROUPS * E_PER_GROUP
D_EXPERT = D_MODEL // 4
ALPHA = (2 * DEPTH) ** 0.25
LN_EPS = 1e-5

LANES = 128
DP_A = LANES
LC_K = LANES
TM = 256
MOE_BLK = 256
VMEM_LIMIT = 48 * 1024 * 1024

T_CTX = B_CTX * L_CTX
T_LAT = B_LAT * N_LAT
T_ALL = T_CTX + T_LAT
N_TILES = T_ALL // TM
ZA_COLS = 4 * H_A * DP_A
Y_A_COLS = H_A * DP_A
IN_SPLITS = (ZA_COLS, D_B, D_KVB, D_KVB, D_C, D_C, D_C, 2 * LANES)
IN_COLS = sum(IN_SPLITS)


def _cparams(*sem):
    return pltpu.CompilerParams(dimension_semantics=sem, vmem_limit_bytes=VMEM_LIMIT)


def _dot(a, b):
    return jnp.dot(a.astype(BF16), b.astype(BF16), preferred_element_type=F32)


def _dot_nt(a, b):
    return lax.dot_general(a.astype(BF16), b.astype(BF16), (((1,), (1,)), ((), ())), preferred_element_type=F32)


def _dot_tn(a, b):
    return lax.dot_general(a.astype(BF16), b.astype(BF16), (((0,), (0,)), ((), ())), preferred_element_type=F32)


def _layer_norm(v, g, b):
    mu = jnp.mean(v, -1, keepdims=True)
    var = jnp.mean(jnp.square(v - mu), -1, keepdims=True)
    return (v - mu) * lax.rsqrt(var + LN_EPS) * g + b


def _ada_kernel(c_ref, w_ref, b_ref, o_ref):
    s = jax.nn.silu(c_ref[...])
    o_ref[0] = _dot(s, w_ref[0]) + b_ref[0]


def _ada_call(cvec, w_ada, b_ada):
    nb = 6
    return pl.pallas_call(
        _ada_kernel,
        grid=(DEPTH, nb),
        in_specs=[
            pl.BlockSpec((8, D_MODEL), lambda l, j: (0, 0)),
            pl.BlockSpec((1, D_MODEL, D_MODEL), lambda l, j: (l, 0, j)),
            pl.BlockSpec((1, 1, D_MODEL), lambda l, j: (l, 0, j)),
        ],
        out_specs=pl.BlockSpec((1, 8, D_MODEL), lambda l, j: (l, 0, j)),
        out_shape=jax.ShapeDtypeStruct((DEPTH, 8, 6 * D_MODEL), F32),
        compiler_params=_cparams("arbitrary", "arbitrary"),
        name="adaln",
    )(cvec, w_ada, b_ada.reshape(DEPTH, 1, 6 * D_MODEL))


def _inproj_kernel(x_ref, sc_ref, sh_ref, w_ref, *out_refs):
    h = (x_ref[...] * (1.0 + sc_ref[0, 0, 0]) + sh_ref[0, 0, 0]).astype(BF16)
    off = 0
    for ref in out_refs:
        n = ref.shape[-1]
        ref[...] = jnp.dot(h, w_ref[:, off:off + n], preferred_element_type=F32)
        off += n


def _mod_spec(layer, which):
    return pl.BlockSpec((1, 1, 1, 1, D_MODEL), lambda i, *_: (layer, i, which, 0, 0))


def _inproj_call(x, mod_t, layer, w):
    return pl.pallas_call(
        _inproj_kernel,
        grid=(N_TILES,),
        in_specs=[
            pl.BlockSpec((TM, D_MODEL), lambda i: (i, 0)),
            _mod_spec(layer, 1), _mod_spec(layer, 0),
            pl.BlockSpec((D_MODEL, IN_COLS), lambda i: (0, 0)),
        ],
        out_specs=[pl.BlockSpec((TM, n), lambda i: (i, 0)) for n in IN_SPLITS],
        out_shape=[jax.ShapeDtypeStruct((T_ALL, n), F32) for n in IN_SPLITS],
        compiler_params=_cparams("arbitrary"),
        name="inproj",
    )(x, mod_t, mod_t, w)


HP_A = 2
N_CH = 2 * HP_A


def _mlstm_kernel(q_ref, k_ref, v_ref, o_ref, g_ref, gb_ref, wn_ref, s0_ref, m0_ref,
                  y_ref, so_ref, mo_ref, hf_scr, hb_scr, *, nc):
    lc = LC_K
    scale = DH_A ** -0.5
    chains = [(j, d) for j in range(HP_A) for d in range(2)]
    ti = lax.broadcasted_iota(jnp.int32, (lc, lc), 0)
    si = lax.broadcasted_iota(jnp.int32, (lc, lc), 1)
    lane_ok = lax.broadcasted_iota(jnp.int32, (1, DP_A), 1) < DH_A
    ones = jnp.ones((lc, DP_A), F32)

    def stack(parts):
        return jnp.concatenate(parts, axis=0)

    def rows_of(x, a):
        return a[x * lc:(x + 1) * lc]

    def spread(vals):
        return stack([jnp.broadcast_to(v, (lc, v.shape[1])) for v in vals])

    mask = stack([si <= ti if d == 0 else si >= ti for _, d in chains])
    mask_t = stack([si >= ti if d == 0 else si <= ti for _, d in chains])
    eye = stack([si == ti for _ in chains])

    def col_sums(a):
        return [jnp.sum(rows_of(x, a), axis=0, keepdims=True) for x in range(N_CH)]

    def body(i, carry):
        smats = carry[:N_CH]
        ms = carry[N_CH:]
        r0s = [pl.multiple_of((i if d == 0 else nc - 1 - i) * lc, lc) for _, d in chains]
        q, kt, v1, icol, fpre = [], [], [], [], []
        for (j, d), r0 in zip(chains, r0s):
            cols = slice(j * DP_A, (j + 1) * DP_A)
            q.append(q_ref[pl.ds(r0, lc), cols].astype(BF16))
            kt.append((k_ref[pl.ds(r0, lc), cols] * scale).T.astype(BF16))
            v1.append(jnp.concatenate([v_ref[pl.ds(r0, lc), cols], ones], axis=1))
            gz = g_ref[pl.ds(r0, lc), :] + gb_ref[0]
            icol.append(gz[:, 4 * j + d:4 * j + d + 1])
            fpre.append(gz[:, 4 * j + 2 + d:4 * j + 3 + d])
        i_col = stack(icol)
        f_col = jax.nn.log_sigmoid(stack(fpre))
        b_rows = col_sums(jnp.where(mask_t, f_col, 0.0))
        bls = col_sums(f_col)
        b_col = jnp.sum(jnp.where(eye, spread(b_rows), 0.0), axis=1, keepdims=True)
        a_rows = col_sums(jnp.where(eye, i_col - b_col, 0.0))
        dmat = jnp.where(mask, b_col + spread(a_rows), NEG_INF)
        inter = b_col + spread(ms)
        m_out = jnp.maximum(inter, jnp.max(dmat, axis=1, keepdims=True))
        wmat = jnp.exp(dmat - m_out)
        sw = (stack([jnp.dot(q[x], kt[x], preferred_element_type=F32) for x in range(N_CH)]) * wmat).astype(BF16)
        sc_in = jnp.exp(inter - m_out)
        tot = (stack([jnp.dot(rows_of(x, sw), v1[x].astype(BF16), preferred_element_type=F32) for x in range(N_CH)])
               + sc_in * stack([_dot(q[x], smats[x]) for x in range(N_CH)]))
        h = tot[:, :DP_A] / jnp.maximum(jnp.abs(tot[:, DP_A:]), jnp.exp(-m_out))
        for x, ((j, d), r0) in enumerate(zip(chains, r0s)):
            dst = hf_scr if d == 0 else hb_scr
            dst[pl.ds(r0, lc), j * DP_A:(j + 1) * DP_A] = rows_of(x, h)
        dec = spread(bls) - b_col + i_col
        m_new = [jnp.maximum(bls[x] + ms[x], jnp.max(rows_of(x, dec), axis=0, keepdims=True)) for x in range(N_CH)]
        wk = jnp.exp(dec - spread(m_new))
        s_new = [jnp.exp(bls[x] + ms[x] - m_new[x]) * smats[x] + _dot(kt[x], rows_of(x, wk) * v1[x])
                 for x in range(N_CH)]
        return tuple(s_new) + tuple(m_new)

    init = tuple(s0_ref[0, d, j] for j, d in chains) + tuple(m0_ref[0, d, j] for j, d in chains)
    final = lax.fori_loop(0, nc, body, init)
    for x, (j, d) in enumerate(chains):
        so_ref[0, d, j] = final[x]
        mo_ref[0, d, j] = final[N_CH + x]

    def finish(c, _):
        r0 = pl.multiple_of(c * lc, lc)
        for j in range(HP_A):
            cols = slice(j * DP_A, (j + 1) * DP_A)
            h = hf_scr[pl.ds(r0, lc), cols] + hb_scr[pl.ds(r0, lc), cols]
            mu = jnp.sum(h, axis=1, keepdims=True) * (1.0 / DH_A)
            dv = jnp.where(lane_ok, h - mu, 0.0)
            var = jnp.sum(dv * dv, axis=1, keepdims=True) * (1.0 / DH_A)
            hn = dv * lax.rsqrt(var + LN_EPS) * wn_ref[j]
            y_ref[pl.ds(r0, lc), cols] = jax.nn.sigmoid(o_ref[pl.ds(r0, lc), cols]) * hn
        return 0

    lax.fori_loop(0, nc, finish, 0)


def _mlstm_call(za, zg, gbias, wn, s0, m0, *, nb, seq, row_blk0):
    nc = seq // LC_K
    ng = H_A // HP_A
    w = HP_A * DP_A

    def zspec(part):
        return pl.BlockSpec((seq, w), lambda b, g: (row_blk0 + b, part * ng + g))

    def state(*tail):
        return pl.BlockSpec((1, 2, HP_A) + tail, lambda b, g: (b, 0, g) + (0,) * len(tail))

    return pl.pallas_call(
        functools.partial(_mlstm_kernel, nc=nc),
        grid=(nb, ng),
        in_specs=[
            zspec(0), zspec(1), zspec(2), zspec(3),
            pl.BlockSpec((seq, LANES), lambda b, g: (row_blk0 + b, g)),
            pl.BlockSpec((1, 1, LANES), lambda b, g: (g, 0, 0)),
            pl.BlockSpec((HP_A, 1, DP_A), lambda b, g: (g, 0, 0)),
            state(DP_A, 2 * DP_A), state(1, 1),
        ],
        out_specs=[pl.BlockSpec((seq, w), lambda b, g: (b, g)), state(DP_A, 2 * DP_A), state(1, 1)],
        out_shape=[
            jax.ShapeDtypeStruct((nb * seq, Y_A_COLS), F32),
            jax.ShapeDtypeStruct((nb, 2, H_A, DP_A, 2 * DP_A), F32),
            jax.ShapeDtypeStruct((nb, 2, H_A, 1, 1), F32),
        ],
        scratch_shapes=[pltpu.VMEM((seq, w), F32), pltpu.VMEM((seq, w), F32)],
        compiler_params=_cparams("arbitrary", "arbitrary"),
        name="mlstm",
    )(za, za, za, za, zg, gbias, wn, s0, m0)


def _pair_attention(qp, kslabs, vaugs, masks, sink_col):
    m_rows = qp.shape[0]
    lo = lax.broadcasted_iota(jnp.int32, qp.shape, 1) < HD
    q2 = jnp.concatenate([jnp.where(lo, qp, 0.0), jnp.where(lo, 0.0, qp)], axis=0).astype(BF16)
    scores = []
    for ks, mk in zip(kslabs, masks):
        s = jnp.dot(q2, ks, preferred_element_type=F32)
        if mk is not None:
            s = jnp.where(mk, s, NEG_INF) if mk.dtype == jnp.bool_ else s + mk
        scores.append(s)
    mx = scores[0].max(axis=1, keepdims=True)
    for s in scores[1:]:
        mx = jnp.maximum(mx, s.max(axis=1, keepdims=True))
    if sink_col is not None:
        mx = jnp.maximum(mx, sink_col)
    acc = None
    for s, va in zip(scores, vaugs):
        pv = jnp.dot(jnp.exp(s - mx).astype(BF16), va, preferred_element_type=F32)
        acc = pv if acc is None else acc + pv
    den = acc[:, LANES:]
    if sink_col is not None:
        den = den + jnp.exp(sink_col - mx)
    o = acc[:, :LANES] / den
    return jnp.where(lo, o[:m_rows], o[m_rows:])


def _gqa_key_slabs(kt):
    a, b = kt[:HD], kt[HD:]
    return [jnp.concatenate([a, a], 0), kt, jnp.concatenate([b, b], 0)]


def _gqa_value_pairs(v):
    lo = lax.broadcasted_iota(jnp.int32, v.shape, 1) < HD
    sw = pltpu.roll(v, HD, 1)
    ones = jnp.ones_like(v)
    return [jnp.concatenate([x, ones], 1) for x in (jnp.where(lo, v, sw), v, jnp.where(lo, sw, v))]


def _sink_col(sink_ref, p, m_rows):
    row = lax.broadcasted_iota(jnp.int32, (2 * m_rows, 1), 0)
    return jnp.where(row < m_rows, sink_ref[0:1, 2 * p:2 * p + 1], sink_ref[0:1, 2 * p + 1:2 * p + 2])


def _ctx_attn_kernel(bq_ref, bk_ref, bv_ref, cq_ref, ck_ref, cv_ref, sink_ref, yb_ref, yc_ref):
    scale = HD ** -0.5
    kslabs = _gqa_key_slabs(bk_ref[...].T)
    vpairs = _gqa_value_pairs(bv_ref[...])
    for p in range(D_B // LANES):
        cols = slice(p * LANES, (p + 1) * LANES)
        yb_ref[:, cols] = _pair_attention(bq_ref[:, cols] * scale, [kslabs[p].astype(BF16)],
                                          [vpairs[p].astype(BF16)], [None], _sink_col(sink_ref, p, L_CTX))
    ckt = ck_ref[...].T
    ones = jnp.ones((L_CTX, LANES), F32)
    for p in range(D_C // LANES):
        cols = slice(p * LANES, (p + 1) * LANES)
        va = jnp.concatenate([cv_ref[:, cols], ones], 1)
        yc_ref[:, cols] = _pair_attention(cq_ref[:, cols] * scale, [ckt[cols].astype(BF16)], [va.astype(BF16)],
                                          [None], None)


def _ctx_attn_call(bq, bk, bv, cq, ck, cv, sink):
    def spec(n):
        return pl.BlockSpec((L_CTX, n), lambda b: (b, 0))

    return pl.pallas_call(
        _ctx_attn_kernel,
        grid=(B_CTX,),
        in_specs=[spec(D_B), spec(D_KVB), spec(D_KVB), spec(D_C), spec(D_C), spec(D_C),
                  pl.BlockSpec((1, 8), lambda b: (0, 0))],
        out_specs=[spec(D_B), spec(D_C)],
        out_shape=[jax.ShapeDtypeStruct((T_CTX, D_B), F32), jax.ShapeDtypeStruct((T_CTX, D_C), F32)],
        compiler_params=_cparams("arbitrary"),
        name="ctx_attn",
    )(bq, bk, bv, cq, ck, cv, sink)


def _rope(x, cos, sin_signed, first):
    rot = jnp.where(first, pltpu.roll(x, LANES - 16, 1), pltpu.roll(x, 16, 1))
    return x * cos + rot * sin_signed


def _win_attn_kernel(q_ref, k_ref, v_ref, kc_ref, vc_ref, cos_ref, sin_ref, sink_ref, y_ref,
                     kpt_scr, va_scr, kcp_scr, vca_scr):
    scale = HD ** -0.5
    n_pairs = D_B // LANES
    nblk = N_LAT // WIN
    nband = 3
    lane = lax.broadcasted_iota(jnp.int32, (1, LANES), 1)
    first = (lane % 32) < 16
    for p, (ks, va) in enumerate(zip(_gqa_key_slabs(kc_ref[0, 0].T), _gqa_value_pairs(vc_ref[0, 0]))):
        kcp_scr[p] = ks.astype(BF16)
        vca_scr[p] = va.astype(BF16)

    def prep(blk, c):
        r0 = pl.multiple_of(blk * WIN, WIN)
        kr = _rope(k_ref[pl.ds(r0, WIN), :], cos_ref[pl.ds(r0, WIN), :], sin_ref[pl.ds(r0, WIN), :], first)
        for p, (ks, va) in enumerate(zip(_gqa_key_slabs(kr.T), _gqa_value_pairs(v_ref[pl.ds(r0, WIN), :]))):
            kpt_scr[p, blk] = ks.astype(BF16)
            va_scr[p, pl.ds(r0, WIN), :] = va.astype(BF16)
        return c

    lax.fori_loop(0, nblk, prep, 0)

    def body(blk, c):
        q0 = pl.multiple_of(blk * WIN, WIN)
        sb = jnp.clip(blk - 1, 0, nblk - nband)
        k0 = pl.multiple_of(sb * WIN, WIN)
        cos = cos_ref[pl.ds(q0, WIN), :]
        sin = sin_ref[pl.ds(q0, WIN), :]
        row = lax.broadcasted_iota(jnp.int32, (2 * WIN, nband * WIN), 0)
        qpos = q0 + jnp.where(row < WIN, row, row - WIN)
        kpos = k0 + lax.broadcasted_iota(jnp.int32, (2 * WIN, nband * WIN), 1)
        mask = jnp.abs(kpos - qpos) <= WIN
        for p in range(n_pairs):
            cols = slice(p * LANES, (p + 1) * LANES)
            qp = _rope(q_ref[pl.ds(q0, WIN), cols], cos, sin, first) * scale
            k_loc = jnp.concatenate([kpt_scr[p, sb + j] for j in range(nband)], axis=1)
            v_loc = va_scr[p, pl.ds(k0, nband * WIN), :]
            y_ref[pl.ds(q0, WIN), cols] = _pair_attention(qp, [kcp_scr[p], k_loc], [vca_scr[p], v_loc],
                                                          [None, mask], _sink_col(sink_ref, p, WIN))
        return c

    lax.fori_loop(0, nblk, body, 0)


def _win_attn_call(bq, bk, bv, cache_k, cache_v, cos, sin, sink, layer):
    rb0 = T_CTX // N_LAT

    def spec(n):
        return pl.BlockSpec((N_LAT, n), lambda b: (rb0 + b, 0))

    cache = pl.BlockSpec((1, 1, PAST_LEN, D_KVB), lambda b: (b, layer, 0, 0))
    tab = pl.BlockSpec((N_LAT, LANES), lambda b: (0, 0))
    return pl.pallas_call(
        _win_attn_kernel,
        grid=(B_LAT,),
        in_specs=[spec(D_B), spec(D_KVB), spec(D_KVB), cache, cache, tab, tab,
                  pl.BlockSpec((1, 8), lambda b: (0, 0))],
        out_specs=pl.BlockSpec((N_LAT, D_B), lambda b: (b, 0)),
        out_shape=jax.ShapeDtypeStruct((T_LAT, D_B), F32),
        scratch_shapes=[pltpu.VMEM((D_B // LANES, N_LAT // WIN, LANES, WIN), BF16),
                        pltpu.VMEM((D_B // LANES, N_LAT, 2 * LANES), BF16),
                        pltpu.VMEM((D_B // LANES, LANES, PAST_LEN), BF16),
                        pltpu.VMEM((D_B // LANES, PAST_LEN, 2 * LANES), BF16)],
        compiler_params=_cparams("arbitrary"),
        name="win_attn",
    )(bq, bk, bv, cache_k, cache_v, cos, sin, sink)


NA_BLK = LANES
NA_SPAN_BLKS = NA_WIN_H * GRID_W // NA_BLK + 1
NA_SPAN_ROWS = NA_SPAN_BLKS * NA_BLK // GRID_W
NA_VARIANTS = ((7, 0), (6, 0), (5, 0), (4, 0), (3, 0), (3, 1), (3, 2), (2, 2), (1, 2), (0, 2))


def _na_attn_kernel(q_ref, k_ref, v_ref, kc_ref, vc_ref, bias_ref, y_ref, kpt_scr, va_scr, kcp_scr, vca_scr):
    scale = HD ** -0.5
    rows = N_LAT // GRID_W
    n_pairs = D_C // LANES
    nblk = N_LAT // NA_BLK
    half = NA_WIN_H // 2
    kct = kc_ref[0, 0].T
    for p in range(n_pairs):
        cols = slice(p * LANES, (p + 1) * LANES)
        kcp_scr[p] = kct[cols].astype(BF16)
        vca_scr[p] = jnp.concatenate([vc_ref[0, 0, :, cols], jnp.ones((PAST_LEN, LANES), F32)], 1).astype(BF16)

    def prep(blk, c):
        r0 = pl.multiple_of(blk * NA_BLK, NA_BLK)
        kt = k_ref[pl.ds(r0, NA_BLK), :].T
        for p in range(n_pairs):
            cols = slice(p * LANES, (p + 1) * LANES)
            kpt_scr[p, blk] = kt[cols].astype(BF16)
            va_scr[p, pl.ds(r0, NA_BLK), :] = jnp.concatenate(
                [v_ref[pl.ds(r0, NA_BLK), cols], jnp.ones((NA_BLK, LANES), F32)], 1).astype(BF16)
        return c

    lax.fori_loop(0, nblk, prep, 0)

    def body(r, c):
        kr0 = jnp.clip(r - half, 0, rows - NA_WIN_H)
        sb = jnp.minimum(kr0 // 2, nblk - NA_SPAN_BLKS)
        var = jnp.where(r <= half, r, jnp.where(r >= rows - half, r - (rows - 2 * half - 2), half + (kr0 & 1)))
        q0 = pl.multiple_of(r * GRID_W, GRID_W)
        k0 = pl.multiple_of(sb * NA_BLK, NA_BLK)
        for p in range(n_pairs):
            cols = slice(p * LANES, (p + 1) * LANES)
            k_loc = jnp.concatenate([kpt_scr[p, sb + j] for j in range(NA_SPAN_BLKS)], axis=1)
            v_loc = va_scr[p, pl.ds(k0, NA_SPAN_BLKS * NA_BLK), :]
            y_ref[pl.ds(q0, GRID_W), cols] = _pair_attention(
                q_ref[pl.ds(q0, GRID_W), cols] * scale, [kcp_scr[p], k_loc], [vca_scr[p], v_loc],
                [None, bias_ref[p, var]], None)
        return c

    lax.fori_loop(0, rows, body, 0)


def _na_attn_call(cq, ck, cv, cache_k, cache_v, bias_tab, layer):
    rb0 = T_CTX // N_LAT
    n_pairs = D_C // LANES
    spec = pl.BlockSpec((N_LAT, D_C), lambda b: (rb0 + b, 0))
    cache = pl.BlockSpec((1, 1, PAST_LEN, D_C), lambda b: (b, layer, 0, 0))
    span = NA_SPAN_BLKS * NA_BLK
    return pl.pallas_call(
        _na_attn_kernel,
        grid=(B_LAT,),
        in_specs=[spec, spec, spec, cache, cache,
                  pl.BlockSpec((n_pairs, len(NA_VARIANTS), 2 * GRID_W, span), lambda b: (0, 0, 0, 0))],
        out_specs=pl.BlockSpec((N_LAT, D_C), lambda b: (b, 0)),
        out_shape=jax.ShapeDtypeStruct((T_LAT, D_C), F32),
        scratch_shapes=[pltpu.VMEM((n_pairs, N_LAT // NA_BLK, LANES, NA_BLK), BF16),
                        pltpu.VMEM((n_pairs, N_LAT, 2 * LANES), BF16),
                        pltpu.VMEM((n_pairs, LANES, PAST_LEN), BF16),
                        pltpu.VMEM((n_pairs, PAST_LEN, 2 * LANES), BF16)],
        compiler_params=_cparams("arbitrary"),
        name="na_attn",
    )(cq, ck, cv, cache_k, cache_v, bias_tab)


def _outproj_kernel(yac_ref, ybc_ref, ycc_ref, yal_ref, ybl_ref, ycl_ref, x_ref, ga_ref, scf_ref, shf_ref,
                    lng_ref, lnb_ref, wo_ref, wr_ref, x1_ref, h2_ref, route_ref, cnt_ref, run_scr):
    @pl.when(pl.program_id(0) == 0)
    def _():
        run_scr[...] = jnp.zeros_like(run_scr)

    is_ctx = pl.program_id(0) < T_CTX // TM

    def pick(c_ref, l_ref):
        return jnp.where(is_ctx, c_ref[...], l_ref[...])

    y = (_dot(pick(yac_ref, yal_ref), wo_ref[0:Y_A_COLS, :])
         + _dot(pick(ybc_ref, ybl_ref), wo_ref[Y_A_COLS:Y_A_COLS + D_B, :])
         + _dot(pick(ycc_ref, ycl_ref), wo_ref[Y_A_COLS + D_B:, :]))
    x1 = _layer_norm(ALPHA * x_ref[...] + ga_ref[0, 0, 0] * y, lng_ref[...], lnb_ref[...])
    x1_ref[...] = x1
    h2 = x1 * (1.0 + scf_ref[0, 0, 0]) + shf_ref[0, 0, 0]
    h2_ref[...] = h2
    w_r = wr_ref[...]
    w_hi = w_r.astype(BF16)
    w_lo = (w_r - w_hi.astype(F32)).astype(BF16)
    h_hi = h2.astype(BF16)
    h_lo = (h2 - h_hi.astype(F32)).astype(BF16)
    p_hi = jnp.dot(h_hi, jnp.concatenate([w_hi, w_lo], axis=1), preferred_element_type=F32)
    logits = p_hi[:, :LANES] + p_hi[:, LANES:] + jnp.dot(h_lo, w_hi, preferred_element_type=F32)
    lane = lax.broadcasted_iota(jnp.int32, logits.shape, 1)
    lanef = lane.astype(F32)
    big = float(LANES)
    lg = jnp.where((lane >= N_EXPERTS) & (lane < N_EXPERTS + N_GROUPS), logits, NEG_INF)
    mg = jnp.max(lg, axis=1, keepdims=True)
    grp = jnp.min(jnp.where(lg == mg, lanef, big), axis=1, keepdims=True) - float(N_EXPERTS)
    g_w = 1.0 / jnp.sum(jnp.exp(lg - mg), axis=1, keepdims=True)
    in_grp = (lane < N_EXPERTS) & ((lane // E_PER_GROUP).astype(F32) == grp)
    le = jnp.where(in_grp, logits, NEG_INF)
    l1 = jnp.max(le, axis=1, keepdims=True)
    i1 = jnp.min(jnp.where(le == l1, lanef, big), axis=1, keepdims=True)
    le2 = jnp.where(lanef == i1, NEG_INF, le)
    l2 = jnp.max(le2, axis=1, keepdims=True)
    i2 = jnp.min(jnp.where(le2 == l2, lanef, big), axis=1, keepdims=True)
    e2 = jnp.exp(l2 - l1)
    w1 = g_w / (1.0 + e2)
    w2 = g_w * e2 / (1.0 + e2)
    oh1 = jnp.where(lanef == i1, 1.0, 0.0)
    oh2 = jnp.where(lanef == i2, 1.0, 0.0)
    rt = lax.broadcasted_iota(jnp.int32, (TM, TM), 0)
    ct = lax.broadcasted_iota(jnp.int32, (TM, TM), 1)
    before = jnp.where(ct < rt, 1.0, 0.0)
    run = run_scr[...]
    tot1 = jnp.sum(oh1, axis=0, keepdims=True)
    r1 = jnp.sum(oh1 * (run + _dot(before, oh1)), axis=1, keepdims=True)
    r2 = jnp.sum(oh2 * (run + tot1 + _dot(before, oh2)), axis=1, keepdims=True)
    run = run + tot1 + jnp.sum(oh2, axis=0, keepdims=True)
    run_scr[...] = run
    cnt_ref[...] = run
    vals = (i1, i2, w1, w2, r1, r2)
    out = jnp.zeros_like(logits)
    for n, v in enumerate(vals):
        out = jnp.where(lane == n, v, out)
    route_ref[...] = out[:, :8]


def _outproj_call(y_ctx, y_lat, x, mod_t, layer, lng, lnb, wo, wr):
    row_vec = pl.BlockSpec((1, D_MODEL), lambda i: (0, 0))
    n_ctx = T_CTX // TM

    def tok(n):
        return pl.BlockSpec((TM, n), lambda i: (i, 0))

    def tok_ctx(n):
        return pl.BlockSpec((TM, n), lambda i: (jnp.minimum(i, n_ctx - 1), 0))

    def tok_lat(n):
        return pl.BlockSpec((TM, n), lambda i: (jnp.maximum(i - n_ctx, 0), 0))

    return pl.pallas_call(
        _outproj_kernel,
        grid=(N_TILES,),
        in_specs=[tok_ctx(Y_A_COLS), tok_ctx(D_B), tok_ctx(D_C), tok_lat(Y_A_COLS), tok_lat(D_B), tok_lat(D_C),
                  tok(D_MODEL), _mod_spec(layer, 2), _mod_spec(layer, 4), _mod_spec(layer, 3), row_vec, row_vec,
                  pl.BlockSpec((Y_A_COLS + D_B + D_C, D_MODEL), lambda i: (0, 0)),
                  pl.BlockSpec((D_MODEL, LANES), lambda i: (0, 0))],
        out_specs=[tok(D_MODEL), tok(D_MODEL), tok(8), pl.BlockSpec((1, LANES), lambda i: (0, 0))],
        out_shape=[jax.ShapeDtypeStruct((T_ALL, D_MODEL), F32), jax.ShapeDtypeStruct((T_ALL, D_MODEL), F32),
                   jax.ShapeDtypeStruct((T_ALL, 8), F32), jax.ShapeDtypeStruct((1, LANES), F32)],
        scratch_shapes=[pltpu.VMEM((1, LANES), F32)],
        compiler_params=_cparams("arbitrary"),
        name="outproj_router",
    )(*y_ctx, *y_lat, x, mod_t, mod_t, mod_t, lng, lnb, wo, wr)


def _dispatch_kernel(pos_ref, pend_ref, h_ref, xs_hbm, zero_scr, sem):
    base = pl.program_id(0) * TM

    @pl.when(pl.program_id(0) == 0)
    def _():
        zero_scr[...] = jnp.zeros_like(zero_scr)

        def fill(e, op):
            prev = pend_ref[e - 1] if e else 0

            @pl.when(pend_ref[e] > prev)
            def _():
                first = pl.multiple_of(pend_ref[e] - MOE_BLK, MOE_BLK)
                op(pltpu.make_async_copy(zero_scr, xs_hbm.at[pl.ds(first, MOE_BLK), :], sem))

        def tail_copy(b):
            first = pl.multiple_of(b * MOE_BLK, MOE_BLK)
            return pltpu.make_async_copy(zero_scr, xs_hbm.at[pl.ds(first, MOE_BLK), :], sem)

        def tail_start(b, c):
            tail_copy(b).start()
            return c

        def tail_wait(b, c):
            tail_copy(b).wait()
            return c

        n_blocks = xs_hbm.shape[0] // MOE_BLK
        first_free = pend_ref[N_EXPERTS - 1] // MOE_BLK
        for e in range(N_EXPERTS):
            fill(e, lambda cp: cp.start())
        lax.fori_loop(first_free, n_blocks, tail_start, 0)
        for e in range(N_EXPERTS):
            fill(e, lambda cp: cp.wait())
        lax.fori_loop(first_free, n_blocks, tail_wait, 0)

    def row(t, p):
        return pltpu.make_async_copy(h_ref.at[pl.ds(t, 1), :], xs_hbm.at[pl.ds(p, 1), :], sem)

    def issue(t, c):
        row(t, pos_ref[base + t]).start()
        row(t, pos_ref[T_ALL + base + t]).start()
        return c

    lax.fori_loop(0, TM, issue, 0, unroll=8)
    whole = pltpu.make_async_copy(h_ref, xs_hbm.at[pl.ds(0, TM), :], sem)
    whole.wait()
    whole.wait()


def _dispatch_call(pos2, p_end, h2, nblk):
    grid_spec = pltpu.PrefetchScalarGridSpec(
        num_scalar_prefetch=2,
        grid=(N_TILES,),
        in_specs=[pl.BlockSpec((TM, D_MODEL), lambda i, p, pe: (i, 0))],
        out_specs=pl.BlockSpec(memory_space=pl.ANY),
        scratch_shapes=[pltpu.VMEM((MOE_BLK, D_MODEL), F32), pltpu.SemaphoreType.DMA(())],
    )
    return pl.pallas_call(
        _dispatch_kernel,
        grid_spec=grid_spec,
        out_shape=jax.ShapeDtypeStruct((nblk * MOE_BLK, D_MODEL), F32),
        compiler_params=_cparams("arbitrary"),
        name="dispatch",
    )(pos2, p_end, h2)


def _expert_kernel(blk_e_ref, nused_ref, xs_ref, wg_ref, wu_ref, wd_ref, out_ref):
    j = pl.program_id(0)

    @pl.when(j < nused_ref[0])
    def _():
        xb = xs_ref[...].astype(BF16)
        g = jnp.dot(xb, wg_ref[0].astype(BF16), preferred_element_type=F32)
        u = jnp.dot(xb, wu_ref[0].astype(BF16), preferred_element_type=F32)
        out_ref[...] = _dot(jax.nn.silu(g) * u, wd_ref[0])

    @pl.when(j >= nused_ref[0])
    def _():
        out_ref[...] = jnp.zeros_like(out_ref)


def _expert_call(blk_e, nused, xs, wg, wu, wd, nblk):
    def wspec(r, c):
        return pl.BlockSpec((1, r, c), lambda j, be, nu: (be[j], 0, 0))

    grid_spec = pltpu.PrefetchScalarGridSpec(
        num_scalar_prefetch=2,
        grid=(nblk,),
        in_specs=[
            pl.BlockSpec((MOE_BLK, D_MODEL), lambda j, be, nu: (jnp.clip(j, 0, jnp.maximum(nu[0] - 1, 0)), 0)),
            wspec(D_MODEL, D_EXPERT), wspec(D_MODEL, D_EXPERT), wspec(D_EXPERT, D_MODEL),
        ],
        out_specs=pl.BlockSpec((MOE_BLK, D_MODEL), lambda j, be, nu: (j, 0)),
    )
    return pl.pallas_call(
        _expert_kernel,
        grid_spec=grid_spec,
        out_shape=jax.ShapeDtypeStruct((nblk * MOE_BLK, D_MODEL), F32),
        compiler_params=_cparams("arbitrary"),
        name="experts",
    )(blk_e, nused, xs, wg, wu, wd)


def _combine_kernel(pos_ref, eo_hbm, x1_ref, route_ref, gf_ref, lng_ref, lnb_ref, x2_ref, buf, sem):
    base = pl.program_id(0) * TM

    def row(p, r, t):
        return pltpu.make_async_copy(eo_hbm.at[pl.ds(p, 1), :], buf.at[r, pl.ds(t, 1), :], sem)

    def issue(t, c):
        row(pos_ref[base + t], 0, t).start()
        row(pos_ref[T_ALL + base + t], 1, t).start()
        return c

    lax.fori_loop(0, TM, issue, 0, unroll=8)
    for r in range(2):
        pltpu.make_async_copy(eo_hbm.at[pl.ds(0, TM), :], buf.at[r], sem).wait()
    route = route_ref[...]
    y = route[:, 2:3] * buf[0] + route[:, 3:4] * buf[1]
    x2_ref[...] = _layer_norm(ALPHA * x1_ref[...] + gf_ref[0, 0, 0] * y, lng_ref[...], lnb_ref[...])


def _combine_call(pos2, eo, x1, route, mod_t, layer, lng, lnb):
    grid_spec = pltpu.PrefetchScalarGridSpec(
        num_scalar_prefetch=1,
        grid=(N_TILES,),
        in_specs=[
            pl.BlockSpec(memory_space=pl.ANY),
            pl.BlockSpec((TM, D_MODEL), lambda i, p: (i, 0)),
            pl.BlockSpec((TM, 8), lambda i, p: (i, 0)),
            _mod_spec(layer, 5),
            pl.BlockSpec((1, D_MODEL), lambda i, p: (0, 0)),
            pl.BlockSpec((1, D_MODEL), lambda i, p: (0, 0)),
        ],
        out_specs=pl.BlockSpec((TM, D_MODEL), lambda i, p: (i, 0)),
        scratch_shapes=[pltpu.VMEM((2, TM, D_MODEL), F32), pltpu.SemaphoreType.DMA(())],
    )
    return pl.pallas_call(
        _combine_kernel,
        grid_spec=grid_spec,
        out_shape=jax.ShapeDtypeStruct((T_ALL, D_MODEL), F32),
        compiler_params=_cparams("arbitrary"),
        name="combine_norm",
    )(pos2, eo, x1, route, mod_t, lng, lnb)


def _gate_cols(grp):
    return np.array([kind * 2 * H_A + d * H_A + grp * HP_A + j
                     for j in range(HP_A) for kind in range(2) for d in range(2)])


def _prep_w_in(w_in):
    a = w_in[..., :4 * D_A].reshape(DEPTH, D_MODEL, 4, H_A, DH_A)
    a = jnp.pad(a, ((0, 0), (0, 0), (0, 0), (0, 0), (0, DP_A - DH_A))).reshape(DEPTH, D_MODEL, ZA_COLS)
    gates = w_in[..., 4 * D_A:4 * D_A + 4 * H_A]
    g = jnp.concatenate([jnp.pad(gates[..., _gate_cols(grp)], ((0, 0), (0, 0), (0, LANES - 4 * HP_A)))
                         for grp in range(H_A // HP_A)], -1)
    rest = w_in[..., 4 * D_A + 4 * H_A:]
    return jnp.concatenate([a, rest, g], -1).astype(BF16)


def _prep_w_out(w_out):
    a = w_out[:, :D_A].reshape(DEPTH, H_A, DH_A, D_MODEL)
    a = jnp.pad(a, ((0, 0), (0, 0), (0, DP_A - DH_A), (0, 0))).reshape(DEPTH, Y_A_COLS, D_MODEL)
    return jnp.concatenate([a, w_out[:, D_A:]], 1).astype(BF16)


def _rope_tables():
    t = np.arange(N_LAT)
    nf = HD // 4
    inv = ROPE_BASE ** (-np.arange(nf, dtype=np.float32) / nf)
    ar = (t // GRID_W).astype(np.float32)[:, None] * inv
    ac = (t % GRID_W).astype(np.float32)[:, None] * inv
    ang = jnp.asarray(np.concatenate([ar, ar, ac, ac], -1), F32)
    cos, sin = jnp.cos(ang), jnp.sin(ang)
    sign = np.where((np.arange(HD) % 32) < 16, -1.0, 1.0).astype(np.float32)
    reps = LANES // HD
    return jnp.tile(cos, (1, reps)), jnp.tile(sin * sign, (1, reps))


def _na_bias_tables(rpb):
    qcol = np.arange(GRID_W)[:, None]
    kcol = np.arange(GRID_W)[None, :]
    dc = np.clip(kcol - qcol, 1 - NA_WIN_W, NA_WIN_W - 1) + NA_WIN_W - 1
    wstart = np.clip(qcol - NA_WIN_W // 2, 0, GRID_W - NA_WIN_W)
    in_win = (kcol >= wstart) & (kcol < wstart + NA_WIN_W)
    sel = (np.arange(2 * NA_WIN_W - 1)[:, None] == dc.reshape(1, -1)).astype(np.float32)
    cols = jnp.einsum("lhrd,dn->lhrn", rpb, jnp.asarray(sel), precision=lax.Precision.HIGHEST)
    cols = jnp.where(in_win.reshape(-1), cols, NEG_INF).reshape(DEPTH, H_C, 2 * NA_WIN_H - 1, GRID_W, GRID_W)
    outside = jnp.full((DEPTH, H_C, GRID_W, GRID_W), NEG_INF, F32)
    variants = []
    for dr0, off in NA_VARIANTS:
        span_rows = [cols[:, :, dr0 + i - off] if 0 <= i - off < NA_WIN_H else outside for i in range(NA_SPAN_ROWS)]
        variants.append(jnp.stack(span_rows, 3).reshape(DEPTH, H_C, GRID_W, NA_SPAN_ROWS * GRID_W))
    tab = jnp.stack(variants, 2).reshape(DEPTH, H_C // 2, 2, len(NA_VARIANTS), GRID_W, NA_SPAN_ROWS * GRID_W)
    return tab.transpose(0, 1, 3, 2, 4, 5).reshape(DEPTH, H_C // 2, len(NA_VARIANTS), 2 * GRID_W, NA_SPAN_ROWS * GRID_W)


def _dispatch_plan(route, counts):
    nblk = 2 * T_ALL // MOE_BLK + N_EXPERTS
    cnt = counts[0, :N_EXPERTS].astype(jnp.int32)
    padded = (cnt + MOE_BLK - 1) // MOE_BLK * MOE_BLK
    p_end = jnp.cumsum(padded)
    p_start = p_end - padded
    e = route[:, 0:2].astype(jnp.int32)
    pos = p_start[e] + route[:, 4:6].astype(jnp.int32)
    pos2 = pos.T.reshape(-1)
    blk_first = jnp.arange(nblk, dtype=jnp.int32) * MOE_BLK
    blk_e = jnp.minimum(jnp.sum((p_end[None, :] <= blk_first[:, None]).astype(jnp.int32), axis=1), N_EXPERTS - 1)
    nused = p_end[-1:] // MOE_BLK
    return blk_e, nused, pos2, p_end, nblk


def kernel(x_prompt, x_sample, state_a_C, state_a_n, state_a_m, cache_b_k, cache_b_v, cache_c_k, cache_c_v, c, c_ctx, w_in, b_a_i, b_a_f, w_a_hnorm, b_sink, rpb, w_out, w_ada, b_ada, ln_g, ln_b, w_router_grp, w_router_exp, w_e_gate, w_e_up, w_e_down):
    w_in_p = _prep_w_in(w_in)
    w_out_p = _prep_w_out(w_out)
    w_r = jnp.pad(jnp.concatenate([w_router_exp, w_router_grp], -1),
                  ((0, 0), (0, 0), (0, LANES - N_EXPERTS - N_GROUPS)))
    wn_p = jnp.pad(w_a_hnorm.reshape(DEPTH, H_A, 1, DH_A), ((0, 0), (0, 0), (0, 0), (0, DP_A - DH_A)))
    gate_b = jnp.concatenate([b_a_i, b_a_f], 1).transpose(0, 2, 1)
    gate_b = jnp.pad(gate_b.reshape(DEPTH, H_A // HP_A, 1, 4 * HP_A), ((0, 0), (0, 0), (0, 0), (0, LANES - 4 * HP_A)))
    sink_p = jnp.pad(b_sink, ((0, 0), (0, 8 - H_B))).reshape(DEPTH, 1, 8)
    cos_t, sin_t = _rope_tables()
    na_bias = _na_bias_tables(rpb)
    cb_k = cache_b_k.reshape(B_LAT, DEPTH, PAST_LEN, D_KVB)
    cb_v = cache_b_v.reshape(B_LAT, DEPTH, PAST_LEN, D_KVB)
    cc_k = cache_c_k.reshape(B_LAT, DEPTH, PAST_LEN, D_C)
    cc_v = cache_c_v.reshape(B_LAT, DEPTH, PAST_LEN, D_C)
    pad_c = ((0, 0), (0, 0), (0, 0), (0, 0), (0, DP_A - DH_A), (0, DP_A - DH_A))
    st_ct = jnp.swapaxes(jnp.pad(state_a_C, pad_c), -1, -2)
    st_nr = jnp.broadcast_to(jnp.pad(state_a_n, pad_c[:-1])[..., None], st_ct.shape)
    st_s = jnp.concatenate([st_ct, st_nr], -1)
    st_m = state_a_m[..., None, None]
    z_s = jnp.zeros((B_CTX, 2, H_A, DP_A, 2 * DP_A), F32)
    z_m = jnp.zeros((B_CTX, 2, H_A, 1, 1), F32)

    cvec = jnp.concatenate([c, c_ctx[None, :], jnp.zeros((3, D_MODEL), F32)], 0)
    mod = _ada_call(cvec, w_ada, b_ada)
    tile_row = np.concatenate([np.full(T_CTX // TM, B_LAT), np.repeat(np.arange(B_LAT), N_LAT // TM)])
    mod_t = mod[:, tile_row].reshape(DEPTH, N_TILES, 6, 1, D_MODEL)

    x = jnp.concatenate([x_prompt.reshape(T_CTX, D_MODEL), x_sample.reshape(T_LAT, D_MODEL)], 0)
    cs_, ns_, ms_, kbs, vbs, kcs, vcs = [], [], [], [], [], [], []
    for l in range(DEPTH):
        za, bq, bk, bv, cq, ck, cv, zg = _inproj_call(x, mod_t, l, w_in_p[l])
        ya_c, s_l, m_l = _mlstm_call(za, zg, gate_b[l], wn_p[l], z_s, z_m, nb=B_CTX, seq=L_CTX, row_blk0=0)
        yb_c, yc_c = _ctx_attn_call(bq, bk, bv, cq, ck, cv, sink_p[l])
        ya_l, _, _ = _mlstm_call(za, zg, gate_b[l], wn_p[l], st_s[:, l], st_m[:, l],
                                 nb=B_LAT, seq=N_LAT, row_blk0=T_CTX // N_LAT)
        yb_l = _win_attn_call(bq, bk, bv, cb_k, cb_v, cos_t, sin_t, sink_p[l], l)
        yc_l = _na_attn_call(cq, ck, cv, cc_k, cc_v, na_bias[l], l)
        x1, h2, route, counts = _outproj_call((ya_c, yb_c, yc_c), (ya_l, yb_l, yc_l), x, mod_t, l,
                                              ln_g[l, 0:1], ln_b[l, 0:1], w_out_p[l], w_r[l])
        blk_e, nused, pos2, p_end, nblk = _dispatch_plan(route, counts)
        xs = _dispatch_call(pos2, p_end, h2, nblk)
        eo = _expert_call(blk_e, nused, xs, w_e_gate[l], w_e_up[l], w_e_down[l], nblk)
        x = _combine_call(pos2, eo, x1, route, mod_t, l, ln_g[l, 1:2], ln_b[l, 1:2])
        cs_.append(jnp.swapaxes(s_l[..., :DH_A, :DH_A], -1, -2))
        ns_.append(s_l[..., :DH_A, DP_A])
        ms_.append(m_l.reshape(B_CTX, 2, H_A))
        kbs.append(bk[:T_CTX].reshape(B_CTX, L_CTX, KV_B, HD))
        vbs.append(bv[:T_CTX].reshape(B_CTX, L_CTX, KV_B, HD))
        kcs.append(ck[:T_CTX].reshape(B_CTX, L_CTX, H_C, HD))
        vcs.append(cv[:T_CTX].reshape(B_CTX, L_CTX, H_C, HD))
    y_prompt = x[:T_CTX].reshape(B_CTX, L_CTX, D_MODEL)
    y_sample = x[T_CTX:].reshape(B_LAT, N_LAT, D_MODEL)
    return (y_prompt, y_sample, jnp.stack(cs_, 1), jnp.stack(ns_, 1), jnp.stack(ms_, 1),
            jnp.stack(kbs, 1), jnp.stack(vbs, 1), jnp.stack(kcs, 1), jnp.stack(vcs, 1))
```

```python
import functools

import numpy as np
import jax
import jax.numpy as jnp
from jax import lax
from jax.experimental import pallas as pl
from jax.experimental.pallas import tpu as pltpu

F32 = jnp.float32
BF16 = jnp.bfloat16
NEG_INF = float("-inf")

D_MODEL = 1024
DEPTH = 4
B_CTX, L_CTX = 16, 256
B_LAT, N_LAT = 4, 2048
PAST_LEN = 512
GRID_W = 64
H_A, DH_A = 4, 96
D_A = H_A * DH_A
H_B, KV_B, HD = 6, 2, 64
G_B = H_B // KV_B
D_B, D_KVB = H_B * HD, KV_B * HD
WIN = 128
ROPE_BASE = 10000.0
H_C = 4
D_C = H_C * HD
NA_WIN_H, NA_WIN_W = 8, 16
N_GROUPS, E_PER_GROUP = 4, 8
N_EXPERTS = N_GROUPS * E_PER_GROUP
D_EXPERT = D_MODEL // 4
ALPHA = (2 * DEPTH) ** 0.25
LN_EPS = 1e-5

LANES = 128
DP_A = LANES
LC_K = LANES
TM = 256
MOE_BLK = 256
VMEM_LIMIT = 48 * 1024 * 1024

T_CTX = B_CTX * L_CTX
T_LAT = B_LAT * N_LAT
T_ALL = T_CTX + T_LAT
N_TILES = T_ALL // TM
ZA_COLS = 4 * H_A * DP_A
Y_A_COLS = H_A * DP_A
IN_SPLITS = (ZA_COLS, D_B, D_KVB, D_KVB, D_C, D_C, D_C, 2 * LANES)
IN_COLS = sum(IN_SPLITS)


def _cparams(*sem):
    return pltpu.CompilerParams(dimension_semantics=sem, vmem_limit_bytes=VMEM_LIMIT)


def _dot(a, b):
    return jnp.dot(a.astype(BF16), b.astype(BF16), preferred_element_type=F32)


def _dot_nt(a, b):
    return lax.dot_general(a.astype(BF16), b.astype(BF16), (((1,), (1,)), ((), ())), preferred_element_type=F32)


def _dot_tn(a, b):
    return lax.dot_general(a.astype(BF16), b.astype(BF16), (((0,), (0,)), ((), ())), preferred_element_type=F32)


def _layer_norm(v, g, b):
    mu = jnp.mean(v, -1, keepdims=True)
    var = jnp.mean(jnp.square(v - mu), -1, keepdims=True)
    return (v - mu) * lax.rsqrt(var + LN_EPS) * g + b


def _ada_kernel(c_ref, w_ref, b_ref, o_ref):
    s = jax.nn.silu(c_ref[...])
    o_ref[0] = _dot(s, w_ref[0]) + b_ref[0]


def _ada_call(cvec, w_ada, b_ada):
    nb = 6
    return pl.pallas_call(
        _ada_kernel,
        grid=(DEPTH, nb),
        in_specs=[
            pl.BlockSpec((8, D_MODEL), lambda l, j: (0, 0)),
            pl.BlockSpec((1, D_MODEL, D_MODEL), lambda l, j: (l, 0, j)),
            pl.BlockSpec((1, 1, D_MODEL), lambda l, j: (l, 0, j)),
        ],
        out_specs=pl.BlockSpec((1, 8, D_MODEL), lambda l, j: (l, 0, j)),
        out_shape=jax.ShapeDtypeStruct((DEPTH, 8, 6 * D_MODEL), F32),
        compiler_params=_cparams("arbitrary", "arbitrary"),
        name="adaln",
    )(cvec, w_ada, b_ada.reshape(DEPTH, 1, 6 * D_MODEL))


def _inproj_kernel(x_ref, sc_ref, sh_ref, w_ref, *out_refs):
    h = (x_ref[...] * (1.0 + sc_ref[0, 0, 0]) + sh_ref[0, 0, 0]).astype(BF16)
    off = 0
    for ref in out_refs:
        n = ref.shape[-1]
        ref[...] = jnp.dot(h, w_ref[0, :, off:off + n], preferred_element_type=F32)
        off += n


def _mod_spec(layer, which):
    return pl.BlockSpec((1, 1, 1, 1, D_MODEL), lambda i, *_: (layer, i, which, 0, 0))


def _inproj_call(x, mod_t, layer, w):
    return pl.pallas_call(
        _inproj_kernel,
        grid=(N_TILES,),
        in_specs=[
            pl.BlockSpec((TM, D_MODEL), lambda i: (i, 0)),
            _mod_spec(layer, 1), _mod_spec(layer, 0),
            pl.BlockSpec((1, D_MODEL, IN_COLS), lambda i: (layer, 0, 0)),
        ],
        out_specs=[pl.BlockSpec((TM, n), lambda i: (i, 0)) for n in IN_SPLITS],
        out_shape=[jax.ShapeDtypeStruct((T_ALL, n), F32) for n in IN_SPLITS],
        compiler_params=_cparams("arbitrary"),
        name="inproj",
    )(x, mod_t, mod_t, w)


HP_A = 2
N_CH = 2 * HP_A


def _mlstm_kernel(q_ref, k_ref, v_ref, o_ref, g_ref, gb_ref, wn_ref, s0_ref, m0_ref,
                  y_ref, so_ref, mo_ref, hf_scr, hb_scr, *, nc):
    lc = LC_K
    scale = DH_A ** -0.5
    chains = [(j, d) for j in range(HP_A) for d in range(2)]
    ti = lax.broadcasted_iota(jnp.int32, (lc, lc), 0)
    si = lax.broadcasted_iota(jnp.int32, (lc, lc), 1)
    lane_ok = lax.broadcasted_iota(jnp.int32, (1, DP_A), 1) < DH_A
    ones = jnp.ones((lc, DP_A), F32)

    def stack(parts):
        return jnp.concatenate(parts, axis=0)

    def rows_of(x, a):
        return a[x * lc:(x + 1) * lc]

    def spread(vals):
        return stack([jnp.broadcast_to(v, (lc, v.shape[1])) for v in vals])

    mask = stack([si <= ti if d == 0 else si >= ti for _, d in chains])
    mask_t = stack([si >= ti if d == 0 else si <= ti for _, d in chains])
    eye = stack([si == ti for _ in chains])

    def col_sums(a):
        return [jnp.sum(rows_of(x, a), axis=0, keepdims=True) for x in range(N_CH)]

    def body(i, carry):
        smats = carry[:N_CH]
        ms = carry[N_CH:]
        r0s = [pl.multiple_of((i if d == 0 else nc - 1 - i) * lc, lc) for _, d in chains]
        q, kt, v1, icol, fpre = [], [], [], [], []
        for (j, d), r0 in zip(chains, r0s):
            cols = slice(j * DP_A, (j + 1) * DP_A)
            q.append(q_ref[pl.ds(r0, lc), cols].astype(BF16))
            kt.append((k_ref[pl.ds(r0, lc), cols] * scale).T.astype(BF16))
            v1.append(jnp.concatenate([v_ref[pl.ds(r0, lc), cols], ones], axis=1))
            gz = g_ref[pl.ds(r0, lc), :] + gb_ref[0]
            icol.append(gz[:, 4 * j + d:4 * j + d + 1])
            fpre.append(gz[:, 4 * j + 2 + d:4 * j + 3 + d])
        i_col = stack(icol)
        f_col = jax.nn.log_sigmoid(stack(fpre))
        b_rows = col_sums(jnp.where(mask_t, f_col, 0.0))
        bls = col_sums(f_col)
        b_col = jnp.sum(jnp.where(eye, spread(b_rows), 0.0), axis=1, keepdims=True)
        a_rows = col_sums(jnp.where(eye, i_col - b_col, 0.0))
        dmat = jnp.where(mask, b_col + spread(a_rows), NEG_INF)
        inter = b_col + spread(ms)
        m_out = jnp.maximum(inter, jnp.max(dmat, axis=1, keepdims=True))
        wmat = jnp.exp(dmat - m_out)
        sw = (stack([jnp.dot(q[x], kt[x], preferred_element_type=F32) for x in range(N_CH)]) * wmat).astype(BF16)
        sc_in = jnp.exp(inter - m_out)
        tot = (stack([jnp.dot(rows_of(x, sw), v1[x].astype(BF16), preferred_element_type=F32) for x in range(N_CH)])
               + sc_in * stack([_dot(q[x], smats[x]) for x in range(N_CH)]))
        h = tot[:, :DP_A] / jnp.maximum(jnp.abs(tot[:, DP_A:]), jnp.exp(-m_out))
        for x, ((j, d), r0) in enumerate(zip(chains, r0s)):
            dst = hf_scr if d == 0 else hb_scr
            dst[pl.ds(r0, lc), j * DP_A:(j + 1) * DP_A] = rows_of(x, h)
        dec = spread(bls) - b_col + i_col
        m_new = [jnp.maximum(bls[x] + ms[x], jnp.max(rows_of(x, dec), axis=0, keepdims=True)) for x in range(N_CH)]
        wk = jnp.exp(dec - spread(m_new))
        s_new = [jnp.exp(bls[x] + ms[x] - m_new[x]) * smats[x] + _dot(kt[x], rows_of(x, wk) * v1[x])
                 for x in range(N_CH)]
        return tuple(s_new) + tuple(m_new)

    init = tuple(s0_ref[0, d, j] for j, d in chains) + tuple(m0_ref[0, d, j] for j, d in chains)
    final = lax.fori_loop(0, nc, body, init)
    for x, (j, d) in enumerate(chains):
        so_ref[0, d, j] = final[x]
        mo_ref[0, d, j] = final[N_CH + x]

    def finish(c, _):
        r0 = pl.multiple_of(c * lc, lc)
        for j in range(HP_A):
            cols = slice(j * DP_A, (j + 1) * DP_A)
            h = hf_scr[pl.ds(r0, lc), cols] + hb_scr[pl.ds(r0, lc), cols]
            mu = jnp.sum(h, axis=1, keepdims=True) * (1.0 / DH_A)
            dv = jnp.where(lane_ok, h - mu, 0.0)
            var = jnp.sum(dv * dv, axis=1, keepdims=True) * (1.0 / DH_A)
            hn = dv * lax.rsqrt(var + LN_EPS) * wn_ref[j]
            y_ref[pl.ds(r0, lc), cols] = jax.nn.sigmoid(o_ref[pl.ds(r0, lc), cols]) * hn
        return 0

    lax.fori_loop(0, nc, finish, 0)


def _mlstm_call(za, zg, gbias, wn, s0, m0, *, nb, seq, row_blk0):
    nc = seq // LC_K
    ng = H_A // HP_A
    w = HP_A * DP_A

    def zspec(part):
        return pl.BlockSpec((seq, w), lambda b, g: (row_blk0 + b, part * ng + g))

    def state(*tail):
        return pl.BlockSpec((1, 2, HP_A) + tail, lambda b, g: (b, 0, g) + (0,) * len(tail))

    return pl.pallas_call(
        functools.partial(_mlstm_kernel, nc=nc),
        grid=(nb, ng),
        in_specs=[
            zspec(0), zspec(1), zspec(2), zspec(3),
            pl.BlockSpec((seq, LANES), lambda b, g: (row_blk0 + b, g)),
            pl.BlockSpec((1, 1, LANES), lambda b, g: (g, 0, 0)),
            pl.BlockSpec((HP_A, 1, DP_A), lambda b, g: (g, 0, 0)),
            state(DP_A, 2 * DP_A), state(1, 1),
        ],
        out_specs=[pl.BlockSpec((seq, w), lambda b, g: (b, g)), state(DP_A, 2 * DP_A), state(1, 1)],
        out_shape=[
            jax.ShapeDtypeStruct((nb * seq, Y_A_COLS), F32),
            jax.ShapeDtypeStruct((nb, 2, H_A, DP_A, 2 * DP_A), F32),
            jax.ShapeDtypeStruct((nb, 2, H_A, 1, 1), F32),
        ],
        scratch_shapes=[pltpu.VMEM((seq, w), F32), pltpu.VMEM((seq, w), F32)],
        compiler_params=_cparams("arbitrary", "arbitrary"),
        name="mlstm",
    )(za, za, za, za, zg, gbias, wn, s0, m0)


def _pair_attention(qp, kslabs, vaugs, masks, sink_col):
    m_rows = qp.shape[0]
    lo = lax.broadcasted_iota(jnp.int32, qp.shape, 1) < HD
    q2 = jnp.concatenate([jnp.where(lo, qp, 0.0), jnp.where(lo, 0.0, qp)], axis=0).astype(BF16)
    scores = []
    for ks, mk in zip(kslabs, masks):
        s = jnp.dot(q2, ks, preferred_element_type=F32)
        if mk is not None:
            s = jnp.where(mk, s, NEG_INF) if mk.dtype == jnp.bool_ else s + mk
        scores.append(s)
    mx = scores[0].max(axis=1, keepdims=True)
    for s in scores[1:]:
        mx = jnp.maximum(mx, s.max(axis=1, keepdims=True))
    if sink_col is not None:
        mx = jnp.maximum(mx, sink_col)
    acc = None
    for s, va in zip(scores, vaugs):
        pv = jnp.dot(jnp.exp(s - mx).astype(BF16), va, preferred_element_type=F32)
        acc = pv if acc is None else acc + pv
    den = acc[:, LANES:]
    if sink_col is not None:
        den = den + jnp.exp(sink_col - mx)
    o = acc[:, :LANES] / den
    return jnp.where(lo, o[:m_rows], o[m_rows:])


def _gqa_key_slabs(kt):
    a, b = kt[:HD], kt[HD:]
    return [jnp.concatenate([a, a], 0), kt, jnp.concatenate([b, b], 0)]


def _gqa_value_pairs(v):
    lo = lax.broadcasted_iota(jnp.int32, v.shape, 1) < HD
    sw = pltpu.roll(v, HD, 1)
    ones = jnp.ones_like(v)
    return [jnp.concatenate([x, ones], 1) for x in (jnp.where(lo, v, sw), v, jnp.where(lo, sw, v))]


def _sink_col(sink_ref, p, m_rows):
    row = lax.broadcasted_iota(jnp.int32, (2 * m_rows, 1), 0)
    return jnp.where(row < m_rows, sink_ref[0:1, 2 * p:2 * p + 1], sink_ref[0:1, 2 * p + 1:2 * p + 2])


def _ctx_attn_kernel(bq_ref, bk_ref, bv_ref, cq_ref, ck_ref, cv_ref, sink_ref, yb_ref, yc_ref):
    scale = HD ** -0.5
    kslabs = _gqa_key_slabs(bk_ref[...].T)
    vpairs = _gqa_value_pairs(bv_ref[...])
    for p in range(D_B // LANES):
        cols = slice(p * LANES, (p + 1) * LANES)
        yb_ref[:, cols] = _pair_attention(bq_ref[:, cols] * scale, [kslabs[p].astype(BF16)],
                                          [vpairs[p].astype(BF16)], [None], _sink_col(sink_ref, p, L_CTX))
    ckt = ck_ref[...].T
    ones = jnp.ones((L_CTX, LANES), F32)
    for p in range(D_C // LANES):
        cols = slice(p * LANES, (p + 1) * LANES)
        va = jnp.concatenate([cv_ref[:, cols], ones], 1)
        yc_ref[:, cols] = _pair_attention(cq_ref[:, cols] * scale, [ckt[cols].astype(BF16)], [va.astype(BF16)],
                                          [None], None)


def _ctx_attn_call(bq, bk, bv, cq, ck, cv, sink):
    def spec(n):
        return pl.BlockSpec((L_CTX, n), lambda b: (b, 0))

    return pl.pallas_call(
        _ctx_attn_kernel,
        grid=(B_CTX,),
        in_specs=[spec(D_B), spec(D_KVB), spec(D_KVB), spec(D_C), spec(D_C), spec(D_C),
                  pl.BlockSpec((1, 8), lambda b: (0, 0))],
        out_specs=[spec(D_B), spec(D_C)],
        out_shape=[jax.ShapeDtypeStruct((T_CTX, D_B), F32), jax.ShapeDtypeStruct((T_CTX, D_C), F32)],
        compiler_params=_cparams("arbitrary"),
        name="ctx_attn",
    )(bq, bk, bv, cq, ck, cv, sink)


def _rope(x, cos, sin_signed, first):
    rot = jnp.where(first, pltpu.roll(x, LANES - 16, 1), pltpu.roll(x, 16, 1))
    return x * cos + rot * sin_signed


def _win_attn_kernel(q_ref, k_ref, v_ref, kc_ref, vc_ref, cos_ref, sin_ref, sink_ref, y_ref,
                     kpt_scr, va_scr, kcp_scr, vca_scr):
    scale = HD ** -0.5
    n_pairs = D_B // LANES
    nblk = N_LAT // WIN
    nband = 3
    lane = lax.broadcasted_iota(jnp.int32, (1, LANES), 1)
    first = (lane % 32) < 16
    for p, (ks, va) in enumerate(zip(_gqa_key_slabs(kc_ref[0, 0].T), _gqa_value_pairs(vc_ref[0, 0]))):
        kcp_scr[p] = ks.astype(BF16)
        vca_scr[p] = va.astype(BF16)

    def prep(blk, c):
        r0 = pl.multiple_of(blk * WIN, WIN)
        kr = _rope(k_ref[pl.ds(r0, WIN), :], cos_ref[pl.ds(r0, WIN), :], sin_ref[pl.ds(r0, WIN), :], first)
        for p, (ks, va) in enumerate(zip(_gqa_key_slabs(kr.T), _gqa_value_pairs(v_ref[pl.ds(r0, WIN), :]))):
            kpt_scr[p, blk] = ks.astype(BF16)
            va_scr[p, pl.ds(r0, WIN), :] = va.astype(BF16)
        return c

    lax.fori_loop(0, nblk, prep, 0)

    def body(blk, c):
        q0 = pl.multiple_of(blk * WIN, WIN)
        sb = jnp.clip(blk - 1, 0, nblk - nband)
        k0 = pl.multiple_of(sb * WIN, WIN)
        cos = cos_ref[pl.ds(q0, WIN), :]
        sin = sin_ref[pl.ds(q0, WIN), :]
        row = lax.broadcasted_iota(jnp.int32, (2 * WIN, nband * WIN), 0)
        qpos = q0 + jnp.where(row < WIN, row, row - WIN)
        kpos = k0 + lax.broadcasted_iota(jnp.int32, (2 * WIN, nband * WIN), 1)
        mask = jnp.abs(kpos - qpos) <= WIN
        for p in range(n_pairs):
            cols = slice(p * LANES, (p + 1) * LANES)
            qp = _rope(q_ref[pl.ds(q0, WIN), cols], cos, sin, first) * scale
            k_loc = jnp.concatenate([kpt_scr[p, sb + j] for j in range(nband)], axis=1)
            v_loc = va_scr[p, pl.ds(k0, nband * WIN), :]
            y_ref[pl.ds(q0, WIN), cols] = _pair_attention(qp, [kcp_scr[p], k_loc], [vca_scr[p], v_loc],
                                                          [None, mask], _sink_col(sink_ref, p, WIN))
        return c

    lax.fori_loop(0, nblk, body, 0)


def _win_attn_call(bq, bk, bv, cache_k, cache_v, cos, sin, sink, layer):
    rb0 = T_CTX // N_LAT

    def spec(n):
        return pl.BlockSpec((N_LAT, n), lambda b: (rb0 + b, 0))

    cache = pl.BlockSpec((1, 1, PAST_LEN, D_KVB), lambda b: (b, layer, 0, 0))
    tab = pl.BlockSpec((N_LAT, LANES), lambda b: (0, 0))
    return pl.pallas_call(
        _win_attn_kernel,
        grid=(B_LAT,),
        in_specs=[spec(D_B), spec(D_KVB), spec(D_KVB), cache, cache, tab, tab,
                  pl.BlockSpec((1, 8), lambda b: (0, 0))],
        out_specs=pl.BlockSpec((N_LAT, D_B), lambda b: (b, 0)),
        out_shape=jax.ShapeDtypeStruct((T_LAT, D_B), F32),
        scratch_shapes=[pltpu.VMEM((D_B // LANES, N_LAT // WIN, LANES, WIN), BF16),
                        pltpu.VMEM((D_B // LANES, N_LAT, 2 * LANES), BF16),
                        pltpu.VMEM((D_B // LANES, LANES, PAST_LEN), BF16),
                        pltpu.VMEM((D_B // LANES, PAST_LEN, 2 * LANES), BF16)],
        compiler_params=_cparams("arbitrary"),
        name="win_attn",
    )(bq, bk, bv, cache_k, cache_v, cos, sin, sink)


NA_BLK = LANES
NA_SPAN_BLKS = NA_WIN_H * GRID_W // NA_BLK + 1
NA_SPAN_ROWS = NA_SPAN_BLKS * NA_BLK // GRID_W
NA_VARIANTS = ((7, 0), (6, 0), (5, 0), (4, 0), (3, 0), (3, 1), (3, 2), (2, 2), (1, 2), (0, 2))


def _na_attn_kernel(q_ref, k_ref, v_ref, kc_ref, vc_ref, bias_ref, y_ref, kpt_scr, va_scr, kcp_scr, vca_scr):
    scale = HD ** -0.5
    rows = N_LAT // GRID_W
    n_pairs = D_C // LANES
    nblk = N_LAT // NA_BLK
    half = NA_WIN_H // 2
    kct = kc_ref[0, 0].T
    for p in range(n_pairs):
        cols = slice(p * LANES, (p + 1) * LANES)
        kcp_scr[p] = kct[cols].astype(BF16)
        vca_scr[p] = jnp.concatenate([vc_ref[0, 0, :, cols], jnp.ones((PAST_LEN, LANES), F32)], 1).astype(BF16)

    def prep(blk, c):
        r0 = pl.multiple_of(blk * NA_BLK, NA_BLK)
        kt = k_ref[pl.ds(r0, NA_BLK), :].T
        for p in range(n_pairs):
            cols = slice(p * LANES, (p + 1) * LANES)
            kpt_scr[p, blk] = kt[cols].astype(BF16)
            va_scr[p, pl.ds(r0, NA_BLK), :] = jnp.concatenate(
                [v_ref[pl.ds(r0, NA_BLK), cols], jnp.ones((NA_BLK, LANES), F32)], 1).astype(BF16)
        return c

    lax.fori_loop(0, nblk, prep, 0)

    def body(r, c):
        kr0 = jnp.clip(r - half, 0, rows - NA_WIN_H)
        sb = jnp.minimum(kr0 // 2, nblk - NA_SPAN_BLKS)
        var = jnp.where(r <= half, r, jnp.where(r >= rows - half, r - (rows - 2 * half - 2), half + (kr0 & 1)))
        q0 = pl.multiple_of(r * GRID_W, GRID_W)
        k0 = pl.multiple_of(sb * NA_BLK, NA_BLK)
        for p in range(n_pairs):
            cols = slice(p * LANES, (p + 1) * LANES)
            k_loc = jnp.concatenate([kpt_scr[p, sb + j] for j in range(NA_SPAN_BLKS)], axis=1)
            v_loc = va_scr[p, pl.ds(k0, NA_SPAN_BLKS * NA_BLK), :]
            y_ref[pl.ds(q0, GRID_W), cols] = _pair_attention(
                q_ref[pl.ds(q0, GRID_W), cols] * scale, [kcp_scr[p], k_loc], [vca_scr[p], v_loc],
                [None, bias_ref[0, p, var]], None)
        return c

    lax.fori_loop(0, rows, body, 0)


def _na_attn_call(cq, ck, cv, cache_k, cache_v, bias_tab, layer):
    rb0 = T_CTX // N_LAT
    n_pairs = D_C // LANES
    spec = pl.BlockSpec((N_LAT, D_C), lambda b: (rb0 + b, 0))
    cache = pl.BlockSpec((1, 1, PAST_LEN, D_C), lambda b: (b, layer, 0, 0))
    span = NA_SPAN_BLKS * NA_BLK
    return pl.pallas_call(
        _na_attn_kernel,
        grid=(B_LAT,),
        in_specs=[spec, spec, spec, cache, cache,
                  pl.BlockSpec((1, n_pairs, len(NA_VARIANTS), 2 * GRID_W, span), lambda b: (layer, 0, 0, 0, 0))],
        out_specs=pl.BlockSpec((N_LAT, D_C), lambda b: (b, 0)),
        out_shape=jax.ShapeDtypeStruct((T_LAT, D_C), F32),
        scratch_shapes=[pltpu.VMEM((n_pairs, N_LAT // NA_BLK, LANES, NA_BLK), BF16),
                        pltpu.VMEM((n_pairs, N_LAT, 2 * LANES), BF16),
                        pltpu.VMEM((n_pairs, LANES, PAST_LEN), BF16),
                        pltpu.VMEM((n_pairs, PAST_LEN, 2 * LANES), BF16)],
        compiler_params=_cparams("arbitrary"),
        name="na_attn",
    )(cq, ck, cv, cache_k, cache_v, bias_tab)


def _outproj_kernel(yac_ref, ybc_ref, ycc_ref, yal_ref, ybl_ref, ycl_ref, x_ref, ga_ref, scf_ref, shf_ref,
                    lng_ref, lnb_ref, wo_ref, wr_ref, x1_ref, h2_ref, route_ref, cnt_ref, run_scr):
    @pl.when(pl.program_id(0) == 0)
    def _():
        run_scr[...] = jnp.zeros_like(run_scr)

    is_ctx = pl.program_id(0) < T_CTX // TM

    def pick(c_ref, l_ref):
        return jnp.where(is_ctx, c_ref[...], l_ref[...])

    y = (_dot(pick(yac_ref, yal_ref), wo_ref[0, 0:Y_A_COLS, :])
         + _dot(pick(ybc_ref, ybl_ref), wo_ref[0, Y_A_COLS:Y_A_COLS + D_B, :])
         + _dot(pick(ycc_ref, ycl_ref), wo_ref[0, Y_A_COLS + D_B:, :]))
    x1 = _layer_norm(ALPHA * x_ref[...] + ga_ref[0, 0, 0] * y, lng_ref[...], lnb_ref[...])
    x1_ref[...] = x1
    h2 = x1 * (1.0 + scf_ref[0, 0, 0]) + shf_ref[0, 0, 0]
    h2_ref[...] = h2
    w_r = wr_ref[...]
    w_hi = w_r.astype(BF16)
    w_lo = (w_r - w_hi.astype(F32)).astype(BF16)
    h_hi = h2.astype(BF16)
    h_lo = (h2 - h_hi.astype(F32)).astype(BF16)
    p_hi = jnp.dot(h_hi, jnp.concatenate([w_hi, w_lo], axis=1), preferred_element_type=F32)
    logits = p_hi[:, :LANES] + p_hi[:, LANES:] + jnp.dot(h_lo, w_hi, preferred_element_type=F32)
    lane = lax.broadcasted_iota(jnp.int32, logits.shape, 1)
    lanef = lane.astype(F32)
    big = float(LANES)
    lg = jnp.where((lane >= N_EXPERTS) & (lane < N_EXPERTS + N_GROUPS), logits, NEG_INF)
    mg = jnp.max(lg, axis=1, keepdims=True)
    grp = jnp.min(jnp.where(lg == mg, lanef, big), axis=1, keepdims=True) - float(N_EXPERTS)
    g_w = 1.0 / jnp.sum(jnp.exp(lg - mg), axis=1, keepdims=True)
    in_grp = (lane < N_EXPERTS) & ((lane // E_PER_GROUP).astype(F32) == grp)
    le = jnp.where(in_grp, logits, NEG_INF)
    l1 = jnp.max(le, axis=1, keepdims=True)
    i1 = jnp.min(jnp.where(le == l1, lanef, big), axis=1, keepdims=True)
    le2 = jnp.where(lanef == i1, NEG_INF, le)
    l2 = jnp.max(le2, axis=1, keepdims=True)
    i2 = jnp.min(jnp.where(le2 == l2, lanef, big), axis=1, keepdims=True)
    e2 = jnp.exp(l2 - l1)
    w1 = g_w / (1.0 + e2)
    w2 = g_w * e2 / (1.0 + e2)
    oh1 = jnp.where(lanef == i1, 1.0, 0.0)
    oh2 = jnp.where(lanef == i2, 1.0, 0.0)
    rt = lax.broadcasted_iota(jnp.int32, (TM, TM), 0)
    ct = lax.broadcasted_iota(jnp.int32, (TM, TM), 1)
    before = jnp.where(ct < rt, 1.0, 0.0)
    run = run_scr[...]
    tot1 = jnp.sum(oh1, axis=0, keepdims=True)
    r1 = jnp.sum(oh1 * (run + _dot(before, oh1)), axis=1, keepdims=True)
    r2 = jnp.sum(oh2 * (run + tot1 + _dot(before, oh2)), axis=1, keepdims=True)
    run = run + tot1 + jnp.sum(oh2, axis=0, keepdims=True)
    run_scr[...] = run
    cnt_ref[...] = run
    vals = (i1, i2, w1, w2, r1, r2)
    out = jnp.zeros_like(logits)
    for n, v in enumerate(vals):
        out = jnp.where(lane == n, v, out)
    route_ref[...] = out[:, :8]


def _outproj_call(y_ctx, y_lat, x, mod_t, layer, lng, lnb, wo, wr):
    row_vec = pl.BlockSpec((1, D_MODEL), lambda i: (0, 0))
    n_ctx = T_CTX // TM

    def tok(n):
        return pl.BlockSpec((TM, n), lambda i: (i, 0))

    def tok_ctx(n):
        return pl.BlockSpec((TM, n), lambda i: (jnp.minimum(i, n_ctx - 1), 0))

    def tok_lat(n):
        return pl.BlockSpec((TM, n), lambda i: (jnp.maximum(i - n_ctx, 0), 0))

    return pl.pallas_call(
        _outproj_kernel,
        grid=(N_TILES,),
        in_specs=[tok_ctx(Y_A_COLS), tok_ctx(D_B), tok_ctx(D_C), tok_lat(Y_A_COLS), tok_lat(D_B), tok_lat(D_C),
                  tok(D_MODEL), _mod_spec(layer, 2), _mod_spec(layer, 4), _mod_spec(layer, 3), row_vec, row_vec,
                  pl.BlockSpec((1, Y_A_COLS + D_B + D_C, D_MODEL), lambda i: (layer, 0, 0)),
                  pl.BlockSpec((D_MODEL, LANES), lambda i: (0, 0))],
        out_specs=[tok(D_MODEL), tok(D_MODEL), tok(8), pl.BlockSpec((1, LANES), lambda i: (0, 0))],
        out_shape=[jax.ShapeDtypeStruct((T_ALL, D_MODEL), F32), jax.ShapeDtypeStruct((T_ALL, D_MODEL), F32),
                   jax.ShapeDtypeStruct((T_ALL, 8), F32), jax.ShapeDtypeStruct((1, LANES), F32)],
        scratch_shapes=[pltpu.VMEM((1, LANES), F32)],
        compiler_params=_cparams("arbitrary"),
        name="outproj_router",
    )(*y_ctx, *y_lat, x, mod_t, mod_t, mod_t, lng, lnb, wo, wr)


def _dispatch_kernel(pos_ref, pend_ref, h_ref, xs_hbm, zero_scr, sem):
    base = pl.program_id(0) * TM

    @pl.when(pl.program_id(0) == 0)
    def _():
        zero_scr[...] = jnp.zeros_like(zero_scr)

        def fill(e, op):
            prev = pend_ref[e - 1] if e else 0

            @pl.when(pend_ref[e] > prev)
            def _():
                first = pl.multiple_of(pend_ref[e] - MOE_BLK, MOE_BLK)
                op(pltpu.make_async_copy(zero_scr, xs_hbm.at[pl.ds(first, MOE_BLK), :], sem))

        def tail_copy(b):
            first = pl.multiple_of(b * MOE_BLK, MOE_BLK)
            return pltpu.make_async_copy(zero_scr, xs_hbm.at[pl.ds(first, MOE_BLK), :], sem)

        def tail_start(b, c):
            tail_copy(b).start()
            return c

        def tail_wait(b, c):
            tail_copy(b).wait()
            return c

        n_blocks = xs_hbm.shape[0] // MOE_BLK
        first_free = pend_ref[N_EXPERTS - 1] // MOE_BLK
        for e in range(N_EXPERTS):
            fill(e, lambda cp: cp.start())
        lax.fori_loop(first_free, n_blocks, tail_start, 0)
        for e in range(N_EXPERTS):
            fill(e, lambda cp: cp.wait())
        lax.fori_loop(first_free, n_blocks, tail_wait, 0)

    def row(t, p):
        return pltpu.make_async_copy(h_ref.at[pl.ds(t, 1), :], xs_hbm.at[pl.ds(p, 1), :], sem)

    def issue(t, c):
        row(t, pos_ref[base + t]).start()
        row(t, pos_ref[T_ALL + base + t]).start()
        return c

    lax.fori_loop(0, TM, issue, 0, unroll=8)
    whole = pltpu.make_async_copy(h_ref, xs_hbm.at[pl.ds(0, TM), :], sem)
    whole.wait()
    whole.wait()


def _dispatch_call(pos2, p_end, h2, nblk):
    grid_spec = pltpu.PrefetchScalarGridSpec(
        num_scalar_prefetch=2,
        grid=(N_TILES,),
        in_specs=[pl.BlockSpec((TM, D_MODEL), lambda i, p, pe: (i, 0))],
        out_specs=pl.BlockSpec(memory_space=pl.ANY),
        scratch_shapes=[pltpu.VMEM((MOE_BLK, D_MODEL), F32), pltpu.SemaphoreType.DMA(())],
    )
    return pl.pallas_call(
        _dispatch_kernel,
        grid_spec=grid_spec,
        out_shape=jax.ShapeDtypeStruct((nblk * MOE_BLK, D_MODEL), F32),
        compiler_params=_cparams("arbitrary"),
        name="dispatch",
    )(pos2, p_end, h2)


def _expert_kernel(blk_e_ref, nused_ref, xs_ref, wg_ref, wu_ref, wd_ref, out_ref):
    j = pl.program_id(0)

    @pl.when(j < nused_ref[0])
    def _():
        xb = xs_ref[...].astype(BF16)
        g = jnp.dot(xb, wg_ref[0, 0].astype(BF16), preferred_element_type=F32)
        u = jnp.dot(xb, wu_ref[0, 0].astype(BF16), preferred_element_type=F32)
        out_ref[...] = _dot(jax.nn.silu(g) * u, wd_ref[0, 0])

    @pl.when(j >= nused_ref[0])
    def _():
        out_ref[...] = jnp.zeros_like(out_ref)


def _expert_call(blk_e, nused, xs, wg, wu, wd, layer, nblk):
    def wspec(r, c):
        return pl.BlockSpec((1, 1, r, c), lambda j, be, nu: (layer, be[j], 0, 0))

    grid_spec = pltpu.PrefetchScalarGridSpec(
        num_scalar_prefetch=2,
        grid=(nblk,),
        in_specs=[
            pl.BlockSpec((MOE_BLK, D_MODEL), lambda j, be, nu: (jnp.clip(j, 0, jnp.maximum(nu[0] - 1, 0)), 0)),
            wspec(D_MODEL, D_EXPERT), wspec(D_MODEL, D_EXPERT), wspec(D_EXPERT, D_MODEL),
        ],
        out_specs=pl.BlockSpec((MOE_BLK, D_MODEL), lambda j, be, nu: (j, 0)),
    )
    return pl.pallas_call(
        _expert_kernel,
        grid_spec=grid_spec,
        out_shape=jax.ShapeDtypeStruct((nblk * MOE_BLK, D_MODEL), F32),
        compiler_params=_cparams("arbitrary"),
        name="experts",
    )(blk_e, nused, xs, wg, wu, wd)


def _combine_kernel(pos_ref, eo_hbm, x1_ref, route_ref, gf_ref, lng_ref, lnb_ref, x2_ref, buf, sem):
    base = pl.program_id(0) * TM

    def row(p, r, t):
        return pltpu.make_async_copy(eo_hbm.at[pl.ds(p, 1), :], buf.at[r, pl.ds(t, 1), :], sem)

    def issue(t, c):
        row(pos_ref[base + t], 0, t).start()
        row(pos_ref[T_ALL + base + t], 1, t).start()
        return c

    lax.fori_loop(0, TM, issue, 0, unroll=8)
    for r in range(2):
        pltpu.make_async_copy(eo_hbm.at[pl.ds(0, TM), :], buf.at[r], sem).wait()
    route = route_ref[...]
    y = route[:, 2:3] * buf[0] + route[:, 3:4] * buf[1]
    x2_ref[...] = _layer_norm(ALPHA * x1_ref[...] + gf_ref[0, 0, 0] * y, lng_ref[...], lnb_ref[...])


def _combine_call(pos2, eo, x1, route, mod_t, layer, lng, lnb):
    grid_spec = pltpu.PrefetchScalarGridSpec(
        num_scalar_prefetch=1,
        grid=(N_TILES,),
        in_specs=[
            pl.BlockSpec(memory_space=pl.ANY),
            pl.BlockSpec((TM, D_MODEL), lambda i, p: (i, 0)),
            pl.BlockSpec((TM, 8), lambda i, p: (i, 0)),
            _mod_spec(layer, 5),
            pl.BlockSpec((1, D_MODEL), lambda i, p: (0, 0)),
            pl.BlockSpec((1, D_MODEL), lambda i, p: (0, 0)),
        ],
        out_specs=pl.BlockSpec((TM, D_MODEL), lambda i, p: (i, 0)),
        scratch_shapes=[pltpu.VMEM((2, TM, D_MODEL), F32), pltpu.SemaphoreType.DMA(())],
    )
    return pl.pallas_call(
        _combine_kernel,
        grid_spec=grid_spec,
        out_shape=jax.ShapeDtypeStruct((T_ALL, D_MODEL), F32),
        compiler_params=_cparams("arbitrary"),
        name="combine_norm",
    )(pos2, eo, x1, route, mod_t, lng, lnb)


def _gate_cols(grp):
    return np.array([kind * 2 * H_A + d * H_A + grp * HP_A + j
                     for j in range(HP_A) for kind in range(2) for d in range(2)])


def _prep_w_in(w_in):
    a = w_in[..., :4 * D_A].reshape(DEPTH, D_MODEL, 4, H_A, DH_A)
    a = jnp.pad(a, ((0, 0), (0, 0), (0, 0), (0, 0), (0, DP_A - DH_A))).reshape(DEPTH, D_MODEL, ZA_COLS)
    gates = w_in[..., 4 * D_A:4 * D_A + 4 * H_A]
    g = jnp.concatenate([jnp.pad(gates[..., _gate_cols(grp)], ((0, 0), (0, 0), (0, LANES - 4 * HP_A)))
                         for grp in range(H_A // HP_A)], -1)
    rest = w_in[..., 4 * D_A + 4 * H_A:]
    return jnp.concatenate([a, rest, g], -1).astype(BF16)


def _prep_w_out(w_out):
    a = w_out[:, :D_A].reshape(DEPTH, H_A, DH_A, D_MODEL)
    a = jnp.pad(a, ((0, 0), (0, 0), (0, DP_A - DH_A), (0, 0))).reshape(DEPTH, Y_A_COLS, D_MODEL)
    return jnp.concatenate([a, w_out[:, D_A:]], 1).astype(BF16)


def _rope_tables():
    t = np.arange(N_LAT)
    nf = HD // 4
    inv = ROPE_BASE ** (-np.arange(nf, dtype=np.float32) / nf)
    ar = (t // GRID_W).astype(np.float32)[:, None] * inv
    ac = (t % GRID_W).astype(np.float32)[:, None] * inv
    ang = jnp.asarray(np.concatenate([ar, ar, ac, ac], -1), F32)
    cos, sin = jnp.cos(ang), jnp.sin(ang)
    sign = np.where((np.arange(HD) % 32) < 16, -1.0, 1.0).astype(np.float32)
    reps = LANES // HD
    return jnp.tile(cos, (1, reps)), jnp.tile(sin * sign, (1, reps))


def _na_bias_tables(rpb):
    qcol = np.arange(GRID_W)[:, None]
    kcol = np.arange(GRID_W)[None, :]
    dc = np.clip(kcol - qcol, 1 - NA_WIN_W, NA_WIN_W - 1) + NA_WIN_W - 1
    wstart = np.clip(qcol - NA_WIN_W // 2, 0, GRID_W - NA_WIN_W)
    in_win = (kcol >= wstart) & (kcol < wstart + NA_WIN_W)
    sel = (np.arange(2 * NA_WIN_W - 1)[:, None] == dc.reshape(1, -1)).astype(np.float32)
    cols = jnp.einsum("lhrd,dn->lhrn", rpb, jnp.asarray(sel), precision=lax.Precision.HIGHEST)
    cols = jnp.where(in_win.reshape(-1), cols, NEG_INF).reshape(DEPTH, H_C, 2 * NA_WIN_H - 1, GRID_W, GRID_W)
    outside = jnp.full((DEPTH, H_C, GRID_W, GRID_W), NEG_INF, F32)
    variants = []
    for dr0, off in NA_VARIANTS:
        span_rows = [cols[:, :, dr0 + i - off] if 0 <= i - off < NA_WIN_H else outside for i in range(NA_SPAN_ROWS)]
        variants.append(jnp.stack(span_rows, 3).reshape(DEPTH, H_C, GRID_W, NA_SPAN_ROWS * GRID_W))
    tab = jnp.stack(variants, 2).reshape(DEPTH, H_C // 2, 2, len(NA_VARIANTS), GRID_W, NA_SPAN_ROWS * GRID_W)
    return tab.transpose(0, 1, 3, 2, 4, 5).reshape(DEPTH, H_C // 2, len(NA_VARIANTS), 2 * GRID_W, NA_SPAN_ROWS * GRID_W)


def _dispatch_plan(route, counts):
    nblk = 2 * T_ALL // MOE_BLK + N_EXPERTS
    cnt = counts[0, :N_EXPERTS].astype(jnp.int32)
    padded = (cnt + MOE_BLK - 1) // MOE_BLK * MOE_BLK
    p_end = jnp.cumsum(padded)
    p_start = p_end - padded
    e = route[:, 0:2].astype(jnp.int32)
    hot = e[..., None] == jnp.arange(N_EXPERTS, dtype=jnp.int32)
    pos = jnp.sum(jnp.where(hot, p_start, 0), -1) + route[:, 4:6].astype(jnp.int32)
    pos2 = pos.T.reshape(-1)
    blk_first = jnp.arange(nblk, dtype=jnp.int32) * MOE_BLK
    blk_e = jnp.minimum(jnp.sum((p_end[None, :] <= blk_first[:, None]).astype(jnp.int32), axis=1), N_EXPERTS - 1)
    nused = p_end[-1:] // MOE_BLK
    return blk_e, nused, pos2, p_end, nblk


def kernel(x_prompt, x_sample, state_a_C, state_a_n, state_a_m, cache_b_k, cache_b_v, cache_c_k, cache_c_v, c, c_ctx, w_in, b_a_i, b_a_f, w_a_hnorm, b_sink, rpb, w_out, w_ada, b_ada, ln_g, ln_b, w_router_grp, w_router_exp, w_e_gate, w_e_up, w_e_down):
    w_in_p = _prep_w_in(w_in)
    w_out_p = _prep_w_out(w_out)
    w_r = jnp.pad(jnp.concatenate([w_router_exp, w_router_grp], -1),
                  ((0, 0), (0, 0), (0, LANES - N_EXPERTS - N_GROUPS)))
    wn_p = jnp.pad(w_a_hnorm.reshape(DEPTH, H_A, 1, DH_A), ((0, 0), (0, 0), (0, 0), (0, DP_A - DH_A)))
    gate_b = jnp.concatenate([b_a_i, b_a_f], 1).transpose(0, 2, 1)
    gate_b = jnp.pad(gate_b.reshape(DEPTH, H_A // HP_A, 1, 4 * HP_A), ((0, 0), (0, 0), (0, 0), (0, LANES - 4 * HP_A)))
    sink_p = jnp.pad(b_sink, ((0, 0), (0, 8 - H_B))).reshape(DEPTH, 1, 8)
    cos_t, sin_t = _rope_tables()
    na_bias = _na_bias_tables(rpb)
    cb_k = cache_b_k.reshape(B_LAT, DEPTH, PAST_LEN, D_KVB)
    cb_v = cache_b_v.reshape(B_LAT, DEPTH, PAST_LEN, D_KVB)
    cc_k = cache_c_k.reshape(B_LAT, DEPTH, PAST_LEN, D_C)
    cc_v = cache_c_v.reshape(B_LAT, DEPTH, PAST_LEN, D_C)
    pad_c = ((0, 0), (0, 0), (0, 0), (0, 0), (0, DP_A - DH_A), (0, DP_A - DH_A))
    st_ct = jnp.swapaxes(jnp.pad(state_a_C, pad_c), -1, -2)
    st_nr = jnp.broadcast_to(jnp.pad(state_a_n, pad_c[:-1])[..., None], st_ct.shape)
    st_s = jnp.concatenate([st_ct, st_nr], -1)
    st_m = state_a_m[..., None, None]
    z_s = jnp.zeros((B_CTX, 2, H_A, DP_A, 2 * DP_A), F32)
    z_m = jnp.zeros((B_CTX, 2, H_A, 1, 1), F32)

    cvec = jnp.concatenate([c, c_ctx[None, :], jnp.zeros((3, D_MODEL), F32)], 0)
    mod = _ada_call(cvec, w_ada, b_ada)
    tile_row = np.concatenate([np.full(T_CTX // TM, B_LAT), np.repeat(np.arange(B_LAT), N_LAT // TM)])
    mod_t = mod[:, tile_row].reshape(DEPTH, N_TILES, 6, 1, D_MODEL)

    x = jnp.concatenate([x_prompt.reshape(T_CTX, D_MODEL), x_sample.reshape(T_LAT, D_MODEL)], 0)
    cs_, ns_, ms_, kbs, vbs, kcs, vcs = [], [], [], [], [], [], []
    for l in range(DEPTH):
        za, bq, bk, bv, cq, ck, cv, zg = _inproj_call(x, mod_t, l, w_in_p)
        ya_c, s_l, m_l = _mlstm_call(za, zg, gate_b[l], wn_p[l], z_s, z_m, nb=B_CTX, seq=L_CTX, row_blk0=0)
        yb_c, yc_c = _ctx_attn_call(bq, bk, bv, cq, ck, cv, sink_p[l])
        ya_l, _, _ = _mlstm_call(za, zg, gate_b[l], wn_p[l], st_s[:, l], st_m[:, l],
                                 nb=B_LAT, seq=N_LAT, row_blk0=T_CTX // N_LAT)
        yb_l = _win_attn_call(bq, bk, bv, cb_k, cb_v, cos_t, sin_t, sink_p[l], l)
        yc_l = _na_attn_call(cq, ck, cv, cc_k, cc_v, na_bias, l)
        x1, h2, route, counts = _outproj_call((ya_c, yb_c, yc_c), (ya_l, yb_l, yc_l), x, mod_t, l,
                                              ln_g[l, 0:1], ln_b[l, 0:1], w_out_p, w_r[l])
        blk_e, nused, pos2, p_end, nblk = _dispatch_plan(route, counts)
        xs = _dispatch_call(pos2, p_end, h2, nblk)
        eo = _expert_call(blk_e, nused, xs, w_e_gate, w_e_up, w_e_down, l, nblk)
        x = _combine_call(pos2, eo, x1, route, mod_t, l, ln_g[l, 1:2], ln_b[l, 1:2])
        cs_.append(jnp.swapaxes(s_l[..., :DH_A, :DH_A], -1, -2))
        ns_.append(s_l[..., :DH_A, DP_A])
        ms_.append(m_l.reshape(B_CTX, 2, H_A))
        kbs.append(bk[:T_CTX].reshape(B_CTX, L_CTX, KV_B, HD))
        vbs.append(bv[:T_CTX].reshape(B_CTX, L_CTX, KV_B, HD))
        kcs.append(ck[:T_CTX].reshape(B_CTX, L_CTX, H_C, HD))
        vcs.append(cv[:T_CTX].reshape(B_CTX, L_CTX, H_C, HD))
    y_prompt = x[:T_CTX].reshape(B_CTX, L_CTX, D_MODEL)
    y_sample = x[T_CTX:].reshape(B_LAT, N_LAT, D_MODEL)
    return (y_prompt, y_sample, jnp.stack(cs_, 1), jnp.stack(ns_, 1), jnp.stack(ms_, 1),
            jnp.stack(kbs, 1), jnp.stack(vbs, 1), jnp.stack(kcs, 1), jnp.stack(vcs, 1))
```

```python
import functools

import numpy as np
import jax
import jax.numpy as jnp
from jax import lax
from jax.experimental import pallas as pl
from jax.experimental.pallas import tpu as pltpu

F32 = jnp.float32
BF16 = jnp.bfloat16
NEG_INF = float("-inf")

D_MODEL = 1024
DEPTH = 4
B_CTX, L_CTX = 16, 256
B_LAT, N_LAT = 4, 2048
PAST_LEN = 512
GRID_W = 64
H_A, DH_A = 4, 96
D_A = H_A * DH_A
H_B, KV_B, HD = 6, 2, 64
G_B = H_B // KV_B
D_B, D_KVB = H_B * HD, KV_B * HD
WIN = 128
ROPE_BASE = 10000.0
H_C = 4
D_C = H_C * HD
NA_WIN_H, NA_WIN_W = 8, 16
N_GROUPS, E_PER_GROUP = 4, 8
N_EXPERTS = N_GROUPS * E_PER_GROUP
D_EXPERT = D_MODEL // 4
ALPHA = (2 * DEPTH) ** 0.25
LN_EPS = 1e-5

LANES = 128
DP_A = LANES
LC_K = LANES
TM = 256
MOE_BLK = 256
VMEM_LIMIT = 48 * 1024 * 1024

T_CTX = B_CTX * L_CTX
T_LAT = B_LAT * N_LAT
T_ALL = T_CTX + T_LAT
N_TILES = T_ALL // TM
ZA_COLS = 4 * H_A * DP_A
Y_A_COLS = H_A * DP_A
IN_SPLITS = (ZA_COLS, D_B, D_KVB, D_KVB, D_C, D_C, D_C, 2 * LANES)
IN_COLS = sum(IN_SPLITS)


def _cparams(*sem):
    return pltpu.CompilerParams(dimension_semantics=sem, vmem_limit_bytes=VMEM_LIMIT)


def _dot(a, b):
    return jnp.dot(a.astype(BF16), b.astype(BF16), preferred_element_type=F32)


def _dot_nt(a, b):
    return lax.dot_general(a.astype(BF16), b.astype(BF16), (((1,), (1,)), ((), ())), preferred_element_type=F32)


def _dot_tn(a, b):
    return lax.dot_general(a.astype(BF16), b.astype(BF16), (((0,), (0,)), ((), ())), preferred_element_type=F32)


def _layer_norm(v, g, b):
    mu = jnp.mean(v, -1, keepdims=True)
    var = jnp.mean(jnp.square(v - mu), -1, keepdims=True)
    return (v - mu) * lax.rsqrt(var + LN_EPS) * g + b


def _ada_kernel(c_ref, w_ref, b_ref, o_ref):
    s = jax.nn.silu(c_ref[...])
    o_ref[0] = _dot(s, w_ref[0]) + b_ref[0]


def _ada_call(cvec, w_ada, b_ada):
    nb = 6
    return pl.pallas_call(
        _ada_kernel,
        grid=(DEPTH, nb),
        in_specs=[
            pl.BlockSpec((8, D_MODEL), lambda l, j: (0, 0)),
            pl.BlockSpec((1, D_MODEL, D_MODEL), lambda l, j: (l, 0, j)),
            pl.BlockSpec((1, 1, D_MODEL), lambda l, j: (l, 0, j)),
        ],
        out_specs=pl.BlockSpec((1, 8, D_MODEL), lambda l, j: (l, 0, j)),
        out_shape=jax.ShapeDtypeStruct((DEPTH, 8, 6 * D_MODEL), F32),
        compiler_params=_cparams("arbitrary", "arbitrary"),
        name="adaln",
    )(cvec, w_ada, b_ada.reshape(DEPTH, 1, 6 * D_MODEL))


def _inproj_kernel(x_ref, sc_ref, sh_ref, w_ref, *out_refs):
    h = (x_ref[...] * (1.0 + sc_ref[0, 0, 0]) + sh_ref[0, 0, 0]).astype(BF16)
    off = 0
    for ref in out_refs:
        n = ref.shape[-1]
        ref[...] = jnp.dot(h, w_ref[0, :, off:off + n], preferred_element_type=F32)
        off += n


def _mod_spec(layer, which):
    return pl.BlockSpec((1, 1, 1, 1, D_MODEL), lambda i, *_: (layer, i, which, 0, 0))


def _inproj_call(x, mod_t, layer, w):
    return pl.pallas_call(
        _inproj_kernel,
        grid=(N_TILES,),
        in_specs=[
            pl.BlockSpec((TM, D_MODEL), lambda i: (i, 0)),
            _mod_spec(layer, 1), _mod_spec(layer, 0),
            pl.BlockSpec((1, D_MODEL, IN_COLS), lambda i: (layer, 0, 0)),
        ],
        out_specs=[pl.BlockSpec((TM, n), lambda i: (i, 0)) for n in IN_SPLITS],
        out_shape=[jax.ShapeDtypeStruct((T_ALL, n), F32) for n in IN_SPLITS],
        compiler_params=_cparams("arbitrary"),
        name="inproj",
    )(x, mod_t, mod_t, w)


HP_A = 2
N_CH = 2 * HP_A


def _mlstm_kernel(q_ref, k_ref, v_ref, o_ref, g_ref, gb_ref, wn_ref, s0_ref, m0_ref,
                  y_ref, so_ref, mo_ref, hf_scr, hb_scr, *, nc):
    lc = LC_K
    scale = DH_A ** -0.5
    chains = [(j, d) for j in range(HP_A) for d in range(2)]
    ti = lax.broadcasted_iota(jnp.int32, (lc, lc), 0)
    si = lax.broadcasted_iota(jnp.int32, (lc, lc), 1)
    lane_ok = lax.broadcasted_iota(jnp.int32, (1, DP_A), 1) < DH_A
    ones = jnp.ones((lc, DP_A), F32)

    def stack(parts):
        return jnp.concatenate(parts, axis=0)

    def rows_of(x, a):
        return a[x * lc:(x + 1) * lc]

    def spread(vals):
        return stack([jnp.broadcast_to(v, (lc, v.shape[1])) for v in vals])

    mask = stack([si <= ti if d == 0 else si >= ti for _, d in chains])
    mask_t = stack([si >= ti if d == 0 else si <= ti for _, d in chains])
    eye = stack([si == ti for _ in chains])

    def col_sums(a):
        return [jnp.sum(rows_of(x, a), axis=0, keepdims=True) for x in range(N_CH)]

    def body(i, carry):
        smats = carry[:N_CH]
        ms = carry[N_CH:]
        r0s = [pl.multiple_of((i if d == 0 else nc - 1 - i) * lc, lc) for _, d in chains]
        q, kt, v1, icol, fpre = [], [], [], [], []
        for (j, d), r0 in zip(chains, r0s):
            cols = slice(j * DP_A, (j + 1) * DP_A)
            q.append(q_ref[pl.ds(r0, lc), cols].astype(BF16))
            kt.append((k_ref[pl.ds(r0, lc), cols] * scale).T.astype(BF16))
            v1.append(jnp.concatenate([v_ref[pl.ds(r0, lc), cols], ones], axis=1))
            gz = g_ref[pl.ds(r0, lc), :] + gb_ref[0]
            icol.append(gz[:, 4 * j + d:4 * j + d + 1])
            fpre.append(gz[:, 4 * j + 2 + d:4 * j + 3 + d])
        i_col = stack(icol)
        f_col = jax.nn.log_sigmoid(stack(fpre))
        b_rows = col_sums(jnp.where(mask_t, f_col, 0.0))
        bls = col_sums(f_col)
        b_col = jnp.sum(jnp.where(eye, spread(b_rows), 0.0), axis=1, keepdims=True)
        a_rows = col_sums(jnp.where(eye, i_col - b_col, 0.0))
        a_sp = jnp.where(mask, spread(a_rows), NEG_INF)
        m_sp = spread(ms)
        gap = jnp.maximum(m_sp, jnp.max(a_sp, axis=1, keepdims=True))
        m_out = b_col + gap
        wmat = jnp.exp(a_sp - gap)
        sw = (stack([jnp.dot(q[x], kt[x], preferred_element_type=F32) for x in range(N_CH)]) * wmat).astype(BF16)
        sc_in = jnp.exp(m_sp - gap)
        tot = (stack([jnp.dot(rows_of(x, sw), v1[x].astype(BF16), preferred_element_type=F32) for x in range(N_CH)])
               + sc_in * stack([_dot(q[x], smats[x]) for x in range(N_CH)]))
        h = tot[:, :DP_A] / jnp.maximum(jnp.abs(tot[:, DP_A:]), jnp.exp(-m_out))
        for x, ((j, d), r0) in enumerate(zip(chains, r0s)):
            dst = hf_scr if d == 0 else hb_scr
            dst[pl.ds(r0, lc), j * DP_A:(j + 1) * DP_A] = rows_of(x, h)
        dec = spread(bls) - b_col + i_col
        m_new = [jnp.maximum(bls[x] + ms[x], jnp.max(rows_of(x, dec), axis=0, keepdims=True)) for x in range(N_CH)]
        wk = jnp.exp(dec - spread(m_new))
        s_new = [jnp.exp(bls[x] + ms[x] - m_new[x]) * smats[x] + _dot(kt[x], rows_of(x, wk) * v1[x])
                 for x in range(N_CH)]
        return tuple(s_new) + tuple(m_new)

    init = tuple(s0_ref[0, d, j] for j, d in chains) + tuple(m0_ref[0, d, j] for j, d in chains)
    final = lax.fori_loop(0, nc, body, init)
    for x, (j, d) in enumerate(chains):
        so_ref[0, d, j] = final[x]
        mo_ref[0, d, j] = final[N_CH + x]

    def finish(c, _):
        r0 = pl.multiple_of(c * lc, lc)
        for j in range(HP_A):
            cols = slice(j * DP_A, (j + 1) * DP_A)
            h = hf_scr[pl.ds(r0, lc), cols] + hb_scr[pl.ds(r0, lc), cols]
            mu = jnp.sum(h, axis=1, keepdims=True) * (1.0 / DH_A)
            dv = jnp.where(lane_ok, h - mu, 0.0)
            var = jnp.sum(dv * dv, axis=1, keepdims=True) * (1.0 / DH_A)
            hn = dv * lax.rsqrt(var + LN_EPS) * wn_ref[j]
            y_ref[pl.ds(r0, lc), cols] = jax.nn.sigmoid(o_ref[pl.ds(r0, lc), cols]) * hn
        return 0

    lax.fori_loop(0, nc, finish, 0)


def _mlstm_call(za, zg, gbias, wn, s0, m0, *, nb, seq, row_blk0):
    nc = seq // LC_K
    ng = H_A // HP_A
    w = HP_A * DP_A

    def zspec(part):
        return pl.BlockSpec((seq, w), lambda b, g: (row_blk0 + b, part * ng + g))

    def state(*tail):
        return pl.BlockSpec((1, 2, HP_A) + tail, lambda b, g: (b, 0, g) + (0,) * len(tail))

    return pl.pallas_call(
        functools.partial(_mlstm_kernel, nc=nc),
        grid=(nb, ng),
        in_specs=[
            zspec(0), zspec(1), zspec(2), zspec(3),
            pl.BlockSpec((seq, LANES), lambda b, g: (row_blk0 + b, g)),
            pl.BlockSpec((1, 1, LANES), lambda b, g: (g, 0, 0)),
            pl.BlockSpec((HP_A, 1, DP_A), lambda b, g: (g, 0, 0)),
            state(DP_A, 2 * DP_A), state(1, 1),
        ],
        out_specs=[pl.BlockSpec((seq, w), lambda b, g: (b, g)), state(DP_A, 2 * DP_A), state(1, 1)],
        out_shape=[
            jax.ShapeDtypeStruct((nb * seq, Y_A_COLS), F32),
            jax.ShapeDtypeStruct((nb, 2, H_A, DP_A, 2 * DP_A), F32),
            jax.ShapeDtypeStruct((nb, 2, H_A, 1, 1), F32),
        ],
        scratch_shapes=[pltpu.VMEM((seq, w), F32), pltpu.VMEM((seq, w), F32)],
        compiler_params=_cparams("arbitrary", "arbitrary"),
        name="mlstm",
    )(za, za, za, za, zg, gbias, wn, s0, m0)


def _pair_attention(qp, kslabs, vaugs, masks, sink_col):
    return _pairs_attention([qp], [kslabs], [vaugs], [masks], [sink_col])[0]


def _pairs_attention(qps, kslabs, vaugs, masks, sink_cols):
    n_p = len(qps)
    m_rows = qps[0].shape[0]
    lo = lax.broadcasted_iota(jnp.int32, qps[0].shape, 1) < HD
    q2 = [jnp.concatenate([jnp.where(lo, qp, 0.0), jnp.where(lo, 0.0, qp)], axis=0).astype(BF16) for qp in qps]
    scores = []
    for i in range(len(kslabs[0])):
        tiles = []
        for p in range(n_p):
            s = jnp.dot(q2[p], kslabs[p][i], preferred_element_type=F32)
            mk = masks[p][i]
            if mk is not None:
                s = jnp.where(mk, s, NEG_INF) if mk.dtype == jnp.bool_ else s + mk
            tiles.append(s)
        scores.append(jnp.concatenate(tiles, axis=0))
    sink = None if sink_cols[0] is None else jnp.concatenate(sink_cols, axis=0)
    mx = scores[0].max(axis=1, keepdims=True)
    for s in scores[1:]:
        mx = jnp.maximum(mx, s.max(axis=1, keepdims=True))
    if sink is not None:
        mx = jnp.maximum(mx, sink)
    probs = [jnp.exp(s - mx).astype(BF16) for s in scores]
    acc = []
    for p in range(n_p):
        rows = slice(p * 2 * m_rows, (p + 1) * 2 * m_rows)
        pv = None
        for e, va in zip(probs, vaugs[p]):
            t = jnp.dot(e[rows], va, preferred_element_type=F32)
            pv = t if pv is None else pv + t
        acc.append(pv)
    acc = jnp.concatenate(acc, axis=0)
    den = acc[:, LANES:]
    if sink is not None:
        den = den + jnp.exp(sink - mx)
    o = acc[:, :LANES] / den
    return [jnp.where(lo, o[2 * p * m_rows:(2 * p + 1) * m_rows], o[(2 * p + 1) * m_rows:(2 * p + 2) * m_rows])
            for p in range(n_p)]


def _gqa_key_slabs(kt):
    a, b = kt[:HD], kt[HD:]
    return [jnp.concatenate([a, a], 0), kt, jnp.concatenate([b, b], 0)]


def _gqa_value_pairs(v):
    lo = lax.broadcasted_iota(jnp.int32, v.shape, 1) < HD
    sw = pltpu.roll(v, HD, 1)
    ones = jnp.ones_like(v)
    return [jnp.concatenate([x, ones], 1) for x in (jnp.where(lo, v, sw), v, jnp.where(lo, sw, v))]


def _sink_col(sink_ref, p, m_rows):
    row = lax.broadcasted_iota(jnp.int32, (2 * m_rows, 1), 0)
    return jnp.where(row < m_rows, sink_ref[0:1, 2 * p:2 * p + 1], sink_ref[0:1, 2 * p + 1:2 * p + 2])


def _ctx_attn_kernel(bq_ref, bk_ref, bv_ref, cq_ref, ck_ref, cv_ref, sink_ref, yb_ref, yc_ref):
    scale = HD ** -0.5
    pairs_b = range(D_B // LANES)
    pairs_c = range(D_C // LANES)

    def tile(p):
        return slice(p * LANES, (p + 1) * LANES)

    kslabs = _gqa_key_slabs(bk_ref[...].T)
    vpairs = _gqa_value_pairs(bv_ref[...])
    outs = _pairs_attention([bq_ref[:, tile(p)] * scale for p in pairs_b], [[kslabs[p].astype(BF16)] for p in pairs_b],
                            [[vpairs[p].astype(BF16)] for p in pairs_b], [[None] for _ in pairs_b],
                            [_sink_col(sink_ref, p, L_CTX) for p in pairs_b])
    for p in pairs_b:
        yb_ref[:, tile(p)] = outs[p]
    ckt = ck_ref[...].T
    ones = jnp.ones((L_CTX, LANES), F32)
    outs = _pairs_attention([cq_ref[:, tile(p)] * scale for p in pairs_c], [[ckt[tile(p)].astype(BF16)] for p in pairs_c],
                            [[jnp.concatenate([cv_ref[:, tile(p)], ones], 1).astype(BF16)] for p in pairs_c],
                            [[None] for _ in pairs_c], [None for _ in pairs_c])
    for p in pairs_c:
        yc_ref[:, tile(p)] = outs[p]


def _ctx_attn_call(bq, bk, bv, cq, ck, cv, sink):
    def spec(n):
        return pl.BlockSpec((L_CTX, n), lambda b: (b, 0))

    return pl.pallas_call(
        _ctx_attn_kernel,
        grid=(B_CTX,),
        in_specs=[spec(D_B), spec(D_KVB), spec(D_KVB), spec(D_C), spec(D_C), spec(D_C),
                  pl.BlockSpec((1, 8), lambda b: (0, 0))],
        out_specs=[spec(D_B), spec(D_C)],
        out_shape=[jax.ShapeDtypeStruct((T_CTX, D_B), F32), jax.ShapeDtypeStruct((T_CTX, D_C), F32)],
        compiler_params=_cparams("arbitrary"),
        name="ctx_attn",
    )(bq, bk, bv, cq, ck, cv, sink)


def _rope(x, cos, sin_signed, first):
    rot = jnp.where(first, pltpu.roll(x, LANES - 16, 1), pltpu.roll(x, 16, 1))
    return x * cos + rot * sin_signed


def _win_attn_kernel(q_ref, k_ref, v_ref, kc_ref, vc_ref, cos_ref, sin_ref, sink_ref, y_ref,
                     kpt_scr, va_scr, kcp_scr, vca_scr):
    scale = HD ** -0.5
    n_pairs = D_B // LANES
    nblk = N_LAT // WIN
    nband = 3
    lane = lax.broadcasted_iota(jnp.int32, (1, LANES), 1)
    first = (lane % 32) < 16
    for p, (ks, va) in enumerate(zip(_gqa_key_slabs(kc_ref[0, 0].T), _gqa_value_pairs(vc_ref[0, 0]))):
        kcp_scr[p] = ks.astype(BF16)
        vca_scr[p] = va.astype(BF16)

    def prep(blk, c):
        r0 = pl.multiple_of(blk * WIN, WIN)
        kr = _rope(k_ref[pl.ds(r0, WIN), :], cos_ref[pl.ds(r0, WIN), :], sin_ref[pl.ds(r0, WIN), :], first)
        for p, (ks, va) in enumerate(zip(_gqa_key_slabs(kr.T), _gqa_value_pairs(v_ref[pl.ds(r0, WIN), :]))):
            kpt_scr[p, blk] = ks.astype(BF16)
            va_scr[p, pl.ds(r0, WIN), :] = va.astype(BF16)
        return c

    lax.fori_loop(0, nblk, prep, 0)

    def body(blk, c):
        q0 = pl.multiple_of(blk * WIN, WIN)
        sb = jnp.clip(blk - 1, 0, nblk - nband)
        k0 = pl.multiple_of(sb * WIN, WIN)
        cos = cos_ref[pl.ds(q0, WIN), :]
        sin = sin_ref[pl.ds(q0, WIN), :]
        row = lax.broadcasted_iota(jnp.int32, (2 * WIN, nband * WIN), 0)
        qpos = q0 + jnp.where(row < WIN, row, row - WIN)
        kpos = k0 + lax.broadcasted_iota(jnp.int32, (2 * WIN, nband * WIN), 1)
        mask = jnp.abs(kpos - qpos) <= WIN
        pairs = range(n_pairs)
        qps = [_rope(q_ref[pl.ds(q0, WIN), p * LANES:(p + 1) * LANES], cos, sin, first) * scale for p in pairs]
        k_loc = [jnp.concatenate([kpt_scr[p, sb + j] for j in range(nband)], axis=1) for p in pairs]
        v_loc = [va_scr[p, pl.ds(k0, nband * WIN), :] for p in pairs]
        outs = _pairs_attention(qps, [[kcp_scr[p], k_loc[p]] for p in pairs], [[vca_scr[p], v_loc[p]] for p in pairs],
                                [[None, mask] for _ in pairs], [_sink_col(sink_ref, p, WIN) for p in pairs])
        for p in pairs:
            y_ref[pl.ds(q0, WIN), p * LANES:(p + 1) * LANES] = outs[p]
        return c

    lax.fori_loop(0, nblk, body, 0)


def _win_attn_call(bq, bk, bv, cache_k, cache_v, cos, sin, sink, layer):
    rb0 = T_CTX // N_LAT

    def spec(n):
        return pl.BlockSpec((N_LAT, n), lambda b: (rb0 + b, 0))

    cache = pl.BlockSpec((1, 1, PAST_LEN, D_KVB), lambda b: (b, layer, 0, 0))
    tab = pl.BlockSpec((N_LAT, LANES), lambda b: (0, 0))
    return pl.pallas_call(
        _win_attn_kernel,
        grid=(B_LAT,),
        in_specs=[spec(D_B), spec(D_KVB), spec(D_KVB), cache, cache, tab, tab,
                  pl.BlockSpec((1, 8), lambda b: (0, 0))],
        out_specs=pl.BlockSpec((N_LAT, D_B), lambda b: (b, 0)),
        out_shape=jax.ShapeDtypeStruct((T_LAT, D_B), F32),
        scratch_shapes=[pltpu.VMEM((D_B // LANES, N_LAT // WIN, LANES, WIN), BF16),
                        pltpu.VMEM((D_B // LANES, N_LAT, 2 * LANES), BF16),
                        pltpu.VMEM((D_B // LANES, LANES, PAST_LEN), BF16),
                        pltpu.VMEM((D_B // LANES, PAST_LEN, 2 * LANES), BF16)],
        compiler_params=_cparams("arbitrary"),
        name="win_attn",
    )(bq, bk, bv, cache_k, cache_v, cos, sin, sink)


NA_BLK = LANES
NA_SPAN_BLKS = NA_WIN_H * GRID_W // NA_BLK + 1
NA_SPAN_ROWS = NA_SPAN_BLKS * NA_BLK // GRID_W
NA_VARIANTS = ((7, 0), (6, 0), (5, 0), (4, 0), (3, 0), (3, 1), (3, 2), (2, 2), (1, 2), (0, 2))


def _na_attn_kernel(q_ref, k_ref, v_ref, kc_ref, vc_ref, bias_ref, y_ref, kpt_scr, va_scr, kcp_scr, vca_scr):
    scale = HD ** -0.5
    rows = N_LAT // GRID_W
    n_pairs = D_C // LANES
    nblk = N_LAT // NA_BLK
    half = NA_WIN_H // 2
    kct = kc_ref[0, 0].T
    for p in range(n_pairs):
        cols = slice(p * LANES, (p + 1) * LANES)
        kcp_scr[p] = kct[cols].astype(BF16)
        vca_scr[p] = jnp.concatenate([vc_ref[0, 0, :, cols], jnp.ones((PAST_LEN, LANES), F32)], 1).astype(BF16)

    def prep(blk, c):
        r0 = pl.multiple_of(blk * NA_BLK, NA_BLK)
        kt = k_ref[pl.ds(r0, NA_BLK), :].T
        for p in range(n_pairs):
            cols = slice(p * LANES, (p + 1) * LANES)
            kpt_scr[p, blk] = kt[cols].astype(BF16)
            va_scr[p, pl.ds(r0, NA_BLK), :] = jnp.concatenate(
                [v_ref[pl.ds(r0, NA_BLK), cols], jnp.ones((NA_BLK, LANES), F32)], 1).astype(BF16)
        return c

    lax.fori_loop(0, nblk, prep, 0)

    def body(r, c):
        kr0 = jnp.clip(r - half, 0, rows - NA_WIN_H)
        sb = jnp.minimum(kr0 // 2, nblk - NA_SPAN_BLKS)
        var = jnp.where(r <= half, r, jnp.where(r >= rows - half, r - (rows - 2 * half - 2), half + (kr0 & 1)))
        q0 = pl.multiple_of(r * GRID_W, GRID_W)
        k0 = pl.multiple_of(sb * NA_BLK, NA_BLK)
        pairs = range(n_pairs)
        qps = [q_ref[pl.ds(q0, GRID_W), p * LANES:(p + 1) * LANES] * scale for p in pairs]
        k_loc = [jnp.concatenate([kpt_scr[p, sb + j] for j in range(NA_SPAN_BLKS)], axis=1) for p in pairs]
        v_loc = [va_scr[p, pl.ds(k0, NA_SPAN_BLKS * NA_BLK), :] for p in pairs]
        outs = _pairs_attention(qps, [[kcp_scr[p], k_loc[p]] for p in pairs], [[vca_scr[p], v_loc[p]] for p in pairs],
                                [[None, bias_ref[0, p, var]] for p in pairs], [None for _ in pairs])
        for p in pairs:
            y_ref[pl.ds(q0, GRID_W), p * LANES:(p + 1) * LANES] = outs[p]
        return c

    lax.fori_loop(0, rows, body, 0)


def _na_attn_call(cq, ck, cv, cache_k, cache_v, bias_tab, layer):
    rb0 = T_CTX // N_LAT
    n_pairs = D_C // LANES
    spec = pl.BlockSpec((N_LAT, D_C), lambda b: (rb0 + b, 0))
    cache = pl.BlockSpec((1, 1, PAST_LEN, D_C), lambda b: (b, layer, 0, 0))
    span = NA_SPAN_BLKS * NA_BLK
    return pl.pallas_call(
        _na_attn_kernel,
        grid=(B_LAT,),
        in_specs=[spec, spec, spec, cache, cache,
                  pl.BlockSpec((1, n_pairs, len(NA_VARIANTS), 2 * GRID_W, span), lambda b: (layer, 0, 0, 0, 0))],
        out_specs=pl.BlockSpec((N_LAT, D_C), lambda b: (b, 0)),
        out_shape=jax.ShapeDtypeStruct((T_LAT, D_C), F32),
        scratch_shapes=[pltpu.VMEM((n_pairs, N_LAT // NA_BLK, LANES, NA_BLK), BF16),
                        pltpu.VMEM((n_pairs, N_LAT, 2 * LANES), BF16),
                        pltpu.VMEM((n_pairs, LANES, PAST_LEN), BF16),
                        pltpu.VMEM((n_pairs, PAST_LEN, 2 * LANES), BF16)],
        compiler_params=_cparams("arbitrary"),
        name="na_attn",
    )(cq, ck, cv, cache_k, cache_v, bias_tab)


def _outproj_kernel(yac_ref, ybc_ref, ycc_ref, yal_ref, ybl_ref, ycl_ref, x_ref, ga_ref, scf_ref, shf_ref,
                    lng_ref, lnb_ref, wo_ref, wr_ref, x1_ref, h2_ref, route_ref, cnt_ref, run_scr):
    @pl.when(pl.program_id(0) == 0)
    def _():
        run_scr[...] = jnp.zeros_like(run_scr)

    is_ctx = pl.program_id(0) < T_CTX // TM

    def pick(c_ref, l_ref):
        return jnp.where(is_ctx, c_ref[...], l_ref[...])

    y = (_dot(pick(yac_ref, yal_ref), wo_ref[0, 0:Y_A_COLS, :])
         + _dot(pick(ybc_ref, ybl_ref), wo_ref[0, Y_A_COLS:Y_A_COLS + D_B, :])
         + _dot(pick(ycc_ref, ycl_ref), wo_ref[0, Y_A_COLS + D_B:, :]))
    x1 = _layer_norm(ALPHA * x_ref[...] + ga_ref[0, 0, 0] * y, lng_ref[...], lnb_ref[...])
    x1_ref[...] = x1
    h2 = x1 * (1.0 + scf_ref[0, 0, 0]) + shf_ref[0, 0, 0]
    h2_ref[...] = h2
    w_r = wr_ref[...]
    w_hi = w_r.astype(BF16)
    w_lo = (w_r - w_hi.astype(F32)).astype(BF16)
    h_hi = h2.astype(BF16)
    h_lo = (h2 - h_hi.astype(F32)).astype(BF16)
    p_hi = jnp.dot(h_hi, jnp.concatenate([w_hi, w_lo], axis=1), preferred_element_type=F32)
    logits = p_hi[:, :LANES] + p_hi[:, LANES:] + jnp.dot(h_lo, w_hi, preferred_element_type=F32)
    lane = lax.broadcasted_iota(jnp.int32, logits.shape, 1)
    lanef = lane.astype(F32)
    big = float(LANES)
    lg = jnp.where((lane >= N_EXPERTS) & (lane < N_EXPERTS + N_GROUPS), logits, NEG_INF)
    mg = jnp.max(lg, axis=1, keepdims=True)
    grp = jnp.min(jnp.where(lg == mg, lanef, big), axis=1, keepdims=True) - float(N_EXPERTS)
    g_w = 1.0 / jnp.sum(jnp.exp(lg - mg), axis=1, keepdims=True)
    in_grp = (lane < N_EXPERTS) & ((lane // E_PER_GROUP).astype(F32) == grp)
    le = jnp.where(in_grp, logits, NEG_INF)
    l1 = jnp.max(le, axis=1, keepdims=True)
    i1 = jnp.min(jnp.where(le == l1, lanef, big), axis=1, keepdims=True)
    le2 = jnp.where(lanef == i1, NEG_INF, le)
    l2 = jnp.max(le2, axis=1, keepdims=True)
    i2 = jnp.min(jnp.where(le2 == l2, lanef, big), axis=1, keepdims=True)
    e2 = jnp.exp(l2 - l1)
    w1 = g_w / (1.0 + e2)
    w2 = g_w * e2 / (1.0 + e2)
    oh1 = jnp.where(lanef == i1, 1.0, 0.0)
    oh2 = jnp.where(lanef == i2, 1.0, 0.0)
    rt = lax.broadcasted_iota(jnp.int32, (TM, TM), 0)
    ct = lax.broadcasted_iota(jnp.int32, (TM, TM), 1)
    before = jnp.where(ct < rt, 1.0, 0.0)
    run = run_scr[...]
    tot1 = jnp.sum(oh1, axis=0, keepdims=True)
    r1 = jnp.sum(oh1 * (run + _dot(before, oh1)), axis=1, keepdims=True)
    r2 = jnp.sum(oh2 * (run + tot1 + _dot(before, oh2)), axis=1, keepdims=True)
    run = run + tot1 + jnp.sum(oh2, axis=0, keepdims=True)
    run_scr[...] = run
    cnt_ref[...] = run
    vals = (i1, i2, w1, w2, r1, r2)
    out = jnp.zeros_like(logits)
    for n, v in enumerate(vals):
        out = jnp.where(lane == n, v, out)
    route_ref[...] = out[:, :8]


def _outproj_call(y_ctx, y_lat, x, mod_t, layer, lng, lnb, wo, wr):
    row_vec = pl.BlockSpec((1, D_MODEL), lambda i: (0, 0))
    n_ctx = T_CTX // TM

    def tok(n):
        return pl.BlockSpec((TM, n), lambda i: (i, 0))

    def tok_ctx(n):
        return pl.BlockSpec((TM, n), lambda i: (jnp.minimum(i, n_ctx - 1), 0))

    def tok_lat(n):
        return pl.BlockSpec((TM, n), lambda i: (jnp.maximum(i - n_ctx, 0), 0))

    return pl.pallas_call(
        _outproj_kernel,
        grid=(N_TILES,),
        in_specs=[tok_ctx(Y_A_COLS), tok_ctx(D_B), tok_ctx(D_C), tok_lat(Y_A_COLS), tok_lat(D_B), tok_lat(D_C),
                  tok(D_MODEL), _mod_spec(layer, 2), _mod_spec(layer, 4), _mod_spec(layer, 3), row_vec, row_vec,
                  pl.BlockSpec((1, Y_A_COLS + D_B + D_C, D_MODEL), lambda i: (layer, 0, 0)),
                  pl.BlockSpec((D_MODEL, LANES), lambda i: (0, 0))],
        out_specs=[tok(D_MODEL), tok(D_MODEL), tok(8), pl.BlockSpec((1, LANES), lambda i: (0, 0))],
        out_shape=[jax.ShapeDtypeStruct((T_ALL, D_MODEL), F32), jax.ShapeDtypeStruct((T_ALL, D_MODEL), F32),
                   jax.ShapeDtypeStruct((T_ALL, 8), F32), jax.ShapeDtypeStruct((1, LANES), F32)],
        scratch_shapes=[pltpu.VMEM((1, LANES), F32)],
        compiler_params=_cparams("arbitrary"),
        name="outproj_router",
    )(*y_ctx, *y_lat, x, mod_t, mod_t, mod_t, lng, lnb, wo, wr)


def _dispatch_kernel(pos_ref, pend_ref, h_ref, xs_hbm, zero_scr, sem):
    base = pl.program_id(0) * TM

    @pl.when(pl.program_id(0) == 0)
    def _():
        zero_scr[...] = jnp.zeros_like(zero_scr)

        def fill(e, op):
            prev = pend_ref[e - 1] if e else 0

            @pl.when(pend_ref[e] > prev)
            def _():
                first = pl.multiple_of(pend_ref[e] - MOE_BLK, MOE_BLK)
                op(pltpu.make_async_copy(zero_scr, xs_hbm.at[pl.ds(first, MOE_BLK), :], sem))

        def tail_copy(b):
            first = pl.multiple_of(b * MOE_BLK, MOE_BLK)
            return pltpu.make_async_copy(zero_scr, xs_hbm.at[pl.ds(first, MOE_BLK), :], sem)

        def tail_start(b, c):
            tail_copy(b).start()
            return c

        def tail_wait(b, c):
            tail_copy(b).wait()
            return c

        n_blocks = xs_hbm.shape[0] // MOE_BLK
        first_free = pend_ref[N_EXPERTS - 1] // MOE_BLK
        for e in range(N_EXPERTS):
            fill(e, lambda cp: cp.start())
        lax.fori_loop(first_free, n_blocks, tail_start, 0)
        for e in range(N_EXPERTS):
            fill(e, lambda cp: cp.wait())
        lax.fori_loop(first_free, n_blocks, tail_wait, 0)

    def row(t, p):
        return pltpu.make_async_copy(h_ref.at[pl.ds(t, 1), :], xs_hbm.at[pl.ds(p, 1), :], sem)

    def issue(t, c):
        row(t, pos_ref[base + t]).start()
        row(t, pos_ref[T_ALL + base + t]).start()
        return c

    lax.fori_loop(0, TM, issue, 0, unroll=8)
    whole = pltpu.make_async_copy(h_ref, xs_hbm.at[pl.ds(0, TM), :], sem)
    whole.wait()
    whole.wait()


def _dispatch_call(pos2, p_end, h2, nblk):
    grid_spec = pltpu.PrefetchScalarGridSpec(
        num_scalar_prefetch=2,
        grid=(N_TILES,),
        in_specs=[pl.BlockSpec((TM, D_MODEL), lambda i, p, pe: (i, 0))],
        out_specs=pl.BlockSpec(memory_space=pl.ANY),
        scratch_shapes=[pltpu.VMEM((MOE_BLK, D_MODEL), F32), pltpu.SemaphoreType.DMA(())],
    )
    return pl.pallas_call(
        _dispatch_kernel,
        grid_spec=grid_spec,
        out_shape=jax.ShapeDtypeStruct((nblk * MOE_BLK, D_MODEL), F32),
        compiler_params=_cparams("arbitrary"),
        name="dispatch",
    )(pos2, p_end, h2)


def _expert_kernel(blk_e_ref, nused_ref, xs_ref, wg_ref, wu_ref, wd_ref, out_ref):
    j = pl.program_id(0)

    @pl.when(j < nused_ref[0])
    def _():
        xb = xs_ref[...].astype(BF16)
        g = jnp.dot(xb, wg_ref[0, 0].astype(BF16), preferred_element_type=F32)
        u = jnp.dot(xb, wu_ref[0, 0].astype(BF16), preferred_element_type=F32)
        out_ref[...] = _dot(jax.nn.silu(g) * u, wd_ref[0, 0])

    @pl.when(j >= nused_ref[0])
    def _():
        out_ref[...] = jnp.zeros_like(out_ref)


def _expert_call(blk_e, nused, xs, wg, wu, wd, layer, nblk):
    def wspec(r, c):
        return pl.BlockSpec((1, 1, r, c), lambda j, be, nu: (layer, be[j], 0, 0))

    grid_spec = pltpu.PrefetchScalarGridSpec(
        num_scalar_prefetch=2,
        grid=(nblk,),
        in_specs=[
            pl.BlockSpec((MOE_BLK, D_MODEL), lambda j, be, nu: (jnp.clip(j, 0, jnp.maximum(nu[0] - 1, 0)), 0)),
            wspec(D_MODEL, D_EXPERT), wspec(D_MODEL, D_EXPERT), wspec(D_EXPERT, D_MODEL),
        ],
        out_specs=pl.BlockSpec((MOE_BLK, D_MODEL), lambda j, be, nu: (j, 0)),
    )
    return pl.pallas_call(
        _expert_kernel,
        grid_spec=grid_spec,
        out_shape=jax.ShapeDtypeStruct((nblk * MOE_BLK, D_MODEL), F32),
        compiler_params=_cparams("arbitrary"),
        name="experts",
    )(blk_e, nused, xs, wg, wu, wd)


def _combine_kernel(pos_ref, eo_hbm, x1_ref, route_ref, gf_ref, lng_ref, lnb_ref, x2_ref, buf, sem):
    base = pl.program_id(0) * TM

    def row(p, r, t):
        return pltpu.make_async_copy(eo_hbm.at[pl.ds(p, 1), :], buf.at[r, pl.ds(t, 1), :], sem)

    def issue(t, c):
        row(pos_ref[base + t], 0, t).start()
        row(pos_ref[T_ALL + base + t], 1, t).start()
        return c

    lax.fori_loop(0, TM, issue, 0, unroll=8)
    for r in range(2):
        pltpu.make_async_copy(eo_hbm.at[pl.ds(0, TM), :], buf.at[r], sem).wait()
    route = route_ref[...]
    y = route[:, 2:3] * buf[0] + route[:, 3:4] * buf[1]
    x2_ref[...] = _layer_norm(ALPHA * x1_ref[...] + gf_ref[0, 0, 0] * y, lng_ref[...], lnb_ref[...])


def _combine_call(pos2, eo, x1, route, mod_t, layer, lng, lnb):
    grid_spec = pltpu.PrefetchScalarGridSpec(
        num_scalar_prefetch=1,
        grid=(N_TILES,),
        in_specs=[
            pl.BlockSpec(memory_space=pl.ANY),
            pl.BlockSpec((TM, D_MODEL), lambda i, p: (i, 0)),
            pl.BlockSpec((TM, 8), lambda i, p: (i, 0)),
            _mod_spec(layer, 5),
            pl.BlockSpec((1, D_MODEL), lambda i, p: (0, 0)),
            pl.BlockSpec((1, D_MODEL), lambda i, p: (0, 0)),
        ],
        out_specs=pl.BlockSpec((TM, D_MODEL), lambda i, p: (i, 0)),
        scratch_shapes=[pltpu.VMEM((2, TM, D_MODEL), F32), pltpu.SemaphoreType.DMA(())],
    )
    return pl.pallas_call(
        _combine_kernel,
        grid_spec=grid_spec,
        out_shape=jax.ShapeDtypeStruct((T_ALL, D_MODEL), F32),
        compiler_params=_cparams("arbitrary"),
        name="combine_norm",
    )(pos2, eo, x1, route, mod_t, lng, lnb)


def _gate_cols(grp):
    return np.array([kind * 2 * H_A + d * H_A + grp * HP_A + j
                     for j in range(HP_A) for kind in range(2) for d in range(2)])


def _prep_w_in(w_in):
    a = w_in[..., :4 * D_A].reshape(DEPTH, D_MODEL, 4, H_A, DH_A)
    a = jnp.pad(a, ((0, 0), (0, 0), (0, 0), (0, 0), (0, DP_A - DH_A))).reshape(DEPTH, D_MODEL, ZA_COLS)
    gates = w_in[..., 4 * D_A:4 * D_A + 4 * H_A]
    g = jnp.concatenate([jnp.pad(gates[..., _gate_cols(grp)], ((0, 0), (0, 0), (0, LANES - 4 * HP_A)))
                         for grp in range(H_A // HP_A)], -1)
    rest = w_in[..., 4 * D_A + 4 * H_A:]
    return jnp.concatenate([a, rest, g], -1).astype(BF16)


def _prep_w_out(w_out):
    a = w_out[:, :D_A].reshape(DEPTH, H_A, DH_A, D_MODEL)
    a = jnp.pad(a, ((0, 0), (0, 0), (0, DP_A - DH_A), (0, 0))).reshape(DEPTH, Y_A_COLS, D_MODEL)
    return jnp.concatenate([a, w_out[:, D_A:]], 1).astype(BF16)


def _rope_tables():
    t = np.arange(N_LAT)
    nf = HD // 4
    inv = ROPE_BASE ** (-np.arange(nf, dtype=np.float32) / nf)
    ar = (t // GRID_W).astype(np.float32)[:, None] * inv
    ac = (t % GRID_W).astype(np.float32)[:, None] * inv
    ang = jnp.asarray(np.concatenate([ar, ar, ac, ac], -1), F32)
    cos, sin = jnp.cos(ang), jnp.sin(ang)
    sign = np.where((np.arange(HD) % 32) < 16, -1.0, 1.0).astype(np.float32)
    reps = LANES // HD
    return jnp.tile(cos, (1, reps)), jnp.tile(sin * sign, (1, reps))


def _na_bias_tables(rpb):
    qcol = np.arange(GRID_W)[:, None]
    kcol = np.arange(GRID_W)[None, :]
    dc = np.clip(kcol - qcol, 1 - NA_WIN_W, NA_WIN_W - 1) + NA_WIN_W - 1
    wstart = np.clip(qcol - NA_WIN_W // 2, 0, GRID_W - NA_WIN_W)
    in_win = (kcol >= wstart) & (kcol < wstart + NA_WIN_W)
    sel = (np.arange(2 * NA_WIN_W - 1)[:, None] == dc.reshape(1, -1)).astype(np.float32)
    cols = jnp.einsum("lhrd,dn->lhrn", rpb, jnp.asarray(sel), precision=lax.Precision.HIGHEST)
    cols = jnp.where(in_win.reshape(-1), cols, NEG_INF).reshape(DEPTH, H_C, 2 * NA_WIN_H - 1, GRID_W, GRID_W)
    outside = jnp.full((DEPTH, H_C, GRID_W, GRID_W), NEG_INF, F32)
    variants = []
    for dr0, off in NA_VARIANTS:
        span_rows = [cols[:, :, dr0 + i - off] if 0 <= i - off < NA_WIN_H else outside for i in range(NA_SPAN_ROWS)]
        variants.append(jnp.stack(span_rows, 3).reshape(DEPTH, H_C, GRID_W, NA_SPAN_ROWS * GRID_W))
    tab = jnp.stack(variants, 2).reshape(DEPTH, H_C // 2, 2, len(NA_VARIANTS), GRID_W, NA_SPAN_ROWS * GRID_W)
    return tab.transpose(0, 1, 3, 2, 4, 5).reshape(DEPTH, H_C // 2, len(NA_VARIANTS), 2 * GRID_W, NA_SPAN_ROWS * GRID_W)


def _dispatch_plan(route, counts):
    nblk = 2 * T_ALL // MOE_BLK + N_EXPERTS
    cnt = counts[0, :N_EXPERTS].astype(jnp.int32)
    padded = (cnt + MOE_BLK - 1) // MOE_BLK * MOE_BLK
    p_end = jnp.cumsum(padded)
    p_start = p_end - padded
    e = route[:, 0:2].astype(jnp.int32)
    hot = e[..., None] == jnp.arange(N_EXPERTS, dtype=jnp.int32)
    pos = jnp.sum(jnp.where(hot, p_start, 0), -1) + route[:, 4:6].astype(jnp.int32)
    pos2 = pos.T.reshape(-1)
    blk_first = jnp.arange(nblk, dtype=jnp.int32) * MOE_BLK
    blk_e = jnp.minimum(jnp.sum((p_end[None, :] <= blk_first[:, None]).astype(jnp.int32), axis=1), N_EXPERTS - 1)
    nused = p_end[-1:] // MOE_BLK
    return blk_e, nused, pos2, p_end, nblk


def kernel(x_prompt, x_sample, state_a_C, state_a_n, state_a_m, cache_b_k, cache_b_v, cache_c_k, cache_c_v, c, c_ctx, w_in, b_a_i, b_a_f, w_a_hnorm, b_sink, rpb, w_out, w_ada, b_ada, ln_g, ln_b, w_router_grp, w_router_exp, w_e_gate, w_e_up, w_e_down):
    w_in_p = _prep_w_in(w_in)
    w_out_p = _prep_w_out(w_out)
    w_r = jnp.pad(jnp.concatenate([w_router_exp, w_router_grp], -1),
                  ((0, 0), (0, 0), (0, LANES - N_EXPERTS - N_GROUPS)))
    wn_p = jnp.pad(w_a_hnorm.reshape(DEPTH, H_A, 1, DH_A), ((0, 0), (0, 0), (0, 0), (0, DP_A - DH_A)))
    gate_b = jnp.concatenate([b_a_i, b_a_f], 1).transpose(0, 2, 1)
    gate_b = jnp.pad(gate_b.reshape(DEPTH, H_A // HP_A, 1, 4 * HP_A), ((0, 0), (0, 0), (0, 0), (0, LANES - 4 * HP_A)))
    sink_p = jnp.pad(b_sink, ((0, 0), (0, 8 - H_B))).reshape(DEPTH, 1, 8)
    cos_t, sin_t = _rope_tables()
    na_bias = _na_bias_tables(rpb)
    cb_k = cache_b_k.reshape(B_LAT, DEPTH, PAST_LEN, D_KVB)
    cb_v = cache_b_v.reshape(B_LAT, DEPTH, PAST_LEN, D_KVB)
    cc_k = cache_c_k.reshape(B_LAT, DEPTH, PAST_LEN, D_C)
    cc_v = cache_c_v.reshape(B_LAT, DEPTH, PAST_LEN, D_C)
    pad_c = ((0, 0), (0, 0), (0, 0), (0, 0), (0, DP_A - DH_A), (0, DP_A - DH_A))
    st_ct = jnp.swapaxes(jnp.pad(state_a_C, pad_c), -1, -2)
    st_nr = jnp.broadcast_to(jnp.pad(state_a_n, pad_c[:-1])[..., None], st_ct.shape)
    st_s = jnp.concatenate([st_ct, st_nr], -1)
    st_m = state_a_m[..., None, None]
    z_s = jnp.zeros((B_CTX, 2, H_A, DP_A, 2 * DP_A), F32)
    z_m = jnp.zeros((B_CTX, 2, H_A, 1, 1), F32)

    cvec = jnp.concatenate([c, c_ctx[None, :], jnp.zeros((3, D_MODEL), F32)], 0)
    mod = _ada_call(cvec, w_ada, b_ada)
    tile_row = np.concatenate([np.full(T_CTX // TM, B_LAT), np.repeat(np.arange(B_LAT), N_LAT // TM)])
    mod_t = mod[:, tile_row].reshape(DEPTH, N_TILES, 6, 1, D_MODEL)

    x = jnp.concatenate([x_prompt.reshape(T_CTX, D_MODEL), x_sample.reshape(T_LAT, D_MODEL)], 0)
    cs_, ns_, ms_, kbs, vbs, kcs, vcs = [], [], [], [], [], [], []
    for l in range(DEPTH):
        za, bq, bk, bv, cq, ck, cv, zg = _inproj_call(x, mod_t, l, w_in_p)
        ya_c, s_l, m_l = _mlstm_call(za, zg, gate_b[l], wn_p[l], z_s, z_m, nb=B_CTX, seq=L_CTX, row_blk0=0)
        yb_c, yc_c = _ctx_attn_call(bq, bk, bv, cq, ck, cv, sink_p[l])
        ya_l, _, _ = _mlstm_call(za, zg, gate_b[l], wn_p[l], st_s[:, l], st_m[:, l],
                                 nb=B_LAT, seq=N_LAT, row_blk0=T_CTX // N_LAT)
        yb_l = _win_attn_call(bq, bk, bv, cb_k, cb_v, cos_t, sin_t, sink_p[l], l)
        yc_l = _na_attn_call(cq, ck, cv, cc_k, cc_v, na_bias, l)
        x1, h2, route, counts = _outproj_call((ya_c, yb_c, yc_c), (ya_l, yb_l, yc_l), x, mod_t, l,
                                              ln_g[l, 0:1], ln_b[l, 0:1], w_out_p, w_r[l])
        blk_e, nused, pos2, p_end, nblk = _dispatch_plan(route, counts)
        xs = _dispatch_call(pos2, p_end, h2, nblk)
        eo = _expert_call(blk_e, nused, xs, w_e_gate, w_e_up, w_e_down, l, nblk)
        x = _combine_call(pos2, eo, x1, route, mod_t, l, ln_g[l, 1:2], ln_b[l, 1:2])
        cs_.append(jnp.swapaxes(s_l[..., :DH_A, :DH_A], -1, -2))
        ns_.append(s_l[..., :DH_A, DP_A])
        ms_.append(m_l.reshape(B_CTX, 2, H_A))
        kbs.append(bk[:T_CTX].reshape(B_CTX, L_CTX, KV_B, HD))
        vbs.append(bv[:T_CTX].reshape(B_CTX, L_CTX, KV_B, HD))
        kcs.append(ck[:T_CTX].reshape(B_CTX, L_CTX, H_C, HD))
        vcs.append(cv[:T_CTX].reshape(B_CTX, L_CTX, H_C, HD))
    y_prompt = x[:T_CTX].reshape(B_CTX, L_CTX, D_MODEL)
    y_sample = x[T_CTX:].reshape(B_LAT, N_LAT, D_MODEL)
    return (y_prompt, y_sample, jnp.stack(cs_, 1), jnp.stack(ns_, 1), jnp.stack(ms_, 1),
            jnp.stack(kbs, 1), jnp.stack(vbs, 1), jnp.stack(kcs, 1), jnp.stack(vcs, 1))
```

```python
import functools

import numpy as np
import jax
import jax.numpy as jnp
from jax import lax
from jax.experimental import pallas as pl
from jax.experimental.pallas import tpu as pltpu

F32 = jnp.float32
BF16 = jnp.bfloat16
NEG_INF = float("-inf")

D_MODEL = 1024
DEPTH = 4
B_CTX, L_CTX = 16, 256
B_LAT, N_LAT = 4, 2048
PAST_LEN = 512
GRID_W = 64
H_A, DH_A = 4, 96
D_A = H_A * DH_A
H_B, KV_B, HD = 6, 2, 64
G_B = H_B // KV_B
D_B, D_KVB = H_B * HD, KV_B * HD
WIN = 128
ROPE_BASE = 10000.0
H_C = 4
D_C = H_C * HD
NA_WIN_H, NA_WIN_W = 8, 16
N_GROUPS, E_PER_GROUP = 4, 8
N_EXPERTS = N_GROUPS * E_PER_GROUP
D_EXPERT = D_MODEL // 4
ALPHA = (2 * DEPTH) ** 0.25
LN_EPS = 1e-5

LANES = 128
DP_A = LANES
LC_K = LANES
TM = 256
MOE_BLK = 256
VMEM_LIMIT = 48 * 1024 * 1024

T_CTX = B_CTX * L_CTX
T_LAT = B_LAT * N_LAT
T_ALL = T_CTX + T_LAT
N_TILES = T_ALL // TM
ZA_COLS = 4 * H_A * DP_A
Y_A_COLS = H_A * DP_A
IN_SPLITS = (ZA_COLS, D_B, D_KVB, D_KVB, D_C, D_C, D_C, 2 * LANES)
IN_COLS = sum(IN_SPLITS)


def _cparams(*sem):
    return pltpu.CompilerParams(dimension_semantics=sem, vmem_limit_bytes=VMEM_LIMIT)


def _dot(a, b):
    return jnp.dot(a.astype(BF16), b.astype(BF16), preferred_element_type=F32)


def _dot_nt(a, b):
    return lax.dot_general(a.astype(BF16), b.astype(BF16), (((1,), (1,)), ((), ())), preferred_element_type=F32)


def _dot_tn(a, b):
    return lax.dot_general(a.astype(BF16), b.astype(BF16), (((0,), (0,)), ((), ())), preferred_element_type=F32)


def _layer_norm(v, g, b):
    mu = jnp.mean(v, -1, keepdims=True)
    var = jnp.mean(jnp.square(v - mu), -1, keepdims=True)
    return (v - mu) * lax.rsqrt(var + LN_EPS) * g + b


def _ada_kernel(c_ref, w_ref, b_ref, o_ref):
    s = jax.nn.silu(c_ref[...])
    o_ref[0] = _dot(s, w_ref[0]) + b_ref[0]


def _ada_call(cvec, w_ada, b_ada):
    nb = 6
    return pl.pallas_call(
        _ada_kernel,
        grid=(DEPTH, nb),
        in_specs=[
            pl.BlockSpec((8, D_MODEL), lambda l, j: (0, 0)),
            pl.BlockSpec((1, D_MODEL, D_MODEL), lambda l, j: (l, 0, j)),
            pl.BlockSpec((1, 1, D_MODEL), lambda l, j: (l, 0, j)),
        ],
        out_specs=pl.BlockSpec((1, 8, D_MODEL), lambda l, j: (l, 0, j)),
        out_shape=jax.ShapeDtypeStruct((DEPTH, 8, 6 * D_MODEL), F32),
        compiler_params=_cparams("arbitrary", "arbitrary"),
        name="adaln",
    )(cvec, w_ada, b_ada.reshape(DEPTH, 1, 6 * D_MODEL))


def _inproj_kernel(x_ref, sc_ref, sh_ref, w_ref, *out_refs):
    h = (x_ref[...] * (1.0 + sc_ref[0, 0, 0]) + sh_ref[0, 0, 0]).astype(BF16)
    off = 0
    for ref in out_refs:
        n = ref.shape[-1]
        ref[...] = jnp.dot(h, w_ref[0, :, off:off + n], preferred_element_type=F32)
        off += n


def _mod_spec(layer, which):
    return pl.BlockSpec((1, 1, 1, 1, D_MODEL), lambda i, *_: (layer, i, which, 0, 0))


def _inproj_call(x, mod_t, layer, w):
    return pl.pallas_call(
        _inproj_kernel,
        grid=(N_TILES,),
        in_specs=[
            pl.BlockSpec((TM, D_MODEL), lambda i: (i, 0)),
            _mod_spec(layer, 1), _mod_spec(layer, 0),
            pl.BlockSpec((1, D_MODEL, IN_COLS), lambda i: (layer, 0, 0)),
        ],
        out_specs=[pl.BlockSpec((TM, n), lambda i: (i, 0)) for n in IN_SPLITS],
        out_shape=[jax.ShapeDtypeStruct((T_ALL, n), F32) for n in IN_SPLITS],
        compiler_params=_cparams("arbitrary"),
        name="inproj",
    )(x, mod_t, mod_t, w)


HP_A = 2
N_CH = 2 * HP_A


def _mlstm_kernel(q_ref, k_ref, v_ref, o_ref, g_ref, gb_ref, wn_ref, s0_ref, m0_ref,
                  y_ref, so_ref, mo_ref, hf_scr, hb_scr, *, nc):
    lc = LC_K
    scale = DH_A ** -0.5
    chains = [(j, d) for j in range(HP_A) for d in range(2)]
    ti = lax.broadcasted_iota(jnp.int32, (lc, lc), 0)
    si = lax.broadcasted_iota(jnp.int32, (lc, lc), 1)
    lane_ok = lax.broadcasted_iota(jnp.int32, (1, DP_A), 1) < DH_A
    ones = jnp.ones((lc, DP_A), F32)

    def stack(parts):
        return jnp.concatenate(parts, axis=0)

    def rows_of(x, a):
        return a[x * lc:(x + 1) * lc]

    def spread(vals):
        return stack([jnp.broadcast_to(v, (lc, v.shape[1])) for v in vals])

    mask = stack([si <= ti if d == 0 else si >= ti for _, d in chains])
    mask_t = stack([si >= ti if d == 0 else si <= ti for _, d in chains])
    eye = stack([si == ti for _ in chains])

    def col_sums(a):
        return [jnp.sum(rows_of(x, a), axis=0, keepdims=True) for x in range(N_CH)]

    def body(i, carry):
        smats = carry[:N_CH]
        ms = carry[N_CH:]
        r0s = [pl.multiple_of((i if d == 0 else nc - 1 - i) * lc, lc) for _, d in chains]
        q, kt, v1, icol, fpre = [], [], [], [], []
        for (j, d), r0 in zip(chains, r0s):
            cols = slice(j * DP_A, (j + 1) * DP_A)
            q.append(q_ref[pl.ds(r0, lc), cols].astype(BF16))
            kt.append((k_ref[pl.ds(r0, lc), cols] * scale).T.astype(BF16))
            v1.append(jnp.concatenate([v_ref[pl.ds(r0, lc), cols], ones], axis=1))
            gz = g_ref[pl.ds(r0, lc), :] + gb_ref[0]
            icol.append(gz[:, 4 * j + d:4 * j + d + 1])
            fpre.append(gz[:, 4 * j + 2 + d:4 * j + 3 + d])
        i_col = stack(icol)
        f_col = jax.nn.log_sigmoid(stack(fpre))
        b_rows = col_sums(jnp.where(mask_t, f_col, 0.0))
        bls = col_sums(f_col)
        b_col = jnp.sum(jnp.where(eye, spread(b_rows), 0.0), axis=1, keepdims=True)
        a_rows = col_sums(jnp.where(eye, i_col - b_col, 0.0))
        a_sp = jnp.where(mask, spread(a_rows), NEG_INF)
        m_sp = spread(ms)
        gap = jnp.maximum(m_sp, jnp.max(a_sp, axis=1, keepdims=True))
        m_out = b_col + gap
        wmat = jnp.exp(a_sp - gap)
        sw = (stack([jnp.dot(q[x], kt[x], preferred_element_type=F32) for x in range(N_CH)]) * wmat).astype(BF16)
        sc_in = jnp.exp(m_sp - gap)
        tot = (stack([jnp.dot(rows_of(x, sw), v1[x].astype(BF16), preferred_element_type=F32) for x in range(N_CH)])
               + sc_in * stack([_dot(q[x], smats[x]) for x in range(N_CH)]))
        h = tot[:, :DP_A] / jnp.maximum(jnp.abs(tot[:, DP_A:]), jnp.exp(-m_out))
        for x, ((j, d), r0) in enumerate(zip(chains, r0s)):
            dst = hf_scr if d == 0 else hb_scr
            dst[pl.ds(r0, lc), j * DP_A:(j + 1) * DP_A] = rows_of(x, h)
        dec = spread(bls) - b_col + i_col
        m_new = [jnp.maximum(bls[x] + ms[x], jnp.max(rows_of(x, dec), axis=0, keepdims=True)) for x in range(N_CH)]
        wk = jnp.exp(dec - spread(m_new))
        s_new = [jnp.exp(bls[x] + ms[x] - m_new[x]) * smats[x] + _dot(kt[x], rows_of(x, wk) * v1[x])
                 for x in range(N_CH)]
        return tuple(s_new) + tuple(m_new)

    init = tuple(s0_ref[0, d, j] for j, d in chains) + tuple(m0_ref[0, d, j] for j, d in chains)
    final = lax.fori_loop(0, nc, body, init)
    for x, (j, d) in enumerate(chains):
        so_ref[0, d, j] = final[x]
        mo_ref[0, d, j] = final[N_CH + x]

    def finish(c, _):
        r0 = pl.multiple_of(c * lc, lc)
        for j in range(HP_A):
            cols = slice(j * DP_A, (j + 1) * DP_A)
            h = hf_scr[pl.ds(r0, lc), cols] + hb_scr[pl.ds(r0, lc), cols]
            mu = jnp.sum(h, axis=1, keepdims=True) * (1.0 / DH_A)
            dv = jnp.where(lane_ok, h - mu, 0.0)
            var = jnp.sum(dv * dv, axis=1, keepdims=True) * (1.0 / DH_A)
            hn = dv * lax.rsqrt(var + LN_EPS) * wn_ref[j]
            y_ref[pl.ds(r0, lc), cols] = jax.nn.sigmoid(o_ref[pl.ds(r0, lc), cols]) * hn
        return 0

    lax.fori_loop(0, nc, finish, 0)


def _mlstm_call(za, zg, gbias, wn, s0, m0, *, nb, seq, row_blk0):
    nc = seq // LC_K
    ng = H_A // HP_A
    w = HP_A * DP_A

    def zspec(part):
        return pl.BlockSpec((seq, w), lambda b, g: (row_blk0 + b, part * ng + g))

    def state(*tail):
        return pl.BlockSpec((1, 2, HP_A) + tail, lambda b, g: (b, 0, g) + (0,) * len(tail))

    return pl.pallas_call(
        functools.partial(_mlstm_kernel, nc=nc),
        grid=(nb, ng),
        in_specs=[
            zspec(0), zspec(1), zspec(2), zspec(3),
            pl.BlockSpec((seq, LANES), lambda b, g: (row_blk0 + b, g)),
            pl.BlockSpec((1, 1, LANES), lambda b, g: (g, 0, 0)),
            pl.BlockSpec((HP_A, 1, DP_A), lambda b, g: (g, 0, 0)),
            state(DP_A, 2 * DP_A), state(1, 1),
        ],
        out_specs=[pl.BlockSpec((seq, w), lambda b, g: (b, g)), state(DP_A, 2 * DP_A), state(1, 1)],
        out_shape=[
            jax.ShapeDtypeStruct((nb * seq, Y_A_COLS), F32),
            jax.ShapeDtypeStruct((nb, 2, H_A, DP_A, 2 * DP_A), F32),
            jax.ShapeDtypeStruct((nb, 2, H_A, 1, 1), F32),
        ],
        scratch_shapes=[pltpu.VMEM((seq, w), F32), pltpu.VMEM((seq, w), F32)],
        compiler_params=_cparams("arbitrary", "arbitrary"),
        name="mlstm",
    )(za, za, za, za, zg, gbias, wn, s0, m0)


def _pair_attention(qp, kslabs, vaugs, masks, sink_col):
    return _pairs_attention([qp], [kslabs], [vaugs], [masks], [sink_col])[0]


def _pairs_attention(qps, kslabs, vaugs, masks, sink_cols):
    n_p = len(qps)
    m_rows = qps[0].shape[0]
    lo = lax.broadcasted_iota(jnp.int32, qps[0].shape, 1) < HD
    q2 = [jnp.concatenate([jnp.where(lo, qp, 0.0), jnp.where(lo, 0.0, qp)], axis=0).astype(BF16) for qp in qps]
    scores = []
    for i in range(len(kslabs[0])):
        tiles = []
        for p in range(n_p):
            s = jnp.dot(q2[p], kslabs[p][i], preferred_element_type=F32)
            mk = masks[p][i]
            if mk is not None:
                s = jnp.where(mk, s, NEG_INF) if mk.dtype == jnp.bool_ else s + mk
            tiles.append(s)
        scores.append(jnp.concatenate(tiles, axis=0))
    sink = None if sink_cols[0] is None else jnp.concatenate(sink_cols, axis=0)
    mx = scores[0].max(axis=1, keepdims=True)
    for s in scores[1:]:
        mx = jnp.maximum(mx, s.max(axis=1, keepdims=True))
    if sink is not None:
        mx = jnp.maximum(mx, sink)
    probs = [jnp.exp(s - mx).astype(BF16) for s in scores]
    acc = []
    for p in range(n_p):
        rows = slice(p * 2 * m_rows, (p + 1) * 2 * m_rows)
        pv = None
        for e, va in zip(probs, vaugs[p]):
            t = jnp.dot(e[rows], va, preferred_element_type=F32)
            pv = t if pv is None else pv + t
        acc.append(pv)
    acc = jnp.concatenate(acc, axis=0)
    den = acc[:, LANES:]
    if sink is not None:
        den = den + jnp.exp(sink - mx)
    o = acc[:, :LANES] / den
    return [jnp.where(lo, o[2 * p * m_rows:(2 * p + 1) * m_rows], o[(2 * p + 1) * m_rows:(2 * p + 2) * m_rows])
            for p in range(n_p)]


def _gqa_key_slabs(kt):
    a, b = kt[:HD], kt[HD:]
    return [jnp.concatenate([a, a], 0), kt, jnp.concatenate([b, b], 0)]


def _gqa_value_pairs(v):
    lo = lax.broadcasted_iota(jnp.int32, v.shape, 1) < HD
    sw = pltpu.roll(v, HD, 1)
    ones = jnp.ones_like(v)
    return [jnp.concatenate([x, ones], 1) for x in (jnp.where(lo, v, sw), v, jnp.where(lo, sw, v))]


def _sink_col(sink_ref, p, m_rows):
    row = lax.broadcasted_iota(jnp.int32, (2 * m_rows, 1), 0)
    return jnp.where(row < m_rows, sink_ref[0:1, 2 * p:2 * p + 1], sink_ref[0:1, 2 * p + 1:2 * p + 2])


def _ctx_attn_kernel(bq_ref, bk_ref, bv_ref, cq_ref, ck_ref, cv_ref, sink_ref, yb_ref, yc_ref):
    scale = HD ** -0.5
    pairs_b = range(D_B // LANES)
    pairs_c = range(D_C // LANES)

    def tile(p):
        return slice(p * LANES, (p + 1) * LANES)

    kslabs = _gqa_key_slabs(bk_ref[...].T)
    vpairs = _gqa_value_pairs(bv_ref[...])
    outs = _pairs_attention([bq_ref[:, tile(p)] * scale for p in pairs_b], [[kslabs[p].astype(BF16)] for p in pairs_b],
                            [[vpairs[p].astype(BF16)] for p in pairs_b], [[None] for _ in pairs_b],
                            [_sink_col(sink_ref, p, L_CTX) for p in pairs_b])
    for p in pairs_b:
        yb_ref[:, tile(p)] = outs[p]
    ckt = ck_ref[...].T
    ones = jnp.ones((L_CTX, LANES), F32)
    outs = _pairs_attention([cq_ref[:, tile(p)] * scale for p in pairs_c], [[ckt[tile(p)].astype(BF16)] for p in pairs_c],
                            [[jnp.concatenate([cv_ref[:, tile(p)], ones], 1).astype(BF16)] for p in pairs_c],
                            [[None] for _ in pairs_c], [None for _ in pairs_c])
    for p in pairs_c:
        yc_ref[:, tile(p)] = outs[p]


def _ctx_attn_call(bq, bk, bv, cq, ck, cv, sink):
    def spec(n):
        return pl.BlockSpec((L_CTX, n), lambda b: (b, 0))

    return pl.pallas_call(
        _ctx_attn_kernel,
        grid=(B_CTX,),
        in_specs=[spec(D_B), spec(D_KVB), spec(D_KVB), spec(D_C), spec(D_C), spec(D_C),
                  pl.BlockSpec((1, 8), lambda b: (0, 0))],
        out_specs=[spec(D_B), spec(D_C)],
        out_shape=[jax.ShapeDtypeStruct((T_CTX, D_B), F32), jax.ShapeDtypeStruct((T_CTX, D_C), F32)],
        compiler_params=_cparams("arbitrary"),
        name="ctx_attn",
    )(bq, bk, bv, cq, ck, cv, sink)


def _rope(x, cos, sin_signed, first):
    rot = jnp.where(first, pltpu.roll(x, LANES - 16, 1), pltpu.roll(x, 16, 1))
    return x * cos + rot * sin_signed


def _win_attn_kernel(q_ref, k_ref, v_ref, kc_ref, vc_ref, cos_ref, sin_ref, sink_ref, y_ref,
                     kpt_scr, va_scr, kcp_scr, vca_scr):
    scale = HD ** -0.5
    n_pairs = D_B // LANES
    nblk = N_LAT // WIN
    nband = 3
    lane = lax.broadcasted_iota(jnp.int32, (1, LANES), 1)
    first = (lane % 32) < 16
    for p, (ks, va) in enumerate(zip(_gqa_key_slabs(kc_ref[0, 0].T), _gqa_value_pairs(vc_ref[0, 0]))):
        kcp_scr[p] = ks.astype(BF16)
        vca_scr[p] = va.astype(BF16)

    def prep(blk, c):
        r0 = pl.multiple_of(blk * WIN, WIN)
        kr = _rope(k_ref[pl.ds(r0, WIN), :], cos_ref[pl.ds(r0, WIN), :], sin_ref[pl.ds(r0, WIN), :], first)
        for p, (ks, va) in enumerate(zip(_gqa_key_slabs(kr.T), _gqa_value_pairs(v_ref[pl.ds(r0, WIN), :]))):
            kpt_scr[p, blk] = ks.astype(BF16)
            va_scr[p, pl.ds(r0, WIN), :] = va.astype(BF16)
        return c

    lax.fori_loop(0, nblk, prep, 0)

    def body(blk, c):
        q0 = pl.multiple_of(blk * WIN, WIN)
        sb = jnp.clip(blk - 1, 0, nblk - nband)
        k0 = pl.multiple_of(sb * WIN, WIN)
        cos = cos_ref[pl.ds(q0, WIN), :]
        sin = sin_ref[pl.ds(q0, WIN), :]
        row = lax.broadcasted_iota(jnp.int32, (2 * WIN, nband * WIN), 0)
        qpos = q0 + jnp.where(row < WIN, row, row - WIN)
        kpos = k0 + lax.broadcasted_iota(jnp.int32, (2 * WIN, nband * WIN), 1)
        mask = jnp.abs(kpos - qpos) <= WIN
        pairs = range(n_pairs)
        qps = [_rope(q_ref[pl.ds(q0, WIN), p * LANES:(p + 1) * LANES], cos, sin, first) * scale for p in pairs]
        k_loc = [jnp.concatenate([kpt_scr[p, sb + j] for j in range(nband)], axis=1) for p in pairs]
        v_loc = [va_scr[p, pl.ds(k0, nband * WIN), :] for p in pairs]
        outs = _pairs_attention(qps, [[kcp_scr[p], k_loc[p]] for p in pairs], [[vca_scr[p], v_loc[p]] for p in pairs],
                                [[None, mask] for _ in pairs], [_sink_col(sink_ref, p, WIN) for p in pairs])
        for p in pairs:
            y_ref[pl.ds(q0, WIN), p * LANES:(p + 1) * LANES] = outs[p]
        return c

    lax.fori_loop(0, nblk, body, 0)


def _win_attn_call(bq, bk, bv, cache_k, cache_v, cos, sin, sink, layer):
    rb0 = T_CTX // N_LAT

    def spec(n):
        return pl.BlockSpec((N_LAT, n), lambda b: (rb0 + b, 0))

    cache = pl.BlockSpec((1, 1, PAST_LEN, D_KVB), lambda b: (b, layer, 0, 0))
    tab = pl.BlockSpec((N_LAT, LANES), lambda b: (0, 0))
    return pl.pallas_call(
        _win_attn_kernel,
        grid=(B_LAT,),
        in_specs=[spec(D_B), spec(D_KVB), spec(D_KVB), cache, cache, tab, tab,
                  pl.BlockSpec((1, 8), lambda b: (0, 0))],
        out_specs=pl.BlockSpec((N_LAT, D_B), lambda b: (b, 0)),
        out_shape=jax.ShapeDtypeStruct((T_LAT, D_B), F32),
        scratch_shapes=[pltpu.VMEM((D_B // LANES, N_LAT // WIN, LANES, WIN), BF16),
                        pltpu.VMEM((D_B // LANES, N_LAT, 2 * LANES), BF16),
                        pltpu.VMEM((D_B // LANES, LANES, PAST_LEN), BF16),
                        pltpu.VMEM((D_B // LANES, PAST_LEN, 2 * LANES), BF16)],
        compiler_params=_cparams("arbitrary"),
        name="win_attn",
    )(bq, bk, bv, cache_k, cache_v, cos, sin, sink)


NA_BLK = LANES
NA_SPAN_BLKS = NA_WIN_H * GRID_W // NA_BLK + 1
NA_SPAN_ROWS = NA_SPAN_BLKS * NA_BLK // GRID_W
NA_VARIANTS = ((7, 0), (6, 0), (5, 0), (4, 0), (3, 0), (3, 1), (3, 2), (2, 2), (1, 2), (0, 2))


def _na_attn_kernel(q_ref, k_ref, v_ref, kc_ref, vc_ref, bias_ref, y_ref, kpt_scr, va_scr, kcp_scr, vca_scr):
    scale = HD ** -0.5
    rows = N_LAT // GRID_W
    n_pairs = D_C // LANES
    nblk = N_LAT // NA_BLK
    half = NA_WIN_H // 2
    kct = kc_ref[0, 0].T
    for p in range(n_pairs):
        cols = slice(p * LANES, (p + 1) * LANES)
        kcp_scr[p] = kct[cols].astype(BF16)
        vca_scr[p] = jnp.concatenate([vc_ref[0, 0, :, cols], jnp.ones((PAST_LEN, LANES), F32)], 1).astype(BF16)

    def prep(blk, c):
        r0 = pl.multiple_of(blk * NA_BLK, NA_BLK)
        kt = k_ref[pl.ds(r0, NA_BLK), :].T
        for p in range(n_pairs):
            cols = slice(p * LANES, (p + 1) * LANES)
            kpt_scr[p, blk] = kt[cols].astype(BF16)
            va_scr[p, pl.ds(r0, NA_BLK), :] = jnp.concatenate(
                [v_ref[pl.ds(r0, NA_BLK), cols], jnp.ones((NA_BLK, LANES), F32)], 1).astype(BF16)
        return c

    lax.fori_loop(0, nblk, prep, 0)

    def body(r, c):
        kr0 = jnp.clip(r - half, 0, rows - NA_WIN_H)
        sb = jnp.minimum(kr0 // 2, nblk - NA_SPAN_BLKS)
        var = jnp.where(r <= half, r, jnp.where(r >= rows - half, r - (rows - 2 * half - 2), half + (kr0 & 1)))
        q0 = pl.multiple_of(r * GRID_W, GRID_W)
        k0 = pl.multiple_of(sb * NA_BLK, NA_BLK)
        pairs = range(n_pairs)
        qps = [q_ref[pl.ds(q0, GRID_W), p * LANES:(p + 1) * LANES] * scale for p in pairs]
        k_loc = [jnp.concatenate([kpt_scr[p, sb + j] for j in range(NA_SPAN_BLKS)], axis=1) for p in pairs]
        v_loc = [va_scr[p, pl.ds(k0, NA_SPAN_BLKS * NA_BLK), :] for p in pairs]
        outs = _pairs_attention(qps, [[kcp_scr[p], k_loc[p]] for p in pairs], [[vca_scr[p], v_loc[p]] for p in pairs],
                                [[None, bias_ref[0, p, var]] for p in pairs], [None for _ in pairs])
        for p in pairs:
            y_ref[pl.ds(q0, GRID_W), p * LANES:(p + 1) * LANES] = outs[p]
        return c

    lax.fori_loop(0, rows, body, 0)


def _na_attn_call(cq, ck, cv, cache_k, cache_v, bias_tab, layer):
    rb0 = T_CTX // N_LAT
    n_pairs = D_C // LANES
    spec = pl.BlockSpec((N_LAT, D_C), lambda b: (rb0 + b, 0))
    cache = pl.BlockSpec((1, 1, PAST_LEN, D_C), lambda b: (b, layer, 0, 0))
    span = NA_SPAN_BLKS * NA_BLK
    return pl.pallas_call(
        _na_attn_kernel,
        grid=(B_LAT,),
        in_specs=[spec, spec, spec, cache, cache,
                  pl.BlockSpec((1, n_pairs, len(NA_VARIANTS), 2 * GRID_W, span), lambda b: (layer, 0, 0, 0, 0))],
        out_specs=pl.BlockSpec((N_LAT, D_C), lambda b: (b, 0)),
        out_shape=jax.ShapeDtypeStruct((T_LAT, D_C), F32),
        scratch_shapes=[pltpu.VMEM((n_pairs, N_LAT // NA_BLK, LANES, NA_BLK), BF16),
                        pltpu.VMEM((n_pairs, N_LAT, 2 * LANES), BF16),
                        pltpu.VMEM((n_pairs, LANES, PAST_LEN), BF16),
                        pltpu.VMEM((n_pairs, PAST_LEN, 2 * LANES), BF16)],
        compiler_params=_cparams("arbitrary"),
        name="na_attn",
    )(cq, ck, cv, cache_k, cache_v, bias_tab)


def _outproj_kernel(yac_ref, ybc_ref, ycc_ref, yal_ref, ybl_ref, ycl_ref, x_ref, ga_ref, scf_ref, shf_ref,
                    lng_ref, lnb_ref, wo_ref, wr_ref, x1_ref, h2_ref, route_ref, cnt_ref, run_scr):
    @pl.when(pl.program_id(0) == 0)
    def _():
        run_scr[...] = jnp.zeros_like(run_scr)

    is_ctx = pl.program_id(0) < T_CTX // TM

    def pick(c_ref, l_ref):
        return jnp.where(is_ctx, c_ref[...], l_ref[...])

    y = (_dot(pick(yac_ref, yal_ref), wo_ref[0, 0:Y_A_COLS, :])
         + _dot(pick(ybc_ref, ybl_ref), wo_ref[0, Y_A_COLS:Y_A_COLS + D_B, :])
         + _dot(pick(ycc_ref, ycl_ref), wo_ref[0, Y_A_COLS + D_B:, :]))
    x1 = _layer_norm(ALPHA * x_ref[...] + ga_ref[0, 0, 0] * y, lng_ref[...], lnb_ref[...])
    x1_ref[...] = x1
    h2 = x1 * (1.0 + scf_ref[0, 0, 0]) + shf_ref[0, 0, 0]
    h2_ref[...] = h2
    w_r = wr_ref[...]
    w_hi = w_r.astype(BF16)
    w_lo = (w_r - w_hi.astype(F32)).astype(BF16)
    h_hi = h2.astype(BF16)
    h_lo = (h2 - h_hi.astype(F32)).astype(BF16)
    p_hi = jnp.dot(h_hi, jnp.concatenate([w_hi, w_lo], axis=1), preferred_element_type=F32)
    logits = p_hi[:, :LANES] + p_hi[:, LANES:] + jnp.dot(h_lo, w_hi, preferred_element_type=F32)
    lane = lax.broadcasted_iota(jnp.int32, logits.shape, 1)
    lanef = lane.astype(F32)
    big = float(LANES)
    lg = jnp.where((lane >= N_EXPERTS) & (lane < N_EXPERTS + N_GROUPS), logits, NEG_INF)
    mg = jnp.max(lg, axis=1, keepdims=True)
    grp = jnp.min(jnp.where(lg == mg, lanef, big), axis=1, keepdims=True) - float(N_EXPERTS)
    g_w = 1.0 / jnp.sum(jnp.exp(lg - mg), axis=1, keepdims=True)
    in_grp = (lane < N_EXPERTS) & ((lane // E_PER_GROUP).astype(F32) == grp)
    le = jnp.where(in_grp, logits, NEG_INF)
    l1 = jnp.max(le, axis=1, keepdims=True)
    i1 = jnp.min(jnp.where(le == l1, lanef, big), axis=1, keepdims=True)
    le2 = jnp.where(lanef == i1, NEG_INF, le)
    l2 = jnp.max(le2, axis=1, keepdims=True)
    i2 = jnp.min(jnp.where(le2 == l2, lanef, big), axis=1, keepdims=True)
    e2 = jnp.exp(l2 - l1)
    w1 = g_w / (1.0 + e2)
    w2 = g_w * e2 / (1.0 + e2)
    oh1 = jnp.where(lanef == i1, 1.0, 0.0)
    oh2 = jnp.where(lanef == i2, 1.0, 0.0)
    rt = lax.broadcasted_iota(jnp.int32, (TM, TM), 0)
    ct = lax.broadcasted_iota(jnp.int32, (TM, TM), 1)
    before = jnp.where(ct < rt, 1.0, 0.0)
    run = run_scr[...]
    tot1 = jnp.sum(oh1, axis=0, keepdims=True)
    r1 = jnp.sum(oh1 * (run + _dot(before, oh1)), axis=1, keepdims=True)
    r2 = jnp.sum(oh2 * (run + tot1 + _dot(before, oh2)), axis=1, keepdims=True)
    run = run + tot1 + jnp.sum(oh2, axis=0, keepdims=True)
    run_scr[...] = run
    cnt_ref[...] = run
    vals = (i1, i2, w1, w2, r1, r2)
    out = jnp.zeros_like(logits)
    for n, v in enumerate(vals):
        out = jnp.where(lane == n, v, out)
    route_ref[...] = out[:, :8]


def _outproj_call(y_ctx, y_lat, x, mod_t, layer, lng, lnb, wo, wr):
    row_vec = pl.BlockSpec((1, D_MODEL), lambda i: (0, 0))
    n_ctx = T_CTX // TM

    def tok(n):
        return pl.BlockSpec((TM, n), lambda i: (i, 0))

    def tok_ctx(n):
        return pl.BlockSpec((TM, n), lambda i: (jnp.minimum(i, n_ctx - 1), 0))

    def tok_lat(n):
        return pl.BlockSpec((TM, n), lambda i: (jnp.maximum(i - n_ctx, 0), 0))

    return pl.pallas_call(
        _outproj_kernel,
        grid=(N_TILES,),
        in_specs=[tok_ctx(Y_A_COLS), tok_ctx(D_B), tok_ctx(D_C), tok_lat(Y_A_COLS), tok_lat(D_B), tok_lat(D_C),
                  tok(D_MODEL), _mod_spec(layer, 2), _mod_spec(layer, 4), _mod_spec(layer, 3), row_vec, row_vec,
                  pl.BlockSpec((1, Y_A_COLS + D_B + D_C, D_MODEL), lambda i: (layer, 0, 0)),
                  pl.BlockSpec((D_MODEL, LANES), lambda i: (0, 0))],
        out_specs=[tok(D_MODEL), tok(D_MODEL), tok(8), pl.BlockSpec((1, LANES), lambda i: (0, 0))],
        out_shape=[jax.ShapeDtypeStruct((T_ALL, D_MODEL), F32), jax.ShapeDtypeStruct((T_ALL, D_MODEL), F32),
                   jax.ShapeDtypeStruct((T_ALL, 8), F32), jax.ShapeDtypeStruct((1, LANES), F32)],
        scratch_shapes=[pltpu.VMEM((1, LANES), F32)],
        compiler_params=_cparams("arbitrary"),
        name="outproj_router",
    )(*y_ctx, *y_lat, x, mod_t, mod_t, mod_t, lng, lnb, wo, wr)


def _dispatch_kernel(pos_ref, pend_ref, cnt_ref, h_ref, xs_hbm, inv_ref, zero_scr, sem):
    base = pl.program_id(0) * TM

    @pl.when(pl.program_id(0) == 0)
    def _():
        zero_scr[...] = jnp.zeros_like(zero_scr)

        def fill(e, op):
            prev = pend_ref[e - 1] if e else 0

            @pl.when(pend_ref[e] > prev)
            def _():
                first = pl.multiple_of(pend_ref[e] - MOE_BLK, MOE_BLK)
                op(pltpu.make_async_copy(zero_scr, xs_hbm.at[pl.ds(first, MOE_BLK), :], sem))

        def tail_copy(b):
            first = pl.multiple_of(b * MOE_BLK, MOE_BLK)
            return pltpu.make_async_copy(zero_scr, xs_hbm.at[pl.ds(first, MOE_BLK), :], sem)

        def tail_start(b, c):
            tail_copy(b).start()
            return c

        def tail_wait(b, c):
            tail_copy(b).wait()
            return c

        def dump(s, c):
            inv_ref[s] = 2 * T_ALL + lax.rem(s // MOE_BLK, 2) * MOE_BLK + lax.rem(s, MOE_BLK)
            return c

        for e in range(N_EXPERTS):
            run_start = pend_ref[e - 1] if e else 0
            lax.fori_loop(run_start + cnt_ref[e], pend_ref[e], dump, 0)
        lax.fori_loop(pend_ref[N_EXPERTS - 1], xs_hbm.shape[0], dump, 0)

        n_blocks = xs_hbm.shape[0] // MOE_BLK
        first_free = pend_ref[N_EXPERTS - 1] // MOE_BLK
        for e in range(N_EXPERTS):
            fill(e, lambda cp: cp.start())
        lax.fori_loop(first_free, n_blocks, tail_start, 0)
        for e in range(N_EXPERTS):
            fill(e, lambda cp: cp.wait())
        lax.fori_loop(first_free, n_blocks, tail_wait, 0)

    def row(t, p):
        return pltpu.make_async_copy(h_ref.at[pl.ds(t, 1), :], xs_hbm.at[pl.ds(p, 1), :], sem)

    def issue(t, c):
        p1 = pos_ref[base + t]
        p2 = pos_ref[T_ALL + base + t]
        row(t, p1).start()
        row(t, p2).start()
        inv_ref[p1] = base + t
        inv_ref[p2] = T_ALL + base + t
        return c

    lax.fori_loop(0, TM, issue, 0, unroll=8)
    whole = pltpu.make_async_copy(h_ref, xs_hbm.at[pl.ds(0, TM), :], sem)
    whole.wait()
    whole.wait()


def _dispatch_call(pos2, p_end, cnt, h2, nblk):
    grid_spec = pltpu.PrefetchScalarGridSpec(
        num_scalar_prefetch=3,
        grid=(N_TILES,),
        in_specs=[pl.BlockSpec((TM, D_MODEL), lambda i, p, pe, cn: (i, 0))],
        out_specs=[pl.BlockSpec(memory_space=pl.ANY), pl.BlockSpec(memory_space=pltpu.SMEM)],
        scratch_shapes=[pltpu.VMEM((MOE_BLK, D_MODEL), F32), pltpu.SemaphoreType.DMA(())],
    )
    return pl.pallas_call(
        _dispatch_kernel,
        grid_spec=grid_spec,
        out_shape=[jax.ShapeDtypeStruct((nblk * MOE_BLK, D_MODEL), F32),
                   jax.ShapeDtypeStruct((nblk * MOE_BLK,), jnp.int32)],
        compiler_params=_cparams("arbitrary"),
        name="dispatch",
    )(pos2, p_end, cnt, h2)


Y_ROWS = 2 * T_ALL + 2 * MOE_BLK


def _expert_kernel(blk_e_ref, nused_ref, inv_ref, xs_ref, wg_ref, wu_ref, wd_ref, y_hbm, obuf, sem):
    j = pl.program_id(0)
    last = pl.num_programs(0) - 1
    used = nused_ref[0]
    slot = lax.rem(j, 2)

    def drain(s):
        pltpu.make_async_copy(obuf.at[s], y_hbm.at[pl.ds(0, MOE_BLK), :], sem.at[s]).wait()

    @pl.when(j == 0)
    def _():
        obuf[1] = jnp.zeros((MOE_BLK, D_MODEL), F32)
        for s in range(2):
            pltpu.make_async_copy(obuf.at[1], y_hbm.at[pl.ds(2 * T_ALL + s * MOE_BLK, MOE_BLK), :], sem.at[1]).start()
        drain(1)
        drain(1)

    @pl.when((j >= 2) & (j - 2 < used))
    def _():
        drain(slot)

    @pl.when(j < used)
    def _():
        xb = xs_ref[...].astype(BF16)
        g = jnp.dot(xb, wg_ref[0, 0].astype(BF16), preferred_element_type=F32)
        u = jnp.dot(xb, wu_ref[0, 0].astype(BF16), preferred_element_type=F32)
        obuf[slot] = _dot(jax.nn.silu(g) * u, wd_ref[0, 0])
        base = j * MOE_BLK

        def issue(r, c):
            pltpu.make_async_copy(obuf.at[slot, pl.ds(r, 1), :], y_hbm.at[pl.ds(inv_ref[base + r], 1), :],
                                  sem.at[slot]).start()
            return c

        lax.fori_loop(0, MOE_BLK, issue, 0, unroll=8)

    @pl.when(j == last)
    def _():
        @pl.when((j >= 1) & (j - 1 < used))
        def _():
            drain(1 - slot)

        @pl.when(j < used)
        def _():
            drain(slot)


def _expert_call(blk_e, nused, inv, xs, wg, wu, wd, layer, nblk):
    def wspec(r, c):
        return pl.BlockSpec((1, 1, r, c), lambda j, be, nu, iv: (layer, be[j], 0, 0))

    grid_spec = pltpu.PrefetchScalarGridSpec(
        num_scalar_prefetch=3,
        grid=(nblk,),
        in_specs=[
            pl.BlockSpec((MOE_BLK, D_MODEL), lambda j, be, nu, iv: (jnp.clip(j, 0, jnp.maximum(nu[0] - 1, 0)), 0)),
            wspec(D_MODEL, D_EXPERT), wspec(D_MODEL, D_EXPERT), wspec(D_EXPERT, D_MODEL),
        ],
        out_specs=pl.BlockSpec(memory_space=pl.ANY),
        scratch_shapes=[pltpu.VMEM((2, MOE_BLK, D_MODEL), F32), pltpu.SemaphoreType.DMA((2,))],
    )
    return pl.pallas_call(
        _expert_kernel,
        grid_spec=grid_spec,
        out_shape=jax.ShapeDtypeStruct((Y_ROWS, D_MODEL), F32),
        compiler_params=_cparams("arbitrary"),
        name="experts",
    )(blk_e, nused, inv, xs, wg, wu, wd)


def _combine_kernel(y0_ref, y1_ref, x1_ref, route_ref, gf_ref, lng_ref, lnb_ref, x2_ref):
    route = route_ref[...]
    y = route[:, 2:3] * y0_ref[...] + route[:, 3:4] * y1_ref[...]
    x2_ref[...] = _layer_norm(ALPHA * x1_ref[...] + gf_ref[0, 0, 0] * y, lng_ref[...], lnb_ref[...])


def _combine_call(y, x1, route, mod_t, layer, lng, lnb):
    return pl.pallas_call(
        _combine_kernel,
        grid=(N_TILES,),
        in_specs=[
            pl.BlockSpec((TM, D_MODEL), lambda i: (i, 0)),
            pl.BlockSpec((TM, D_MODEL), lambda i: (N_TILES + i, 0)),
            pl.BlockSpec((TM, D_MODEL), lambda i: (i, 0)),
            pl.BlockSpec((TM, 8), lambda i: (i, 0)),
            _mod_spec(layer, 5),
            pl.BlockSpec((1, D_MODEL), lambda i: (0, 0)),
            pl.BlockSpec((1, D_MODEL), lambda i: (0, 0)),
        ],
        out_specs=pl.BlockSpec((TM, D_MODEL), lambda i: (i, 0)),
        out_shape=jax.ShapeDtypeStruct((T_ALL, D_MODEL), F32),
        compiler_params=_cparams("arbitrary"),
        name="combine_norm",
    )(y, y, x1, route, mod_t, lng, lnb)


def _gate_cols(grp):
    return np.array([kind * 2 * H_A + d * H_A + grp * HP_A + j
                     for j in range(HP_A) for kind in range(2) for d in range(2)])


def _prep_w_in(w_in):
    a = w_in[..., :4 * D_A].reshape(DEPTH, D_MODEL, 4, H_A, DH_A)
    a = jnp.pad(a, ((0, 0), (0, 0), (0, 0), (0, 0), (0, DP_A - DH_A))).reshape(DEPTH, D_MODEL, ZA_COLS)
    gates = w_in[..., 4 * D_A:4 * D_A + 4 * H_A]
    g = jnp.concatenate([jnp.pad(gates[..., _gate_cols(grp)], ((0, 0), (0, 0), (0, LANES - 4 * HP_A)))
                         for grp in range(H_A // HP_A)], -1)
    rest = w_in[..., 4 * D_A + 4 * H_A:]
    return jnp.concatenate([a, rest, g], -1).astype(BF16)


def _prep_w_out(w_out):
    a = w_out[:, :D_A].reshape(DEPTH, H_A, DH_A, D_MODEL)
    a = jnp.pad(a, ((0, 0), (0, 0), (0, DP_A - DH_A), (0, 0))).reshape(DEPTH, Y_A_COLS, D_MODEL)
    return jnp.concatenate([a, w_out[:, D_A:]], 1).astype(BF16)


def _rope_tables():
    t = np.arange(N_LAT)
    nf = HD // 4
    inv = ROPE_BASE ** (-np.arange(nf, dtype=np.float32) / nf)
    ar = (t // GRID_W).astype(np.float32)[:, None] * inv
    ac = (t % GRID_W).astype(np.float32)[:, None] * inv
    ang = jnp.asarray(np.concatenate([ar, ar, ac, ac], -1), F32)
    cos, sin = jnp.cos(ang), jnp.sin(ang)
    sign = np.where((np.arange(HD) % 32) < 16, -1.0, 1.0).astype(np.float32)
    reps = LANES // HD
    return jnp.tile(cos, (1, reps)), jnp.tile(sin * sign, (1, reps))


def _na_bias_tables(rpb):
    qcol = np.arange(GRID_W)[:, None]
    kcol = np.arange(GRID_W)[None, :]
    dc = np.clip(kcol - qcol, 1 - NA_WIN_W, NA_WIN_W - 1) + NA_WIN_W - 1
    wstart = np.clip(qcol - NA_WIN_W // 2, 0, GRID_W - NA_WIN_W)
    in_win = (kcol >= wstart) & (kcol < wstart + NA_WIN_W)
    sel = (np.arange(2 * NA_WIN_W - 1)[:, None] == dc.reshape(1, -1)).astype(np.float32)
    cols = jnp.einsum("lhrd,dn->lhrn", rpb, jnp.asarray(sel), precision=lax.Precision.HIGHEST)
    cols = jnp.where(in_win.reshape(-1), cols, NEG_INF).reshape(DEPTH, H_C, 2 * NA_WIN_H - 1, GRID_W, GRID_W)
    outside = jnp.full((DEPTH, H_C, GRID_W, GRID_W), NEG_INF, F32)
    variants = []
    for dr0, off in NA_VARIANTS:
        span_rows = [cols[:, :, dr0 + i - off] if 0 <= i - off < NA_WIN_H else outside for i in range(NA_SPAN_ROWS)]
        variants.append(jnp.stack(span_rows, 3).reshape(DEPTH, H_C, GRID_W, NA_SPAN_ROWS * GRID_W))
    tab = jnp.stack(variants, 2).reshape(DEPTH, H_C // 2, 2, len(NA_VARIANTS), GRID_W, NA_SPAN_ROWS * GRID_W)
    return tab.transpose(0, 1, 3, 2, 4, 5).reshape(DEPTH, H_C // 2, len(NA_VARIANTS), 2 * GRID_W, NA_SPAN_ROWS * GRID_W)


def _dispatch_plan(route, counts):
    nblk = 2 * T_ALL // MOE_BLK + N_EXPERTS
    cnt = counts[0, :N_EXPERTS].astype(jnp.int32)
    padded = (cnt + MOE_BLK - 1) // MOE_BLK * MOE_BLK
    p_end = jnp.cumsum(padded)
    p_start = p_end - padded
    e = route[:, 0:2].astype(jnp.int32)
    hot = e[..., None] == jnp.arange(N_EXPERTS, dtype=jnp.int32)
    pos = jnp.sum(jnp.where(hot, p_start, 0), -1) + route[:, 4:6].astype(jnp.int32)
    pos2 = pos.T.reshape(-1)
    blk_first = jnp.arange(nblk, dtype=jnp.int32) * MOE_BLK
    blk_e = jnp.minimum(jnp.sum((p_end[None, :] <= blk_first[:, None]).astype(jnp.int32), axis=1), N_EXPERTS - 1)
    nused = p_end[-1:] // MOE_BLK
    return blk_e, nused, pos2, p_end, cnt, nblk


def kernel(x_prompt, x_sample, state_a_C, state_a_n, state_a_m, cache_b_k, cache_b_v, cache_c_k, cache_c_v, c, c_ctx, w_in, b_a_i, b_a_f, w_a_hnorm, b_sink, rpb, w_out, w_ada, b_ada, ln_g, ln_b, w_router_grp, w_router_exp, w_e_gate, w_e_up, w_e_down):
    w_in_p = _prep_w_in(w_in)
    w_out_p = _prep_w_out(w_out)
    w_r = jnp.pad(jnp.concatenate([w_router_exp, w_router_grp], -1),
                  ((0, 0), (0, 0), (0, LANES - N_EXPERTS - N_GROUPS)))
    wn_p = jnp.pad(w_a_hnorm.reshape(DEPTH, H_A, 1, DH_A), ((0, 0), (0, 0), (0, 0), (0, DP_A - DH_A)))
    gate_b = jnp.concatenate([b_a_i, b_a_f], 1).transpose(0, 2, 1)
    gate_b = jnp.pad(gate_b.reshape(DEPTH, H_A // HP_A, 1, 4 * HP_A), ((0, 0), (0, 0), (0, 0), (0, LANES - 4 * HP_A)))
    sink_p = jnp.pad(b_sink, ((0, 0), (0, 8 - H_B))).reshape(DEPTH, 1, 8)
    cos_t, sin_t = _rope_tables()
    na_bias = _na_bias_tables(rpb)
    cb_k = cache_b_k.reshape(B_LAT, DEPTH, PAST_LEN, D_KVB)
    cb_v = cache_b_v.reshape(B_LAT, DEPTH, PAST_LEN, D_KVB)
    cc_k = cache_c_k.reshape(B_LAT, DEPTH, PAST_LEN, D_C)
    cc_v = cache_c_v.reshape(B_LAT, DEPTH, PAST_LEN, D_C)
    pad_c = ((0, 0), (0, 0), (0, 0), (0, 0), (0, DP_A - DH_A), (0, DP_A - DH_A))
    st_ct = jnp.swapaxes(jnp.pad(state_a_C, pad_c), -1, -2)
    st_nr = jnp.broadcast_to(jnp.pad(state_a_n, pad_c[:-1])[..., None], st_ct.shape)
    st_s = jnp.concatenate([st_ct, st_nr], -1)
    st_m = state_a_m[..., None, None]
    z_s = jnp.zeros((B_CTX, 2, H_A, DP_A, 2 * DP_A), F32)
    z_m = jnp.zeros((B_CTX, 2, H_A, 1, 1), F32)

    cvec = jnp.concatenate([c, c_ctx[None, :], jnp.zeros((3, D_MODEL), F32)], 0)
    mod = _ada_call(cvec, w_ada, b_ada)
    tile_row = np.concatenate([np.full(T_CTX // TM, B_LAT), np.repeat(np.arange(B_LAT), N_LAT // TM)])
    mod_t = mod[:, tile_row].reshape(DEPTH, N_TILES, 6, 1, D_MODEL)

    x = jnp.concatenate([x_prompt.reshape(T_CTX, D_MODEL), x_sample.reshape(T_LAT, D_MODEL)], 0)
    cs_, ns_, ms_, kbs, vbs, kcs, vcs = [], [], [], [], [], [], []
    for l in range(DEPTH):
        za, bq, bk, bv, cq, ck, cv, zg = _inproj_call(x, mod_t, l, w_in_p)
        ya_c, s_l, m_l = _mlstm_call(za, zg, gate_b[l], wn_p[l], z_s, z_m, nb=B_CTX, seq=L_CTX, row_blk0=0)
        yb_c, yc_c = _ctx_attn_call(bq, bk, bv, cq, ck, cv, sink_p[l])
        ya_l, _, _ = _mlstm_call(za, zg, gate_b[l], wn_p[l], st_s[:, l], st_m[:, l],
                                 nb=B_LAT, seq=N_LAT, row_blk0=T_CTX // N_LAT)
        yb_l = _win_attn_call(bq, bk, bv, cb_k, cb_v, cos_t, sin_t, sink_p[l], l)
        yc_l = _na_attn_call(cq, ck, cv, cc_k, cc_v, na_bias, l)
        x1, h2, route, counts = _outproj_call((ya_c, yb_c, yc_c), (ya_l, yb_l, yc_l), x, mod_t, l,
                                              ln_g[l, 0:1], ln_b[l, 0:1], w_out_p, w_r[l])
        blk_e, nused, pos2, p_end, cnt, nblk = _dispatch_plan(route, counts)
        xs, inv = _dispatch_call(pos2, p_end, cnt, h2, nblk)
        y = _expert_call(blk_e, nused, inv, xs, w_e_gate, w_e_up, w_e_down, l, nblk)
        x = _combine_call(y, x1, route, mod_t, l, ln_g[l, 1:2], ln_b[l, 1:2])
        cs_.append(jnp.swapaxes(s_l[..., :DH_A, :DH_A], -1, -2))
        ns_.append(s_l[..., :DH_A, DP_A])
        ms_.append(m_l.reshape(B_CTX, 2, H_A))
        kbs.append(bk[:T_CTX].reshape(B_CTX, L_CTX, KV_B, HD))
        vbs.append(bv[:T_CTX].reshape(B_CTX, L_CTX, KV_B, HD))
        kcs.append(ck[:T_CTX].reshape(B_CTX, L_CTX, H_C, HD))
        vcs.append(cv[:T_CTX].reshape(B_CTX, L_CTX, H_C, HD))
    y_prompt = x[:T_CTX].reshape(B_CTX, L_CTX, D_MODEL)
    y_sample = x[T_CTX:].reshape(B_LAT, N_LAT, D_MODEL)
    return (y_prompt, y_sample, jnp.stack(cs_, 1), jnp.stack(ns_, 1), jnp.stack(ms_, 1),
            jnp.stack(kbs, 1), jnp.stack(vbs, 1), jnp.stack(kcs, 1), jnp.stack(vcs, 1))
```

```python
import functools

import numpy as np
import jax
import jax.numpy as jnp
from jax import lax
from jax.experimental import pallas as pl
from jax.experimental.pallas import tpu as pltpu

F32 = jnp.float32
BF16 = jnp.bfloat16
NEG_INF = float("-inf")

D_MODEL = 1024
DEPTH = 4
B_CTX, L_CTX = 16, 256
B_LAT, N_LAT = 4, 2048
PAST_LEN = 512
GRID_W = 64
H_A, DH_A = 4, 96
D_A = H_A * DH_A
H_B, KV_B, HD = 6, 2, 64
G_B = H_B // KV_B
D_B, D_KVB = H_B * HD, KV_B * HD
WIN = 128
ROPE_BASE = 10000.0
H_C = 4
D_C = H_C * HD
NA_WIN_H, NA_WIN_W = 8, 16
N_GROUPS, E_PER_GROUP = 4, 8
N_EXPERTS = N_GROUPS * E_PER_GROUP
D_EXPERT = D_MODEL // 4
ALPHA = (2 * DEPTH) ** 0.25
LN_EPS = 1e-5

LANES = 128
DP_A = LANES
LC_K = LANES
TM = 256
MOE_BLK = 256
VMEM_LIMIT = 48 * 1024 * 1024

T_CTX = B_CTX * L_CTX
T_LAT = B_LAT * N_LAT
T_ALL = T_CTX + T_LAT
N_TILES = T_ALL // TM
ZA_COLS = 4 * H_A * DP_A
Y_A_COLS = H_A * DP_A
IN_SPLITS = (ZA_COLS, D_B, D_KVB, D_KVB, D_C, D_C, D_C, 2 * LANES)
IN_COLS = sum(IN_SPLITS)


def _cparams(*sem):
    return pltpu.CompilerParams(dimension_semantics=sem, vmem_limit_bytes=VMEM_LIMIT)


def _dot(a, b):
    return jnp.dot(a.astype(BF16), b.astype(BF16), preferred_element_type=F32)


def _dot_nt(a, b):
    return lax.dot_general(a.astype(BF16), b.astype(BF16), (((1,), (1,)), ((), ())), preferred_element_type=F32)


def _dot_tn(a, b):
    return lax.dot_general(a.astype(BF16), b.astype(BF16), (((0,), (0,)), ((), ())), preferred_element_type=F32)


def _layer_norm(v, g, b):
    mu = jnp.mean(v, -1, keepdims=True)
    var = jnp.mean(jnp.square(v - mu), -1, keepdims=True)
    return (v - mu) * lax.rsqrt(var + LN_EPS) * g + b


def _ada_kernel(c_ref, w_ref, b_ref, o_ref):
    s = jax.nn.silu(c_ref[...])
    o_ref[0] = _dot(s, w_ref[0]) + b_ref[0]


def _ada_call(cvec, w_ada, b_ada):
    nb = 6
    return pl.pallas_call(
        _ada_kernel,
        grid=(DEPTH, nb),
        in_specs=[
            pl.BlockSpec((8, D_MODEL), lambda l, j: (0, 0)),
            pl.BlockSpec((1, D_MODEL, D_MODEL), lambda l, j: (l, 0, j)),
            pl.BlockSpec((1, 1, D_MODEL), lambda l, j: (l, 0, j)),
        ],
        out_specs=pl.BlockSpec((1, 8, D_MODEL), lambda l, j: (l, 0, j)),
        out_shape=jax.ShapeDtypeStruct((DEPTH, 8, 6 * D_MODEL), F32),
        compiler_params=_cparams("arbitrary", "arbitrary"),
        name="adaln",
    )(cvec, w_ada, b_ada.reshape(DEPTH, 1, 6 * D_MODEL))


def _inproj_kernel(x_ref, sc_ref, sh_ref, w_ref, *out_refs):
    h = (x_ref[...] * (1.0 + sc_ref[0, 0, 0]) + sh_ref[0, 0, 0]).astype(BF16)
    off = 0
    for ref in out_refs:
        n = ref.shape[-1]
        ref[...] = jnp.dot(h, w_ref[0, :, off:off + n], preferred_element_type=F32)
        off += n


def _mod_spec(layer, which):
    return pl.BlockSpec((1, 1, 1, 1, D_MODEL), lambda i, *_: (layer, i, which, 0, 0))


def _inproj_call(x, mod_t, layer, w):
    return pl.pallas_call(
        _inproj_kernel,
        grid=(N_TILES,),
        in_specs=[
            pl.BlockSpec((TM, D_MODEL), lambda i: (i, 0)),
            _mod_spec(layer, 1), _mod_spec(layer, 0),
            pl.BlockSpec((1, D_MODEL, IN_COLS), lambda i: (layer, 0, 0)),
        ],
        out_specs=[pl.BlockSpec((TM, n), lambda i: (i, 0)) for n in IN_SPLITS],
        out_shape=[jax.ShapeDtypeStruct((T_ALL, n), F32) for n in IN_SPLITS],
        compiler_params=_cparams("arbitrary"),
        name="inproj",
    )(x, mod_t, mod_t, w)


HP_A = 2
N_CH = 2 * HP_A


def _mlstm_kernel(q_ref, k_ref, v_ref, o_ref, g_ref, gb_ref, wn_ref, s0_ref, m0_ref,
                  y_ref, so_ref, mo_ref, hf_scr, hb_scr, *, nc):
    lc = LC_K
    scale = DH_A ** -0.5
    chains = [(j, d) for j in range(HP_A) for d in range(2)]
    ti = lax.broadcasted_iota(jnp.int32, (lc, lc), 0)
    si = lax.broadcasted_iota(jnp.int32, (lc, lc), 1)
    lane_ok = lax.broadcasted_iota(jnp.int32, (1, DP_A), 1) < DH_A
    ones = jnp.ones((lc, DP_A), F32)

    def stack(parts):
        return jnp.concatenate(parts, axis=0)

    def rows_of(x, a):
        return a[x * lc:(x + 1) * lc]

    def spread(vals):
        return stack([jnp.broadcast_to(v, (lc, v.shape[1])) for v in vals])

    mask = stack([si <= ti if d == 0 else si >= ti for _, d in chains])
    mask_t = stack([si >= ti if d == 0 else si <= ti for _, d in chains])
    eye = stack([si == ti for _ in chains])

    def col_sums(a):
        return [jnp.sum(rows_of(x, a), axis=0, keepdims=True) for x in range(N_CH)]

    def body(i, carry):
        smats = carry[:N_CH]
        ms = carry[N_CH:]
        r0s = [pl.multiple_of((i if d == 0 else nc - 1 - i) * lc, lc) for _, d in chains]
        q, kt, v1, icol, fpre = [], [], [], [], []
        for (j, d), r0 in zip(chains, r0s):
            cols = slice(j * DP_A, (j + 1) * DP_A)
            q.append(q_ref[pl.ds(r0, lc), cols].astype(BF16))
            kt.append((k_ref[pl.ds(r0, lc), cols] * scale).T.astype(BF16))
            v1.append(jnp.concatenate([v_ref[pl.ds(r0, lc), cols], ones], axis=1))
            gz = g_ref[pl.ds(r0, lc), :] + gb_ref[0]
            icol.append(gz[:, 4 * j + d:4 * j + d + 1])
            fpre.append(gz[:, 4 * j + 2 + d:4 * j + 3 + d])
        i_col = stack(icol)
        f_col = jax.nn.log_sigmoid(stack(fpre))
        b_rows = col_sums(jnp.where(mask_t, f_col, 0.0))
        bls = col_sums(f_col)
        b_col = jnp.sum(jnp.where(eye, spread(b_rows), 0.0), axis=1, keepdims=True)
        a_rows = col_sums(jnp.where(eye, i_col - b_col, 0.0))
        a_sp = jnp.where(mask, spread(a_rows), NEG_INF)
        m_sp = spread(ms)
        gap = jnp.maximum(m_sp, jnp.max(a_sp, axis=1, keepdims=True))
        m_out = b_col + gap
        wmat = jnp.exp(a_sp - gap)
        sw = (stack([jnp.dot(q[x], kt[x], preferred_element_type=F32) for x in range(N_CH)]) * wmat).astype(BF16)
        sc_in = jnp.exp(m_sp - gap)
        tot = (stack([jnp.dot(rows_of(x, sw), v1[x].astype(BF16), preferred_element_type=F32) for x in range(N_CH)])
               + sc_in * stack([_dot(q[x], smats[x]) for x in range(N_CH)]))
        h = tot[:, :DP_A] / jnp.maximum(jnp.abs(tot[:, DP_A:]), jnp.exp(-m_out))
        for x, ((j, d), r0) in enumerate(zip(chains, r0s)):
            dst = hf_scr if d == 0 else hb_scr
            dst[pl.ds(r0, lc), j * DP_A:(j + 1) * DP_A] = rows_of(x, h)
        dec = spread(bls) - b_col + i_col
        m_new = [jnp.maximum(bls[x] + ms[x], jnp.max(rows_of(x, dec), axis=0, keepdims=True)) for x in range(N_CH)]
        wk = jnp.exp(dec - spread(m_new))
        s_new = [jnp.exp(bls[x] + ms[x] - m_new[x]) * smats[x] + _dot(kt[x], rows_of(x, wk) * v1[x])
                 for x in range(N_CH)]
        return tuple(s_new) + tuple(m_new)

    init = tuple(s0_ref[0, d, j] for j, d in chains) + tuple(m0_ref[0, d, j] for j, d in chains)
    final = lax.fori_loop(0, nc, body, init)
    for x, (j, d) in enumerate(chains):
        so_ref[0, d, j] = final[x]
        mo_ref[0, d, j] = final[N_CH + x]

    def finish(c, _):
        r0 = pl.multiple_of(c * lc, lc)
        for j in range(HP_A):
            cols = slice(j * DP_A, (j + 1) * DP_A)
            h = hf_scr[pl.ds(r0, lc), cols] + hb_scr[pl.ds(r0, lc), cols]
            mu = jnp.sum(h, axis=1, keepdims=True) * (1.0 / DH_A)
            dv = jnp.where(lane_ok, h - mu, 0.0)
            var = jnp.sum(dv * dv, axis=1, keepdims=True) * (1.0 / DH_A)
            hn = dv * lax.rsqrt(var + LN_EPS) * wn_ref[j]
            y_ref[pl.ds(r0, lc), cols] = jax.nn.sigmoid(o_ref[pl.ds(r0, lc), cols]) * hn
        return 0

    lax.fori_loop(0, nc, finish, 0)


def _mlstm_call(za, zg, gbias, wn, s0, m0, *, nb, seq, row_blk0):
    nc = seq // LC_K
    ng = H_A // HP_A
    w = HP_A * DP_A

    def zspec(part):
        return pl.BlockSpec((seq, w), lambda b, g: (row_blk0 + b, part * ng + g))

    def state(*tail):
        return pl.BlockSpec((1, 2, HP_A) + tail, lambda b, g: (b, 0, g) + (0,) * len(tail))

    return pl.pallas_call(
        functools.partial(_mlstm_kernel, nc=nc),
        grid=(nb, ng),
        in_specs=[
            zspec(0), zspec(1), zspec(2), zspec(3),
            pl.BlockSpec((seq, LANES), lambda b, g: (row_blk0 + b, g)),
            pl.BlockSpec((1, 1, LANES), lambda b, g: (g, 0, 0)),
            pl.BlockSpec((HP_A, 1, DP_A), lambda b, g: (g, 0, 0)),
            state(DP_A, 2 * DP_A), state(1, 1),
        ],
        out_specs=[pl.BlockSpec((seq, w), lambda b, g: (b, g)), state(DP_A, 2 * DP_A), state(1, 1)],
        out_shape=[
            jax.ShapeDtypeStruct((nb * seq, Y_A_COLS), F32),
            jax.ShapeDtypeStruct((nb, 2, H_A, DP_A, 2 * DP_A), F32),
            jax.ShapeDtypeStruct((nb, 2, H_A, 1, 1), F32),
        ],
        scratch_shapes=[pltpu.VMEM((seq, w), F32), pltpu.VMEM((seq, w), F32)],
        compiler_params=_cparams("arbitrary", "arbitrary"),
        name="mlstm",
    )(za, za, za, za, zg, gbias, wn, s0, m0)


def _pair_attention(qp, kslabs, vaugs, masks, sink_col):
    return _pairs_attention([qp], [kslabs], [vaugs], [masks], [sink_col])[0]


def _pairs_attention(qps, kslabs, vaugs, masks, sink_cols):
    n_p = len(qps)
    m_rows = qps[0].shape[0]
    lo = lax.broadcasted_iota(jnp.int32, qps[0].shape, 1) < HD
    q2 = [jnp.concatenate([jnp.where(lo, qp, 0.0), jnp.where(lo, 0.0, qp)], axis=0).astype(BF16) for qp in qps]
    scores = []
    for i in range(len(kslabs[0])):
        tiles = []
        for p in range(n_p):
            s = jnp.dot(q2[p], kslabs[p][i], preferred_element_type=F32)
            mk = masks[p][i]
            if mk is not None:
                s = jnp.where(mk, s, NEG_INF) if mk.dtype == jnp.bool_ else s + mk
            tiles.append(s)
        scores.append(jnp.concatenate(tiles, axis=0))
    sink = None if sink_cols[0] is None else jnp.concatenate(sink_cols, axis=0)
    mx = scores[0].max(axis=1, keepdims=True)
    for s in scores[1:]:
        mx = jnp.maximum(mx, s.max(axis=1, keepdims=True))
    if sink is not None:
        mx = jnp.maximum(mx, sink)
    probs = [jnp.exp(s - mx).astype(BF16) for s in scores]
    acc = []
    for p in range(n_p):
        rows = slice(p * 2 * m_rows, (p + 1) * 2 * m_rows)
        pv = None
        for e, va in zip(probs, vaugs[p]):
            t = jnp.dot(e[rows], va, preferred_element_type=F32)
            pv = t if pv is None else pv + t
        acc.append(pv)
    acc = jnp.concatenate(acc, axis=0)
    den = acc[:, LANES:]
    if sink is not None:
        den = den + jnp.exp(sink - mx)
    o = acc[:, :LANES] / den
    return [jnp.where(lo, o[2 * p * m_rows:(2 * p + 1) * m_rows], o[(2 * p + 1) * m_rows:(2 * p + 2) * m_rows])
            for p in range(n_p)]


def _gqa_key_slabs(kt):
    a, b = kt[:HD], kt[HD:]
    return [jnp.concatenate([a, a], 0), kt, jnp.concatenate([b, b], 0)]


def _gqa_value_pairs(v):
    lo = lax.broadcasted_iota(jnp.int32, v.shape, 1) < HD
    sw = pltpu.roll(v, HD, 1)
    ones = jnp.ones_like(v)
    return [jnp.concatenate([x, ones], 1) for x in (jnp.where(lo, v, sw), v, jnp.where(lo, sw, v))]


def _sink_col(sink_ref, p, m_rows):
    row = lax.broadcasted_iota(jnp.int32, (2 * m_rows, 1), 0)
    return jnp.where(row < m_rows, sink_ref[0:1, 2 * p:2 * p + 1], sink_ref[0:1, 2 * p + 1:2 * p + 2])


def _ctx_attn_kernel(bq_ref, bk_ref, bv_ref, cq_ref, ck_ref, cv_ref, sink_ref, yb_ref, yc_ref):
    scale = HD ** -0.5
    pairs_b = range(D_B // LANES)
    pairs_c = range(D_C // LANES)

    def tile(p):
        return slice(p * LANES, (p + 1) * LANES)

    kslabs = _gqa_key_slabs(bk_ref[...].T)
    vpairs = _gqa_value_pairs(bv_ref[...])
    outs = _pairs_attention([bq_ref[:, tile(p)] * scale for p in pairs_b], [[kslabs[p].astype(BF16)] for p in pairs_b],
                            [[vpairs[p].astype(BF16)] for p in pairs_b], [[None] for _ in pairs_b],
                            [_sink_col(sink_ref, p, L_CTX) for p in pairs_b])
    for p in pairs_b:
        yb_ref[:, tile(p)] = outs[p]
    ckt = ck_ref[...].T
    ones = jnp.ones((L_CTX, LANES), F32)
    outs = _pairs_attention([cq_ref[:, tile(p)] * scale for p in pairs_c], [[ckt[tile(p)].astype(BF16)] for p in pairs_c],
                            [[jnp.concatenate([cv_ref[:, tile(p)], ones], 1).astype(BF16)] for p in pairs_c],
                            [[None] for _ in pairs_c], [None for _ in pairs_c])
    for p in pairs_c:
        yc_ref[:, tile(p)] = outs[p]


def _ctx_attn_call(bq, bk, bv, cq, ck, cv, sink):
    def spec(n):
        return pl.BlockSpec((L_CTX, n), lambda b: (b, 0))

    return pl.pallas_call(
        _ctx_attn_kernel,
        grid=(B_CTX,),
        in_specs=[spec(D_B), spec(D_KVB), spec(D_KVB), spec(D_C), spec(D_C), spec(D_C),
                  pl.BlockSpec((1, 8), lambda b: (0, 0))],
        out_specs=[spec(D_B), spec(D_C)],
        out_shape=[jax.ShapeDtypeStruct((T_CTX, D_B), F32), jax.ShapeDtypeStruct((T_CTX, D_C), F32)],
        compiler_params=_cparams("arbitrary"),
        name="ctx_attn",
    )(bq, bk, bv, cq, ck, cv, sink)


def _rope(x, cos, sin_signed, first):
    rot = jnp.where(first, pltpu.roll(x, LANES - 16, 1), pltpu.roll(x, 16, 1))
    return x * cos + rot * sin_signed


def _win_attn_kernel(q_ref, k_ref, v_ref, kc_ref, vc_ref, cos_ref, sin_ref, sink_ref, y_ref,
                     kpt_scr, va_scr, kcp_scr, vca_scr):
    scale = HD ** -0.5
    n_pairs = D_B // LANES
    nblk = N_LAT // WIN
    nband = 3
    lane = lax.broadcasted_iota(jnp.int32, (1, LANES), 1)
    first = (lane % 32) < 16
    for p, (ks, va) in enumerate(zip(_gqa_key_slabs(kc_ref[0, 0].T), _gqa_value_pairs(vc_ref[0, 0]))):
        kcp_scr[p] = ks.astype(BF16)
        vca_scr[p] = va.astype(BF16)

    def prep(blk, c):
        r0 = pl.multiple_of(blk * WIN, WIN)
        kr = _rope(k_ref[pl.ds(r0, WIN), :], cos_ref[pl.ds(r0, WIN), :], sin_ref[pl.ds(r0, WIN), :], first)
        for p, (ks, va) in enumerate(zip(_gqa_key_slabs(kr.T), _gqa_value_pairs(v_ref[pl.ds(r0, WIN), :]))):
            kpt_scr[p, blk] = ks.astype(BF16)
            va_scr[p, pl.ds(r0, WIN), :] = va.astype(BF16)
        return c

    lax.fori_loop(0, nblk, prep, 0)

    def body(blk, c):
        q0 = pl.multiple_of(blk * WIN, WIN)
        sb = jnp.clip(blk - 1, 0, nblk - nband)
        k0 = pl.multiple_of(sb * WIN, WIN)
        cos = cos_ref[pl.ds(q0, WIN), :]
        sin = sin_ref[pl.ds(q0, WIN), :]
        row = lax.broadcasted_iota(jnp.int32, (2 * WIN, nband * WIN), 0)
        qpos = q0 + jnp.where(row < WIN, row, row - WIN)
        kpos = k0 + lax.broadcasted_iota(jnp.int32, (2 * WIN, nband * WIN), 1)
        mask = jnp.abs(kpos - qpos) <= WIN
        pairs = range(n_pairs)
        qps = [_rope(q_ref[pl.ds(q0, WIN), p * LANES:(p + 1) * LANES], cos, sin, first) * scale for p in pairs]
        k_loc = [jnp.concatenate([kpt_scr[p, sb + j] for j in range(nband)], axis=1) for p in pairs]
        v_loc = [va_scr[p, pl.ds(k0, nband * WIN), :] for p in pairs]
        outs = _pairs_attention(qps, [[kcp_scr[p], k_loc[p]] for p in pairs], [[vca_scr[p], v_loc[p]] for p in pairs],
                                [[None, mask] for _ in pairs], [_sink_col(sink_ref, p, WIN) for p in pairs])
        for p in pairs:
            y_ref[pl.ds(q0, WIN), p * LANES:(p + 1) * LANES] = outs[p]
        return c

    lax.fori_loop(0, nblk, body, 0)


def _win_attn_call(bq, bk, bv, cache_k, cache_v, cos, sin, sink, layer):
    rb0 = T_CTX // N_LAT

    def spec(n):
        return pl.BlockSpec((N_LAT, n), lambda b: (rb0 + b, 0))

    cache = pl.BlockSpec((1, 1, PAST_LEN, D_KVB), lambda b: (b, layer, 0, 0))
    tab = pl.BlockSpec((N_LAT, LANES), lambda b: (0, 0))
    return pl.pallas_call(
        _win_attn_kernel,
        grid=(B_LAT,),
        in_specs=[spec(D_B), spec(D_KVB), spec(D_KVB), cache, cache, tab, tab,
                  pl.BlockSpec((1, 8), lambda b: (0, 0))],
        out_specs=pl.BlockSpec((N_LAT, D_B), lambda b: (b, 0)),
        out_shape=jax.ShapeDtypeStruct((T_LAT, D_B), F32),
        scratch_shapes=[pltpu.VMEM((D_B // LANES, N_LAT // WIN, LANES, WIN), BF16),
                        pltpu.VMEM((D_B // LANES, N_LAT, 2 * LANES), BF16),
                        pltpu.VMEM((D_B // LANES, LANES, PAST_LEN), BF16),
                        pltpu.VMEM((D_B // LANES, PAST_LEN, 2 * LANES), BF16)],
        compiler_params=_cparams("arbitrary"),
        name="win_attn",
    )(bq, bk, bv, cache_k, cache_v, cos, sin, sink)


NA_BLK = LANES
NA_SPAN_BLKS = NA_WIN_H * GRID_W // NA_BLK + 1
NA_SPAN_ROWS = NA_SPAN_BLKS * NA_BLK // GRID_W
NA_VARIANTS = ((7, 0), (6, 0), (5, 0), (4, 0), (3, 0), (3, 1), (3, 2), (2, 2), (1, 2), (0, 2))


def _na_attn_kernel(q_ref, k_ref, v_ref, kc_ref, vc_ref, bias_ref, y_ref, kpt_scr, va_scr, kcp_scr, vca_scr):
    scale = HD ** -0.5
    rows = N_LAT // GRID_W
    n_pairs = D_C // LANES
    nblk = N_LAT // NA_BLK
    half = NA_WIN_H // 2
    kct = kc_ref[0, 0].T
    for p in range(n_pairs):
        cols = slice(p * LANES, (p + 1) * LANES)
        kcp_scr[p] = kct[cols].astype(BF16)
        vca_scr[p] = jnp.concatenate([vc_ref[0, 0, :, cols], jnp.ones((PAST_LEN, LANES), F32)], 1).astype(BF16)

    def prep(blk, c):
        r0 = pl.multiple_of(blk * NA_BLK, NA_BLK)
        kt = k_ref[pl.ds(r0, NA_BLK), :].T
        for p in range(n_pairs):
            cols = slice(p * LANES, (p + 1) * LANES)
            kpt_scr[p, blk] = kt[cols].astype(BF16)
            va_scr[p, pl.ds(r0, NA_BLK), :] = jnp.concatenate(
                [v_ref[pl.ds(r0, NA_BLK), cols], jnp.ones((NA_BLK, LANES), F32)], 1).astype(BF16)
        return c

    lax.fori_loop(0, nblk, prep, 0)

    def body(r, c):
        kr0 = jnp.clip(r - half, 0, rows - NA_WIN_H)
        sb = jnp.minimum(kr0 // 2, nblk - NA_SPAN_BLKS)
        var = jnp.where(r <= half, r, jnp.where(r >= rows - half, r - (rows - 2 * half - 2), half + (kr0 & 1)))
        q0 = pl.multiple_of(r * GRID_W, GRID_W)
        k0 = pl.multiple_of(sb * NA_BLK, NA_BLK)
        pairs = range(n_pairs)
        qps = [q_ref[pl.ds(q0, GRID_W), p * LANES:(p + 1) * LANES] * scale for p in pairs]
        k_loc = [jnp.concatenate([kpt_scr[p, sb + j] for j in range(NA_SPAN_BLKS)], axis=1) for p in pairs]
        v_loc = [va_scr[p, pl.ds(k0, NA_SPAN_BLKS * NA_BLK), :] for p in pairs]
        outs = _pairs_attention(qps, [[kcp_scr[p], k_loc[p]] for p in pairs], [[vca_scr[p], v_loc[p]] for p in pairs],
                                [[None, bias_ref[0, p, var]] for p in pairs], [None for _ in pairs])
        for p in pairs:
            y_ref[pl.ds(q0, GRID_W), p * LANES:(p + 1) * LANES] = outs[p]
        return c

    lax.fori_loop(0, rows, body, 0)


def _na_attn_call(cq, ck, cv, cache_k, cache_v, bias_tab, layer):
    rb0 = T_CTX // N_LAT
    n_pairs = D_C // LANES
    spec = pl.BlockSpec((N_LAT, D_C), lambda b: (rb0 + b, 0))
    cache = pl.BlockSpec((1, 1, PAST_LEN, D_C), lambda b: (b, layer, 0, 0))
    span = NA_SPAN_BLKS * NA_BLK
    return pl.pallas_call(
        _na_attn_kernel,
        grid=(B_LAT,),
        in_specs=[spec, spec, spec, cache, cache,
                  pl.BlockSpec((1, n_pairs, len(NA_VARIANTS), 2 * GRID_W, span), lambda b: (layer, 0, 0, 0, 0))],
        out_specs=pl.BlockSpec((N_LAT, D_C), lambda b: (b, 0)),
        out_shape=jax.ShapeDtypeStruct((T_LAT, D_C), F32),
        scratch_shapes=[pltpu.VMEM((n_pairs, N_LAT // NA_BLK, LANES, NA_BLK), BF16),
                        pltpu.VMEM((n_pairs, N_LAT, 2 * LANES), BF16),
                        pltpu.VMEM((n_pairs, LANES, PAST_LEN), BF16),
                        pltpu.VMEM((n_pairs, PAST_LEN, 2 * LANES), BF16)],
        compiler_params=_cparams("arbitrary"),
        name="na_attn",
    )(cq, ck, cv, cache_k, cache_v, bias_tab)


def _outproj_kernel(yac_ref, ybc_ref, ycc_ref, yal_ref, ybl_ref, ycl_ref, x_ref, ga_ref, scf_ref, shf_ref,
                    lng_ref, lnb_ref, wo_ref, wr_ref, x1_ref, h2_ref, route_ref, cnt_ref, run_scr):
    @pl.when(pl.program_id(0) == 0)
    def _():
        run_scr[...] = jnp.zeros_like(run_scr)

    is_ctx = pl.program_id(0) < T_CTX // TM

    def pick(c_ref, l_ref):
        return jnp.where(is_ctx, c_ref[...], l_ref[...])

    y = (_dot(pick(yac_ref, yal_ref), wo_ref[0, 0:Y_A_COLS, :])
         + _dot(pick(ybc_ref, ybl_ref), wo_ref[0, Y_A_COLS:Y_A_COLS + D_B, :])
         + _dot(pick(ycc_ref, ycl_ref), wo_ref[0, Y_A_COLS + D_B:, :]))
    x1 = _layer_norm(ALPHA * x_ref[...] + ga_ref[0, 0, 0] * y, lng_ref[...], lnb_ref[...])
    x1_ref[...] = x1
    h2 = x1 * (1.0 + scf_ref[0, 0, 0]) + shf_ref[0, 0, 0]
    h2_ref[...] = h2
    w_r = wr_ref[...]
    w_hi = w_r.astype(BF16)
    w_lo = (w_r - w_hi.astype(F32)).astype(BF16)
    h_hi = h2.astype(BF16)
    h_lo = (h2 - h_hi.astype(F32)).astype(BF16)
    p_hi = jnp.dot(h_hi, jnp.concatenate([w_hi, w_lo], axis=1), preferred_element_type=F32)
    logits = p_hi[:, :LANES] + p_hi[:, LANES:] + jnp.dot(h_lo, w_hi, preferred_element_type=F32)
    lane = lax.broadcasted_iota(jnp.int32, logits.shape, 1)
    lanef = lane.astype(F32)
    big = float(LANES)
    lg = jnp.where((lane >= N_EXPERTS) & (lane < N_EXPERTS + N_GROUPS), logits, NEG_INF)
    mg = jnp.max(lg, axis=1, keepdims=True)
    grp = jnp.min(jnp.where(lg == mg, lanef, big), axis=1, keepdims=True) - float(N_EXPERTS)
    g_w = 1.0 / jnp.sum(jnp.exp(lg - mg), axis=1, keepdims=True)
    in_grp = (lane < N_EXPERTS) & ((lane // E_PER_GROUP).astype(F32) == grp)
    le = jnp.where(in_grp, logits, NEG_INF)
    l1 = jnp.max(le, axis=1, keepdims=True)
    i1 = jnp.min(jnp.where(le == l1, lanef, big), axis=1, keepdims=True)
    le2 = jnp.where(lanef == i1, NEG_INF, le)
    l2 = jnp.max(le2, axis=1, keepdims=True)
    i2 = jnp.min(jnp.where(le2 == l2, lanef, big), axis=1, keepdims=True)
    e2 = jnp.exp(l2 - l1)
    w1 = g_w / (1.0 + e2)
    w2 = g_w * e2 / (1.0 + e2)
    oh1 = jnp.where(lanef == i1, 1.0, 0.0)
    oh2 = jnp.where(lanef == i2, 1.0, 0.0)
    rt = lax.broadcasted_iota(jnp.int32, (TM, TM), 0)
    ct = lax.broadcasted_iota(jnp.int32, (TM, TM), 1)
    before = jnp.where(ct < rt, 1.0, 0.0)
    run = run_scr[...]
    tot1 = jnp.sum(oh1, axis=0, keepdims=True)
    r1 = jnp.sum(oh1 * (run + _dot(before, oh1)), axis=1, keepdims=True)
    r2 = jnp.sum(oh2 * (run + tot1 + _dot(before, oh2)), axis=1, keepdims=True)
    run = run + tot1 + jnp.sum(oh2, axis=0, keepdims=True)
    run_scr[...] = run
    cnt_ref[...] = run
    vals = (i1, i2, w1, w2, r1, r2)
    out = jnp.zeros_like(logits)
    for n, v in enumerate(vals):
        out = jnp.where(lane == n, v, out)
    route_ref[...] = out[:, :8]


def _outproj_call(y_ctx, y_lat, x, mod_t, layer, lng, lnb, wo, wr):
    row_vec = pl.BlockSpec((1, D_MODEL), lambda i: (0, 0))
    n_ctx = T_CTX // TM

    def tok(n):
        return pl.BlockSpec((TM, n), lambda i: (i, 0))

    def tok_ctx(n):
        return pl.BlockSpec((TM, n), lambda i: (jnp.minimum(i, n_ctx - 1), 0))

    def tok_lat(n):
        return pl.BlockSpec((TM, n), lambda i: (jnp.maximum(i - n_ctx, 0), 0))

    return pl.pallas_call(
        _outproj_kernel,
        grid=(N_TILES,),
        in_specs=[tok_ctx(Y_A_COLS), tok_ctx(D_B), tok_ctx(D_C), tok_lat(Y_A_COLS), tok_lat(D_B), tok_lat(D_C),
                  tok(D_MODEL), _mod_spec(layer, 2), _mod_spec(layer, 4), _mod_spec(layer, 3), row_vec, row_vec,
                  pl.BlockSpec((1, Y_A_COLS + D_B + D_C, D_MODEL), lambda i: (layer, 0, 0)),
                  pl.BlockSpec((D_MODEL, LANES), lambda i: (0, 0))],
        out_specs=[tok(D_MODEL), tok(D_MODEL), tok(8), pl.BlockSpec((1, LANES), lambda i: (0, 0))],
        out_shape=[jax.ShapeDtypeStruct((T_ALL, D_MODEL), F32), jax.ShapeDtypeStruct((T_ALL, D_MODEL), F32),
                   jax.ShapeDtypeStruct((T_ALL, 8), F32), jax.ShapeDtypeStruct((1, LANES), F32)],
        scratch_shapes=[pltpu.VMEM((1, LANES), F32)],
        compiler_params=_cparams("arbitrary"),
        name="outproj_router",
    )(*y_ctx, *y_lat, x, mod_t, mod_t, mod_t, lng, lnb, wo, wr)


def _dispatch_kernel(pos_ref, pend_ref, h_ref, xs_hbm, zero_scr, sem):
    base = pl.program_id(0) * TM

    @pl.when(pl.program_id(0) == 0)
    def _():
        zero_scr[...] = jnp.zeros_like(zero_scr)

        def fill(e, op):
            prev = pend_ref[e - 1] if e else 0

            @pl.when(pend_ref[e] > prev)
            def _():
                first = pl.multiple_of(pend_ref[e] - MOE_BLK, MOE_BLK)
                op(pltpu.make_async_copy(zero_scr, xs_hbm.at[pl.ds(first, MOE_BLK), :], sem))

        def tail_copy(b):
            first = pl.multiple_of(b * MOE_BLK, MOE_BLK)
            return pltpu.make_async_copy(zero_scr, xs_hbm.at[pl.ds(first, MOE_BLK), :], sem)

        def tail_start(b, c):
            tail_copy(b).start()
            return c

        def tail_wait(b, c):
            tail_copy(b).wait()
            return c

        n_blocks = xs_hbm.shape[0] // MOE_BLK
        first_free = pend_ref[N_EXPERTS - 1] // MOE_BLK
        for e in range(N_EXPERTS):
            fill(e, lambda cp: cp.start())
        lax.fori_loop(first_free, n_blocks, tail_start, 0)
        for e in range(N_EXPERTS):
            fill(e, lambda cp: cp.wait())
        lax.fori_loop(first_free, n_blocks, tail_wait, 0)

    def row(t, p):
        return pltpu.make_async_copy(h_ref.at[pl.ds(t, 1), :], xs_hbm.at[pl.ds(p, 1), :], sem)

    def issue(t, c):
        row(t, pos_ref[base + t]).start(priority=0)
        row(t, pos_ref[T_ALL + base + t]).start(priority=1)
        return c

    lax.fori_loop(0, TM, issue, 0, unroll=8)
    whole = pltpu.make_async_copy(h_ref, xs_hbm.at[pl.ds(0, TM), :], sem)
    whole.wait()
    whole.wait()


def _dispatch_call(pos2, p_end, h2, nblk):
    grid_spec = pltpu.PrefetchScalarGridSpec(
        num_scalar_prefetch=2,
        grid=(N_TILES,),
        in_specs=[pl.BlockSpec((TM, D_MODEL), lambda i, p, pe: (i, 0))],
        out_specs=pl.BlockSpec(memory_space=pl.ANY),
        scratch_shapes=[pltpu.VMEM((MOE_BLK, D_MODEL), F32), pltpu.SemaphoreType.DMA(())],
    )
    return pl.pallas_call(
        _dispatch_kernel,
        grid_spec=grid_spec,
        out_shape=jax.ShapeDtypeStruct((nblk * MOE_BLK, D_MODEL), F32),
        compiler_params=_cparams("arbitrary"),
        name="dispatch",
    )(pos2, p_end, h2)


def _expert_kernel(blk_e_ref, nused_ref, xs_ref, wg_ref, wu_ref, wd_ref, out_ref):
    j = pl.program_id(0)

    @pl.when(j < nused_ref[0])
    def _():
        xb = xs_ref[...].astype(BF16)
        g = jnp.dot(xb, wg_ref[0, 0].astype(BF16), preferred_element_type=F32)
        u = jnp.dot(xb, wu_ref[0, 0].astype(BF16), preferred_element_type=F32)
        out_ref[...] = _dot(jax.nn.silu(g) * u, wd_ref[0, 0])

    @pl.when(j >= nused_ref[0])
    def _():
        out_ref[...] = jnp.zeros_like(out_ref)


def _expert_call(blk_e, nused, xs, wg, wu, wd, layer, nblk):
    def wspec(r, c):
        return pl.BlockSpec((1, 1, r, c), lambda j, be, nu: (layer, be[j], 0, 0))

    grid_spec = pltpu.PrefetchScalarGridSpec(
        num_scalar_prefetch=2,
        grid=(nblk,),
        in_specs=[
            pl.BlockSpec((MOE_BLK, D_MODEL), lambda j, be, nu: (jnp.clip(j, 0, jnp.maximum(nu[0] - 1, 0)), 0)),
            wspec(D_MODEL, D_EXPERT), wspec(D_MODEL, D_EXPERT), wspec(D_EXPERT, D_MODEL),
        ],
        out_specs=pl.BlockSpec((MOE_BLK, D_MODEL), lambda j, be, nu: (j, 0)),
    )
    return pl.pallas_call(
        _expert_kernel,
        grid_spec=grid_spec,
        out_shape=jax.ShapeDtypeStruct((nblk * MOE_BLK, D_MODEL), F32),
        compiler_params=_cparams("arbitrary"),
        name="experts",
    )(blk_e, nused, xs, wg, wu, wd)


def _combine_kernel(pos_ref, eo_hbm, x1_ref, route_ref, gf_ref, lng_ref, lnb_ref, x2_ref, buf, sem):
    base = pl.program_id(0) * TM

    def row(p, r, t):
        return pltpu.make_async_copy(eo_hbm.at[pl.ds(p, 1), :], buf.at[r, pl.ds(t, 1), :], sem)

    def issue(t, c):
        row(pos_ref[base + t], 0, t).start(priority=0)
        row(pos_ref[T_ALL + base + t], 1, t).start(priority=1)
        return c

    lax.fori_loop(0, TM, issue, 0, unroll=8)
    for r in range(2):
        pltpu.make_async_copy(eo_hbm.at[pl.ds(0, TM), :], buf.at[r], sem).wait()
    route = route_ref[...]
    y = route[:, 2:3] * buf[0] + route[:, 3:4] * buf[1]
    x2_ref[...] = _layer_norm(ALPHA * x1_ref[...] + gf_ref[0, 0, 0] * y, lng_ref[...], lnb_ref[...])


def _combine_call(pos2, eo, x1, route, mod_t, layer, lng, lnb):
    grid_spec = pltpu.PrefetchScalarGridSpec(
        num_scalar_prefetch=1,
        grid=(N_TILES,),
        in_specs=[
            pl.BlockSpec(memory_space=pl.ANY),
            pl.BlockSpec((TM, D_MODEL), lambda i, p: (i, 0)),
            pl.BlockSpec((TM, 8), lambda i, p: (i, 0)),
            _mod_spec(layer, 5),
            pl.BlockSpec((1, D_MODEL), lambda i, p: (0, 0)),
            pl.BlockSpec((1, D_MODEL), lambda i, p: (0, 0)),
        ],
        out_specs=pl.BlockSpec((TM, D_MODEL), lambda i, p: (i, 0)),
        scratch_shapes=[pltpu.VMEM((2, TM, D_MODEL), F32), pltpu.SemaphoreType.DMA(())],
    )
    return pl.pallas_call(
        _combine_kernel,
        grid_spec=grid_spec,
        out_shape=jax.ShapeDtypeStruct((T_ALL, D_MODEL), F32),
        compiler_params=_cparams("arbitrary"),
        name="combine_norm",
    )(pos2, eo, x1, route, mod_t, lng, lnb)


def _gate_cols(grp):
    return np.array([kind * 2 * H_A + d * H_A + grp * HP_A + j
                     for j in range(HP_A) for kind in range(2) for d in range(2)])


def _prep_w_in(w_in):
    a = w_in[..., :4 * D_A].reshape(DEPTH, D_MODEL, 4, H_A, DH_A)
    a = jnp.pad(a, ((0, 0), (0, 0), (0, 0), (0, 0), (0, DP_A - DH_A))).reshape(DEPTH, D_MODEL, ZA_COLS)
    gates = w_in[..., 4 * D_A:4 * D_A + 4 * H_A]
    g = jnp.concatenate([jnp.pad(gates[..., _gate_cols(grp)], ((0, 0), (0, 0), (0, LANES - 4 * HP_A)))
                         for grp in range(H_A // HP_A)], -1)
    rest = w_in[..., 4 * D_A + 4 * H_A:]
    return jnp.concatenate([a, rest, g], -1).astype(BF16)


def _prep_w_out(w_out):
    a = w_out[:, :D_A].reshape(DEPTH, H_A, DH_A, D_MODEL)
    a = jnp.pad(a, ((0, 0), (0, 0), (0, DP_A - DH_A), (0, 0))).reshape(DEPTH, Y_A_COLS, D_MODEL)
    return jnp.concatenate([a, w_out[:, D_A:]], 1).astype(BF16)


def _rope_tables():
    t = np.arange(N_LAT)
    nf = HD // 4
    inv = ROPE_BASE ** (-np.arange(nf, dtype=np.float32) / nf)
    ar = (t // GRID_W).astype(np.float32)[:, None] * inv
    ac = (t % GRID_W).astype(np.float32)[:, None] * inv
    ang = jnp.asarray(np.concatenate([ar, ar, ac, ac], -1), F32)
    cos, sin = jnp.cos(ang), jnp.sin(ang)
    sign = np.where((np.arange(HD) % 32) < 16, -1.0, 1.0).astype(np.float32)
    reps = LANES // HD
    return jnp.tile(cos, (1, reps)), jnp.tile(sin * sign, (1, reps))


def _na_bias_tables(rpb):
    qcol = np.arange(GRID_W)[:, None]
    kcol = np.arange(GRID_W)[None, :]
    dc = np.clip(kcol - qcol, 1 - NA_WIN_W, NA_WIN_W - 1) + NA_WIN_W - 1
    wstart = np.clip(qcol - NA_WIN_W // 2, 0, GRID_W - NA_WIN_W)
    in_win = (kcol >= wstart) & (kcol < wstart + NA_WIN_W)
    sel = (np.arange(2 * NA_WIN_W - 1)[:, None] == dc.reshape(1, -1)).astype(np.float32)
    cols = jnp.einsum("lhrd,dn->lhrn", rpb, jnp.asarray(sel), precision=lax.Precision.HIGHEST)
    cols = jnp.where(in_win.reshape(-1), cols, NEG_INF).reshape(DEPTH, H_C, 2 * NA_WIN_H - 1, GRID_W, GRID_W)
    outside = jnp.full((DEPTH, H_C, GRID_W, GRID_W), NEG_INF, F32)
    variants = []
    for dr0, off in NA_VARIANTS:
        span_rows = [cols[:, :, dr0 + i - off] if 0 <= i - off < NA_WIN_H else outside for i in range(NA_SPAN_ROWS)]
        variants.append(jnp.stack(span_rows, 3).reshape(DEPTH, H_C, GRID_W, NA_SPAN_ROWS * GRID_W))
    tab = jnp.stack(variants, 2).reshape(DEPTH, H_C // 2, 2, len(NA_VARIANTS), GRID_W, NA_SPAN_ROWS * GRID_W)
    return tab.transpose(0, 1, 3, 2, 4, 5).reshape(DEPTH, H_C // 2, len(NA_VARIANTS), 2 * GRID_W, NA_SPAN_ROWS * GRID_W)


def _dispatch_plan(route, counts):
    nblk = 2 * T_ALL // MOE_BLK + N_EXPERTS
    cnt = counts[0, :N_EXPERTS].astype(jnp.int32)
    padded = (cnt + MOE_BLK - 1) // MOE_BLK * MOE_BLK
    p_end = jnp.cumsum(padded)
    p_start = p_end - padded
    e = route[:, 0:2].astype(jnp.int32)
    hot = e[..., None] == jnp.arange(N_EXPERTS, dtype=jnp.int32)
    pos = jnp.sum(jnp.where(hot, p_start, 0), -1) + route[:, 4:6].astype(jnp.int32)
    pos2 = pos.T.reshape(-1)
    blk_first = jnp.arange(nblk, dtype=jnp.int32) * MOE_BLK
    blk_e = jnp.minimum(jnp.sum((p_end[None, :] <= blk_first[:, None]).astype(jnp.int32), axis=1), N_EXPERTS - 1)
    nused = p_end[-1:] // MOE_BLK
    return blk_e, nused, pos2, p_end, nblk


def kernel(x_prompt, x_sample, state_a_C, state_a_n, state_a_m, cache_b_k, cache_b_v, cache_c_k, cache_c_v, c, c_ctx, w_in, b_a_i, b_a_f, w_a_hnorm, b_sink, rpb, w_out, w_ada, b_ada, ln_g, ln_b, w_router_grp, w_router_exp, w_e_gate, w_e_up, w_e_down):
    w_in_p = _prep_w_in(w_in)
    w_out_p = _prep_w_out(w_out)
    w_r = jnp.pad(jnp.concatenate([w_router_exp, w_router_grp], -1),
                  ((0, 0), (0, 0), (0, LANES - N_EXPERTS - N_GROUPS)))
    wn_p = jnp.pad(w_a_hnorm.reshape(DEPTH, H_A, 1, DH_A), ((0, 0), (0, 0), (0, 0), (0, DP_A - DH_A)))
    gate_b = jnp.concatenate([b_a_i, b_a_f], 1).transpose(0, 2, 1)
    gate_b = jnp.pad(gate_b.reshape(DEPTH, H_A // HP_A, 1, 4 * HP_A), ((0, 0), (0, 0), (0, 0), (0, LANES - 4 * HP_A)))
    sink_p = jnp.pad(b_sink, ((0, 0), (0, 8 - H_B))).reshape(DEPTH, 1, 8)
    cos_t, sin_t = _rope_tables()
    na_bias = _na_bias_tables(rpb)
    cb_k = cache_b_k.reshape(B_LAT, DEPTH, PAST_LEN, D_KVB)
    cb_v = cache_b_v.reshape(B_LAT, DEPTH, PAST_LEN, D_KVB)
    cc_k = cache_c_k.reshape(B_LAT, DEPTH, PAST_LEN, D_C)
    cc_v = cache_c_v.reshape(B_LAT, DEPTH, PAST_LEN, D_C)
    pad_c = ((0, 0), (0, 0), (0, 0), (0, 0), (0, DP_A - DH_A), (0, DP_A - DH_A))
    st_ct = jnp.swapaxes(jnp.pad(state_a_C, pad_c), -1, -2)
    st_nr = jnp.broadcast_to(jnp.pad(state_a_n, pad_c[:-1])[..., None], st_ct.shape)
    st_s = jnp.concatenate([st_ct, st_nr], -1)
    st_m = state_a_m[..., None, None]
    z_s = jnp.zeros((B_CTX, 2, H_A, DP_A, 2 * DP_A), F32)
    z_m = jnp.zeros((B_CTX, 2, H_A, 1, 1), F32)

    cvec = jnp.concatenate([c, c_ctx[None, :], jnp.zeros((3, D_MODEL), F32)], 0)
    mod = _ada_call(cvec, w_ada, b_ada)
    tile_row = np.concatenate([np.full(T_CTX // TM, B_LAT), np.repeat(np.arange(B_LAT), N_LAT // TM)])
    mod_t = mod[:, tile_row].reshape(DEPTH, N_TILES, 6, 1, D_MODEL)

    x = jnp.concatenate([x_prompt.reshape(T_CTX, D_MODEL), x_sample.reshape(T_LAT, D_MODEL)], 0)
    cs_, ns_, ms_, kbs, vbs, kcs, vcs = [], [], [], [], [], [], []
    for l in range(DEPTH):
        za, bq, bk, bv, cq, ck, cv, zg = _inproj_call(x, mod_t, l, w_in_p)
        ya_c, s_l, m_l = _mlstm_call(za, zg, gate_b[l], wn_p[l], z_s, z_m, nb=B_CTX, seq=L_CTX, row_blk0=0)
        yb_c, yc_c = _ctx_attn_call(bq, bk, bv, cq, ck, cv, sink_p[l])
        ya_l, _, _ = _mlstm_call(za, zg, gate_b[l], wn_p[l], st_s[:, l], st_m[:, l],
                                 nb=B_LAT, seq=N_LAT, row_blk0=T_CTX // N_LAT)
        yb_l = _win_attn_call(bq, bk, bv, cb_k, cb_v, cos_t, sin_t, sink_p[l], l)
        yc_l = _na_attn_call(cq, ck, cv, cc_k, cc_v, na_bias, l)
        x1, h2, route, counts = _outproj_call((ya_c, yb_c, yc_c), (ya_l, yb_l, yc_l), x, mod_t, l,
                                              ln_g[l, 0:1], ln_b[l, 0:1], w_out_p, w_r[l])
        blk_e, nused, pos2, p_end, nblk = _dispatch_plan(route, counts)
        xs = _dispatch_call(pos2, p_end, h2, nblk)
        eo = _expert_call(blk_e, nused, xs, w_e_gate, w_e_up, w_e_down, l, nblk)
        x = _combine_call(pos2, eo, x1, route, mod_t, l, ln_g[l, 1:2], ln_b[l, 1:2])
        cs_.append(jnp.swapaxes(s_l[..., :DH_A, :DH_A], -1, -2))
        ns_.append(s_l[..., :DH_A, DP_A])
        ms_.append(m_l.reshape(B_CTX, 2, H_A))
        kbs.append(bk[:T_CTX].reshape(B_CTX, L_CTX, KV_B, HD))
        vbs.append(bv[:T_CTX].reshape(B_CTX, L_CTX, KV_B, HD))
        kcs.append(ck[:T_CTX].reshape(B_CTX, L_CTX, H_C, HD))
        vcs.append(cv[:T_CTX].reshape(B_CTX, L_CTX, H_C, HD))
    y_prompt = x[:T_CTX].reshape(B_CTX, L_CTX, D_MODEL)
    y_sample = x[T_CTX:].reshape(B_LAT, N_LAT, D_MODEL)
    return (y_prompt, y_sample, jnp.stack(cs_, 1), jnp.stack(ns_, 1), jnp.stack(ms_, 1),
            jnp.stack(kbs, 1), jnp.stack(vbs, 1), jnp.stack(kcs, 1), jnp.stack(vcs, 1))
```

```python
import functools

import numpy as np
import jax
import jax.numpy as jnp
from jax import lax
from jax.experimental import pallas as pl
from jax.experimental.pallas import tpu as pltpu

F32 = jnp.float32
BF16 = jnp.bfloat16
NEG_INF = float("-inf")

D_MODEL = 1024
DEPTH = 4
B_CTX, L_CTX = 16, 256
B_LAT, N_LAT = 4, 2048
PAST_LEN = 512
GRID_W = 64
H_A, DH_A = 4, 96
D_A = H_A * DH_A
H_B, KV_B, HD = 6, 2, 64
G_B = H_B // KV_B
D_B, D_KVB = H_B * HD, KV_B * HD
WIN = 128
ROPE_BASE = 10000.0
H_C = 4
D_C = H_C * HD
NA_WIN_H, NA_WIN_W = 8, 16
N_GROUPS, E_PER_GROUP = 4, 8
N_EXPERTS = N_GROUPS * E_PER_GROUP
D_EXPERT = D_MODEL // 4
ALPHA = (2 * DEPTH) ** 0.25
LN_EPS = 1e-5

LANES = 128
DP_A = LANES
LC_K = LANES
TM = 256
MOE_BLK = 512
VMEM_LIMIT = 48 * 1024 * 1024

T_CTX = B_CTX * L_CTX
T_LAT = B_LAT * N_LAT
T_ALL = T_CTX + T_LAT
N_TILES = T_ALL // TM
ZA_COLS = 4 * H_A * DP_A
Y_A_COLS = H_A * DP_A
IN_SPLITS = (ZA_COLS, D_B, D_KVB, D_KVB, D_C, D_C, D_C, 2 * LANES)
IN_COLS = sum(IN_SPLITS)


def _cparams(*sem):
    return pltpu.CompilerParams(dimension_semantics=sem, vmem_limit_bytes=VMEM_LIMIT)


def _dot(a, b):
    return jnp.dot(a.astype(BF16), b.astype(BF16), preferred_element_type=F32)


def _dot_nt(a, b):
    return lax.dot_general(a.astype(BF16), b.astype(BF16), (((1,), (1,)), ((), ())), preferred_element_type=F32)


def _dot_tn(a, b):
    return lax.dot_general(a.astype(BF16), b.astype(BF16), (((0,), (0,)), ((), ())), preferred_element_type=F32)


def _layer_norm(v, g, b):
    mu = jnp.mean(v, -1, keepdims=True)
    var = jnp.mean(jnp.square(v - mu), -1, keepdims=True)
    return (v - mu) * lax.rsqrt(var + LN_EPS) * g + b


def _ada_kernel(c_ref, w_ref, b_ref, o_ref):
    s = jax.nn.silu(c_ref[...])
    o_ref[0] = _dot(s, w_ref[0]) + b_ref[0]


def _ada_call(cvec, w_ada, b_ada):
    nb = 6
    return pl.pallas_call(
        _ada_kernel,
        grid=(DEPTH, nb),
        in_specs=[
            pl.BlockSpec((8, D_MODEL), lambda l, j: (0, 0)),
            pl.BlockSpec((1, D_MODEL, D_MODEL), lambda l, j: (l, 0, j)),
            pl.BlockSpec((1, 1, D_MODEL), lambda l, j: (l, 0, j)),
        ],
        out_specs=pl.BlockSpec((1, 8, D_MODEL), lambda l, j: (l, 0, j)),
        out_shape=jax.ShapeDtypeStruct((DEPTH, 8, 6 * D_MODEL), F32),
        compiler_params=_cparams("arbitrary", "arbitrary"),
        name="adaln",
    )(cvec, w_ada, b_ada.reshape(DEPTH, 1, 6 * D_MODEL))


def _inproj_kernel(x_ref, sc_ref, sh_ref, w_ref, *out_refs):
    h = (x_ref[...] * (1.0 + sc_ref[0, 0, 0]) + sh_ref[0, 0, 0]).astype(BF16)
    off = 0
    for ref in out_refs:
        n = ref.shape[-1]
        ref[...] = jnp.dot(h, w_ref[0, :, off:off + n], preferred_element_type=F32)
        off += n


def _mod_spec(layer, which):
    return pl.BlockSpec((1, 1, 1, 1, D_MODEL), lambda i, *_: (layer, i, which, 0, 0))


def _inproj_call(x, mod_t, layer, w):
    return pl.pallas_call(
        _inproj_kernel,
        grid=(N_TILES,),
        in_specs=[
            pl.BlockSpec((TM, D_MODEL), lambda i: (i, 0)),
            _mod_spec(layer, 1), _mod_spec(layer, 0),
            pl.BlockSpec((1, D_MODEL, IN_COLS), lambda i: (layer, 0, 0)),
        ],
        out_specs=[pl.BlockSpec((TM, n), lambda i: (i, 0)) for n in IN_SPLITS],
        out_shape=[jax.ShapeDtypeStruct((T_ALL, n), F32) for n in IN_SPLITS],
        compiler_params=_cparams("arbitrary"),
        name="inproj",
    )(x, mod_t, mod_t, w)


HP_A = 2
N_CH = 2 * HP_A


def _mlstm_kernel(q_ref, k_ref, v_ref, o_ref, g_ref, gb_ref, wn_ref, s0_ref, m0_ref,
                  y_ref, so_ref, mo_ref, hf_scr, hb_scr, *, nc):
    lc = LC_K
    scale = DH_A ** -0.5
    chains = [(j, d) for j in range(HP_A) for d in range(2)]
    ti = lax.broadcasted_iota(jnp.int32, (lc, lc), 0)
    si = lax.broadcasted_iota(jnp.int32, (lc, lc), 1)
    lane_ok = lax.broadcasted_iota(jnp.int32, (1, DP_A), 1) < DH_A
    ones = jnp.ones((lc, DP_A), F32)

    def stack(parts):
        return jnp.concatenate(parts, axis=0)

    def rows_of(x, a):
        return a[x * lc:(x + 1) * lc]

    def spread(vals):
        return stack([jnp.broadcast_to(v, (lc, v.shape[1])) for v in vals])

    mask = stack([si <= ti if d == 0 else si >= ti for _, d in chains])
    mask_t = stack([si >= ti if d == 0 else si <= ti for _, d in chains])
    eye = stack([si == ti for _ in chains])

    def col_sums(a):
        return [jnp.sum(rows_of(x, a), axis=0, keepdims=True) for x in range(N_CH)]

    def body(i, carry):
        smats = carry[:N_CH]
        ms = carry[N_CH:]
        r0s = [pl.multiple_of((i if d == 0 else nc - 1 - i) * lc, lc) for _, d in chains]
        q, kt, v1, icol, fpre = [], [], [], [], []
        for (j, d), r0 in zip(chains, r0s):
            cols = slice(j * DP_A, (j + 1) * DP_A)
            q.append(q_ref[pl.ds(r0, lc), cols].astype(BF16))
            kt.append((k_ref[pl.ds(r0, lc), cols] * scale).T.astype(BF16))
            v1.append(jnp.concatenate([v_ref[pl.ds(r0, lc), cols], ones], axis=1))
            gz = g_ref[pl.ds(r0, lc), :] + gb_ref[0]
            icol.append(gz[:, 4 * j + d:4 * j + d + 1])
            fpre.append(gz[:, 4 * j + 2 + d:4 * j + 3 + d])
        i_col = stack(icol)
        f_col = jax.nn.log_sigmoid(stack(fpre))
        b_rows = col_sums(jnp.where(mask_t, f_col, 0.0))
        bls = col_sums(f_col)
        b_col = jnp.sum(jnp.where(eye, spread(b_rows), 0.0), axis=1, keepdims=True)
        a_rows = col_sums(jnp.where(eye, i_col - b_col, 0.0))
        a_sp = jnp.where(mask, spread(a_rows), NEG_INF)
        m_sp = spread(ms)
        gap = jnp.maximum(m_sp, jnp.max(a_sp, axis=1, keepdims=True))
        m_out = b_col + gap
        wmat = jnp.exp(a_sp - gap)
        sw = (stack([jnp.dot(q[x], kt[x], preferred_element_type=F32) for x in range(N_CH)]) * wmat).astype(BF16)
        sc_in = jnp.exp(m_sp - gap)
        tot = (stack([jnp.dot(rows_of(x, sw), v1[x].astype(BF16), preferred_element_type=F32) for x in range(N_CH)])
               + sc_in * stack([_dot(q[x], smats[x]) for x in range(N_CH)]))
        h = tot[:, :DP_A] / jnp.maximum(jnp.abs(tot[:, DP_A:]), jnp.exp(-m_out))
        for x, ((j, d), r0) in enumerate(zip(chains, r0s)):
            dst = hf_scr if d == 0 else hb_scr
            dst[pl.ds(r0, lc), j * DP_A:(j + 1) * DP_A] = rows_of(x, h)
        dec = spread(bls) - b_col + i_col
        m_new = [jnp.maximum(bls[x] + ms[x], jnp.max(rows_of(x, dec), axis=0, keepdims=True)) for x in range(N_CH)]
        wk = jnp.exp(dec - spread(m_new))
        s_new = [jnp.exp(bls[x] + ms[x] - m_new[x]) * smats[x] + _dot(kt[x], rows_of(x, wk) * v1[x])
                 for x in range(N_CH)]
        return tuple(s_new) + tuple(m_new)

    init = tuple(s0_ref[0, d, j] for j, d in chains) + tuple(m0_ref[0, d, j] for j, d in chains)
    final = lax.fori_loop(0, nc, body, init)
    for x, (j, d) in enumerate(chains):
        so_ref[0, d, j] = final[x]
        mo_ref[0, d, j] = final[N_CH + x]

    def finish(c, _):
        r0 = pl.multiple_of(c * lc, lc)
        for j in range(HP_A):
            cols = slice(j * DP_A, (j + 1) * DP_A)
            h = hf_scr[pl.ds(r0, lc), cols] + hb_scr[pl.ds(r0, lc), cols]
            mu = jnp.sum(h, axis=1, keepdims=True) * (1.0 / DH_A)
            dv = jnp.where(lane_ok, h - mu, 0.0)
            var = jnp.sum(dv * dv, axis=1, keepdims=True) * (1.0 / DH_A)
            hn = dv * lax.rsqrt(var + LN_EPS) * wn_ref[j]
            y_ref[pl.ds(r0, lc), cols] = jax.nn.sigmoid(o_ref[pl.ds(r0, lc), cols]) * hn
        return 0

    lax.fori_loop(0, nc, finish, 0)


def _mlstm_call(za, zg, gbias, wn, s0, m0, *, nb, seq, row_blk0):
    nc = seq // LC_K
    ng = H_A // HP_A
    w = HP_A * DP_A

    def zspec(part):
        return pl.BlockSpec((seq, w), lambda b, g: (row_blk0 + b, part * ng + g))

    def state(*tail):
        return pl.BlockSpec((1, 2, HP_A) + tail, lambda b, g: (b, 0, g) + (0,) * len(tail))

    return pl.pallas_call(
        functools.partial(_mlstm_kernel, nc=nc),
        grid=(nb, ng),
        in_specs=[
            zspec(0), zspec(1), zspec(2), zspec(3),
            pl.BlockSpec((seq, LANES), lambda b, g: (row_blk0 + b, g)),
            pl.BlockSpec((1, 1, LANES), lambda b, g: (g, 0, 0)),
            pl.BlockSpec((HP_A, 1, DP_A), lambda b, g: (g, 0, 0)),
            state(DP_A, 2 * DP_A), state(1, 1),
        ],
        out_specs=[pl.BlockSpec((seq, w), lambda b, g: (b, g)), state(DP_A, 2 * DP_A), state(1, 1)],
        out_shape=[
            jax.ShapeDtypeStruct((nb * seq, Y_A_COLS), F32),
            jax.ShapeDtypeStruct((nb, 2, H_A, DP_A, 2 * DP_A), F32),
            jax.ShapeDtypeStruct((nb, 2, H_A, 1, 1), F32),
        ],
        scratch_shapes=[pltpu.VMEM((seq, w), F32), pltpu.VMEM((seq, w), F32)],
        compiler_params=_cparams("arbitrary", "arbitrary"),
        name="mlstm",
    )(za, za, za, za, zg, gbias, wn, s0, m0)


def _pair_attention(qp, kslabs, vaugs, masks, sink_col):
    return _pairs_attention([qp], [kslabs], [vaugs], [masks], [sink_col])[0]


def _pairs_attention(qps, kslabs, vaugs, masks, sink_cols):
    n_p = len(qps)
    m_rows = qps[0].shape[0]
    lo = lax.broadcasted_iota(jnp.int32, qps[0].shape, 1) < HD
    q2 = [jnp.concatenate([jnp.where(lo, qp, 0.0), jnp.where(lo, 0.0, qp)], axis=0).astype(BF16) for qp in qps]
    scores = []
    for i in range(len(kslabs[0])):
        tiles = []
        for p in range(n_p):
            s = jnp.dot(q2[p], kslabs[p][i], preferred_element_type=F32)
            mk = masks[p][i]
            if mk is not None:
                s = jnp.where(mk, s, NEG_INF) if mk.dtype == jnp.bool_ else s + mk
            tiles.append(s)
        scores.append(jnp.concatenate(tiles, axis=0))
    sink = None if sink_cols[0] is None else jnp.concatenate(sink_cols, axis=0)
    mx = scores[0].max(axis=1, keepdims=True)
    for s in scores[1:]:
        mx = jnp.maximum(mx, s.max(axis=1, keepdims=True))
    if sink is not None:
        mx = jnp.maximum(mx, sink)
    probs = [jnp.exp(s - mx).astype(BF16) for s in scores]
    acc = []
    for p in range(n_p):
        rows = slice(p * 2 * m_rows, (p + 1) * 2 * m_rows)
        pv = None
        for e, va in zip(probs, vaugs[p]):
            t = jnp.dot(e[rows], va, preferred_element_type=F32)
            pv = t if pv is None else pv + t
        acc.append(pv)
    acc = jnp.concatenate(acc, axis=0)
    den = acc[:, LANES:]
    if sink is not None:
        den = den + jnp.exp(sink - mx)
    o = acc[:, :LANES] / den
    return [jnp.where(lo, o[2 * p * m_rows:(2 * p + 1) * m_rows], o[(2 * p + 1) * m_rows:(2 * p + 2) * m_rows])
            for p in range(n_p)]


def _gqa_key_slabs(kt):
    a, b = kt[:HD], kt[HD:]
    return [jnp.concatenate([a, a], 0), kt, jnp.concatenate([b, b], 0)]


def _gqa_value_pairs(v):
    lo = lax.broadcasted_iota(jnp.int32, v.shape, 1) < HD
    sw = pltpu.roll(v, HD, 1)
    ones = jnp.ones_like(v)
    return [jnp.concatenate([x, ones], 1) for x in (jnp.where(lo, v, sw), v, jnp.where(lo, sw, v))]


def _sink_col(sink_ref, p, m_rows):
    row = lax.broadcasted_iota(jnp.int32, (2 * m_rows, 1), 0)
    return jnp.where(row < m_rows, sink_ref[0:1, 2 * p:2 * p + 1], sink_ref[0:1, 2 * p + 1:2 * p + 2])


def _ctx_attn_kernel(bq_ref, bk_ref, bv_ref, cq_ref, ck_ref, cv_ref, sink_ref, yb_ref, yc_ref):
    scale = HD ** -0.5
    pairs_b = range(D_B // LANES)
    pairs_c = range(D_C // LANES)

    def tile(p):
        return slice(p * LANES, (p + 1) * LANES)

    kslabs = _gqa_key_slabs(bk_ref[...].T)
    vpairs = _gqa_value_pairs(bv_ref[...])
    outs = _pairs_attention([bq_ref[:, tile(p)] * scale for p in pairs_b], [[kslabs[p].astype(BF16)] for p in pairs_b],
                            [[vpairs[p].astype(BF16)] for p in pairs_b], [[None] for _ in pairs_b],
                            [_sink_col(sink_ref, p, L_CTX) for p in pairs_b])
    for p in pairs_b:
        yb_ref[:, tile(p)] = outs[p]
    ckt = ck_ref[...].T
    ones = jnp.ones((L_CTX, LANES), F32)
    outs = _pairs_attention([cq_ref[:, tile(p)] * scale for p in pairs_c], [[ckt[tile(p)].astype(BF16)] for p in pairs_c],
                            [[jnp.concatenate([cv_ref[:, tile(p)], ones], 1).astype(BF16)] for p in pairs_c],
                            [[None] for _ in pairs_c], [None for _ in pairs_c])
    for p in pairs_c:
        yc_ref[:, tile(p)] = outs[p]


def _ctx_attn_call(bq, bk, bv, cq, ck, cv, sink):
    def spec(n):
        return pl.BlockSpec((L_CTX, n), lambda b: (b, 0))

    return pl.pallas_call(
        _ctx_attn_kernel,
        grid=(B_CTX,),
        in_specs=[spec(D_B), spec(D_KVB), spec(D_KVB), spec(D_C), spec(D_C), spec(D_C),
                  pl.BlockSpec((1, 8), lambda b: (0, 0))],
        out_specs=[spec(D_B), spec(D_C)],
        out_shape=[jax.ShapeDtypeStruct((T_CTX, D_B), F32), jax.ShapeDtypeStruct((T_CTX, D_C), F32)],
        compiler_params=_cparams("arbitrary"),
        name="ctx_attn",
    )(bq, bk, bv, cq, ck, cv, sink)


def _rope(x, cos, sin_signed, first):
    rot = jnp.where(first, pltpu.roll(x, LANES - 16, 1), pltpu.roll(x, 16, 1))
    return x * cos + rot * sin_signed


def _win_attn_kernel(q_ref, k_ref, v_ref, kc_ref, vc_ref, cos_ref, sin_ref, sink_ref, y_ref,
                     kpt_scr, va_scr, kcp_scr, vca_scr):
    scale = HD ** -0.5
    n_pairs = D_B // LANES
    nblk = N_LAT // WIN
    nband = 3
    lane = lax.broadcasted_iota(jnp.int32, (1, LANES), 1)
    first = (lane % 32) < 16
    for p, (ks, va) in enumerate(zip(_gqa_key_slabs(kc_ref[0, 0].T), _gqa_value_pairs(vc_ref[0, 0]))):
        kcp_scr[p] = ks.astype(BF16)
        vca_scr[p] = va.astype(BF16)

    def prep(blk, c):
        r0 = pl.multiple_of(blk * WIN, WIN)
        kr = _rope(k_ref[pl.ds(r0, WIN), :], cos_ref[pl.ds(r0, WIN), :], sin_ref[pl.ds(r0, WIN), :], first)
        for p, (ks, va) in enumerate(zip(_gqa_key_slabs(kr.T), _gqa_value_pairs(v_ref[pl.ds(r0, WIN), :]))):
            kpt_scr[p, blk] = ks.astype(BF16)
            va_scr[p, pl.ds(r0, WIN), :] = va.astype(BF16)
        return c

    lax.fori_loop(0, nblk, prep, 0)

    def body(blk, c):
        q0 = pl.multiple_of(blk * WIN, WIN)
        sb = jnp.clip(blk - 1, 0, nblk - nband)
        k0 = pl.multiple_of(sb * WIN, WIN)
        cos = cos_ref[pl.ds(q0, WIN), :]
        sin = sin_ref[pl.ds(q0, WIN), :]
        row = lax.broadcasted_iota(jnp.int32, (2 * WIN, nband * WIN), 0)
        qpos = q0 + jnp.where(row < WIN, row, row - WIN)
        kpos = k0 + lax.broadcasted_iota(jnp.int32, (2 * WIN, nband * WIN), 1)
        mask = jnp.abs(kpos - qpos) <= WIN
        pairs = range(n_pairs)
        qps = [_rope(q_ref[pl.ds(q0, WIN), p * LANES:(p + 1) * LANES], cos, sin, first) * scale for p in pairs]
        k_loc = [jnp.concatenate([kpt_scr[p, sb + j] for j in range(nband)], axis=1) for p in pairs]
        v_loc = [va_scr[p, pl.ds(k0, nband * WIN), :] for p in pairs]
        outs = _pairs_attention(qps, [[kcp_scr[p], k_loc[p]] for p in pairs], [[vca_scr[p], v_loc[p]] for p in pairs],
                                [[None, mask] for _ in pairs], [_sink_col(sink_ref, p, WIN) for p in pairs])
        for p in pairs:
            y_ref[pl.ds(q0, WIN), p * LANES:(p + 1) * LANES] = outs[p]
        return c

    lax.fori_loop(0, nblk, body, 0)


def _win_attn_call(bq, bk, bv, cache_k, cache_v, cos, sin, sink, layer):
    rb0 = T_CTX // N_LAT

    def spec(n):
        return pl.BlockSpec((N_LAT, n), lambda b: (rb0 + b, 0))

    cache = pl.BlockSpec((1, 1, PAST_LEN, D_KVB), lambda b: (b, layer, 0, 0))
    tab = pl.BlockSpec((N_LAT, LANES), lambda b: (0, 0))
    return pl.pallas_call(
        _win_attn_kernel,
        grid=(B_LAT,),
        in_specs=[spec(D_B), spec(D_KVB), spec(D_KVB), cache, cache, tab, tab,
                  pl.BlockSpec((1, 8), lambda b: (0, 0))],
        out_specs=pl.BlockSpec((N_LAT, D_B), lambda b: (b, 0)),
        out_shape=jax.ShapeDtypeStruct((T_LAT, D_B), F32),
        scratch_shapes=[pltpu.VMEM((D_B // LANES, N_LAT // WIN, LANES, WIN), BF16),
                        pltpu.VMEM((D_B // LANES, N_LAT, 2 * LANES), BF16),
                        pltpu.VMEM((D_B // LANES, LANES, PAST_LEN), BF16),
                        pltpu.VMEM((D_B // LANES, PAST_LEN, 2 * LANES), BF16)],
        compiler_params=_cparams("arbitrary"),
        name="win_attn",
    )(bq, bk, bv, cache_k, cache_v, cos, sin, sink)


NA_BLK = LANES
NA_SPAN_BLKS = NA_WIN_H * GRID_W // NA_BLK + 1
NA_SPAN_ROWS = NA_SPAN_BLKS * NA_BLK // GRID_W
NA_VARIANTS = ((7, 0), (6, 0), (5, 0), (4, 0), (3, 0), (3, 1), (3, 2), (2, 2), (1, 2), (0, 2))


def _na_attn_kernel(q_ref, k_ref, v_ref, kc_ref, vc_ref, bias_ref, y_ref, kpt_scr, va_scr, kcp_scr, vca_scr):
    scale = HD ** -0.5
    rows = N_LAT // GRID_W
    n_pairs = D_C // LANES
    nblk = N_LAT // NA_BLK
    half = NA_WIN_H // 2
    kct = kc_ref[0, 0].T
    for p in range(n_pairs):
        cols = slice(p * LANES, (p + 1) * LANES)
        kcp_scr[p] = kct[cols].astype(BF16)
        vca_scr[p] = jnp.concatenate([vc_ref[0, 0, :, cols], jnp.ones((PAST_LEN, LANES), F32)], 1).astype(BF16)

    def prep(blk, c):
        r0 = pl.multiple_of(blk * NA_BLK, NA_BLK)
        kt = k_ref[pl.ds(r0, NA_BLK), :].T
        for p in range(n_pairs):
            cols = slice(p * LANES, (p + 1) * LANES)
            kpt_scr[p, blk] = kt[cols].astype(BF16)
            va_scr[p, pl.ds(r0, NA_BLK), :] = jnp.concatenate(
                [v_ref[pl.ds(r0, NA_BLK), cols], jnp.ones((NA_BLK, LANES), F32)], 1).astype(BF16)
        return c

    lax.fori_loop(0, nblk, prep, 0)

    def body(r, c):
        kr0 = jnp.clip(r - half, 0, rows - NA_WIN_H)
        sb = jnp.minimum(kr0 // 2, nblk - NA_SPAN_BLKS)
        var = jnp.where(r <= half, r, jnp.where(r >= rows - half, r - (rows - 2 * half - 2), half + (kr0 & 1)))
        q0 = pl.multiple_of(r * GRID_W, GRID_W)
        k0 = pl.multiple_of(sb * NA_BLK, NA_BLK)
        pairs = range(n_pairs)
        qps = [q_ref[pl.ds(q0, GRID_W), p * LANES:(p + 1) * LANES] * scale for p in pairs]
        k_loc = [jnp.concatenate([kpt_scr[p, sb + j] for j in range(NA_SPAN_BLKS)], axis=1) for p in pairs]
        v_loc = [va_scr[p, pl.ds(k0, NA_SPAN_BLKS * NA_BLK), :] for p in pairs]
        outs = _pairs_attention(qps, [[kcp_scr[p], k_loc[p]] for p in pairs], [[vca_scr[p], v_loc[p]] for p in pairs],
                                [[None, bias_ref[0, p, var]] for p in pairs], [None for _ in pairs])
        for p in pairs:
            y_ref[pl.ds(q0, GRID_W), p * LANES:(p + 1) * LANES] = outs[p]
        return c

    lax.fori_loop(0, rows, body, 0)


def _na_attn_call(cq, ck, cv, cache_k, cache_v, bias_tab, layer):
    rb0 = T_CTX // N_LAT
    n_pairs = D_C // LANES
    spec = pl.BlockSpec((N_LAT, D_C), lambda b: (rb0 + b, 0))
    cache = pl.BlockSpec((1, 1, PAST_LEN, D_C), lambda b: (b, layer, 0, 0))
    span = NA_SPAN_BLKS * NA_BLK
    return pl.pallas_call(
        _na_attn_kernel,
        grid=(B_LAT,),
        in_specs=[spec, spec, spec, cache, cache,
                  pl.BlockSpec((1, n_pairs, len(NA_VARIANTS), 2 * GRID_W, span), lambda b: (layer, 0, 0, 0, 0))],
        out_specs=pl.BlockSpec((N_LAT, D_C), lambda b: (b, 0)),
        out_shape=jax.ShapeDtypeStruct((T_LAT, D_C), F32),
        scratch_shapes=[pltpu.VMEM((n_pairs, N_LAT // NA_BLK, LANES, NA_BLK), BF16),
                        pltpu.VMEM((n_pairs, N_LAT, 2 * LANES), BF16),
                        pltpu.VMEM((n_pairs, LANES, PAST_LEN), BF16),
                        pltpu.VMEM((n_pairs, PAST_LEN, 2 * LANES), BF16)],
        compiler_params=_cparams("arbitrary"),
        name="na_attn",
    )(cq, ck, cv, cache_k, cache_v, bias_tab)


def _outproj_kernel(yac_ref, ybc_ref, ycc_ref, yal_ref, ybl_ref, ycl_ref, x_ref, ga_ref, scf_ref, shf_ref,
                    lng_ref, lnb_ref, wo_ref, wr_ref, x1_ref, h2_ref, route_ref, cnt_ref, run_scr):
    @pl.when(pl.program_id(0) == 0)
    def _():
        run_scr[...] = jnp.zeros_like(run_scr)

    is_ctx = pl.program_id(0) < T_CTX // TM

    def pick(c_ref, l_ref):
        return jnp.where(is_ctx, c_ref[...], l_ref[...])

    y = (_dot(pick(yac_ref, yal_ref), wo_ref[0, 0:Y_A_COLS, :])
         + _dot(pick(ybc_ref, ybl_ref), wo_ref[0, Y_A_COLS:Y_A_COLS + D_B, :])
         + _dot(pick(ycc_ref, ycl_ref), wo_ref[0, Y_A_COLS + D_B:, :]))
    x1 = _layer_norm(ALPHA * x_ref[...] + ga_ref[0, 0, 0] * y, lng_ref[...], lnb_ref[...])
    x1_ref[...] = x1
    h2 = x1 * (1.0 + scf_ref[0, 0, 0]) + shf_ref[0, 0, 0]
    h2_ref[...] = h2
    w_r = wr_ref[...]
    w_hi = w_r.astype(BF16)
    w_lo = (w_r - w_hi.astype(F32)).astype(BF16)
    h_hi = h2.astype(BF16)
    h_lo = (h2 - h_hi.astype(F32)).astype(BF16)
    p_hi = jnp.dot(h_hi, jnp.concatenate([w_hi, w_lo], axis=1), preferred_element_type=F32)
    logits = p_hi[:, :LANES] + p_hi[:, LANES:] + jnp.dot(h_lo, w_hi, preferred_element_type=F32)
    lane = lax.broadcasted_iota(jnp.int32, logits.shape, 1)
    lanef = lane.astype(F32)
    big = float(LANES)
    lg = jnp.where((lane >= N_EXPERTS) & (lane < N_EXPERTS + N_GROUPS), logits, NEG_INF)
    mg = jnp.max(lg, axis=1, keepdims=True)
    grp = jnp.min(jnp.where(lg == mg, lanef, big), axis=1, keepdims=True) - float(N_EXPERTS)
    g_w = 1.0 / jnp.sum(jnp.exp(lg - mg), axis=1, keepdims=True)
    in_grp = (lane < N_EXPERTS) & ((lane // E_PER_GROUP).astype(F32) == grp)
    le = jnp.where(in_grp, logits, NEG_INF)
    l1 = jnp.max(le, axis=1, keepdims=True)
    i1 = jnp.min(jnp.where(le == l1, lanef, big), axis=1, keepdims=True)
    le2 = jnp.where(lanef == i1, NEG_INF, le)
    l2 = jnp.max(le2, axis=1, keepdims=True)
    i2 = jnp.min(jnp.where(le2 == l2, lanef, big), axis=1, keepdims=True)
    e2 = jnp.exp(l2 - l1)
    w1 = g_w / (1.0 + e2)
    w2 = g_w * e2 / (1.0 + e2)
    oh1 = jnp.where(lanef == i1, 1.0, 0.0)
    oh2 = jnp.where(lanef == i2, 1.0, 0.0)
    rt = lax.broadcasted_iota(jnp.int32, (TM, TM), 0)
    ct = lax.broadcasted_iota(jnp.int32, (TM, TM), 1)
    before = jnp.where(ct < rt, 1.0, 0.0)
    run = run_scr[...]
    tot1 = jnp.sum(oh1, axis=0, keepdims=True)
    r1 = jnp.sum(oh1 * (run + _dot(before, oh1)), axis=1, keepdims=True)
    r2 = jnp.sum(oh2 * (run + tot1 + _dot(before, oh2)), axis=1, keepdims=True)
    run = run + tot1 + jnp.sum(oh2, axis=0, keepdims=True)
    run_scr[...] = run
    cnt_ref[...] = run
    vals = (i1, i2, w1, w2, r1, r2)
    out = jnp.zeros_like(logits)
    for n, v in enumerate(vals):
        out = jnp.where(lane == n, v, out)
    route_ref[...] = out[:, :8]


def _outproj_call(y_ctx, y_lat, x, mod_t, layer, lng, lnb, wo, wr):
    row_vec = pl.BlockSpec((1, D_MODEL), lambda i: (0, 0))
    n_ctx = T_CTX // TM

    def tok(n):
        return pl.BlockSpec((TM, n), lambda i: (i, 0))

    def tok_ctx(n):
        return pl.BlockSpec((TM, n), lambda i: (jnp.minimum(i, n_ctx - 1), 0))

    def tok_lat(n):
        return pl.BlockSpec((TM, n), lambda i: (jnp.maximum(i - n_ctx, 0), 0))

    return pl.pallas_call(
        _outproj_kernel,
        grid=(N_TILES,),
        in_specs=[tok_ctx(Y_A_COLS), tok_ctx(D_B), tok_ctx(D_C), tok_lat(Y_A_COLS), tok_lat(D_B), tok_lat(D_C),
                  tok(D_MODEL), _mod_spec(layer, 2), _mod_spec(layer, 4), _mod_spec(layer, 3), row_vec, row_vec,
                  pl.BlockSpec((1, Y_A_COLS + D_B + D_C, D_MODEL), lambda i: (layer, 0, 0)),
                  pl.BlockSpec((D_MODEL, LANES), lambda i: (0, 0))],
        out_specs=[tok(D_MODEL), tok(D_MODEL), tok(8), pl.BlockSpec((1, LANES), lambda i: (0, 0))],
        out_shape=[jax.ShapeDtypeStruct((T_ALL, D_MODEL), F32), jax.ShapeDtypeStruct((T_ALL, D_MODEL), F32),
                   jax.ShapeDtypeStruct((T_ALL, 8), F32), jax.ShapeDtypeStruct((1, LANES), F32)],
        scratch_shapes=[pltpu.VMEM((1, LANES), F32)],
        compiler_params=_cparams("arbitrary"),
        name="outproj_router",
    )(*y_ctx, *y_lat, x, mod_t, mod_t, mod_t, lng, lnb, wo, wr)


def _dispatch_kernel(pos_ref, pend_ref, h_ref, xs_hbm, zero_scr, sem):
    base = pl.program_id(0) * TM

    @pl.when(pl.program_id(0) == 0)
    def _():
        zero_scr[...] = jnp.zeros_like(zero_scr)

        def fill(e, op):
            prev = pend_ref[e - 1] if e else 0

            @pl.when(pend_ref[e] > prev)
            def _():
                first = pl.multiple_of(pend_ref[e] - MOE_BLK, MOE_BLK)
                op(pltpu.make_async_copy(zero_scr, xs_hbm.at[pl.ds(first, MOE_BLK), :], sem))

        def tail_copy(b):
            first = pl.multiple_of(b * MOE_BLK, MOE_BLK)
            return pltpu.make_async_copy(zero_scr, xs_hbm.at[pl.ds(first, MOE_BLK), :], sem)

        def tail_start(b, c):
            tail_copy(b).start()
            return c

        def tail_wait(b, c):
            tail_copy(b).wait()
            return c

        n_blocks = xs_hbm.shape[0] // MOE_BLK
        first_free = pend_ref[N_EXPERTS - 1] // MOE_BLK
        for e in range(N_EXPERTS):
            fill(e, lambda cp: cp.start())
        lax.fori_loop(first_free, n_blocks, tail_start, 0)
        for e in range(N_EXPERTS):
            fill(e, lambda cp: cp.wait())
        lax.fori_loop(first_free, n_blocks, tail_wait, 0)

    def row(t, p):
        return pltpu.make_async_copy(h_ref.at[pl.ds(t, 1), :], xs_hbm.at[pl.ds(p, 1), :], sem)

    def issue(t, c):
        row(t, pos_ref[base + t]).start(priority=0)
        row(t, pos_ref[T_ALL + base + t]).start(priority=1)
        return c

    lax.fori_loop(0, TM, issue, 0, unroll=8)
    whole = pltpu.make_async_copy(h_ref, xs_hbm.at[pl.ds(0, TM), :], sem)
    whole.wait()
    whole.wait()


def _dispatch_call(pos2, p_end, h2, nblk):
    grid_spec = pltpu.PrefetchScalarGridSpec(
        num_scalar_prefetch=2,
        grid=(N_TILES,),
        in_specs=[pl.BlockSpec((TM, D_MODEL), lambda i, p, pe: (i, 0))],
        out_specs=pl.BlockSpec(memory_space=pl.ANY),
        scratch_shapes=[pltpu.VMEM((MOE_BLK, D_MODEL), F32), pltpu.SemaphoreType.DMA(())],
    )
    return pl.pallas_call(
        _dispatch_kernel,
        grid_spec=grid_spec,
        out_shape=jax.ShapeDtypeStruct((nblk * MOE_BLK, D_MODEL), F32),
        compiler_params=_cparams("arbitrary"),
        name="dispatch",
    )(pos2, p_end, h2)


def _expert_kernel(blk_e_ref, nused_ref, xs_ref, wg_ref, wu_ref, wd_ref, out_ref):
    j = pl.program_id(0)

    @pl.when(j < nused_ref[0])
    def _():
        xb = xs_ref[...].astype(BF16)
        g = jnp.dot(xb, wg_ref[0, 0].astype(BF16), preferred_element_type=F32)
        u = jnp.dot(xb, wu_ref[0, 0].astype(BF16), preferred_element_type=F32)
        out_ref[...] = _dot(jax.nn.silu(g) * u, wd_ref[0, 0])

    @pl.when(j >= nused_ref[0])
    def _():
        out_ref[...] = jnp.zeros_like(out_ref)


def _expert_call(blk_e, nused, xs, wg, wu, wd, layer, nblk):
    def wspec(r, c):
        return pl.BlockSpec((1, 1, r, c), lambda j, be, nu: (layer, be[j], 0, 0))

    grid_spec = pltpu.PrefetchScalarGridSpec(
        num_scalar_prefetch=2,
        grid=(nblk,),
        in_specs=[
            pl.BlockSpec((MOE_BLK, D_MODEL), lambda j, be, nu: (jnp.clip(j, 0, jnp.maximum(nu[0] - 1, 0)), 0)),
            wspec(D_MODEL, D_EXPERT), wspec(D_MODEL, D_EXPERT), wspec(D_EXPERT, D_MODEL),
        ],
        out_specs=pl.BlockSpec((MOE_BLK, D_MODEL), lambda j, be, nu: (j, 0)),
    )
    return pl.pallas_call(
        _expert_kernel,
        grid_spec=grid_spec,
        out_shape=jax.ShapeDtypeStruct((nblk * MOE_BLK, D_MODEL), F32),
        compiler_params=_cparams("arbitrary"),
        name="experts",
    )(blk_e, nused, xs, wg, wu, wd)


def _combine_kernel(pos_ref, eo_hbm, x1_ref, route_ref, gf_ref, lng_ref, lnb_ref, x2_ref, buf, sem):
    base = pl.program_id(0) * TM

    def row(p, r, t):
        return pltpu.make_async_copy(eo_hbm.at[pl.ds(p, 1), :], buf.at[r, pl.ds(t, 1), :], sem)

    def issue(t, c):
        row(pos_ref[base + t], 0, t).start(priority=0)
        row(pos_ref[T_ALL + base + t], 1, t).start(priority=1)
        return c

    lax.fori_loop(0, TM, issue, 0, unroll=8)
    for r in range(2):
        pltpu.make_async_copy(eo_hbm.at[pl.ds(0, TM), :], buf.at[r], sem).wait()
    route = route_ref[...]
    y = route[:, 2:3] * buf[0] + route[:, 3:4] * buf[1]
    x2_ref[...] = _layer_norm(ALPHA * x1_ref[...] + gf_ref[0, 0, 0] * y, lng_ref[...], lnb_ref[...])


def _combine_call(pos2, eo, x1, route, mod_t, layer, lng, lnb):
    grid_spec = pltpu.PrefetchScalarGridSpec(
        num_scalar_prefetch=1,
        grid=(N_TILES,),
        in_specs=[
            pl.BlockSpec(memory_space=pl.ANY),
            pl.BlockSpec((TM, D_MODEL), lambda i, p: (i, 0)),
            pl.BlockSpec((TM, 8), lambda i, p: (i, 0)),
            _mod_spec(layer, 5),
            pl.BlockSpec((1, D_MODEL), lambda i, p: (0, 0)),
            pl.BlockSpec((1, D_MODEL), lambda i, p: (0, 0)),
        ],
        out_specs=pl.BlockSpec((TM, D_MODEL), lambda i, p: (i, 0)),
        scratch_shapes=[pltpu.VMEM((2, TM, D_MODEL), F32), pltpu.SemaphoreType.DMA(())],
    )
    return pl.pallas_call(
        _combine_kernel,
        grid_spec=grid_spec,
        out_shape=jax.ShapeDtypeStruct((T_ALL, D_MODEL), F32),
        compiler_params=_cparams("arbitrary"),
        name="combine_norm",
    )(pos2, eo, x1, route, mod_t, lng, lnb)


def _gate_cols(grp):
    return np.array([kind * 2 * H_A + d * H_A + grp * HP_A + j
                     for j in range(HP_A) for kind in range(2) for d in range(2)])


def _prep_w_in(w_in):
    a = w_in[..., :4 * D_A].reshape(DEPTH, D_MODEL, 4, H_A, DH_A)
    a = jnp.pad(a, ((0, 0), (0, 0), (0, 0), (0, 0), (0, DP_A - DH_A))).reshape(DEPTH, D_MODEL, ZA_COLS)
    gates = w_in[..., 4 * D_A:4 * D_A + 4 * H_A]
    g = jnp.concatenate([jnp.pad(gates[..., _gate_cols(grp)], ((0, 0), (0, 0), (0, LANES - 4 * HP_A)))
                         for grp in range(H_A // HP_A)], -1)
    rest = w_in[..., 4 * D_A + 4 * H_A:]
    return jnp.concatenate([a, rest, g], -1).astype(BF16)


def _prep_w_out(w_out):
    a = w_out[:, :D_A].reshape(DEPTH, H_A, DH_A, D_MODEL)
    a = jnp.pad(a, ((0, 0), (0, 0), (0, DP_A - DH_A), (0, 0))).reshape(DEPTH, Y_A_COLS, D_MODEL)
    return jnp.concatenate([a, w_out[:, D_A:]], 1).astype(BF16)


def _rope_tables():
    t = np.arange(N_LAT)
    nf = HD // 4
    inv = ROPE_BASE ** (-np.arange(nf, dtype=np.float32) / nf)
    ar = (t // GRID_W).astype(np.float32)[:, None] * inv
    ac = (t % GRID_W).astype(np.float32)[:, None] * inv
    ang = jnp.asarray(np.concatenate([ar, ar, ac, ac], -1), F32)
    cos, sin = jnp.cos(ang), jnp.sin(ang)
    sign = np.where((np.arange(HD) % 32) < 16, -1.0, 1.0).astype(np.float32)
    reps = LANES // HD
    return jnp.tile(cos, (1, reps)), jnp.tile(sin * sign, (1, reps))


def _na_bias_tables(rpb):
    qcol = np.arange(GRID_W)[:, None]
    kcol = np.arange(GRID_W)[None, :]
    dc = np.clip(kcol - qcol, 1 - NA_WIN_W, NA_WIN_W - 1) + NA_WIN_W - 1
    wstart = np.clip(qcol - NA_WIN_W // 2, 0, GRID_W - NA_WIN_W)
    in_win = (kcol >= wstart) & (kcol < wstart + NA_WIN_W)
    sel = (np.arange(2 * NA_WIN_W - 1)[:, None] == dc.reshape(1, -1)).astype(np.float32)
    cols = jnp.einsum("lhrd,dn->lhrn", rpb, jnp.asarray(sel), precision=lax.Precision.HIGHEST)
    cols = jnp.where(in_win.reshape(-1), cols, NEG_INF).reshape(DEPTH, H_C, 2 * NA_WIN_H - 1, GRID_W, GRID_W)
    outside = jnp.full((DEPTH, H_C, GRID_W, GRID_W), NEG_INF, F32)
    variants = []
    for dr0, off in NA_VARIANTS:
        span_rows = [cols[:, :, dr0 + i - off] if 0 <= i - off < NA_WIN_H else outside for i in range(NA_SPAN_ROWS)]
        variants.append(jnp.stack(span_rows, 3).reshape(DEPTH, H_C, GRID_W, NA_SPAN_ROWS * GRID_W))
    tab = jnp.stack(variants, 2).reshape(DEPTH, H_C // 2, 2, len(NA_VARIANTS), GRID_W, NA_SPAN_ROWS * GRID_W)
    return tab.transpose(0, 1, 3, 2, 4, 5).reshape(DEPTH, H_C // 2, len(NA_VARIANTS), 2 * GRID_W, NA_SPAN_ROWS * GRID_W)


def _dispatch_plan(route, counts):
    nblk = 2 * T_ALL // MOE_BLK + N_EXPERTS
    cnt = counts[0, :N_EXPERTS].astype(jnp.int32)
    padded = (cnt + MOE_BLK - 1) // MOE_BLK * MOE_BLK
    p_end = jnp.cumsum(padded)
    p_start = p_end - padded
    e = route[:, 0:2].astype(jnp.int32)
    hot = e[..., None] == jnp.arange(N_EXPERTS, dtype=jnp.int32)
    pos = jnp.sum(jnp.where(hot, p_start, 0), -1) + route[:, 4:6].astype(jnp.int32)
    pos2 = pos.T.reshape(-1)
    blk_first = jnp.arange(nblk, dtype=jnp.int32) * MOE_BLK
    blk_e = jnp.minimum(jnp.sum((p_end[None, :] <= blk_first[:, None]).astype(jnp.int32), axis=1), N_EXPERTS - 1)
    nused = p_end[-1:] // MOE_BLK
    return blk_e, nused, pos2, p_end, nblk


def kernel(x_prompt, x_sample, state_a_C, state_a_n, state_a_m, cache_b_k, cache_b_v, cache_c_k, cache_c_v, c, c_ctx, w_in, b_a_i, b_a_f, w_a_hnorm, b_sink, rpb, w_out, w_ada, b_ada, ln_g, ln_b, w_router_grp, w_router_exp, w_e_gate, w_e_up, w_e_down):
    w_in_p = _prep_w_in(w_in)
    w_out_p = _prep_w_out(w_out)
    w_r = jnp.pad(jnp.concatenate([w_router_exp, w_router_grp], -1),
                  ((0, 0), (0, 0), (0, LANES - N_EXPERTS - N_GROUPS)))
    wn_p = jnp.pad(w_a_hnorm.reshape(DEPTH, H_A, 1, DH_A), ((0, 0), (0, 0), (0, 0), (0, DP_A - DH_A)))
    gate_b = jnp.concatenate([b_a_i, b_a_f], 1).transpose(0, 2, 1)
    gate_b = jnp.pad(gate_b.reshape(DEPTH, H_A // HP_A, 1, 4 * HP_A), ((0, 0), (0, 0), (0, 0), (0, LANES - 4 * HP_A)))
    sink_p = jnp.pad(b_sink, ((0, 0), (0, 8 - H_B))).reshape(DEPTH, 1, 8)
    cos_t, sin_t = _rope_tables()
    na_bias = _na_bias_tables(rpb)
    cb_k = cache_b_k.reshape(B_LAT, DEPTH, PAST_LEN, D_KVB)
    cb_v = cache_b_v.reshape(B_LAT, DEPTH, PAST_LEN, D_KVB)
    cc_k = cache_c_k.reshape(B_LAT, DEPTH, PAST_LEN, D_C)
    cc_v = cache_c_v.reshape(B_LAT, DEPTH, PAST_LEN, D_C)
    pad_c = ((0, 0), (0, 0), (0, 0), (0, 0), (0, DP_A - DH_A), (0, DP_A - DH_A))
    st_ct = jnp.swapaxes(jnp.pad(state_a_C, pad_c), -1, -2)
    st_nr = jnp.broadcast_to(jnp.pad(state_a_n, pad_c[:-1])[..., None], st_ct.shape)
    st_s = jnp.concatenate([st_ct, st_nr], -1)
    st_m = state_a_m[..., None, None]
    z_s = jnp.zeros((B_CTX, 2, H_A, DP_A, 2 * DP_A), F32)
    z_m = jnp.zeros((B_CTX, 2, H_A, 1, 1), F32)

    cvec = jnp.concatenate([c, c_ctx[None, :], jnp.zeros((3, D_MODEL), F32)], 0)
    mod = _ada_call(cvec, w_ada, b_ada)
    tile_row = np.concatenate([np.full(T_CTX // TM, B_LAT), np.repeat(np.arange(B_LAT), N_LAT // TM)])
    mod_t = mod[:, tile_row].reshape(DEPTH, N_TILES, 6, 1, D_MODEL)

    x = jnp.concatenate([x_prompt.reshape(T_CTX, D_MODEL), x_sample.reshape(T_LAT, D_MODEL)], 0)
    cs_, ns_, ms_, kbs, vbs, kcs, vcs = [], [], [], [], [], [], []
    for l in range(DEPTH):
        za, bq, bk, bv, cq, ck, cv, zg = _inproj_call(x, mod_t, l, w_in_p)
        ya_c, s_l, m_l = _mlstm_call(za, zg, gate_b[l], wn_p[l], z_s, z_m, nb=B_CTX, seq=L_CTX, row_blk0=0)
        yb_c, yc_c = _ctx_attn_call(bq, bk, bv, cq, ck, cv, sink_p[l])
        ya_l, _, _ = _mlstm_call(za, zg, gate_b[l], wn_p[l], st_s[:, l], st_m[:, l],
                                 nb=B_LAT, seq=N_LAT, row_blk0=T_CTX // N_LAT)
        yb_l = _win_attn_call(bq, bk, bv, cb_k, cb_v, cos_t, sin_t, sink_p[l], l)
        yc_l = _na_attn_call(cq, ck, cv, cc_k, cc_v, na_bias, l)
        x1, h2, route, counts = _outproj_call((ya_c, yb_c, yc_c), (ya_l, yb_l, yc_l), x, mod_t, l,
                                              ln_g[l, 0:1], ln_b[l, 0:1], w_out_p, w_r[l])
        blk_e, nused, pos2, p_end, nblk = _dispatch_plan(route, counts)
        xs = _dispatch_call(pos2, p_end, h2, nblk)
        eo = _expert_call(blk_e, nused, xs, w_e_gate, w_e_up, w_e_down, l, nblk)
        x = _combine_call(pos2, eo, x1, route, mod_t, l, ln_g[l, 1:2], ln_b[l, 1:2])
        cs_.append(jnp.swapaxes(s_l[..., :DH_A, :DH_A], -1, -2))
        ns_.append(s_l[..., :DH_A, DP_A])
        ms_.append(m_l.reshape(B_CTX, 2, H_A))
        kbs.append(bk[:T_CTX].reshape(B_CTX, L_CTX, KV_B, HD))
        vbs.append(bv[:T_CTX].reshape(B_CTX, L_CTX, KV_B, HD))
        kcs.append(ck[:T_CTX].reshape(B_CTX, L_CTX, H_C, HD))
        vcs.append(cv[:T_CTX].reshape(B_CTX, L_CTX, H_C, HD))
    y_prompt = x[:T_CTX].reshape(B_CTX, L_CTX, D_MODEL)
    y_sample = x[T_CTX:].reshape(B_LAT, N_LAT, D_MODEL)
    return (y_prompt, y_sample, jnp.stack(cs_, 1), jnp.stack(ns_, 1), jnp.stack(ms_, 1),
            jnp.stack(kbs, 1), jnp.stack(vbs, 1), jnp.stack(kcs, 1), jnp.stack(vcs, 1))
```

```python
import functools

import numpy as np
import jax
import jax.numpy as jnp
from jax import lax
from jax.experimental import pallas as pl
from jax.experimental.pallas import tpu as pltpu

F32 = jnp.float32
BF16 = jnp.bfloat16
NEG_INF = float("-inf")

D_MODEL = 1024
DEPTH = 4
B_CTX, L_CTX = 16, 256
B_LAT, N_LAT = 4, 2048
PAST_LEN = 512
GRID_W = 64
H_A, DH_A = 4, 96
D_A = H_A * DH_A
H_B, KV_B, HD = 6, 2, 64
G_B = H_B // KV_B
D_B, D_KVB = H_B * HD, KV_B * HD
WIN = 128
ROPE_BASE = 10000.0
H_C = 4
D_C = H_C * HD
NA_WIN_H, NA_WIN_W = 8, 16
N_GROUPS, E_PER_GROUP = 4, 8
N_EXPERTS = N_GROUPS * E_PER_GROUP
D_EXPERT = D_MODEL // 4
ALPHA = (2 * DEPTH) ** 0.25
LN_EPS = 1e-5

LANES = 128
DP_A = LANES
LC_K = LANES
TM = 256
MOE_BLK = 256
VMEM_LIMIT = 48 * 1024 * 1024

T_CTX = B_CTX * L_CTX
T_LAT = B_LAT * N_LAT
T_ALL = T_CTX + T_LAT
N_TILES = T_ALL // TM
ZA_COLS = 4 * H_A * DP_A
Y_A_COLS = H_A * DP_A
IN_SPLITS = (ZA_COLS, D_B, D_KVB, D_KVB, D_C, D_C, D_C, 2 * LANES)
IN_COLS = sum(IN_SPLITS)


def _cparams(*sem):
    return pltpu.CompilerParams(dimension_semantics=sem, vmem_limit_bytes=VMEM_LIMIT)


def _dot(a, b):
    return jnp.dot(a.astype(BF16), b.astype(BF16), preferred_element_type=F32)


def _dot_nt(a, b):
    return lax.dot_general(a.astype(BF16), b.astype(BF16), (((1,), (1,)), ((), ())), preferred_element_type=F32)


def _dot_tn(a, b):
    return lax.dot_general(a.astype(BF16), b.astype(BF16), (((0,), (0,)), ((), ())), preferred_element_type=F32)


def _layer_norm(v, g, b):
    mu = jnp.mean(v, -1, keepdims=True)
    var = jnp.mean(jnp.square(v - mu), -1, keepdims=True)
    return (v - mu) * lax.rsqrt(var + LN_EPS) * g + b


def _ada_kernel(c_ref, w_ref, b_ref, o_ref):
    s = jax.nn.silu(c_ref[...])
    o_ref[0] = _dot(s, w_ref[0]) + b_ref[0]


def _ada_call(cvec, w_ada, b_ada):
    nb = 6
    return pl.pallas_call(
        _ada_kernel,
        grid=(DEPTH, nb),
        in_specs=[
            pl.BlockSpec((8, D_MODEL), lambda l, j: (0, 0)),
            pl.BlockSpec((1, D_MODEL, D_MODEL), lambda l, j: (l, 0, j)),
            pl.BlockSpec((1, 1, D_MODEL), lambda l, j: (l, 0, j)),
        ],
        out_specs=pl.BlockSpec((1, 8, D_MODEL), lambda l, j: (l, 0, j)),
        out_shape=jax.ShapeDtypeStruct((DEPTH, 8, 6 * D_MODEL), F32),
        compiler_params=_cparams("arbitrary", "arbitrary"),
        name="adaln",
    )(cvec, w_ada, b_ada.reshape(DEPTH, 1, 6 * D_MODEL))


def _inproj_kernel(x_ref, sc_ref, sh_ref, w_ref, *out_refs):
    kt_ref = out_refs[-1]
    h = (x_ref[...] * (1.0 + sc_ref[0, 0, 0]) + sh_ref[0, 0, 0]).astype(BF16)
    off = 0
    for ref in out_refs[:-1]:
        n = ref.shape[-1]
        z = jnp.dot(h, w_ref[0, :, off:off + n], preferred_element_type=F32)
        ref[...] = z
        if off == 0:
            for hd in range(H_A):
                k0 = (H_A + hd) * DP_A
                for c in range(TM // LC_K):
                    kt_ref[hd, c] = z[c * LC_K:(c + 1) * LC_K, k0:k0 + DP_A].T
        off += n


def _mod_spec(layer, which):
    return pl.BlockSpec((1, 1, 1, 1, D_MODEL), lambda i, *_: (layer, i, which, 0, 0))


def _inproj_call(x, mod_t, layer, w):
    return pl.pallas_call(
        _inproj_kernel,
        grid=(N_TILES,),
        in_specs=[
            pl.BlockSpec((TM, D_MODEL), lambda i: (i, 0)),
            _mod_spec(layer, 1), _mod_spec(layer, 0),
            pl.BlockSpec((1, D_MODEL, IN_COLS), lambda i: (layer, 0, 0)),
        ],
        out_specs=[pl.BlockSpec((TM, n), lambda i: (i, 0)) for n in IN_SPLITS]
        + [pl.BlockSpec((H_A, TM // LC_K, DP_A, LC_K), lambda i: (0, i, 0, 0))],
        out_shape=[jax.ShapeDtypeStruct((T_ALL, n), F32) for n in IN_SPLITS]
        + [jax.ShapeDtypeStruct((H_A, T_ALL // LC_K, DP_A, LC_K), F32)],
        compiler_params=_cparams("arbitrary"),
        name="inproj",
    )(x, mod_t, mod_t, w)


HP_A = 2
N_CH = 2 * HP_A


def _mlstm_kernel(q_ref, kt_ref, v_ref, o_ref, g_ref, gb_ref, wn_ref, s0_ref, m0_ref,
                  y_ref, so_ref, mo_ref, hf_scr, hb_scr, *, nc):
    lc = LC_K
    scale = DH_A ** -0.5
    chains = [(j, d) for j in range(HP_A) for d in range(2)]
    ti = lax.broadcasted_iota(jnp.int32, (lc, lc), 0)
    si = lax.broadcasted_iota(jnp.int32, (lc, lc), 1)
    lane_ok = lax.broadcasted_iota(jnp.int32, (1, DP_A), 1) < DH_A
    ones = jnp.ones((lc, DP_A), F32)

    def stack(parts):
        return jnp.concatenate(parts, axis=0)

    def rows_of(x, a):
        return a[x * lc:(x + 1) * lc]

    def spread(vals):
        return stack([jnp.broadcast_to(v, (lc, v.shape[1])) for v in vals])

    mask = stack([si <= ti if d == 0 else si >= ti for _, d in chains])
    mask_t = stack([si >= ti if d == 0 else si <= ti for _, d in chains])
    eye = stack([si == ti for _ in chains])

    def col_sums(a):
        return [jnp.sum(rows_of(x, a), axis=0, keepdims=True) for x in range(N_CH)]

    def body(i, carry):
        smats = carry[:N_CH]
        ms = carry[N_CH:]
        cidx = [i if d == 0 else nc - 1 - i for _, d in chains]
        r0s = [pl.multiple_of(c * lc, lc) for c in cidx]
        q, kt, v1, icol, fpre = [], [], [], [], []
        for (j, d), c, r0 in zip(chains, cidx, r0s):
            cols = slice(j * DP_A, (j + 1) * DP_A)
            q.append(q_ref[pl.ds(r0, lc), cols].astype(BF16))
            kt.append((kt_ref[j, c] * scale).astype(BF16))
            v1.append(jnp.concatenate([v_ref[pl.ds(r0, lc), cols], ones], axis=1))
            gz = g_ref[pl.ds(r0, lc), :] + gb_ref[0]
            icol.append(gz[:, 4 * j + d:4 * j + d + 1])
            fpre.append(gz[:, 4 * j + 2 + d:4 * j + 3 + d])
        i_col = stack(icol)
        f_col = jax.nn.log_sigmoid(stack(fpre))
        b_rows = col_sums(jnp.where(mask_t, f_col, 0.0))
        bls = col_sums(f_col)
        b_col = jnp.sum(jnp.where(eye, spread(b_rows), 0.0), axis=1, keepdims=True)
        a_rows = col_sums(jnp.where(eye, i_col - b_col, 0.0))
        a_sp = jnp.where(mask, spread(a_rows), NEG_INF)
        m_sp = spread(ms)
        gap = jnp.maximum(m_sp, jnp.max(a_sp, axis=1, keepdims=True))
        m_out = b_col + gap
        wmat = jnp.exp(a_sp - gap)
        sw = (stack([jnp.dot(q[x], kt[x], preferred_element_type=F32) for x in range(N_CH)]) * wmat).astype(BF16)
        sc_in = jnp.exp(m_sp - gap)
        tot = (stack([jnp.dot(rows_of(x, sw), v1[x].astype(BF16), preferred_element_type=F32) for x in range(N_CH)])
               + sc_in * stack([_dot(q[x], smats[x]) for x in range(N_CH)]))
        h = tot[:, :DP_A] / jnp.maximum(jnp.abs(tot[:, DP_A:]), jnp.exp(-m_out))
        for x, ((j, d), r0) in enumerate(zip(chains, r0s)):
            dst = hf_scr if d == 0 else hb_scr
            dst[pl.ds(r0, lc), j * DP_A:(j + 1) * DP_A] = rows_of(x, h)
        dec = spread(bls) - b_col + i_col
        m_new = [jnp.maximum(bls[x] + ms[x], jnp.max(rows_of(x, dec), axis=0, keepdims=True)) for x in range(N_CH)]
        wk = jnp.exp(dec - spread(m_new))
        s_new = [jnp.exp(bls[x] + ms[x] - m_new[x]) * smats[x] + _dot(kt[x], rows_of(x, wk) * v1[x])
                 for x in range(N_CH)]
        return tuple(s_new) + tuple(m_new)

    init = tuple(s0_ref[0, d, j] for j, d in chains) + tuple(m0_ref[0, d, j] for j, d in chains)
    final = lax.fori_loop(0, nc, body, init)
    for x, (j, d) in enumerate(chains):
        so_ref[0, d, j] = final[x]
        mo_ref[0, d, j] = final[N_CH + x]

    def finish(c, _):
        r0 = pl.multiple_of(c * lc, lc)
        for j in range(HP_A):
            cols = slice(j * DP_A, (j + 1) * DP_A)
            h = hf_scr[pl.ds(r0, lc), cols] + hb_scr[pl.ds(r0, lc), cols]
            mu = jnp.sum(h, axis=1, keepdims=True) * (1.0 / DH_A)
            dv = jnp.where(lane_ok, h - mu, 0.0)
            var = jnp.sum(dv * dv, axis=1, keepdims=True) * (1.0 / DH_A)
            hn = dv * lax.rsqrt(var + LN_EPS) * wn_ref[j]
            y_ref[pl.ds(r0, lc), cols] = jax.nn.sigmoid(o_ref[pl.ds(r0, lc), cols]) * hn
        return 0

    lax.fori_loop(0, nc, finish, 0)


def _mlstm_call(za, kt, zg, gbias, wn, s0, m0, *, nb, seq, row_blk0):
    nc = seq // LC_K
    ng = H_A // HP_A
    w = HP_A * DP_A

    def zspec(part):
        return pl.BlockSpec((seq, w), lambda b, g: (row_blk0 + b, part * ng + g))

    def state(*tail):
        return pl.BlockSpec((1, 2, HP_A) + tail, lambda b, g: (b, 0, g) + (0,) * len(tail))

    return pl.pallas_call(
        functools.partial(_mlstm_kernel, nc=nc),
        grid=(nb, ng),
        in_specs=[
            zspec(0),
            pl.BlockSpec((HP_A, nc, DP_A, LC_K), lambda b, g: (g, row_blk0 + b, 0, 0)),
            zspec(2), zspec(3),
            pl.BlockSpec((seq, LANES), lambda b, g: (row_blk0 + b, g)),
            pl.BlockSpec((1, 1, LANES), lambda b, g: (g, 0, 0)),
            pl.BlockSpec((HP_A, 1, DP_A), lambda b, g: (g, 0, 0)),
            state(DP_A, 2 * DP_A), state(1, 1),
        ],
        out_specs=[pl.BlockSpec((seq, w), lambda b, g: (b, g)), state(DP_A, 2 * DP_A), state(1, 1)],
        out_shape=[
            jax.ShapeDtypeStruct((nb * seq, Y_A_COLS), F32),
            jax.ShapeDtypeStruct((nb, 2, H_A, DP_A, 2 * DP_A), F32),
            jax.ShapeDtypeStruct((nb, 2, H_A, 1, 1), F32),
        ],
        scratch_shapes=[pltpu.VMEM((seq, w), F32), pltpu.VMEM((seq, w), F32)],
        compiler_params=_cparams("arbitrary", "arbitrary"),
        name="mlstm",
    )(za, kt, za, za, zg, gbias, wn, s0, m0)


def _pair_attention(qp, kslabs, vaugs, masks, sink_col):
    return _pairs_attention([qp], [kslabs], [vaugs], [masks], [sink_col])[0]


def _pairs_attention(qps, kslabs, vaugs, masks, sink_cols):
    n_p = len(qps)
    m_rows = qps[0].shape[0]
    lo = lax.broadcasted_iota(jnp.int32, qps[0].shape, 1) < HD
    q2 = [jnp.concatenate([jnp.where(lo, qp, 0.0), jnp.where(lo, 0.0, qp)], axis=0).astype(BF16) for qp in qps]
    scores = []
    for i in range(len(kslabs[0])):
        tiles = []
        for p in range(n_p):
            s = jnp.dot(q2[p], kslabs[p][i], preferred_element_type=F32)
            mk = masks[p][i]
            if mk is not None:
                s = jnp.where(mk, s, NEG_INF) if mk.dtype == jnp.bool_ else s + mk
            tiles.append(s)
        scores.append(jnp.concatenate(tiles, axis=0))
    sink = None if sink_cols[0] is None else jnp.concatenate(sink_cols, axis=0)
    mx = scores[0].max(axis=1, keepdims=True)
    for s in scores[1:]:
        mx = jnp.maximum(mx, s.max(axis=1, keepdims=True))
    if sink is not None:
        mx = jnp.maximum(mx, sink)
    probs = [jnp.exp(s - mx).astype(BF16) for s in scores]
    acc = []
    for p in range(n_p):
        rows = slice(p * 2 * m_rows, (p + 1) * 2 * m_rows)
        pv = None
        for e, va in zip(probs, vaugs[p]):
            t = jnp.dot(e[rows], va, preferred_element_type=F32)
            pv = t if pv is None else pv + t
        acc.append(pv)
    acc = jnp.concatenate(acc, axis=0)
    den = acc[:, LANES:]
    if sink is not None:
        den = den + jnp.exp(sink - mx)
    o = acc[:, :LANES] / den
    return [jnp.where(lo, o[2 * p * m_rows:(2 * p + 1) * m_rows], o[(2 * p + 1) * m_rows:(2 * p + 2) * m_rows])
            for p in range(n_p)]


def _gqa_key_slabs(kt):
    a, b = kt[:HD], kt[HD:]
    return [jnp.concatenate([a, a], 0), kt, jnp.concatenate([b, b], 0)]


def _gqa_value_pairs(v):
    lo = lax.broadcasted_iota(jnp.int32, v.shape, 1) < HD
    sw = pltpu.roll(v, HD, 1)
    ones = jnp.ones_like(v)
    return [jnp.concatenate([x, ones], 1) for x in (jnp.where(lo, v, sw), v, jnp.where(lo, sw, v))]


def _sink_col(sink_ref, p, m_rows):
    row = lax.broadcasted_iota(jnp.int32, (2 * m_rows, 1), 0)
    return jnp.where(row < m_rows, sink_ref[0:1, 2 * p:2 * p + 1], sink_ref[0:1, 2 * p + 1:2 * p + 2])


def _ctx_attn_kernel(bq_ref, bk_ref, bv_ref, cq_ref, ck_ref, cv_ref, sink_ref, yb_ref, yc_ref):
    scale = HD ** -0.5
    pairs_b = range(D_B // LANES)
    pairs_c = range(D_C // LANES)

    def tile(p):
        return slice(p * LANES, (p + 1) * LANES)

    kslabs = _gqa_key_slabs(bk_ref[...].T)
    vpairs = _gqa_value_pairs(bv_ref[...])
    outs = _pairs_attention([bq_ref[:, tile(p)] * scale for p in pairs_b], [[kslabs[p].astype(BF16)] for p in pairs_b],
                            [[vpairs[p].astype(BF16)] for p in pairs_b], [[None] for _ in pairs_b],
                            [_sink_col(sink_ref, p, L_CTX) for p in pairs_b])
    for p in pairs_b:
        yb_ref[:, tile(p)] = outs[p]
    ckt = ck_ref[...].T
    ones = jnp.ones((L_CTX, LANES), F32)
    outs = _pairs_attention([cq_ref[:, tile(p)] * scale for p in pairs_c], [[ckt[tile(p)].astype(BF16)] for p in pairs_c],
                            [[jnp.concatenate([cv_ref[:, tile(p)], ones], 1).astype(BF16)] for p in pairs_c],
                            [[None] for _ in pairs_c], [None for _ in pairs_c])
    for p in pairs_c:
        yc_ref[:, tile(p)] = outs[p]


def _ctx_attn_call(bq, bk, bv, cq, ck, cv, sink):
    def spec(n):
        return pl.BlockSpec((L_CTX, n), lambda b: (b, 0))

    return pl.pallas_call(
        _ctx_attn_kernel,
        grid=(B_CTX,),
        in_specs=[spec(D_B), spec(D_KVB), spec(D_KVB), spec(D_C), spec(D_C), spec(D_C),
                  pl.BlockSpec((1, 8), lambda b: (0, 0))],
        out_specs=[spec(D_B), spec(D_C)],
        out_shape=[jax.ShapeDtypeStruct((T_CTX, D_B), F32), jax.ShapeDtypeStruct((T_CTX, D_C), F32)],
        compiler_params=_cparams("arbitrary"),
        name="ctx_attn",
    )(bq, bk, bv, cq, ck, cv, sink)


def _rope(x, cos, sin_signed, first):
    rot = jnp.where(first, pltpu.roll(x, LANES - 16, 1), pltpu.roll(x, 16, 1))
    return x * cos + rot * sin_signed


def _win_attn_kernel(q_ref, k_ref, v_ref, kc_ref, vc_ref, cos_ref, sin_ref, sink_ref, y_ref,
                     kpt_scr, va_scr, kcp_scr, vca_scr):
    scale = HD ** -0.5
    n_pairs = D_B // LANES
    nblk = N_LAT // WIN
    nband = 3
    lane = lax.broadcasted_iota(jnp.int32, (1, LANES), 1)
    first = (lane % 32) < 16
    for p, (ks, va) in enumerate(zip(_gqa_key_slabs(kc_ref[0, 0].T), _gqa_value_pairs(vc_ref[0, 0]))):
        kcp_scr[p] = ks.astype(BF16)
        vca_scr[p] = va.astype(BF16)

    def prep(blk, c):
        r0 = pl.multiple_of(blk * WIN, WIN)
        kr = _rope(k_ref[pl.ds(r0, WIN), :], cos_ref[pl.ds(r0, WIN), :], sin_ref[pl.ds(r0, WIN), :], first)
        for p, (ks, va) in enumerate(zip(_gqa_key_slabs(kr.T), _gqa_value_pairs(v_ref[pl.ds(r0, WIN), :]))):
            kpt_scr[p, blk] = ks.astype(BF16)
            va_scr[p, pl.ds(r0, WIN), :] = va.astype(BF16)
        return c

    lax.fori_loop(0, nblk, prep, 0)

    def body(blk, c):
        q0 = pl.multiple_of(blk * WIN, WIN)
        sb = jnp.clip(blk - 1, 0, nblk - nband)
        k0 = pl.multiple_of(sb * WIN, WIN)
        cos = cos_ref[pl.ds(q0, WIN), :]
        sin = sin_ref[pl.ds(q0, WIN), :]
        row = lax.broadcasted_iota(jnp.int32, (2 * WIN, nband * WIN), 0)
        qpos = q0 + jnp.where(row < WIN, row, row - WIN)
        kpos = k0 + lax.broadcasted_iota(jnp.int32, (2 * WIN, nband * WIN), 1)
        mask = jnp.abs(kpos - qpos) <= WIN
        pairs = range(n_pairs)
        qps = [_rope(q_ref[pl.ds(q0, WIN), p * LANES:(p + 1) * LANES], cos, sin, first) * scale for p in pairs]
        k_loc = [jnp.concatenate([kpt_scr[p, sb + j] for j in range(nband)], axis=1) for p in pairs]
        v_loc = [va_scr[p, pl.ds(k0, nband * WIN), :] for p in pairs]
        outs = _pairs_attention(qps, [[kcp_scr[p], k_loc[p]] for p in pairs], [[vca_scr[p], v_loc[p]] for p in pairs],
                                [[None, mask] for _ in pairs], [_sink_col(sink_ref, p, WIN) for p in pairs])
        for p in pairs:
            y_ref[pl.ds(q0, WIN), p * LANES:(p + 1) * LANES] = outs[p]
        return c

    lax.fori_loop(0, nblk, body, 0)


def _win_attn_call(bq, bk, bv, cache_k, cache_v, cos, sin, sink, layer):
    rb0 = T_CTX // N_LAT

    def spec(n):
        return pl.BlockSpec((N_LAT, n), lambda b: (rb0 + b, 0))

    cache = pl.BlockSpec((1, 1, PAST_LEN, D_KVB), lambda b: (b, layer, 0, 0))
    tab = pl.BlockSpec((N_LAT, LANES), lambda b: (0, 0))
    return pl.pallas_call(
        _win_attn_kernel,
        grid=(B_LAT,),
        in_specs=[spec(D_B), spec(D_KVB), spec(D_KVB), cache, cache, tab, tab,
                  pl.BlockSpec((1, 8), lambda b: (0, 0))],
        out_specs=pl.BlockSpec((N_LAT, D_B), lambda b: (b, 0)),
        out_shape=jax.ShapeDtypeStruct((T_LAT, D_B), F32),
        scratch_shapes=[pltpu.VMEM((D_B // LANES, N_LAT // WIN, LANES, WIN), BF16),
                        pltpu.VMEM((D_B // LANES, N_LAT, 2 * LANES), BF16),
                        pltpu.VMEM((D_B // LANES, LANES, PAST_LEN), BF16),
                        pltpu.VMEM((D_B // LANES, PAST_LEN, 2 * LANES), BF16)],
        compiler_params=_cparams("arbitrary"),
        name="win_attn",
    )(bq, bk, bv, cache_k, cache_v, cos, sin, sink)


NA_BLK = LANES
NA_SPAN_BLKS = NA_WIN_H * GRID_W // NA_BLK + 1
NA_SPAN_ROWS = NA_SPAN_BLKS * NA_BLK // GRID_W
NA_VARIANTS = ((7, 0), (6, 0), (5, 0), (4, 0), (3, 0), (3, 1), (3, 2), (2, 2), (1, 2), (0, 2))


def _na_attn_kernel(q_ref, k_ref, v_ref, kc_ref, vc_ref, bias_ref, y_ref, kpt_scr, va_scr, kcp_scr, vca_scr):
    scale = HD ** -0.5
    rows = N_LAT // GRID_W
    n_pairs = D_C // LANES
    nblk = N_LAT // NA_BLK
    half = NA_WIN_H // 2
    kct = kc_ref[0, 0].T
    for p in range(n_pairs):
        cols = slice(p * LANES, (p + 1) * LANES)
        kcp_scr[p] = kct[cols].astype(BF16)
        vca_scr[p] = jnp.concatenate([vc_ref[0, 0, :, cols], jnp.ones((PAST_LEN, LANES), F32)], 1).astype(BF16)

    def prep(blk, c):
        r0 = pl.multiple_of(blk * NA_BLK, NA_BLK)
        kt = k_ref[pl.ds(r0, NA_BLK), :].T
        for p in range(n_pairs):
            cols = slice(p * LANES, (p + 1) * LANES)
            kpt_scr[p, blk] = kt[cols].astype(BF16)
            va_scr[p, pl.ds(r0, NA_BLK), :] = jnp.concatenate(
                [v_ref[pl.ds(r0, NA_BLK), cols], jnp.ones((NA_BLK, LANES), F32)], 1).astype(BF16)
        return c

    lax.fori_loop(0, nblk, prep, 0)

    def body(r, c):
        kr0 = jnp.clip(r - half, 0, rows - NA_WIN_H)
        sb = jnp.minimum(kr0 // 2, nblk - NA_SPAN_BLKS)
        var = jnp.where(r <= half, r, jnp.where(r >= rows - half, r - (rows - 2 * half - 2), half + (kr0 & 1)))
        q0 = pl.multiple_of(r * GRID_W, GRID_W)
        k0 = pl.multiple_of(sb * NA_BLK, NA_BLK)
        pairs = range(n_pairs)
        qps = [q_ref[pl.ds(q0, GRID_W), p * LANES:(p + 1) * LANES] * scale for p in pairs]
        k_loc = [jnp.concatenate([kpt_scr[p, sb + j] for j in range(NA_SPAN_BLKS)], axis=1) for p in pairs]
        v_loc = [va_scr[p, pl.ds(k0, NA_SPAN_BLKS * NA_BLK), :] for p in pairs]
        outs = _pairs_attention(qps, [[kcp_scr[p], k_loc[p]] for p in pairs], [[vca_scr[p], v_loc[p]] for p in pairs],
                                [[None, bias_ref[0, p, var]] for p in pairs], [None for _ in pairs])
        for p in pairs:
            y_ref[pl.ds(q0, GRID_W), p * LANES:(p + 1) * LANES] = outs[p]
        return c

    lax.fori_loop(0, rows, body, 0)


def _na_attn_call(cq, ck, cv, cache_k, cache_v, bias_tab, layer):
    rb0 = T_CTX // N_LAT
    n_pairs = D_C // LANES
    spec = pl.BlockSpec((N_LAT, D_C), lambda b: (rb0 + b, 0))
    cache = pl.BlockSpec((1, 1, PAST_LEN, D_C), lambda b: (b, layer, 0, 0))
    span = NA_SPAN_BLKS * NA_BLK
    return pl.pallas_call(
        _na_attn_kernel,
        grid=(B_LAT,),
        in_specs=[spec, spec, spec, cache, cache,
                  pl.BlockSpec((1, n_pairs, len(NA_VARIANTS), 2 * GRID_W, span), lambda b: (layer, 0, 0, 0, 0))],
        out_specs=pl.BlockSpec((N_LAT, D_C), lambda b: (b, 0)),
        out_shape=jax.ShapeDtypeStruct((T_LAT, D_C), F32),
        scratch_shapes=[pltpu.VMEM((n_pairs, N_LAT // NA_BLK, LANES, NA_BLK), BF16),
                        pltpu.VMEM((n_pairs, N_LAT, 2 * LANES), BF16),
                        pltpu.VMEM((n_pairs, LANES, PAST_LEN), BF16),
                        pltpu.VMEM((n_pairs, PAST_LEN, 2 * LANES), BF16)],
        compiler_params=_cparams("arbitrary"),
        name="na_attn",
    )(cq, ck, cv, cache_k, cache_v, bias_tab)


def _outproj_kernel(yac_ref, ybc_ref, ycc_ref, yal_ref, ybl_ref, ycl_ref, x_ref, ga_ref, scf_ref, shf_ref,
                    lng_ref, lnb_ref, wo_ref, wr_ref, x1_ref, h2_ref, route_ref, cnt_ref, run_scr):
    @pl.when(pl.program_id(0) == 0)
    def _():
        run_scr[...] = jnp.zeros_like(run_scr)

    is_ctx = pl.program_id(0) < T_CTX // TM

    def pick(c_ref, l_ref):
        return jnp.where(is_ctx, c_ref[...], l_ref[...])

    y = (_dot(pick(yac_ref, yal_ref), wo_ref[0, 0:Y_A_COLS, :])
         + _dot(pick(ybc_ref, ybl_ref), wo_ref[0, Y_A_COLS:Y_A_COLS + D_B, :])
         + _dot(pick(ycc_ref, ycl_ref), wo_ref[0, Y_A_COLS + D_B:, :]))
    x1 = _layer_norm(ALPHA * x_ref[...] + ga_ref[0, 0, 0] * y, lng_ref[...], lnb_ref[...])
    x1_ref[...] = x1
    h2 = x1 * (1.0 + scf_ref[0, 0, 0]) + shf_ref[0, 0, 0]
    h2_ref[...] = h2
    w_r = wr_ref[...]
    w_hi = w_r.astype(BF16)
    w_lo = (w_r - w_hi.astype(F32)).astype(BF16)
    h_hi = h2.astype(BF16)
    h_lo = (h2 - h_hi.astype(F32)).astype(BF16)
    p_hi = jnp.dot(h_hi, jnp.concatenate([w_hi, w_lo], axis=1), preferred_element_type=F32)
    logits = p_hi[:, :LANES] + p_hi[:, LANES:] + jnp.dot(h_lo, w_hi, preferred_element_type=F32)
    lane = lax.broadcasted_iota(jnp.int32, logits.shape, 1)
    lanef = lane.astype(F32)
    big = float(LANES)
    lg = jnp.where((lane >= N_EXPERTS) & (lane < N_EXPERTS + N_GROUPS), logits, NEG_INF)
    mg = jnp.max(lg, axis=1, keepdims=True)
    grp = jnp.min(jnp.where(lg == mg, lanef, big), axis=1, keepdims=True) - float(N_EXPERTS)
    g_w = 1.0 / jnp.sum(jnp.exp(lg - mg), axis=1, keepdims=True)
    in_grp = (lane < N_EXPERTS) & ((lane // E_PER_GROUP).astype(F32) == grp)
    le = jnp.where(in_grp, logits, NEG_INF)
    l1 = jnp.max(le, axis=1, keepdims=True)
    i1 = jnp.min(jnp.where(le == l1, lanef, big), axis=1, keepdims=True)
    le2 = jnp.where(lanef == i1, NEG_INF, le)
    l2 = jnp.max(le2, axis=1, keepdims=True)
    i2 = jnp.min(jnp.where(le2 == l2, lanef, big), axis=1, keepdims=True)
    e2 = jnp.exp(l2 - l1)
    w1 = g_w / (1.0 + e2)
    w2 = g_w * e2 / (1.0 + e2)
    oh1 = jnp.where(lanef == i1, 1.0, 0.0)
    oh2 = jnp.where(lanef == i2, 1.0, 0.0)
    rt = lax.broadcasted_iota(jnp.int32, (TM, TM), 0)
    ct = lax.broadcasted_iota(jnp.int32, (TM, TM), 1)
    before = jnp.where(ct < rt, 1.0, 0.0)
    run = run_scr[...]
    tot1 = jnp.sum(oh1, axis=0, keepdims=True)
    r1 = jnp.sum(oh1 * (run + _dot(before, oh1)), axis=1, keepdims=True)
    r2 = jnp.sum(oh2 * (run + tot1 + _dot(before, oh2)), axis=1, keepdims=True)
    run = run + tot1 + jnp.sum(oh2, axis=0, keepdims=True)
    run_scr[...] = run
    cnt_ref[...] = run
    vals = (i1, i2, w1, w2, r1, r2)
    out = jnp.zeros_like(logits)
    for n, v in enumerate(vals):
        out = jnp.where(lane == n, v, out)
    route_ref[...] = out[:, :8]


def _outproj_call(y_ctx, y_lat, x, mod_t, layer, lng, lnb, wo, wr):
    row_vec = pl.BlockSpec((1, D_MODEL), lambda i: (0, 0))
    n_ctx = T_CTX // TM

    def tok(n):
        return pl.BlockSpec((TM, n), lambda i: (i, 0))

    def tok_ctx(n):
        return pl.BlockSpec((TM, n), lambda i: (jnp.minimum(i, n_ctx - 1), 0))

    def tok_lat(n):
        return pl.BlockSpec((TM, n), lambda i: (jnp.maximum(i - n_ctx, 0), 0))

    return pl.pallas_call(
        _outproj_kernel,
        grid=(N_TILES,),
        in_specs=[tok_ctx(Y_A_COLS), tok_ctx(D_B), tok_ctx(D_C), tok_lat(Y_A_COLS), tok_lat(D_B), tok_lat(D_C),
                  tok(D_MODEL), _mod_spec(layer, 2), _mod_spec(layer, 4), _mod_spec(layer, 3), row_vec, row_vec,
                  pl.BlockSpec((1, Y_A_COLS + D_B + D_C, D_MODEL), lambda i: (layer, 0, 0)),
                  pl.BlockSpec((D_MODEL, LANES), lambda i: (0, 0))],
        out_specs=[tok(D_MODEL), tok(D_MODEL), tok(8), pl.BlockSpec((1, LANES), lambda i: (0, 0))],
        out_shape=[jax.ShapeDtypeStruct((T_ALL, D_MODEL), F32), jax.ShapeDtypeStruct((T_ALL, D_MODEL), F32),
                   jax.ShapeDtypeStruct((T_ALL, 8), F32), jax.ShapeDtypeStruct((1, LANES), F32)],
        scratch_shapes=[pltpu.VMEM((1, LANES), F32)],
        compiler_params=_cparams("arbitrary"),
        name="outproj_router",
    )(*y_ctx, *y_lat, x, mod_t, mod_t, mod_t, lng, lnb, wo, wr)


def _dispatch_kernel(pos_ref, pend_ref, h_ref, xs_hbm, zero_scr, sem):
    base = pl.program_id(0) * TM

    @pl.when(pl.program_id(0) == 0)
    def _():
        zero_scr[...] = jnp.zeros_like(zero_scr)

        def fill(e, op):
            prev = pend_ref[e - 1] if e else 0

            @pl.when(pend_ref[e] > prev)
            def _():
                first = pl.multiple_of(pend_ref[e] - MOE_BLK, MOE_BLK)
                op(pltpu.make_async_copy(zero_scr, xs_hbm.at[pl.ds(first, MOE_BLK), :], sem))

        def tail_copy(b):
            first = pl.multiple_of(b * MOE_BLK, MOE_BLK)
            return pltpu.make_async_copy(zero_scr, xs_hbm.at[pl.ds(first, MOE_BLK), :], sem)

        def tail_start(b, c):
            tail_copy(b).start()
            return c

        def tail_wait(b, c):
            tail_copy(b).wait()
            return c

        n_blocks = xs_hbm.shape[0] // MOE_BLK
        first_free = pend_ref[N_EXPERTS - 1] // MOE_BLK
        for e in range(N_EXPERTS):
            fill(e, lambda cp: cp.start())
        lax.fori_loop(first_free, n_blocks, tail_start, 0)
        for e in range(N_EXPERTS):
            fill(e, lambda cp: cp.wait())
        lax.fori_loop(first_free, n_blocks, tail_wait, 0)

    def row(t, p):
        return pltpu.make_async_copy(h_ref.at[pl.ds(t, 1), :], xs_hbm.at[pl.ds(p, 1), :], sem)

    def issue(t, c):
        row(t, pos_ref[base + t]).start(priority=0)
        row(t, pos_ref[T_ALL + base + t]).start(priority=1)
        return c

    lax.fori_loop(0, TM, issue, 0, unroll=8)
    whole = pltpu.make_async_copy(h_ref, xs_hbm.at[pl.ds(0, TM), :], sem)
    whole.wait()
    whole.wait()


def _dispatch_call(pos2, p_end, h2, nblk):
    grid_spec = pltpu.PrefetchScalarGridSpec(
        num_scalar_prefetch=2,
        grid=(N_TILES,),
        in_specs=[pl.BlockSpec((TM, D_MODEL), lambda i, p, pe: (i, 0))],
        out_specs=pl.BlockSpec(memory_space=pl.ANY),
        scratch_shapes=[pltpu.VMEM((MOE_BLK, D_MODEL), F32), pltpu.SemaphoreType.DMA(())],
    )
    return pl.pallas_call(
        _dispatch_kernel,
        grid_spec=grid_spec,
        out_shape=jax.ShapeDtypeStruct((nblk * MOE_BLK, D_MODEL), F32),
        compiler_params=_cparams("arbitrary"),
        name="dispatch",
    )(pos2, p_end, h2)


def _expert_kernel(blk_e_ref, nused_ref, xs_ref, wg_ref, wu_ref, wd_ref, out_ref):
    j = pl.program_id(0)

    @pl.when(j < nused_ref[0])
    def _():
        xb = xs_ref[...].astype(BF16)
        g = jnp.dot(xb, wg_ref[0, 0].astype(BF16), preferred_element_type=F32)
        u = jnp.dot(xb, wu_ref[0, 0].astype(BF16), preferred_element_type=F32)
        out_ref[...] = _dot(jax.nn.silu(g) * u, wd_ref[0, 0])

    @pl.when(j >= nused_ref[0])
    def _():
        out_ref[...] = jnp.zeros_like(out_ref)


def _expert_call(blk_e, nused, xs, wg, wu, wd, layer, nblk):
    def wspec(r, c):
        return pl.BlockSpec((1, 1, r, c), lambda j, be, nu: (layer, be[j], 0, 0))

    grid_spec = pltpu.PrefetchScalarGridSpec(
        num_scalar_prefetch=2,
        grid=(nblk,),
        in_specs=[
            pl.BlockSpec((MOE_BLK, D_MODEL), lambda j, be, nu: (jnp.clip(j, 0, jnp.maximum(nu[0] - 1, 0)), 0)),
            wspec(D_MODEL, D_EXPERT), wspec(D_MODEL, D_EXPERT), wspec(D_EXPERT, D_MODEL),
        ],
        out_specs=pl.BlockSpec((MOE_BLK, D_MODEL), lambda j, be, nu: (j, 0)),
    )
    return pl.pallas_call(
        _expert_kernel,
        grid_spec=grid_spec,
        out_shape=jax.ShapeDtypeStruct((nblk * MOE_BLK, D_MODEL), F32),
        compiler_params=_cparams("arbitrary"),
        name="experts",
    )(blk_e, nused, xs, wg, wu, wd)


def _combine_kernel(pos_ref, eo_hbm, x1_ref, route_ref, gf_ref, lng_ref, lnb_ref, x2_ref, buf, sem):
    base = pl.program_id(0) * TM

    def row(p, r, t):
        return pltpu.make_async_copy(eo_hbm.at[pl.ds(p, 1), :], buf.at[r, pl.ds(t, 1), :], sem)

    def issue(t, c):
        row(pos_ref[base + t], 0, t).start(priority=0)
        row(pos_ref[T_ALL + base + t], 1, t).start(priority=1)
        return c

    lax.fori_loop(0, TM, issue, 0, unroll=8)
    for r in range(2):
        pltpu.make_async_copy(eo_hbm.at[pl.ds(0, TM), :], buf.at[r], sem).wait()
    route = route_ref[...]
    y = route[:, 2:3] * buf[0] + route[:, 3:4] * buf[1]
    x2_ref[...] = _layer_norm(ALPHA * x1_ref[...] + gf_ref[0, 0, 0] * y, lng_ref[...], lnb_ref[...])


def _combine_call(pos2, eo, x1, route, mod_t, layer, lng, lnb):
    grid_spec = pltpu.PrefetchScalarGridSpec(
        num_scalar_prefetch=1,
        grid=(N_TILES,),
        in_specs=[
            pl.BlockSpec(memory_space=pl.ANY),
            pl.BlockSpec((TM, D_MODEL), lambda i, p: (i, 0)),
            pl.BlockSpec((TM, 8), lambda i, p: (i, 0)),
            _mod_spec(layer, 5),
            pl.BlockSpec((1, D_MODEL), lambda i, p: (0, 0)),
            pl.BlockSpec((1, D_MODEL), lambda i, p: (0, 0)),
        ],
        out_specs=pl.BlockSpec((TM, D_MODEL), lambda i, p: (i, 0)),
        scratch_shapes=[pltpu.VMEM((2, TM, D_MODEL), F32), pltpu.SemaphoreType.DMA(())],
    )
    return pl.pallas_call(
        _combine_kernel,
        grid_spec=grid_spec,
        out_shape=jax.ShapeDtypeStruct((T_ALL, D_MODEL), F32),
        compiler_params=_cparams("arbitrary"),
        name="combine_norm",
    )(pos2, eo, x1, route, mod_t, lng, lnb)


def _gate_cols(grp):
    return np.array([kind * 2 * H_A + d * H_A + grp * HP_A + j
                     for j in range(HP_A) for kind in range(2) for d in range(2)])


def _prep_w_in(w_in):
    a = w_in[..., :4 * D_A].reshape(DEPTH, D_MODEL, 4, H_A, DH_A)
    a = jnp.pad(a, ((0, 0), (0, 0), (0, 0), (0, 0), (0, DP_A - DH_A))).reshape(DEPTH, D_MODEL, ZA_COLS)
    gates = w_in[..., 4 * D_A:4 * D_A + 4 * H_A]
    g = jnp.concatenate([jnp.pad(gates[..., _gate_cols(grp)], ((0, 0), (0, 0), (0, LANES - 4 * HP_A)))
                         for grp in range(H_A // HP_A)], -1)
    rest = w_in[..., 4 * D_A + 4 * H_A:]
    return jnp.concatenate([a, rest, g], -1).astype(BF16)


def _prep_w_out(w_out):
    a = w_out[:, :D_A].reshape(DEPTH, H_A, DH_A, D_MODEL)
    a = jnp.pad(a, ((0, 0), (0, 0), (0, DP_A - DH_A), (0, 0))).reshape(DEPTH, Y_A_COLS, D_MODEL)
    return jnp.concatenate([a, w_out[:, D_A:]], 1).astype(BF16)


def _rope_tables():
    t = np.arange(N_LAT)
    nf = HD // 4
    inv = ROPE_BASE ** (-np.arange(nf, dtype=np.float32) / nf)
    ar = (t // GRID_W).astype(np.float32)[:, None] * inv
    ac = (t % GRID_W).astype(np.float32)[:, None] * inv
    ang = jnp.asarray(np.concatenate([ar, ar, ac, ac], -1), F32)
    cos, sin = jnp.cos(ang), jnp.sin(ang)
    sign = np.where((np.arange(HD) % 32) < 16, -1.0, 1.0).astype(np.float32)
    reps = LANES // HD
    return jnp.tile(cos, (1, reps)), jnp.tile(sin * sign, (1, reps))


def _na_bias_tables(rpb):
    qcol = np.arange(GRID_W)[:, None]
    kcol = np.arange(GRID_W)[None, :]
    dc = np.clip(kcol - qcol, 1 - NA_WIN_W, NA_WIN_W - 1) + NA_WIN_W - 1
    wstart = np.clip(qcol - NA_WIN_W // 2, 0, GRID_W - NA_WIN_W)
    in_win = (kcol >= wstart) & (kcol < wstart + NA_WIN_W)
    sel = (np.arange(2 * NA_WIN_W - 1)[:, None] == dc.reshape(1, -1)).astype(np.float32)
    cols = jnp.einsum("lhrd,dn->lhrn", rpb, jnp.asarray(sel), precision=lax.Precision.HIGHEST)
    cols = jnp.where(in_win.reshape(-1), cols, NEG_INF).reshape(DEPTH, H_C, 2 * NA_WIN_H - 1, GRID_W, GRID_W)
    outside = jnp.full((DEPTH, H_C, GRID_W, GRID_W), NEG_INF, F32)
    variants = []
    for dr0, off in NA_VARIANTS:
        span_rows = [cols[:, :, dr0 + i - off] if 0 <= i - off < NA_WIN_H else outside for i in range(NA_SPAN_ROWS)]
        variants.append(jnp.stack(span_rows, 3).reshape(DEPTH, H_C, GRID_W, NA_SPAN_ROWS * GRID_W))
    tab = jnp.stack(variants, 2).reshape(DEPTH, H_C // 2, 2, len(NA_VARIANTS), GRID_W, NA_SPAN_ROWS * GRID_W)
    return tab.transpose(0, 1, 3, 2, 4, 5).reshape(DEPTH, H_C // 2, len(NA_VARIANTS), 2 * GRID_W, NA_SPAN_ROWS * GRID_W)


def _dispatch_plan(route, counts):
    nblk = 2 * T_ALL // MOE_BLK + N_EXPERTS
    cnt = counts[0, :N_EXPERTS].astype(jnp.int32)
    padded = (cnt + MOE_BLK - 1) // MOE_BLK * MOE_BLK
    p_end = jnp.cumsum(padded)
    p_start = p_end - padded
    e = route[:, 0:2].astype(jnp.int32)
    hot = e[..., None] == jnp.arange(N_EXPERTS, dtype=jnp.int32)
    pos = jnp.sum(jnp.where(hot, p_start, 0), -1) + route[:, 4:6].astype(jnp.int32)
    pos2 = pos.T.reshape(-1)
    blk_first = jnp.arange(nblk, dtype=jnp.int32) * MOE_BLK
    blk_e = jnp.minimum(jnp.sum((p_end[None, :] <= blk_first[:, None]).astype(jnp.int32), axis=1), N_EXPERTS - 1)
    nused = p_end[-1:] // MOE_BLK
    return blk_e, nused, pos2, p_end, nblk


def kernel(x_prompt, x_sample, state_a_C, state_a_n, state_a_m, cache_b_k, cache_b_v, cache_c_k, cache_c_v, c, c_ctx, w_in, b_a_i, b_a_f, w_a_hnorm, b_sink, rpb, w_out, w_ada, b_ada, ln_g, ln_b, w_router_grp, w_router_exp, w_e_gate, w_e_up, w_e_down):
    w_in_p = _prep_w_in(w_in)
    w_out_p = _prep_w_out(w_out)
    w_r = jnp.pad(jnp.concatenate([w_router_exp, w_router_grp], -1),
                  ((0, 0), (0, 0), (0, LANES - N_EXPERTS - N_GROUPS)))
    wn_p = jnp.pad(w_a_hnorm.reshape(DEPTH, H_A, 1, DH_A), ((0, 0), (0, 0), (0, 0), (0, DP_A - DH_A)))
    gate_b = jnp.concatenate([b_a_i, b_a_f], 1).transpose(0, 2, 1)
    gate_b = jnp.pad(gate_b.reshape(DEPTH, H_A // HP_A, 1, 4 * HP_A), ((0, 0), (0, 0), (0, 0), (0, LANES - 4 * HP_A)))
    sink_p = jnp.pad(b_sink, ((0, 0), (0, 8 - H_B))).reshape(DEPTH, 1, 8)
    cos_t, sin_t = _rope_tables()
    na_bias = _na_bias_tables(rpb)
    cb_k = cache_b_k.reshape(B_LAT, DEPTH, PAST_LEN, D_KVB)
    cb_v = cache_b_v.reshape(B_LAT, DEPTH, PAST_LEN, D_KVB)
    cc_k = cache_c_k.reshape(B_LAT, DEPTH, PAST_LEN, D_C)
    cc_v = cache_c_v.reshape(B_LAT, DEPTH, PAST_LEN, D_C)
    pad_c = ((0, 0), (0, 0), (0, 0), (0, 0), (0, DP_A - DH_A), (0, DP_A - DH_A))
    st_ct = jnp.swapaxes(jnp.pad(state_a_C, pad_c), -1, -2)
    st_nr = jnp.broadcast_to(jnp.pad(state_a_n, pad_c[:-1])[..., None], st_ct.shape)
    st_s = jnp.concatenate([st_ct, st_nr], -1)
    st_m = state_a_m[..., None, None]
    z_s = jnp.zeros((B_CTX, 2, H_A, DP_A, 2 * DP_A), F32)
    z_m = jnp.zeros((B_CTX, 2, H_A, 1, 1), F32)

    cvec = jnp.concatenate([c, c_ctx[None, :], jnp.zeros((3, D_MODEL), F32)], 0)
    mod = _ada_call(cvec, w_ada, b_ada)
    tile_row = np.concatenate([np.full(T_CTX // TM, B_LAT), np.repeat(np.arange(B_LAT), N_LAT // TM)])
    mod_t = mod[:, tile_row].reshape(DEPTH, N_TILES, 6, 1, D_MODEL)

    x = jnp.concatenate([x_prompt.reshape(T_CTX, D_MODEL), x_sample.reshape(T_LAT, D_MODEL)], 0)
    cs_, ns_, ms_, kbs, vbs, kcs, vcs = [], [], [], [], [], [], []
    for l in range(DEPTH):
        za, bq, bk, bv, cq, ck, cv, zg, kt = _inproj_call(x, mod_t, l, w_in_p)
        ya_c, s_l, m_l = _mlstm_call(za, kt, zg, gate_b[l], wn_p[l], z_s, z_m, nb=B_CTX, seq=L_CTX, row_blk0=0)
        yb_c, yc_c = _ctx_attn_call(bq, bk, bv, cq, ck, cv, sink_p[l])
        ya_l, _, _ = _mlstm_call(za, kt, zg, gate_b[l], wn_p[l], st_s[:, l], st_m[:, l],
                                 nb=B_LAT, seq=N_LAT, row_blk0=T_CTX // N_LAT)
        yb_l = _win_attn_call(bq, bk, bv, cb_k, cb_v, cos_t, sin_t, sink_p[l], l)
        yc_l = _na_attn_call(cq, ck, cv, cc_k, cc_v, na_bias, l)
        x1, h2, route, counts = _outproj_call((ya_c, yb_c, yc_c), (ya_l, yb_l, yc_l), x, mod_t, l,
                                              ln_g[l, 0:1], ln_b[l, 0:1], w_out_p, w_r[l])
        blk_e, nused, pos2, p_end, nblk = _dispatch_plan(route, counts)
        xs = _dispatch_call(pos2, p_end, h2, nblk)
        eo = _expert_call(blk_e, nused, xs, w_e_gate, w_e_up, w_e_down, l, nblk)
        x = _combine_call(pos2, eo, x1, route, mod_t, l, ln_g[l, 1:2], ln_b[l, 1:2])
        cs_.append(jnp.swapaxes(s_l[..., :DH_A, :DH_A], -1, -2))
        ns_.append(s_l[..., :DH_A, DP_A])
        ms_.append(m_l.reshape(B_CTX, 2, H_A))
        kbs.append(bk[:T_CTX].reshape(B_CTX, L_CTX, KV_B, HD))
        vbs.append(bv[:T_CTX].reshape(B_CTX, L_CTX, KV_B, HD))
        kcs.append(ck[:T_CTX].reshape(B_CTX, L_CTX, H_C, HD))
        vcs.append(cv[:T_CTX].reshape(B_CTX, L_CTX, H_C, HD))
    y_prompt = x[:T_CTX].reshape(B_CTX, L_CTX, D_MODEL)
    y_sample = x[T_CTX:].reshape(B_LAT, N_LAT, D_MODEL)
    return (y_prompt, y_sample, jnp.stack(cs_, 1), jnp.stack(ns_, 1), jnp.stack(ms_, 1),
            jnp.stack(kbs, 1), jnp.stack(vbs, 1), jnp.stack(kcs, 1), jnp.stack(vcs, 1))
```

```python
import functools

import numpy as np
import jax
import jax.numpy as jnp
from jax import lax
from jax.experimental import pallas as pl
from jax.experimental.pallas import tpu as pltpu

F32 = jnp.float32
BF16 = jnp.bfloat16
NEG_INF = float("-inf")

D_MODEL = 1024
DEPTH = 4
B_CTX, L_CTX = 16, 256
B_LAT, N_LAT = 4, 2048
PAST_LEN = 512
GRID_W = 64
H_A, DH_A = 4, 96
D_A = H_A * DH_A
H_B, KV_B, HD = 6, 2, 64
G_B = H_B // KV_B
D_B, D_KVB = H_B * HD, KV_B * HD
WIN = 128
ROPE_BASE = 10000.0
H_C = 4
D_C = H_C * HD
NA_WIN_H, NA_WIN_W = 8, 16
N_GROUPS, E_PER_GROUP = 4, 8
N_EXPERTS = N_GROUPS * E_PER_GROUP
D_EXPERT = D_MODEL // 4
ALPHA = (2 * DEPTH) ** 0.25
LN_EPS = 1e-5

LANES = 128
DP_A = LANES
LC_K = LANES
TM = 256
MOE_BLK = 256
VMEM_LIMIT = 48 * 1024 * 1024

T_CTX = B_CTX * L_CTX
T_LAT = B_LAT * N_LAT
T_ALL = T_CTX + T_LAT
N_TILES = T_ALL // TM
ZA_COLS = 4 * H_A * DP_A
Y_A_COLS = H_A * DP_A
IN_SPLITS = (ZA_COLS, D_B, D_KVB, D_KVB, D_C, D_C, D_C, 2 * LANES)
IN_COLS = sum(IN_SPLITS)


def _cparams(*sem):
    return pltpu.CompilerParams(dimension_semantics=sem, vmem_limit_bytes=VMEM_LIMIT)


def _dot(a, b):
    return jnp.dot(a.astype(BF16), b.astype(BF16), preferred_element_type=F32)


def _dot_nt(a, b):
    return lax.dot_general(a.astype(BF16), b.astype(BF16), (((1,), (1,)), ((), ())), preferred_element_type=F32)


def _dot_tn(a, b):
    return lax.dot_general(a.astype(BF16), b.astype(BF16), (((0,), (0,)), ((), ())), preferred_element_type=F32)


def _layer_norm(v, g, b):
    mu = jnp.mean(v, -1, keepdims=True)
    var = jnp.mean(jnp.square(v - mu), -1, keepdims=True)
    return (v - mu) * lax.rsqrt(var + LN_EPS) * g + b


def _ada_kernel(c_ref, w_ref, b_ref, o_ref):
    s = jax.nn.silu(c_ref[...])
    o_ref[0] = _dot(s, w_ref[0]) + b_ref[0]


def _ada_call(cvec, w_ada, b_ada):
    nb = 6
    return pl.pallas_call(
        _ada_kernel,
        grid=(DEPTH, nb),
        in_specs=[
            pl.BlockSpec((8, D_MODEL), lambda l, j: (0, 0)),
            pl.BlockSpec((1, D_MODEL, D_MODEL), lambda l, j: (l, 0, j)),
            pl.BlockSpec((1, 1, D_MODEL), lambda l, j: (l, 0, j)),
        ],
        out_specs=pl.BlockSpec((1, 8, D_MODEL), lambda l, j: (l, 0, j)),
        out_shape=jax.ShapeDtypeStruct((DEPTH, 8, 6 * D_MODEL), F32),
        compiler_params=_cparams("arbitrary", "arbitrary"),
        name="adaln",
    )(cvec, w_ada, b_ada.reshape(DEPTH, 1, 6 * D_MODEL))


def _inproj_kernel(x_ref, sc_ref, sh_ref, w_ref, *out_refs):
    h = (x_ref[...] * (1.0 + sc_ref[0, 0, 0]) + sh_ref[0, 0, 0]).astype(BF16)
    off = 0
    for ref in out_refs:
        n = ref.shape[-1]
        ref[...] = jnp.dot(h, w_ref[0, :, off:off + n], preferred_element_type=F32)
        off += n


def _mod_spec(layer, which):
    return pl.BlockSpec((1, 1, 1, 1, D_MODEL), lambda i, *_: (layer, i, which, 0, 0))


def _inproj_call(x, mod_t, layer, w):
    return pl.pallas_call(
        _inproj_kernel,
        grid=(N_TILES,),
        in_specs=[
            pl.BlockSpec((TM, D_MODEL), lambda i: (i, 0)),
            _mod_spec(layer, 1), _mod_spec(layer, 0),
            pl.BlockSpec((1, D_MODEL, IN_COLS), lambda i: (layer, 0, 0)),
        ],
        out_specs=[pl.BlockSpec((TM, n), lambda i: (i, 0)) for n in IN_SPLITS],
        out_shape=[jax.ShapeDtypeStruct((T_ALL, n), F32) for n in IN_SPLITS],
        compiler_params=_cparams("arbitrary"),
        name="inproj",
    )(x, mod_t, mod_t, w)


HP_A = 2
N_CH = 2 * HP_A


def _mlstm_kernel(q_ref, k_ref, v_ref, o_ref, g_ref, gb_ref, wn_ref, s0_ref, m0_ref,
                  y_ref, so_ref, mo_ref, hf_scr, hb_scr, *, nc):
    lc = LC_K
    scale = DH_A ** -0.5
    chains = [(j, d) for j in range(HP_A) for d in range(2)]
    ti = lax.broadcasted_iota(jnp.int32, (lc, lc), 0)
    si = lax.broadcasted_iota(jnp.int32, (lc, lc), 1)
    lane_ok = lax.broadcasted_iota(jnp.int32, (1, DP_A), 1) < DH_A
    ones = jnp.ones((lc, DP_A), F32)

    def stack(parts):
        return jnp.concatenate(parts, axis=0)

    def rows_of(x, a):
        return a[x * lc:(x + 1) * lc]

    def spread(vals):
        return stack([jnp.broadcast_to(v, (lc, v.shape[1])) for v in vals])

    mask = stack([si <= ti if d == 0 else si >= ti for _, d in chains])
    mask_t = stack([si >= ti if d == 0 else si <= ti for _, d in chains])
    eye = stack([si == ti for _ in chains])

    def col_sums(a):
        return [jnp.sum(rows_of(x, a), axis=0, keepdims=True) for x in range(N_CH)]

    def body(i, carry):
        smats = carry[:N_CH]
        ms = carry[N_CH:]
        r0s = [pl.multiple_of((i if d == 0 else nc - 1 - i) * lc, lc) for _, d in chains]
        q, kt, v1, icol, fpre = [], [], [], [], []
        for (j, d), r0 in zip(chains, r0s):
            cols = slice(j * DP_A, (j + 1) * DP_A)
            q.append(q_ref[pl.ds(r0, lc), cols].astype(BF16))
            kt.append((k_ref[pl.ds(r0, lc), cols] * scale).T.astype(BF16))
            v1.append(jnp.concatenate([v_ref[pl.ds(r0, lc), cols], ones], axis=1))
            gz = g_ref[pl.ds(r0, lc), :] + gb_ref[0]
            icol.append(gz[:, 4 * j + d:4 * j + d + 1])
            fpre.append(gz[:, 4 * j + 2 + d:4 * j + 3 + d])
        i_col = stack(icol)
        f_col = jax.nn.log_sigmoid(stack(fpre))
        b_rows = col_sums(jnp.where(mask_t, f_col, 0.0))
        bls = col_sums(f_col)
        b_col = jnp.sum(jnp.where(eye, spread(b_rows), 0.0), axis=1, keepdims=True)
        a_rows = col_sums(jnp.where(eye, i_col - b_col, 0.0))
        a_sp = jnp.where(mask, spread(a_rows), NEG_INF)
        m_sp = spread(ms)
        gap = jnp.maximum(m_sp, jnp.max(a_sp, axis=1, keepdims=True))
        m_out = b_col + gap
        wmat = jnp.exp(a_sp - gap)
        sw = (stack([jnp.dot(q[x], kt[x], preferred_element_type=F32) for x in range(N_CH)]) * wmat).astype(BF16)
        sc_in = jnp.exp(m_sp - gap)
        tot = (stack([jnp.dot(rows_of(x, sw), v1[x].astype(BF16), preferred_element_type=F32) for x in range(N_CH)])
               + sc_in * stack([_dot(q[x], smats[x]) for x in range(N_CH)]))
        h = tot[:, :DP_A] / jnp.maximum(jnp.abs(tot[:, DP_A:]), jnp.exp(-m_out))
        for x, ((j, d), r0) in enumerate(zip(chains, r0s)):
            dst = hf_scr if d == 0 else hb_scr
            dst[pl.ds(r0, lc), j * DP_A:(j + 1) * DP_A] = rows_of(x, h)
        dec = spread(bls) - b_col + i_col
        m_new = [jnp.maximum(bls[x] + ms[x], jnp.max(rows_of(x, dec), axis=0, keepdims=True)) for x in range(N_CH)]
        wk = jnp.exp(dec - spread(m_new))
        s_new = [jnp.exp(bls[x] + ms[x] - m_new[x]) * smats[x] + _dot(kt[x], rows_of(x, wk) * v1[x])
                 for x in range(N_CH)]
        return tuple(s_new) + tuple(m_new)

    init = tuple(s0_ref[0, d, j] for j, d in chains) + tuple(m0_ref[0, d, j] for j, d in chains)
    final = lax.fori_loop(0, nc, body, init)
    for x, (j, d) in enumerate(chains):
        so_ref[0, d, j] = final[x]
        mo_ref[0, d, j] = final[N_CH + x]

    def finish(c, _):
        r0 = pl.multiple_of(c * lc, lc)
        for j in range(HP_A):
            cols = slice(j * DP_A, (j + 1) * DP_A)
            h = hf_scr[pl.ds(r0, lc), cols] + hb_scr[pl.ds(r0, lc), cols]
            mu = jnp.sum(h, axis=1, keepdims=True) * (1.0 / DH_A)
            dv = jnp.where(lane_ok, h - mu, 0.0)
            var = jnp.sum(dv * dv, axis=1, keepdims=True) * (1.0 / DH_A)
            hn = dv * lax.rsqrt(var + LN_EPS) * wn_ref[j]
            y_ref[pl.ds(r0, lc), cols] = jax.nn.sigmoid(o_ref[pl.ds(r0, lc), cols]) * hn
        return 0

    lax.fori_loop(0, nc, finish, 0)


def _mlstm_call(za, zg, gbias, wn, s0, m0, *, nb, seq, row_blk0):
    nc = seq // LC_K
    ng = H_A // HP_A
    w = HP_A * DP_A

    def zspec(part):
        return pl.BlockSpec((seq, w), lambda b, g: (row_blk0 + b, part * ng + g))

    def state(*tail):
        return pl.BlockSpec((1, 2, HP_A) + tail, lambda b, g: (b, 0, g) + (0,) * len(tail))

    return pl.pallas_call(
        functools.partial(_mlstm_kernel, nc=nc),
        grid=(nb, ng),
        in_specs=[
            zspec(0), zspec(1), zspec(2), zspec(3),
            pl.BlockSpec((seq, LANES), lambda b, g: (row_blk0 + b, g)),
            pl.BlockSpec((1, 1, LANES), lambda b, g: (g, 0, 0)),
            pl.BlockSpec((HP_A, 1, DP_A), lambda b, g: (g, 0, 0)),
            state(DP_A, 2 * DP_A), state(1, 1),
        ],
        out_specs=[pl.BlockSpec((seq, w), lambda b, g: (b, g)), state(DP_A, 2 * DP_A), state(1, 1)],
        out_shape=[
            jax.ShapeDtypeStruct((nb * seq, Y_A_COLS), F32),
            jax.ShapeDtypeStruct((nb, 2, H_A, DP_A, 2 * DP_A), F32),
            jax.ShapeDtypeStruct((nb, 2, H_A, 1, 1), F32),
        ],
        scratch_shapes=[pltpu.VMEM((seq, w), F32), pltpu.VMEM((seq, w), F32)],
        compiler_params=_cparams("arbitrary", "arbitrary"),
        name="mlstm",
    )(za, za, za, za, zg, gbias, wn, s0, m0)


def _pair_attention(qp, kslabs, vaugs, masks, sink_col):
    return _pairs_attention([qp], [kslabs], [vaugs], [masks], [sink_col])[0]


def _pairs_attention(qps, kslabs, vaugs, masks, sink_cols):
    n_p = len(qps)
    m_rows = qps[0].shape[0]
    lo = lax.broadcasted_iota(jnp.int32, qps[0].shape, 1) < HD
    q2 = [jnp.concatenate([jnp.where(lo, qp, 0.0), jnp.where(lo, 0.0, qp)], axis=0).astype(BF16) for qp in qps]
    scores = []
    for i in range(len(kslabs[0])):
        tiles = []
        for p in range(n_p):
            s = jnp.dot(q2[p], kslabs[p][i], preferred_element_type=F32)
            mk = masks[p][i]
            if mk is not None:
                s = jnp.where(mk, s, NEG_INF) if mk.dtype == jnp.bool_ else s + mk
            tiles.append(s)
        scores.append(jnp.concatenate(tiles, axis=0))
    sink = None if sink_cols[0] is None else jnp.concatenate(sink_cols, axis=0)
    mx = scores[0].max(axis=1, keepdims=True)
    for s in scores[1:]:
        mx = jnp.maximum(mx, s.max(axis=1, keepdims=True))
    if sink is not None:
        mx = jnp.maximum(mx, sink)
    probs = [jnp.exp(s - mx).astype(BF16) for s in scores]
    acc = []
    for p in range(n_p):
        rows = slice(p * 2 * m_rows, (p + 1) * 2 * m_rows)
        pv = None
        for e, va in zip(probs, vaugs[p]):
            t = jnp.dot(e[rows], va, preferred_element_type=F32)
            pv = t if pv is None else pv + t
        acc.append(pv)
    acc = jnp.concatenate(acc, axis=0)
    den = acc[:, LANES:]
    if sink is not None:
        den = den + jnp.exp(sink - mx)
    o = acc[:, :LANES] / den
    return [jnp.where(lo, o[2 * p * m_rows:(2 * p + 1) * m_rows], o[(2 * p + 1) * m_rows:(2 * p + 2) * m_rows])
            for p in range(n_p)]


def _gqa_key_slabs(kt):
    a, b = kt[:HD], kt[HD:]
    return [jnp.concatenate([a, a], 0), kt, jnp.concatenate([b, b], 0)]


def _gqa_value_pairs(v):
    lo = lax.broadcasted_iota(jnp.int32, v.shape, 1) < HD
    sw = pltpu.roll(v, HD, 1)
    ones = jnp.ones_like(v)
    return [jnp.concatenate([x, ones], 1) for x in (jnp.where(lo, v, sw), v, jnp.where(lo, sw, v))]


def _sink_col(sink_ref, p, m_rows):
    row = lax.broadcasted_iota(jnp.int32, (2 * m_rows, 1), 0)
    return jnp.where(row < m_rows, sink_ref[0:1, 2 * p:2 * p + 1], sink_ref[0:1, 2 * p + 1:2 * p + 2])


def _ctx_attn_kernel(bq_ref, bk_ref, bv_ref, cq_ref, ck_ref, cv_ref, sink_ref, yb_ref, yc_ref):
    scale = HD ** -0.5
    pairs_b = range(D_B // LANES)
    pairs_c = range(D_C // LANES)

    def tile(p):
        return slice(p * LANES, (p + 1) * LANES)

    kslabs = _gqa_key_slabs(bk_ref[...].T)
    vpairs = _gqa_value_pairs(bv_ref[...])
    outs = _pairs_attention([bq_ref[:, tile(p)] * scale for p in pairs_b], [[kslabs[p].astype(BF16)] for p in pairs_b],
                            [[vpairs[p].astype(BF16)] for p in pairs_b], [[None] for _ in pairs_b],
                            [_sink_col(sink_ref, p, L_CTX) for p in pairs_b])
    for p in pairs_b:
        yb_ref[:, tile(p)] = outs[p]
    ckt = ck_ref[...].T
    ones = jnp.ones((L_CTX, LANES), F32)
    outs = _pairs_attention([cq_ref[:, tile(p)] * scale for p in pairs_c], [[ckt[tile(p)].astype(BF16)] for p in pairs_c],
                            [[jnp.concatenate([cv_ref[:, tile(p)], ones], 1).astype(BF16)] for p in pairs_c],
                            [[None] for _ in pairs_c], [None for _ in pairs_c])
    for p in pairs_c:
        yc_ref[:, tile(p)] = outs[p]


def _ctx_attn_call(bq, bk, bv, cq, ck, cv, sink):
    def spec(n):
        return pl.BlockSpec((L_CTX, n), lambda b: (b, 0))

    return pl.pallas_call(
        _ctx_attn_kernel,
        grid=(B_CTX,),
        in_specs=[spec(D_B), spec(D_KVB), spec(D_KVB), spec(D_C), spec(D_C), spec(D_C),
                  pl.BlockSpec((1, 8), lambda b: (0, 0))],
        out_specs=[spec(D_B), spec(D_C)],
        out_shape=[jax.ShapeDtypeStruct((T_CTX, D_B), F32), jax.ShapeDtypeStruct((T_CTX, D_C), F32)],
        compiler_params=_cparams("arbitrary"),
        name="ctx_attn",
    )(bq, bk, bv, cq, ck, cv, sink)


def _rope(x, cos, sin_signed, first):
    rot = jnp.where(first, pltpu.roll(x, LANES - 16, 1), pltpu.roll(x, 16, 1))
    return x * cos + rot * sin_signed


def _win_attn_kernel(q_ref, k_ref, v_ref, kc_ref, vc_ref, cos_ref, sin_ref, sink_ref, y_ref,
                     kpt_scr, va_scr, kcp_scr, vca_scr):
    scale = HD ** -0.5
    n_pairs = D_B // LANES
    nblk = N_LAT // WIN
    nband = 3
    lane = lax.broadcasted_iota(jnp.int32, (1, LANES), 1)
    first = (lane % 32) < 16
    for p, (ks, va) in enumerate(zip(_gqa_key_slabs(kc_ref[0, 0].T), _gqa_value_pairs(vc_ref[0, 0]))):
        kcp_scr[p] = ks.astype(BF16)
        vca_scr[p] = va.astype(BF16)

    def prep(blk, c):
        r0 = pl.multiple_of(blk * WIN, WIN)
        kr = _rope(k_ref[pl.ds(r0, WIN), :], cos_ref[pl.ds(r0, WIN), :], sin_ref[pl.ds(r0, WIN), :], first)
        for p, (ks, va) in enumerate(zip(_gqa_key_slabs(kr.T), _gqa_value_pairs(v_ref[pl.ds(r0, WIN), :]))):
            kpt_scr[p, blk] = ks.astype(BF16)
            va_scr[p, pl.ds(r0, WIN), :] = va.astype(BF16)
        return c

    lax.fori_loop(0, nblk, prep, 0)

    def body(blk, c):
        q0 = pl.multiple_of(blk * WIN, WIN)
        sb = jnp.clip(blk - 1, 0, nblk - nband)
        k0 = pl.multiple_of(sb * WIN, WIN)
        cos = cos_ref[pl.ds(q0, WIN), :]
        sin = sin_ref[pl.ds(q0, WIN), :]
        row = lax.broadcasted_iota(jnp.int32, (2 * WIN, nband * WIN), 0)
        qpos = q0 + jnp.where(row < WIN, row, row - WIN)
        kpos = k0 + lax.broadcasted_iota(jnp.int32, (2 * WIN, nband * WIN), 1)
        mask = jnp.abs(kpos - qpos) <= WIN
        pairs = range(n_pairs)
        qps = [_rope(q_ref[pl.ds(q0, WIN), p * LANES:(p + 1) * LANES], cos, sin, first) * scale for p in pairs]
        k_loc = [jnp.concatenate([kpt_scr[p, sb + j] for j in range(nband)], axis=1) for p in pairs]
        v_loc = [va_scr[p, pl.ds(k0, nband * WIN), :] for p in pairs]
        outs = _pairs_attention(qps, [[kcp_scr[p], k_loc[p]] for p in pairs], [[vca_scr[p], v_loc[p]] for p in pairs],
                                [[None, mask] for _ in pairs], [_sink_col(sink_ref, p, WIN) for p in pairs])
        for p in pairs:
            y_ref[pl.ds(q0, WIN), p * LANES:(p + 1) * LANES] = outs[p]
        return c

    lax.fori_loop(0, nblk, body, 0)


def _win_attn_call(bq, bk, bv, cache_k, cache_v, cos, sin, sink, layer):
    rb0 = T_CTX // N_LAT

    def spec(n):
        return pl.BlockSpec((N_LAT, n), lambda b: (rb0 + b, 0))

    cache = pl.BlockSpec((1, 1, PAST_LEN, D_KVB), lambda b: (b, layer, 0, 0))
    tab = pl.BlockSpec((N_LAT, LANES), lambda b: (0, 0))
    return pl.pallas_call(
        _win_attn_kernel,
        grid=(B_LAT,),
        in_specs=[spec(D_B), spec(D_KVB), spec(D_KVB), cache, cache, tab, tab,
                  pl.BlockSpec((1, 8), lambda b: (0, 0))],
        out_specs=pl.BlockSpec((N_LAT, D_B), lambda b: (b, 0)),
        out_shape=jax.ShapeDtypeStruct((T_LAT, D_B), F32),
        scratch_shapes=[pltpu.VMEM((D_B // LANES, N_LAT // WIN, LANES, WIN), BF16),
                        pltpu.VMEM((D_B // LANES, N_LAT, 2 * LANES), BF16),
                        pltpu.VMEM((D_B // LANES, LANES, PAST_LEN), BF16),
                        pltpu.VMEM((D_B // LANES, PAST_LEN, 2 * LANES), BF16)],
        compiler_params=_cparams("arbitrary"),
        name="win_attn",
    )(bq, bk, bv, cache_k, cache_v, cos, sin, sink)


NA_BLK = LANES
NA_SPAN_BLKS = NA_WIN_H * GRID_W // NA_BLK + 1
NA_SPAN_ROWS = NA_SPAN_BLKS * NA_BLK // GRID_W
NA_VARIANTS = ((7, 0), (6, 0), (5, 0), (4, 0), (3, 0), (3, 1), (3, 2), (2, 2), (1, 2), (0, 2))


def _na_attn_kernel(q_ref, k_ref, v_ref, kc_ref, vc_ref, bias_ref, y_ref, kpt_scr, va_scr, kcp_scr, vca_scr):
    scale = HD ** -0.5
    rows = N_LAT // GRID_W
    n_pairs = D_C // LANES
    nblk = N_LAT // NA_BLK
    half = NA_WIN_H // 2
    kct = kc_ref[0, 0].T
    for p in range(n_pairs):
        cols = slice(p * LANES, (p + 1) * LANES)
        kcp_scr[p] = kct[cols].astype(BF16)
        vca_scr[p] = jnp.concatenate([vc_ref[0, 0, :, cols], jnp.ones((PAST_LEN, LANES), F32)], 1).astype(BF16)

    def prep(blk, c):
        r0 = pl.multiple_of(blk * NA_BLK, NA_BLK)
        kt = k_ref[pl.ds(r0, NA_BLK), :].T
        for p in range(n_pairs):
            cols = slice(p * LANES, (p + 1) * LANES)
            kpt_scr[p, blk] = kt[cols].astype(BF16)
            va_scr[p, pl.ds(r0, NA_BLK), :] = jnp.concatenate(
                [v_ref[pl.ds(r0, NA_BLK), cols], jnp.ones((NA_BLK, LANES), F32)], 1).astype(BF16)
        return c

    lax.fori_loop(0, nblk, prep, 0)

    def body(r, c):
        kr0 = jnp.clip(r - half, 0, rows - NA_WIN_H)
        sb = jnp.minimum(kr0 // 2, nblk - NA_SPAN_BLKS)
        var = jnp.where(r <= half, r, jnp.where(r >= rows - half, r - (rows - 2 * half - 2), half + (kr0 & 1)))
        q0 = pl.multiple_of(r * GRID_W, GRID_W)
        k0 = pl.multiple_of(sb * NA_BLK, NA_BLK)
        pairs = range(n_pairs)
        qps = [q_ref[pl.ds(q0, GRID_W), p * LANES:(p + 1) * LANES] * scale for p in pairs]
        k_loc = [jnp.concatenate([kpt_scr[p, sb + j] for j in range(NA_SPAN_BLKS)], axis=1) for p in pairs]
        v_loc = [va_scr[p, pl.ds(k0, NA_SPAN_BLKS * NA_BLK), :] for p in pairs]
        outs = _pairs_attention(qps, [[kcp_scr[p], k_loc[p]] for p in pairs], [[vca_scr[p], v_loc[p]] for p in pairs],
                                [[None, bias_ref[0, p, var]] for p in pairs], [None for _ in pairs])
        for p in pairs:
            y_ref[pl.ds(q0, GRID_W), p * LANES:(p + 1) * LANES] = outs[p]
        return c

    lax.fori_loop(0, rows, body, 0)


def _na_attn_call(cq, ck, cv, cache_k, cache_v, bias_tab, layer):
    rb0 = T_CTX // N_LAT
    n_pairs = D_C // LANES
    spec = pl.BlockSpec((N_LAT, D_C), lambda b: (rb0 + b, 0))
    cache = pl.BlockSpec((1, 1, PAST_LEN, D_C), lambda b: (b, layer, 0, 0))
    span = NA_SPAN_BLKS * NA_BLK
    return pl.pallas_call(
        _na_attn_kernel,
        grid=(B_LAT,),
        in_specs=[spec, spec, spec, cache, cache,
                  pl.BlockSpec((1, n_pairs, len(NA_VARIANTS), 2 * GRID_W, span), lambda b: (layer, 0, 0, 0, 0))],
        out_specs=pl.BlockSpec((N_LAT, D_C), lambda b: (b, 0)),
        out_shape=jax.ShapeDtypeStruct((T_LAT, D_C), F32),
        scratch_shapes=[pltpu.VMEM((n_pairs, N_LAT // NA_BLK, LANES, NA_BLK), BF16),
                        pltpu.VMEM((n_pairs, N_LAT, 2 * LANES), BF16),
                        pltpu.VMEM((n_pairs, LANES, PAST_LEN), BF16),
                        pltpu.VMEM((n_pairs, PAST_LEN, 2 * LANES), BF16)],
        compiler_params=_cparams("arbitrary"),
        name="na_attn",
    )(cq, ck, cv, cache_k, cache_v, bias_tab)


def _outproj_kernel(yac_ref, ybc_ref, ycc_ref, yal_ref, ybl_ref, ycl_ref, x_ref, ga_ref, scf_ref, shf_ref,
                    lng_ref, lnb_ref, wo_ref, wr_ref, x1_ref, h2_ref, route_ref, cnt_ref, run_scr):
    @pl.when(pl.program_id(0) == 0)
    def _():
        run_scr[...] = jnp.zeros_like(run_scr)

    is_ctx = pl.program_id(0) < T_CTX // TM

    def pick(c_ref, l_ref):
        return jnp.where(is_ctx, c_ref[...], l_ref[...])

    y = (_dot(pick(yac_ref, yal_ref), wo_ref[0, 0:Y_A_COLS, :])
         + _dot(pick(ybc_ref, ybl_ref), wo_ref[0, Y_A_COLS:Y_A_COLS + D_B, :])
         + _dot(pick(ycc_ref, ycl_ref), wo_ref[0, Y_A_COLS + D_B:, :]))
    x1 = _layer_norm(ALPHA * x_ref[...] + ga_ref[0, 0, 0] * y, lng_ref[...], lnb_ref[...])
    x1_ref[...] = x1
    h2 = x1 * (1.0 + scf_ref[0, 0, 0]) + shf_ref[0, 0, 0]
    h2_ref[...] = h2
    w_r = wr_ref[...]
    w_hi = w_r.astype(BF16)
    w_lo = (w_r - w_hi.astype(F32)).astype(BF16)
    h_hi = h2.astype(BF16)
    h_lo = (h2 - h_hi.astype(F32)).astype(BF16)
    p_hi = jnp.dot(h_hi, jnp.concatenate([w_hi, w_lo], axis=1), preferred_element_type=F32)
    logits = p_hi[:, :LANES] + p_hi[:, LANES:] + jnp.dot(h_lo, w_hi, preferred_element_type=F32)
    lane = lax.broadcasted_iota(jnp.int32, logits.shape, 1)
    lanef = lane.astype(F32)
    big = float(LANES)
    lg = jnp.where((lane >= N_EXPERTS) & (lane < N_EXPERTS + N_GROUPS), logits, NEG_INF)
    mg = jnp.max(lg, axis=1, keepdims=True)
    grp = jnp.min(jnp.where(lg == mg, lanef, big), axis=1, keepdims=True) - float(N_EXPERTS)
    g_w = 1.0 / jnp.sum(jnp.exp(lg - mg), axis=1, keepdims=True)
    in_grp = (lane < N_EXPERTS) & ((lane // E_PER_GROUP).astype(F32) == grp)
    le = jnp.where(in_grp, logits, NEG_INF)
    l1 = jnp.max(le, axis=1, keepdims=True)
    i1 = jnp.min(jnp.where(le == l1, lanef, big), axis=1, keepdims=True)
    le2 = jnp.where(lanef == i1, NEG_INF, le)
    l2 = jnp.max(le2, axis=1, keepdims=True)
    i2 = jnp.min(jnp.where(le2 == l2, lanef, big), axis=1, keepdims=True)
    e2 = jnp.exp(l2 - l1)
    w1 = g_w / (1.0 + e2)
    w2 = g_w * e2 / (1.0 + e2)
    oh1 = jnp.where(lanef == i1, 1.0, 0.0)
    oh2 = jnp.where(lanef == i2, 1.0, 0.0)
    rt = lax.broadcasted_iota(jnp.int32, (TM, TM), 0)
    ct = lax.broadcasted_iota(jnp.int32, (TM, TM), 1)
    before = jnp.where(ct < rt, 1.0, 0.0)
    run = run_scr[...]
    tot1 = jnp.sum(oh1, axis=0, keepdims=True)
    r1 = jnp.sum(oh1 * (run + _dot(before, oh1)), axis=1, keepdims=True)
    r2 = jnp.sum(oh2 * (run + tot1 + _dot(before, oh2)), axis=1, keepdims=True)
    run = run + tot1 + jnp.sum(oh2, axis=0, keepdims=True)
    run_scr[...] = run
    cnt_ref[...] = run
    vals = (i1, i2, w1, w2, r1, r2)
    out = jnp.zeros_like(logits)
    for n, v in enumerate(vals):
        out = jnp.where(lane == n, v, out)
    route_ref[...] = out[:, :8]


def _outproj_call(y_ctx, y_lat, x, mod_t, layer, lng, lnb, wo, wr):
    row_vec = pl.BlockSpec((1, D_MODEL), lambda i: (0, 0))
    n_ctx = T_CTX // TM

    def tok(n):
        return pl.BlockSpec((TM, n), lambda i: (i, 0))

    def tok_ctx(n):
        return pl.BlockSpec((TM, n), lambda i: (jnp.minimum(i, n_ctx - 1), 0))

    def tok_lat(n):
        return pl.BlockSpec((TM, n), lambda i: (jnp.maximum(i - n_ctx, 0), 0))

    return pl.pallas_call(
        _outproj_kernel,
        grid=(N_TILES,),
        in_specs=[tok_ctx(Y_A_COLS), tok_ctx(D_B), tok_ctx(D_C), tok_lat(Y_A_COLS), tok_lat(D_B), tok_lat(D_C),
                  tok(D_MODEL), _mod_spec(layer, 2), _mod_spec(layer, 4), _mod_spec(layer, 3), row_vec, row_vec,
                  pl.BlockSpec((1, Y_A_COLS + D_B + D_C, D_MODEL), lambda i: (layer, 0, 0)),
                  pl.BlockSpec((D_MODEL, LANES), lambda i: (0, 0))],
        out_specs=[tok(D_MODEL), tok(D_MODEL), tok(8), pl.BlockSpec((1, LANES), lambda i: (0, 0))],
        out_shape=[jax.ShapeDtypeStruct((T_ALL, D_MODEL), F32), jax.ShapeDtypeStruct((T_ALL, D_MODEL), F32),
                   jax.ShapeDtypeStruct((T_ALL, 8), F32), jax.ShapeDtypeStruct((1, LANES), F32)],
        scratch_shapes=[pltpu.VMEM((1, LANES), F32)],
        compiler_params=_cparams("arbitrary"),
        name="outproj_router",
    )(*y_ctx, *y_lat, x, mod_t, mod_t, mod_t, lng, lnb, wo, wr)


def _dispatch_kernel(pos_ref, pend_ref, h_ref, xs_hbm, zero_scr, sem):
    base = pl.program_id(0) * TM

    @pl.when(pl.program_id(0) == 0)
    def _():
        zero_scr[...] = jnp.zeros_like(zero_scr)

        def fill(e, op):
            prev = pend_ref[e - 1] if e else 0

            @pl.when(pend_ref[e] > prev)
            def _():
                first = pl.multiple_of(pend_ref[e] - MOE_BLK, MOE_BLK)
                op(pltpu.make_async_copy(zero_scr, xs_hbm.at[pl.ds(first, MOE_BLK), :], sem))

        def tail_copy(b):
            first = pl.multiple_of(b * MOE_BLK, MOE_BLK)
            return pltpu.make_async_copy(zero_scr, xs_hbm.at[pl.ds(first, MOE_BLK), :], sem)

        def tail_start(b, c):
            tail_copy(b).start()
            return c

        def tail_wait(b, c):
            tail_copy(b).wait()
            return c

        n_blocks = xs_hbm.shape[0] // MOE_BLK
        first_free = pend_ref[N_EXPERTS - 1] // MOE_BLK
        for e in range(N_EXPERTS):
            fill(e, lambda cp: cp.start())
        lax.fori_loop(first_free, n_blocks, tail_start, 0)
        for e in range(N_EXPERTS):
            fill(e, lambda cp: cp.wait())
        lax.fori_loop(first_free, n_blocks, tail_wait, 0)

    def row(t, p):
        return pltpu.make_async_copy(h_ref.at[pl.ds(t, 1), :], xs_hbm.at[pl.ds(p, 1), :], sem)

    def issue(t, c):
        row(t, pos_ref[base + t]).start(priority=0)
        row(t, pos_ref[T_ALL + base + t]).start(priority=1)
        return c

    lax.fori_loop(0, TM, issue, 0, unroll=32)
    whole = pltpu.make_async_copy(h_ref, xs_hbm.at[pl.ds(0, TM), :], sem)
    whole.wait()
    whole.wait()


def _dispatch_call(pos2, p_end, h2, nblk):
    grid_spec = pltpu.PrefetchScalarGridSpec(
        num_scalar_prefetch=2,
        grid=(N_TILES,),
        in_specs=[pl.BlockSpec((TM, D_MODEL), lambda i, p, pe: (i, 0))],
        out_specs=pl.BlockSpec(memory_space=pl.ANY),
        scratch_shapes=[pltpu.VMEM((MOE_BLK, D_MODEL), F32), pltpu.SemaphoreType.DMA(())],
    )
    return pl.pallas_call(
        _dispatch_kernel,
        grid_spec=grid_spec,
        out_shape=jax.ShapeDtypeStruct((nblk * MOE_BLK, D_MODEL), F32),
        compiler_params=_cparams("arbitrary"),
        name="dispatch",
    )(pos2, p_end, h2)


def _expert_kernel(blk_e_ref, nused_ref, xs_ref, wg_ref, wu_ref, wd_ref, out_ref):
    j = pl.program_id(0)

    @pl.when(j < nused_ref[0])
    def _():
        xb = xs_ref[...].astype(BF16)
        g = jnp.dot(xb, wg_ref[0, 0].astype(BF16), preferred_element_type=F32)
        u = jnp.dot(xb, wu_ref[0, 0].astype(BF16), preferred_element_type=F32)
        out_ref[...] = _dot(jax.nn.silu(g) * u, wd_ref[0, 0])

    @pl.when(j >= nused_ref[0])
    def _():
        out_ref[...] = jnp.zeros_like(out_ref)


def _expert_call(blk_e, nused, xs, wg, wu, wd, layer, nblk):
    def wspec(r, c):
        return pl.BlockSpec((1, 1, r, c), lambda j, be, nu: (layer, be[j], 0, 0))

    grid_spec = pltpu.PrefetchScalarGridSpec(
        num_scalar_prefetch=2,
        grid=(nblk,),
        in_specs=[
            pl.BlockSpec((MOE_BLK, D_MODEL), lambda j, be, nu: (jnp.clip(j, 0, jnp.maximum(nu[0] - 1, 0)), 0)),
            wspec(D_MODEL, D_EXPERT), wspec(D_MODEL, D_EXPERT), wspec(D_EXPERT, D_MODEL),
        ],
        out_specs=pl.BlockSpec((MOE_BLK, D_MODEL), lambda j, be, nu: (j, 0)),
    )
    return pl.pallas_call(
        _expert_kernel,
        grid_spec=grid_spec,
        out_shape=jax.ShapeDtypeStruct((nblk * MOE_BLK, D_MODEL), F32),
        compiler_params=_cparams("arbitrary"),
        name="experts",
    )(blk_e, nused, xs, wg, wu, wd)


def _combine_kernel(pos_ref, eo_hbm, x1_ref, route_ref, gf_ref, lng_ref, lnb_ref, x2_ref, buf, sem):
    base = pl.program_id(0) * TM

    def row(p, r, t):
        return pltpu.make_async_copy(eo_hbm.at[pl.ds(p, 1), :], buf.at[r, pl.ds(t, 1), :], sem)

    def issue(t, c):
        row(pos_ref[base + t], 0, t).start(priority=0)
        row(pos_ref[T_ALL + base + t], 1, t).start(priority=1)
        return c

    lax.fori_loop(0, TM, issue, 0, unroll=32)
    for r in range(2):
        pltpu.make_async_copy(eo_hbm.at[pl.ds(0, TM), :], buf.at[r], sem).wait()
    route = route_ref[...]
    y = route[:, 2:3] * buf[0] + route[:, 3:4] * buf[1]
    x2_ref[...] = _layer_norm(ALPHA * x1_ref[...] + gf_ref[0, 0, 0] * y, lng_ref[...], lnb_ref[...])


def _combine_call(pos2, eo, x1, route, mod_t, layer, lng, lnb):
    grid_spec = pltpu.PrefetchScalarGridSpec(
        num_scalar_prefetch=1,
        grid=(N_TILES,),
        in_specs=[
            pl.BlockSpec(memory_space=pl.ANY),
            pl.BlockSpec((TM, D_MODEL), lambda i, p: (i, 0)),
            pl.BlockSpec((TM, 8), lambda i, p: (i, 0)),
            _mod_spec(layer, 5),
            pl.BlockSpec((1, D_MODEL), lambda i, p: (0, 0)),
            pl.BlockSpec((1, D_MODEL), lambda i, p: (0, 0)),
        ],
        out_specs=pl.BlockSpec((TM, D_MODEL), lambda i, p: (i, 0)),
        scratch_shapes=[pltpu.VMEM((2, TM, D_MODEL), F32), pltpu.SemaphoreType.DMA(())],
    )
    return pl.pallas_call(
        _combine_kernel,
        grid_spec=grid_spec,
        out_shape=jax.ShapeDtypeStruct((T_ALL, D_MODEL), F32),
        compiler_params=_cparams("arbitrary"),
        name="combine_norm",
    )(pos2, eo, x1, route, mod_t, lng, lnb)


def _gate_cols(grp):
    return np.array([kind * 2 * H_A + d * H_A + grp * HP_A + j
                     for j in range(HP_A) for kind in range(2) for d in range(2)])


def _prep_w_in(w_in):
    a = w_in[..., :4 * D_A].reshape(DEPTH, D_MODEL, 4, H_A, DH_A)
    a = jnp.pad(a, ((0, 0), (0, 0), (0, 0), (0, 0), (0, DP_A - DH_A))).reshape(DEPTH, D_MODEL, ZA_COLS)
    gates = w_in[..., 4 * D_A:4 * D_A + 4 * H_A]
    g = jnp.concatenate([jnp.pad(gates[..., _gate_cols(grp)], ((0, 0), (0, 0), (0, LANES - 4 * HP_A)))
                         for grp in range(H_A // HP_A)], -1)
    rest = w_in[..., 4 * D_A + 4 * H_A:]
    return jnp.concatenate([a, rest, g], -1).astype(BF16)


def _prep_w_out(w_out):
    a = w_out[:, :D_A].reshape(DEPTH, H_A, DH_A, D_MODEL)
    a = jnp.pad(a, ((0, 0), (0, 0), (0, DP_A - DH_A), (0, 0))).reshape(DEPTH, Y_A_COLS, D_MODEL)
    return jnp.concatenate([a, w_out[:, D_A:]], 1).astype(BF16)


def _rope_tables():
    t = np.arange(N_LAT)
    nf = HD // 4
    inv = ROPE_BASE ** (-np.arange(nf, dtype=np.float32) / nf)
    ar = (t // GRID_W).astype(np.float32)[:, None] * inv
    ac = (t % GRID_W).astype(np.float32)[:, None] * inv
    ang = jnp.asarray(np.concatenate([ar, ar, ac, ac], -1), F32)
    cos, sin = jnp.cos(ang), jnp.sin(ang)
    sign = np.where((np.arange(HD) % 32) < 16, -1.0, 1.0).astype(np.float32)
    reps = LANES // HD
    return jnp.tile(cos, (1, reps)), jnp.tile(sin * sign, (1, reps))


def _na_bias_tables(rpb):
    qcol = np.arange(GRID_W)[:, None]
    kcol = np.arange(GRID_W)[None, :]
    dc = np.clip(kcol - qcol, 1 - NA_WIN_W, NA_WIN_W - 1) + NA_WIN_W - 1
    wstart = np.clip(qcol - NA_WIN_W // 2, 0, GRID_W - NA_WIN_W)
    in_win = (kcol >= wstart) & (kcol < wstart + NA_WIN_W)
    sel = (np.arange(2 * NA_WIN_W - 1)[:, None] == dc.reshape(1, -1)).astype(np.float32)
    cols = jnp.einsum("lhrd,dn->lhrn", rpb, jnp.asarray(sel), precision=lax.Precision.HIGHEST)
    cols = jnp.where(in_win.reshape(-1), cols, NEG_INF).reshape(DEPTH, H_C, 2 * NA_WIN_H - 1, GRID_W, GRID_W)
    outside = jnp.full((DEPTH, H_C, GRID_W, GRID_W), NEG_INF, F32)
    variants = []
    for dr0, off in NA_VARIANTS:
        span_rows = [cols[:, :, dr0 + i - off] if 0 <= i - off < NA_WIN_H else outside for i in range(NA_SPAN_ROWS)]
        variants.append(jnp.stack(span_rows, 3).reshape(DEPTH, H_C, GRID_W, NA_SPAN_ROWS * GRID_W))
    tab = jnp.stack(variants, 2).reshape(DEPTH, H_C // 2, 2, len(NA_VARIANTS), GRID_W, NA_SPAN_ROWS * GRID_W)
    return tab.transpose(0, 1, 3, 2, 4, 5).reshape(DEPTH, H_C // 2, len(NA_VARIANTS), 2 * GRID_W, NA_SPAN_ROWS * GRID_W)


def _dispatch_plan(route, counts):
    nblk = 2 * T_ALL // MOE_BLK + N_EXPERTS
    cnt = counts[0, :N_EXPERTS].astype(jnp.int32)
    padded = (cnt + MOE_BLK - 1) // MOE_BLK * MOE_BLK
    p_end = jnp.cumsum(padded)
    p_start = p_end - padded
    e = route[:, 0:2].astype(jnp.int32)
    hot = e[..., None] == jnp.arange(N_EXPERTS, dtype=jnp.int32)
    pos = jnp.sum(jnp.where(hot, p_start, 0), -1) + route[:, 4:6].astype(jnp.int32)
    pos2 = pos.T.reshape(-1)
    blk_first = jnp.arange(nblk, dtype=jnp.int32) * MOE_BLK
    blk_e = jnp.minimum(jnp.sum((p_end[None, :] <= blk_first[:, None]).astype(jnp.int32), axis=1), N_EXPERTS - 1)
    nused = p_end[-1:] // MOE_BLK
    return blk_e, nused, pos2, p_end, nblk


def kernel(x_prompt, x_sample, state_a_C, state_a_n, state_a_m, cache_b_k, cache_b_v, cache_c_k, cache_c_v, c, c_ctx, w_in, b_a_i, b_a_f, w_a_hnorm, b_sink, rpb, w_out, w_ada, b_ada, ln_g, ln_b, w_router_grp, w_router_exp, w_e_gate, w_e_up, w_e_down):
    w_in_p = _prep_w_in(w_in)
    w_out_p = _prep_w_out(w_out)
    w_r = jnp.pad(jnp.concatenate([w_router_exp, w_router_grp], -1),
                  ((0, 0), (0, 0), (0, LANES - N_EXPERTS - N_GROUPS)))
    wn_p = jnp.pad(w_a_hnorm.reshape(DEPTH, H_A, 1, DH_A), ((0, 0), (0, 0), (0, 0), (0, DP_A - DH_A)))
    gate_b = jnp.concatenate([b_a_i, b_a_f], 1).transpose(0, 2, 1)
    gate_b = jnp.pad(gate_b.reshape(DEPTH, H_A // HP_A, 1, 4 * HP_A), ((0, 0), (0, 0), (0, 0), (0, LANES - 4 * HP_A)))
    sink_p = jnp.pad(b_sink, ((0, 0), (0, 8 - H_B))).reshape(DEPTH, 1, 8)
    cos_t, sin_t = _rope_tables()
    na_bias = _na_bias_tables(rpb)
    cb_k = cache_b_k.reshape(B_LAT, DEPTH, PAST_LEN, D_KVB)
    cb_v = cache_b_v.reshape(B_LAT, DEPTH, PAST_LEN, D_KVB)
    cc_k = cache_c_k.reshape(B_LAT, DEPTH, PAST_LEN, D_C)
    cc_v = cache_c_v.reshape(B_LAT, DEPTH, PAST_LEN, D_C)
    pad_c = ((0, 0), (0, 0), (0, 0), (0, 0), (0, DP_A - DH_A), (0, DP_A - DH_A))
    st_ct = jnp.swapaxes(jnp.pad(state_a_C, pad_c), -1, -2)
    st_nr = jnp.broadcast_to(jnp.pad(state_a_n, pad_c[:-1])[..., None], st_ct.shape)
    st_s = jnp.concatenate([st_ct, st_nr], -1)
    st_m = state_a_m[..., None, None]
    z_s = jnp.zeros((B_CTX, 2, H_A, DP_A, 2 * DP_A), F32)
    z_m = jnp.zeros((B_CTX, 2, H_A, 1, 1), F32)

    cvec = jnp.concatenate([c, c_ctx[None, :], jnp.zeros((3, D_MODEL), F32)], 0)
    mod = _ada_call(cvec, w_ada, b_ada)
    tile_row = np.concatenate([np.full(T_CTX // TM, B_LAT), np.repeat(np.arange(B_LAT), N_LAT // TM)])
    mod_t = mod[:, tile_row].reshape(DEPTH, N_TILES, 6, 1, D_MODEL)

    x = jnp.concatenate([x_prompt.reshape(T_CTX, D_MODEL), x_sample.reshape(T_LAT, D_MODEL)], 0)
    cs_, ns_, ms_, kbs, vbs, kcs, vcs = [], [], [], [], [], [], []
    for l in range(DEPTH):
        za, bq, bk, bv, cq, ck, cv, zg = _inproj_call(x, mod_t, l, w_in_p)
        ya_c, s_l, m_l = _mlstm_call(za, zg, gate_b[l], wn_p[l], z_s, z_m, nb=B_CTX, seq=L_CTX, row_blk0=0)
        yb_c, yc_c = _ctx_attn_call(bq, bk, bv, cq, ck, cv, sink_p[l])
        ya_l, _, _ = _mlstm_call(za, zg, gate_b[l], wn_p[l], st_s[:, l], st_m[:, l],
                                 nb=B_LAT, seq=N_LAT, row_blk0=T_CTX // N_LAT)
        yb_l = _win_attn_call(bq, bk, bv, cb_k, cb_v, cos_t, sin_t, sink_p[l], l)
        yc_l = _na_attn_call(cq, ck, cv, cc_k, cc_v, na_bias, l)
        x1, h2, route, counts = _outproj_call((ya_c, yb_c, yc_c), (ya_l, yb_l, yc_l), x, mod_t, l,
                                              ln_g[l, 0:1], ln_b[l, 0:1], w_out_p, w_r[l])
        blk_e, nused, pos2, p_end, nblk = _dispatch_plan(route, counts)
        xs = _dispatch_call(pos2, p_end, h2, nblk)
        eo = _expert_call(blk_e, nused, xs, w_e_gate, w_e_up, w_e_down, l, nblk)
        x = _combine_call(pos2, eo, x1, route, mod_t, l, ln_g[l, 1:2], ln_b[l, 1:2])
        cs_.append(jnp.swapaxes(s_l[..., :DH_A, :DH_A], -1, -2))
        ns_.append(s_l[..., :DH_A, DP_A])
        ms_.append(m_l.reshape(B_CTX, 2, H_A))
        kbs.append(bk[:T_CTX].reshape(B_CTX, L_CTX, KV_B, HD))
        vbs.append(bv[:T_CTX].reshape(B_CTX, L_CTX, KV_B, HD))
        kcs.append(ck[:T_CTX].reshape(B_CTX, L_CTX, H_C, HD))
        vcs.append(cv[:T_CTX].reshape(B_CTX, L_CTX, H_C, HD))
    y_prompt = x[:T_CTX].reshape(B_CTX, L_CTX, D_MODEL)
    y_sample = x[T_CTX:].reshape(B_LAT, N_LAT, D_MODEL)
    return (y_prompt, y_sample, jnp.stack(cs_, 1), jnp.stack(ns_, 1), jnp.stack(ms_, 1),
            jnp.stack(kbs, 1), jnp.stack(vbs, 1), jnp.stack(kcs, 1), jnp.stack(vcs, 1))
```

```python
import functools

import numpy as np
import jax
import jax.numpy as jnp
from jax import lax
from jax.experimental import pallas as pl
from jax.experimental.pallas import tpu as pltpu

F32 = jnp.float32
BF16 = jnp.bfloat16
NEG_INF = float("-inf")

D_MODEL = 1024
DEPTH = 4
B_CTX, L_CTX = 16, 256
B_LAT, N_LAT = 4, 2048
PAST_LEN = 512
GRID_W = 64
H_A, DH_A = 4, 96
D_A = H_A * DH_A
H_B, KV_B, HD = 6, 2, 64
G_B = H_B // KV_B
D_B, D_KVB = H_B * HD, KV_B * HD
WIN = 128
ROPE_BASE = 10000.0
H_C = 4
D_C = H_C * HD
NA_WIN_H, NA_WIN_W = 8, 16
N_GROUPS, E_PER_GROUP = 4, 8
N_EXPERTS = N_GROUPS * E_PER_GROUP
D_EXPERT = D_MODEL // 4
ALPHA = (2 * DEPTH) ** 0.25
LN_EPS = 1e-5

LANES = 128
DP_A = LANES
LC_K = LANES
TM = 256
MOE_BLK = 256
VMEM_LIMIT = 48 * 1024 * 1024

T_CTX = B_CTX * L_CTX
T_LAT = B_LAT * N_LAT
T_ALL = T_CTX + T_LAT
N_TILES = T_ALL // TM
ZA_COLS = 4 * H_A * DP_A
Y_A_COLS = H_A * DP_A
IN_SPLITS = (ZA_COLS, D_B, D_KVB, D_KVB, D_C, D_C, D_C, 2 * LANES)
IN_COLS = sum(IN_SPLITS)


def _cparams(*sem):
    return pltpu.CompilerParams(dimension_semantics=sem, vmem_limit_bytes=VMEM_LIMIT)


def _dot(a, b):
    return jnp.dot(a.astype(BF16), b.astype(BF16), preferred_element_type=F32)


def _dot_nt(a, b):
    return lax.dot_general(a.astype(BF16), b.astype(BF16), (((1,), (1,)), ((), ())), preferred_element_type=F32)


def _dot_tn(a, b):
    return lax.dot_general(a.astype(BF16), b.astype(BF16), (((0,), (0,)), ((), ())), preferred_element_type=F32)


def _layer_norm(v, g, b):
    mu = jnp.mean(v, -1, keepdims=True)
    var = jnp.mean(jnp.square(v - mu), -1, keepdims=True)
    return (v - mu) * lax.rsqrt(var + LN_EPS) * g + b


def _ada_kernel(c_ref, w_ref, b_ref, o_ref):
    s = jax.nn.silu(c_ref[...])
    o_ref[0] = _dot(s, w_ref[0]) + b_ref[0]


def _ada_call(cvec, w_ada, b_ada):
    nb = 6
    return pl.pallas_call(
        _ada_kernel,
        grid=(DEPTH, nb),
        in_specs=[
            pl.BlockSpec((8, D_MODEL), lambda l, j: (0, 0)),
            pl.BlockSpec((1, D_MODEL, D_MODEL), lambda l, j: (l, 0, j)),
            pl.BlockSpec((1, 1, D_MODEL), lambda l, j: (l, 0, j)),
        ],
        out_specs=pl.BlockSpec((1, 8, D_MODEL), lambda l, j: (l, 0, j)),
        out_shape=jax.ShapeDtypeStruct((DEPTH, 8, 6 * D_MODEL), F32),
        compiler_params=_cparams("arbitrary", "arbitrary"),
        name="adaln",
    )(cvec, w_ada, b_ada.reshape(DEPTH, 1, 6 * D_MODEL))


def _inproj_kernel(x_ref, sc_ref, sh_ref, w_ref, *out_refs):
    h = (x_ref[...] * (1.0 + sc_ref[0, 0, 0]) + sh_ref[0, 0, 0]).astype(BF16)
    off = 0
    for ref in out_refs:
        n = ref.shape[-1]
        ref[...] = jnp.dot(h, w_ref[0, :, off:off + n], preferred_element_type=F32)
        off += n


def _mod_spec(layer, which):
    return pl.BlockSpec((1, 1, 1, 1, D_MODEL), lambda i, *_: (layer, i, which, 0, 0))


def _inproj_call(x, mod_t, layer, w):
    return pl.pallas_call(
        _inproj_kernel,
        grid=(N_TILES,),
        in_specs=[
            pl.BlockSpec((TM, D_MODEL), lambda i: (i, 0)),
            _mod_spec(layer, 1), _mod_spec(layer, 0),
            pl.BlockSpec((1, D_MODEL, IN_COLS), lambda i: (layer, 0, 0)),
        ],
        out_specs=[pl.BlockSpec((TM, n), lambda i: (i, 0)) for n in IN_SPLITS],
        out_shape=[jax.ShapeDtypeStruct((T_ALL, n), F32) for n in IN_SPLITS],
        compiler_params=_cparams("arbitrary"),
        name="inproj",
    )(x, mod_t, mod_t, w)


HP_A = 2
N_CH = 2 * HP_A


def _mlstm_kernel(q_ref, k_ref, v_ref, o_ref, g_ref, gb_ref, wn_ref, s0_ref, m0_ref,
                  y_ref, so_ref, mo_ref, hf_scr, hb_scr, *, nc):
    lc = LC_K
    scale = DH_A ** -0.5
    chains = [(j, d) for j in range(HP_A) for d in range(2)]
    ti = lax.broadcasted_iota(jnp.int32, (lc, lc), 0)
    si = lax.broadcasted_iota(jnp.int32, (lc, lc), 1)
    lane_ok = lax.broadcasted_iota(jnp.int32, (1, DP_A), 1) < DH_A
    ones = jnp.ones((lc, DP_A), F32)

    def stack(parts):
        return jnp.concatenate(parts, axis=0)

    def rows_of(x, a):
        return a[x * lc:(x + 1) * lc]

    def spread(vals):
        return stack([jnp.broadcast_to(v, (lc, v.shape[1])) for v in vals])

    mask = stack([si <= ti if d == 0 else si >= ti for _, d in chains])
    mask_t = stack([si >= ti if d == 0 else si <= ti for _, d in chains])
    eye = stack([si == ti for _ in chains])

    def col_sums(a):
        return [jnp.sum(rows_of(x, a), axis=0, keepdims=True) for x in range(N_CH)]

    def body(i, carry):
        smats = carry[:N_CH]
        ms = carry[N_CH:]
        r0s = [pl.multiple_of((i if d == 0 else nc - 1 - i) * lc, lc) for _, d in chains]
        q, kt, v1, icol, fpre = [], [], [], [], []
        for (j, d), r0 in zip(chains, r0s):
            cols = slice(j * DP_A, (j + 1) * DP_A)
            q.append(q_ref[pl.ds(r0, lc), cols].astype(BF16))
            kt.append((k_ref[pl.ds(r0, lc), cols] * scale).T.astype(BF16))
            v1.append(jnp.concatenate([v_ref[pl.ds(r0, lc), cols], ones], axis=1))
            gz = g_ref[pl.ds(r0, lc), :] + gb_ref[0]
            icol.append(gz[:, 4 * j + d:4 * j + d + 1])
            fpre.append(gz[:, 4 * j + 2 + d:4 * j + 3 + d])
        i_col = stack(icol)
        f_col = jax.nn.log_sigmoid(stack(fpre))
        b_rows = col_sums(jnp.where(mask_t, f_col, 0.0))
        bls = col_sums(f_col)
        b_col = jnp.sum(jnp.where(eye, spread(b_rows), 0.0), axis=1, keepdims=True)
        a_rows = col_sums(jnp.where(eye, i_col - b_col, 0.0))
        a_sp = jnp.where(mask, spread(a_rows), NEG_INF)
        m_sp = spread(ms)
        gap = jnp.maximum(m_sp, jnp.max(a_sp, axis=1, keepdims=True))
        m_out = b_col + gap
        wmat = jnp.exp(a_sp - gap)
        sw = (stack([jnp.dot(q[x], kt[x], preferred_element_type=F32) for x in range(N_CH)]) * wmat).astype(BF16)
        sc_in = jnp.exp(m_sp - gap)
        tot = (stack([jnp.dot(rows_of(x, sw), v1[x].astype(BF16), preferred_element_type=F32) for x in range(N_CH)])
               + sc_in * stack([_dot(q[x], smats[x]) for x in range(N_CH)]))
        h = tot[:, :DP_A] / jnp.maximum(jnp.abs(tot[:, DP_A:]), jnp.exp(-m_out))
        for x, ((j, d), r0) in enumerate(zip(chains, r0s)):
            dst = hf_scr if d == 0 else hb_scr
            dst[pl.ds(r0, lc), j * DP_A:(j + 1) * DP_A] = rows_of(x, h)
        dec = spread(bls) - b_col + i_col
        m_new = [jnp.maximum(bls[x] + ms[x], jnp.max(rows_of(x, dec), axis=0, keepdims=True)) for x in range(N_CH)]
        wk = jnp.exp(dec - spread(m_new))
        s_new = [jnp.exp(bls[x] + ms[x] - m_new[x]) * smats[x] + _dot(kt[x], rows_of(x, wk) * v1[x])
                 for x in range(N_CH)]
        return tuple(s_new) + tuple(m_new)

    init = tuple(s0_ref[0, d, j] for j, d in chains) + tuple(m0_ref[0, d, j] for j, d in chains)
    final = lax.fori_loop(0, nc, body, init)
    for x, (j, d) in enumerate(chains):
        so_ref[0, d, j] = final[x]
        mo_ref[0, d, j] = final[N_CH + x]

    def finish(c, _):
        r0 = pl.multiple_of(c * lc, lc)
        for j in range(HP_A):
            cols = slice(j * DP_A, (j + 1) * DP_A)
            h = hf_scr[pl.ds(r0, lc), cols] + hb_scr[pl.ds(r0, lc), cols]
            mu = jnp.sum(h, axis=1, keepdims=True) * (1.0 / DH_A)
            dv = jnp.where(lane_ok, h - mu, 0.0)
            var = jnp.sum(dv * dv, axis=1, keepdims=True) * (1.0 / DH_A)
            hn = dv * lax.rsqrt(var + LN_EPS) * wn_ref[j]
            y_ref[pl.ds(r0, lc), cols] = jax.nn.sigmoid(o_ref[pl.ds(r0, lc), cols]) * hn
        return 0

    lax.fori_loop(0, nc, finish, 0)


def _mlstm_call(za, zg, gbias, wn, s0, m0, *, nb, seq, row_blk0):
    nc = seq // LC_K
    ng = H_A // HP_A
    w = HP_A * DP_A

    def zspec(part):
        return pl.BlockSpec((seq, w), lambda b, g: (row_blk0 + b, part * ng + g))

    def state(*tail):
        return pl.BlockSpec((1, 2, HP_A) + tail, lambda b, g: (b, 0, g) + (0,) * len(tail))

    return pl.pallas_call(
        functools.partial(_mlstm_kernel, nc=nc),
        grid=(nb, ng),
        in_specs=[
            zspec(0), zspec(1), zspec(2), zspec(3),
            pl.BlockSpec((seq, LANES), lambda b, g: (row_blk0 + b, g)),
            pl.BlockSpec((1, 1, LANES), lambda b, g: (g, 0, 0)),
            pl.BlockSpec((HP_A, 1, DP_A), lambda b, g: (g, 0, 0)),
            state(DP_A, 2 * DP_A), state(1, 1),
        ],
        out_specs=[pl.BlockSpec((seq, w), lambda b, g: (b, g)), state(DP_A, 2 * DP_A), state(1, 1)],
        out_shape=[
            jax.ShapeDtypeStruct((nb * seq, Y_A_COLS), F32),
            jax.ShapeDtypeStruct((nb, 2, H_A, DP_A, 2 * DP_A), F32),
            jax.ShapeDtypeStruct((nb, 2, H_A, 1, 1), F32),
        ],
        scratch_shapes=[pltpu.VMEM((seq, w), F32), pltpu.VMEM((seq, w), F32)],
        compiler_params=_cparams("arbitrary", "arbitrary"),
        name="mlstm",
    )(za, za, za, za, zg, gbias, wn, s0, m0)


def _pair_attention(qp, kslabs, vaugs, masks, sink_col):
    return _pairs_attention([qp], [kslabs], [vaugs], [masks], [sink_col])[0]


def _pairs_attention(qps, kslabs, vaugs, masks, sink_cols):
    n_p = len(qps)
    m_rows = qps[0].shape[0]
    lo = lax.broadcasted_iota(jnp.int32, qps[0].shape, 1) < HD
    q2 = [jnp.concatenate([jnp.where(lo, qp, 0.0), jnp.where(lo, 0.0, qp)], axis=0).astype(BF16) for qp in qps]
    scores = []
    for i in range(len(kslabs[0])):
        tiles = []
        for p in range(n_p):
            s = jnp.dot(q2[p], kslabs[p][i], preferred_element_type=F32)
            mk = masks[p][i]
            if mk is not None:
                s = jnp.where(mk, s, NEG_INF) if mk.dtype == jnp.bool_ else s + mk
            tiles.append(s)
        scores.append(jnp.concatenate(tiles, axis=0))
    sink = None if sink_cols[0] is None else jnp.concatenate(sink_cols, axis=0)
    mx = scores[0].max(axis=1, keepdims=True)
    for s in scores[1:]:
        mx = jnp.maximum(mx, s.max(axis=1, keepdims=True))
    if sink is not None:
        mx = jnp.maximum(mx, sink)
    probs = [jnp.exp(s - mx).astype(BF16) for s in scores]
    acc = []
    for p in range(n_p):
        rows = slice(p * 2 * m_rows, (p + 1) * 2 * m_rows)
        pv = None
        for e, va in zip(probs, vaugs[p]):
            t = jnp.dot(e[rows], va, preferred_element_type=F32)
            pv = t if pv is None else pv + t
        acc.append(pv)
    acc = jnp.concatenate(acc, axis=0)
    den = acc[:, LANES:]
    if sink is not None:
        den = den + jnp.exp(sink - mx)
    o = acc[:, :LANES] / den
    return [jnp.where(lo, o[2 * p * m_rows:(2 * p + 1) * m_rows], o[(2 * p + 1) * m_rows:(2 * p + 2) * m_rows])
            for p in range(n_p)]


def _gqa_key_slabs(kt):
    a, b = kt[:HD], kt[HD:]
    return [jnp.concatenate([a, a], 0), kt, jnp.concatenate([b, b], 0)]


def _gqa_value_pairs(v):
    lo = lax.broadcasted_iota(jnp.int32, v.shape, 1) < HD
    sw = pltpu.roll(v, HD, 1)
    ones = jnp.ones_like(v)
    return [jnp.concatenate([x, ones], 1) for x in (jnp.where(lo, v, sw), v, jnp.where(lo, sw, v))]


def _sink_col(sink_ref, p, m_rows):
    row = lax.broadcasted_iota(jnp.int32, (2 * m_rows, 1), 0)
    return jnp.where(row < m_rows, sink_ref[0:1, 2 * p:2 * p + 1], sink_ref[0:1, 2 * p + 1:2 * p + 2])


def _ctx_attn_kernel(bq_ref, bk_ref, bv_ref, cq_ref, ck_ref, cv_ref, sink_ref, yb_ref, yc_ref):
    scale = HD ** -0.5
    pairs_b = range(D_B // LANES)
    pairs_c = range(D_C // LANES)

    def tile(p):
        return slice(p * LANES, (p + 1) * LANES)

    kslabs = _gqa_key_slabs(bk_ref[...].T)
    vpairs = _gqa_value_pairs(bv_ref[...])
    outs = _pairs_attention([bq_ref[:, tile(p)] * scale for p in pairs_b], [[kslabs[p].astype(BF16)] for p in pairs_b],
                            [[vpairs[p].astype(BF16)] for p in pairs_b], [[None] for _ in pairs_b],
                            [_sink_col(sink_ref, p, L_CTX) for p in pairs_b])
    for p in pairs_b:
        yb_ref[:, tile(p)] = outs[p]
    ckt = ck_ref[...].T
    ones = jnp.ones((L_CTX, LANES), F32)
    outs = _pairs_attention([cq_ref[:, tile(p)] * scale for p in pairs_c], [[ckt[tile(p)].astype(BF16)] for p in pairs_c],
                            [[jnp.concatenate([cv_ref[:, tile(p)], ones], 1).astype(BF16)] for p in pairs_c],
                            [[None] for _ in pairs_c], [None for _ in pairs_c])
    for p in pairs_c:
        yc_ref[:, tile(p)] = outs[p]


def _ctx_attn_call(bq, bk, bv, cq, ck, cv, sink):
    def spec(n):
        return pl.BlockSpec((L_CTX, n), lambda b: (b, 0))

    return pl.pallas_call(
        _ctx_attn_kernel,
        grid=(B_CTX,),
        in_specs=[spec(D_B), spec(D_KVB), spec(D_KVB), spec(D_C), spec(D_C), spec(D_C),
                  pl.BlockSpec((1, 8), lambda b: (0, 0))],
        out_specs=[spec(D_B), spec(D_C)],
        out_shape=[jax.ShapeDtypeStruct((T_CTX, D_B), F32), jax.ShapeDtypeStruct((T_CTX, D_C), F32)],
        compiler_params=_cparams("arbitrary"),
        name="ctx_attn",
    )(bq, bk, bv, cq, ck, cv, sink)


def _rope(x, cos, sin_signed, first):
    rot = jnp.where(first, pltpu.roll(x, LANES - 16, 1), pltpu.roll(x, 16, 1))
    return x * cos + rot * sin_signed


def _win_attn_kernel(q_ref, k_ref, v_ref, kc_ref, vc_ref, cos_ref, sin_ref, sink_ref, y_ref,
                     kpt_scr, va_scr, kcp_scr, vca_scr):
    scale = HD ** -0.5
    n_pairs = D_B // LANES
    nblk = N_LAT // WIN
    nband = 3
    lane = lax.broadcasted_iota(jnp.int32, (1, LANES), 1)
    first = (lane % 32) < 16
    for p, (ks, va) in enumerate(zip(_gqa_key_slabs(kc_ref[0, 0].T), _gqa_value_pairs(vc_ref[0, 0]))):
        kcp_scr[p] = ks.astype(BF16)
        vca_scr[p] = va.astype(BF16)

    def prep(blk, c):
        r0 = pl.multiple_of(blk * WIN, WIN)
        kr = _rope(k_ref[pl.ds(r0, WIN), :], cos_ref[pl.ds(r0, WIN), :], sin_ref[pl.ds(r0, WIN), :], first)
        for p, (ks, va) in enumerate(zip(_gqa_key_slabs(kr.T), _gqa_value_pairs(v_ref[pl.ds(r0, WIN), :]))):
            kpt_scr[p, blk] = ks.astype(BF16)
            va_scr[p, pl.ds(r0, WIN), :] = va.astype(BF16)
        return c

    lax.fori_loop(0, nblk, prep, 0)

    def body(blk, c):
        q0 = pl.multiple_of(blk * WIN, WIN)
        sb = jnp.clip(blk - 1, 0, nblk - nband)
        k0 = pl.multiple_of(sb * WIN, WIN)
        cos = cos_ref[pl.ds(q0, WIN), :]
        sin = sin_ref[pl.ds(q0, WIN), :]
        row = lax.broadcasted_iota(jnp.int32, (2 * WIN, nband * WIN), 0)
        qpos = q0 + jnp.where(row < WIN, row, row - WIN)
        kpos = k0 + lax.broadcasted_iota(jnp.int32, (2 * WIN, nband * WIN), 1)
        mask = jnp.abs(kpos - qpos) <= WIN
        pairs = range(n_pairs)
        qps = [_rope(q_ref[pl.ds(q0, WIN), p * LANES:(p + 1) * LANES], cos, sin, first) * scale for p in pairs]
        k_loc = [jnp.concatenate([kpt_scr[p, sb + j] for j in range(nband)], axis=1) for p in pairs]
        v_loc = [va_scr[p, pl.ds(k0, nband * WIN), :] for p in pairs]
        outs = _pairs_attention(qps, [[kcp_scr[p], k_loc[p]] for p in pairs], [[vca_scr[p], v_loc[p]] for p in pairs],
                                [[None, mask] for _ in pairs], [_sink_col(sink_ref, p, WIN) for p in pairs])
        for p in pairs:
            y_ref[pl.ds(q0, WIN), p * LANES:(p + 1) * LANES] = outs[p]
        return c

    lax.fori_loop(0, nblk, body, 0)


def _win_attn_call(bq, bk, bv, cache_k, cache_v, cos, sin, sink, layer):
    rb0 = T_CTX // N_LAT

    def spec(n):
        return pl.BlockSpec((N_LAT, n), lambda b: (rb0 + b, 0))

    cache = pl.BlockSpec((1, 1, PAST_LEN, D_KVB), lambda b: (b, layer, 0, 0))
    tab = pl.BlockSpec((N_LAT, LANES), lambda b: (0, 0))
    return pl.pallas_call(
        _win_attn_kernel,
        grid=(B_LAT,),
        in_specs=[spec(D_B), spec(D_KVB), spec(D_KVB), cache, cache, tab, tab,
                  pl.BlockSpec((1, 8), lambda b: (0, 0))],
        out_specs=pl.BlockSpec((N_LAT, D_B), lambda b: (b, 0)),
        out_shape=jax.ShapeDtypeStruct((T_LAT, D_B), F32),
        scratch_shapes=[pltpu.VMEM((D_B // LANES, N_LAT // WIN, LANES, WIN), BF16),
                        pltpu.VMEM((D_B // LANES, N_LAT, 2 * LANES), BF16),
                        pltpu.VMEM((D_B // LANES, LANES, PAST_LEN), BF16),
                        pltpu.VMEM((D_B // LANES, PAST_LEN, 2 * LANES), BF16)],
        compiler_params=_cparams("arbitrary"),
        name="win_attn",
    )(bq, bk, bv, cache_k, cache_v, cos, sin, sink)


NA_BLK = LANES
NA_SPAN_BLKS = NA_WIN_H * GRID_W // NA_BLK + 1
NA_SPAN_ROWS = NA_SPAN_BLKS * NA_BLK // GRID_W
NA_VARIANTS = ((7, 0), (6, 0), (5, 0), (4, 0), (3, 0), (3, 1), (3, 2), (2, 2), (1, 2), (0, 2))


def _na_attn_kernel(q_ref, k_ref, v_ref, kc_ref, vc_ref, bias_ref, y_ref, kpt_scr, va_scr, kcp_scr, vca_scr):
    scale = HD ** -0.5
    rows = N_LAT // GRID_W
    n_pairs = D_C // LANES
    nblk = N_LAT // NA_BLK
    half = NA_WIN_H // 2
    kct = kc_ref[0, 0].T
    for p in range(n_pairs):
        cols = slice(p * LANES, (p + 1) * LANES)
        kcp_scr[p] = kct[cols].astype(BF16)
        vca_scr[p] = jnp.concatenate([vc_ref[0, 0, :, cols], jnp.ones((PAST_LEN, LANES), F32)], 1).astype(BF16)

    def prep(blk, c):
        r0 = pl.multiple_of(blk * NA_BLK, NA_BLK)
        kt = k_ref[pl.ds(r0, NA_BLK), :].T
        for p in range(n_pairs):
            cols = slice(p * LANES, (p + 1) * LANES)
            kpt_scr[p, blk] = kt[cols].astype(BF16)
            va_scr[p, pl.ds(r0, NA_BLK), :] = jnp.concatenate(
                [v_ref[pl.ds(r0, NA_BLK), cols], jnp.ones((NA_BLK, LANES), F32)], 1).astype(BF16)
        return c

    lax.fori_loop(0, nblk, prep, 0)

    def body(r, c):
        kr0 = jnp.clip(r - half, 0, rows - NA_WIN_H)
        sb = jnp.minimum(kr0 // 2, nblk - NA_SPAN_BLKS)
        var = jnp.where(r <= half, r, jnp.where(r >= rows - half, r - (rows - 2 * half - 2), half + (kr0 & 1)))
        q0 = pl.multiple_of(r * GRID_W, GRID_W)
        k0 = pl.multiple_of(sb * NA_BLK, NA_BLK)
        pairs = range(n_pairs)
        qps = [q_ref[pl.ds(q0, GRID_W), p * LANES:(p + 1) * LANES] * scale for p in pairs]
        k_loc = [jnp.concatenate([kpt_scr[p, sb + j] for j in range(NA_SPAN_BLKS)], axis=1) for p in pairs]
        v_loc = [va_scr[p, pl.ds(k0, NA_SPAN_BLKS * NA_BLK), :] for p in pairs]
        outs = _pairs_attention(qps, [[kcp_scr[p], k_loc[p]] for p in pairs], [[vca_scr[p], v_loc[p]] for p in pairs],
                                [[None, bias_ref[0, p, var]] for p in pairs], [None for _ in pairs])
        for p in pairs:
            y_ref[pl.ds(q0, GRID_W), p * LANES:(p + 1) * LANES] = outs[p]
        return c

    lax.fori_loop(0, rows, body, 0)


def _na_attn_call(cq, ck, cv, cache_k, cache_v, bias_tab, layer):
    rb0 = T_CTX // N_LAT
    n_pairs = D_C // LANES
    spec = pl.BlockSpec((N_LAT, D_C), lambda b: (rb0 + b, 0))
    cache = pl.BlockSpec((1, 1, PAST_LEN, D_C), lambda b: (b, layer, 0, 0))
    span = NA_SPAN_BLKS * NA_BLK
    return pl.pallas_call(
        _na_attn_kernel,
        grid=(B_LAT,),
        in_specs=[spec, spec, spec, cache, cache,
                  pl.BlockSpec((1, n_pairs, len(NA_VARIANTS), 2 * GRID_W, span), lambda b: (layer, 0, 0, 0, 0))],
        out_specs=pl.BlockSpec((N_LAT, D_C), lambda b: (b, 0)),
        out_shape=jax.ShapeDtypeStruct((T_LAT, D_C), F32),
        scratch_shapes=[pltpu.VMEM((n_pairs, N_LAT // NA_BLK, LANES, NA_BLK), BF16),
                        pltpu.VMEM((n_pairs, N_LAT, 2 * LANES), BF16),
                        pltpu.VMEM((n_pairs, LANES, PAST_LEN), BF16),
                        pltpu.VMEM((n_pairs, PAST_LEN, 2 * LANES), BF16)],
        compiler_params=_cparams("arbitrary"),
        name="na_attn",
    )(cq, ck, cv, cache_k, cache_v, bias_tab)


def _outproj_kernel(yac_ref, ybc_ref, ycc_ref, yal_ref, ybl_ref, ycl_ref, x_ref, ga_ref, scf_ref, shf_ref,
                    lng_ref, lnb_ref, wo_ref, wr_ref, x1_ref, h2_ref, route_ref, cnt_ref, run_scr):
    @pl.when(pl.program_id(0) == 0)
    def _():
        run_scr[...] = jnp.zeros_like(run_scr)

    is_ctx = pl.program_id(0) < T_CTX // TM

    def pick(c_ref, l_ref):
        return jnp.where(is_ctx, c_ref[...], l_ref[...])

    y = (_dot(pick(yac_ref, yal_ref), wo_ref[0, 0:Y_A_COLS, :])
         + _dot(pick(ybc_ref, ybl_ref), wo_ref[0, Y_A_COLS:Y_A_COLS + D_B, :])
         + _dot(pick(ycc_ref, ycl_ref), wo_ref[0, Y_A_COLS + D_B:, :]))
    x1 = _layer_norm(ALPHA * x_ref[...] + ga_ref[0, 0, 0] * y, lng_ref[...], lnb_ref[...])
    x1_ref[...] = x1
    h2 = x1 * (1.0 + scf_ref[0, 0, 0]) + shf_ref[0, 0, 0]
    h2_ref[...] = h2
    w_r = wr_ref[...]
    w_hi = w_r.astype(BF16)
    w_lo = (w_r - w_hi.astype(F32)).astype(BF16)
    h_hi = h2.astype(BF16)
    h_lo = (h2 - h_hi.astype(F32)).astype(BF16)
    p_hi = jnp.dot(h_hi, jnp.concatenate([w_hi, w_lo], axis=1), preferred_element_type=F32)
    logits = p_hi[:, :LANES] + p_hi[:, LANES:] + jnp.dot(h_lo, w_hi, preferred_element_type=F32)
    lane = lax.broadcasted_iota(jnp.int32, logits.shape, 1)
    lanef = lane.astype(F32)
    big = float(LANES)
    lg = jnp.where((lane >= N_EXPERTS) & (lane < N_EXPERTS + N_GROUPS), logits, NEG_INF)
    mg = jnp.max(lg, axis=1, keepdims=True)
    grp = jnp.min(jnp.where(lg == mg, lanef, big), axis=1, keepdims=True) - float(N_EXPERTS)
    g_w = 1.0 / jnp.sum(jnp.exp(lg - mg), axis=1, keepdims=True)
    in_grp = (lane < N_EXPERTS) & ((lane // E_PER_GROUP).astype(F32) == grp)
    le = jnp.where(in_grp, logits, NEG_INF)
    l1 = jnp.max(le, axis=1, keepdims=True)
    i1 = jnp.min(jnp.where(le == l1, lanef, big), axis=1, keepdims=True)
    le2 = jnp.where(lanef == i1, NEG_INF, le)
    l2 = jnp.max(le2, axis=1, keepdims=True)
    i2 = jnp.min(jnp.where(le2 == l2, lanef, big), axis=1, keepdims=True)
    e2 = jnp.exp(l2 - l1)
    w1 = g_w / (1.0 + e2)
    w2 = g_w * e2 / (1.0 + e2)
    oh1 = jnp.where(lanef == i1, 1.0, 0.0)
    oh2 = jnp.where(lanef == i2, 1.0, 0.0)
    rt = lax.broadcasted_iota(jnp.int32, (TM, TM), 0)
    ct = lax.broadcasted_iota(jnp.int32, (TM, TM), 1)
    before = jnp.where(ct < rt, 1.0, 0.0)
    run = run_scr[...]
    tot1 = jnp.sum(oh1, axis=0, keepdims=True)
    r1 = jnp.sum(oh1 * (run + _dot(before, oh1)), axis=1, keepdims=True)
    r2 = jnp.sum(oh2 * (run + tot1 + _dot(before, oh2)), axis=1, keepdims=True)
    run = run + tot1 + jnp.sum(oh2, axis=0, keepdims=True)
    run_scr[...] = run
    cnt_ref[...] = run
    vals = (i1, i2, w1, w2, r1, r2)
    out = jnp.zeros_like(logits)
    for n, v in enumerate(vals):
        out = jnp.where(lane == n, v, out)
    route_ref[...] = out[:, :8]


def _outproj_call(y_ctx, y_lat, x, mod_t, layer, lng, lnb, wo, wr):
    row_vec = pl.BlockSpec((1, D_MODEL), lambda i: (0, 0))
    n_ctx = T_CTX // TM

    def tok(n):
        return pl.BlockSpec((TM, n), lambda i: (i, 0))

    def tok_ctx(n):
        return pl.BlockSpec((TM, n), lambda i: (jnp.minimum(i, n_ctx - 1), 0))

    def tok_lat(n):
        return pl.BlockSpec((TM, n), lambda i: (jnp.maximum(i - n_ctx, 0), 0))

    return pl.pallas_call(
        _outproj_kernel,
        grid=(N_TILES,),
        in_specs=[tok_ctx(Y_A_COLS), tok_ctx(D_B), tok_ctx(D_C), tok_lat(Y_A_COLS), tok_lat(D_B), tok_lat(D_C),
                  tok(D_MODEL), _mod_spec(layer, 2), _mod_spec(layer, 4), _mod_spec(layer, 3), row_vec, row_vec,
                  pl.BlockSpec((1, Y_A_COLS + D_B + D_C, D_MODEL), lambda i: (layer, 0, 0)),
                  pl.BlockSpec((D_MODEL, LANES), lambda i: (0, 0))],
        out_specs=[tok(D_MODEL), tok(D_MODEL), tok(8), pl.BlockSpec((1, LANES), lambda i: (0, 0))],
        out_shape=[jax.ShapeDtypeStruct((T_ALL, D_MODEL), F32), jax.ShapeDtypeStruct((T_ALL, D_MODEL), F32),
                   jax.ShapeDtypeStruct((T_ALL, 8), F32), jax.ShapeDtypeStruct((1, LANES), F32)],
        scratch_shapes=[pltpu.VMEM((1, LANES), F32)],
        compiler_params=_cparams("arbitrary"),
        name="outproj_router",
    )(*y_ctx, *y_lat, x, mod_t, mod_t, mod_t, lng, lnb, wo, wr)


def _dispatch_kernel(pos_ref, pend_ref, h_ref, xs_hbm, zero_scr, sem):
    base = pl.program_id(0) * TM

    @pl.when(pl.program_id(0) == 0)
    def _():
        zero_scr[...] = jnp.zeros_like(zero_scr)

        def fill(e, op):
            prev = pend_ref[e - 1] if e else 0

            @pl.when(pend_ref[e] > prev)
            def _():
                first = pl.multiple_of(pend_ref[e] - MOE_BLK, MOE_BLK)
                op(pltpu.make_async_copy(zero_scr, xs_hbm.at[pl.ds(first, MOE_BLK), :], sem))

        def tail_copy(b):
            first = pl.multiple_of(b * MOE_BLK, MOE_BLK)
            return pltpu.make_async_copy(zero_scr, xs_hbm.at[pl.ds(first, MOE_BLK), :], sem)

        def tail_start(b, c):
            tail_copy(b).start()
            return c

        def tail_wait(b, c):
            tail_copy(b).wait()
            return c

        n_blocks = xs_hbm.shape[0] // MOE_BLK
        first_free = pend_ref[N_EXPERTS - 1] // MOE_BLK
        for e in range(N_EXPERTS):
            fill(e, lambda cp: cp.start())
        lax.fori_loop(first_free, n_blocks, tail_start, 0)
        for e in range(N_EXPERTS):
            fill(e, lambda cp: cp.wait())
        lax.fori_loop(first_free, n_blocks, tail_wait, 0)

    def row(t, p):
        return pltpu.make_async_copy(h_ref.at[pl.ds(t, 1), :], xs_hbm.at[pl.ds(p, 1), :], sem)

    def issue(t, c):
        row(t, pos_ref[base + t]).start(priority=0)
        row(t, pos_ref[T_ALL + base + t]).start(priority=1)
        return c

    lax.fori_loop(0, TM, issue, 0, unroll=True)
    whole = pltpu.make_async_copy(h_ref, xs_hbm.at[pl.ds(0, TM), :], sem)
    whole.wait()
    whole.wait()


def _dispatch_call(pos2, p_end, h2, nblk):
    grid_spec = pltpu.PrefetchScalarGridSpec(
        num_scalar_prefetch=2,
        grid=(N_TILES,),
        in_specs=[pl.BlockSpec((TM, D_MODEL), lambda i, p, pe: (i, 0))],
        out_specs=pl.BlockSpec(memory_space=pl.ANY),
        scratch_shapes=[pltpu.VMEM((MOE_BLK, D_MODEL), F32), pltpu.SemaphoreType.DMA(())],
    )
    return pl.pallas_call(
        _dispatch_kernel,
        grid_spec=grid_spec,
        out_shape=jax.ShapeDtypeStruct((nblk * MOE_BLK, D_MODEL), F32),
        compiler_params=_cparams("arbitrary"),
        name="dispatch",
    )(pos2, p_end, h2)


def _expert_kernel(blk_e_ref, nused_ref, xs_ref, wg_ref, wu_ref, wd_ref, out_ref):
    j = pl.program_id(0)

    @pl.when(j < nused_ref[0])
    def _():
        xb = xs_ref[...].astype(BF16)
        g = jnp.dot(xb, wg_ref[0, 0].astype(BF16), preferred_element_type=F32)
        u = jnp.dot(xb, wu_ref[0, 0].astype(BF16), preferred_element_type=F32)
        out_ref[...] = _dot(jax.nn.silu(g) * u, wd_ref[0, 0])

    @pl.when(j >= nused_ref[0])
    def _():
        out_ref[...] = jnp.zeros_like(out_ref)


def _expert_call(blk_e, nused, xs, wg, wu, wd, layer, nblk):
    def wspec(r, c):
        return pl.BlockSpec((1, 1, r, c), lambda j, be, nu: (layer, be[j], 0, 0))

    grid_spec = pltpu.PrefetchScalarGridSpec(
        num_scalar_prefetch=2,
        grid=(nblk,),
        in_specs=[
            pl.BlockSpec((MOE_BLK, D_MODEL), lambda j, be, nu: (jnp.clip(j, 0, jnp.maximum(nu[0] - 1, 0)), 0)),
            wspec(D_MODEL, D_EXPERT), wspec(D_MODEL, D_EXPERT), wspec(D_EXPERT, D_MODEL),
        ],
        out_specs=pl.BlockSpec((MOE_BLK, D_MODEL), lambda j, be, nu: (j, 0)),
    )
    return pl.pallas_call(
        _expert_kernel,
        grid_spec=grid_spec,
        out_shape=jax.ShapeDtypeStruct((nblk * MOE_BLK, D_MODEL), F32),
        compiler_params=_cparams("arbitrary"),
        name="experts",
    )(blk_e, nused, xs, wg, wu, wd)


def _combine_kernel(pos_ref, eo_hbm, x1_ref, route_ref, gf_ref, lng_ref, lnb_ref, x2_ref, buf, sem):
    base = pl.program_id(0) * TM

    def row(p, r, t):
        return pltpu.make_async_copy(eo_hbm.at[pl.ds(p, 1), :], buf.at[r, pl.ds(t, 1), :], sem)

    def issue(t, c):
        row(pos_ref[base + t], 0, t).start(priority=0)
        row(pos_ref[T_ALL + base + t], 1, t).start(priority=1)
        return c

    lax.fori_loop(0, TM, issue, 0, unroll=True)
    for r in range(2):
        pltpu.make_async_copy(eo_hbm.at[pl.ds(0, TM), :], buf.at[r], sem).wait()
    route = route_ref[...]
    y = route[:, 2:3] * buf[0] + route[:, 3:4] * buf[1]
    x2_ref[...] = _layer_norm(ALPHA * x1_ref[...] + gf_ref[0, 0, 0] * y, lng_ref[...], lnb_ref[...])


def _combine_call(pos2, eo, x1, route, mod_t, layer, lng, lnb):
    grid_spec = pltpu.PrefetchScalarGridSpec(
        num_scalar_prefetch=1,
        grid=(N_TILES,),
        in_specs=[
            pl.BlockSpec(memory_space=pl.ANY),
            pl.BlockSpec((TM, D_MODEL), lambda i, p: (i, 0)),
            pl.BlockSpec((TM, 8), lambda i, p: (i, 0)),
            _mod_spec(layer, 5),
            pl.BlockSpec((1, D_MODEL), lambda i, p: (0, 0)),
            pl.BlockSpec((1, D_MODEL), lambda i, p: (0, 0)),
        ],
        out_specs=pl.BlockSpec((TM, D_MODEL), lambda i, p: (i, 0)),
        scratch_shapes=[pltpu.VMEM((2, TM, D_MODEL), F32), pltpu.SemaphoreType.DMA(())],
    )
    return pl.pallas_call(
        _combine_kernel,
        grid_spec=grid_spec,
        out_shape=jax.ShapeDtypeStruct((T_ALL, D_MODEL), F32),
        compiler_params=_cparams("arbitrary"),
        name="combine_norm",
    )(pos2, eo, x1, route, mod_t, lng, lnb)


def _gate_cols(grp):
    return np.array([kind * 2 * H_A + d * H_A + grp * HP_A + j
                     for j in range(HP_A) for kind in range(2) for d in range(2)])


def _prep_w_in(w_in):
    a = w_in[..., :4 * D_A].reshape(DEPTH, D_MODEL, 4, H_A, DH_A)
    a = jnp.pad(a, ((0, 0), (0, 0), (0, 0), (0, 0), (0, DP_A - DH_A))).reshape(DEPTH, D_MODEL, ZA_COLS)
    gates = w_in[..., 4 * D_A:4 * D_A + 4 * H_A]
    g = jnp.concatenate([jnp.pad(gates[..., _gate_cols(grp)], ((0, 0), (0, 0), (0, LANES - 4 * HP_A)))
                         for grp in range(H_A // HP_A)], -1)
    rest = w_in[..., 4 * D_A + 4 * H_A:]
    return jnp.concatenate([a, rest, g], -1).astype(BF16)


def _prep_w_out(w_out):
    a = w_out[:, :D_A].reshape(DEPTH, H_A, DH_A, D_MODEL)
    a = jnp.pad(a, ((0, 0), (0, 0), (0, DP_A - DH_A), (0, 0))).reshape(DEPTH, Y_A_COLS, D_MODEL)
    return jnp.concatenate([a, w_out[:, D_A:]], 1).astype(BF16)


def _rope_tables():
    t = np.arange(N_LAT)
    nf = HD // 4
    inv = ROPE_BASE ** (-np.arange(nf, dtype=np.float32) / nf)
    ar = (t // GRID_W).astype(np.float32)[:, None] * inv
    ac = (t % GRID_W).astype(np.float32)[:, None] * inv
    ang = jnp.asarray(np.concatenate([ar, ar, ac, ac], -1), F32)
    cos, sin = jnp.cos(ang), jnp.sin(ang)
    sign = np.where((np.arange(HD) % 32) < 16, -1.0, 1.0).astype(np.float32)
    reps = LANES // HD
    return jnp.tile(cos, (1, reps)), jnp.tile(sin * sign, (1, reps))


def _na_bias_tables(rpb):
    qcol = np.arange(GRID_W)[:, None]
    kcol = np.arange(GRID_W)[None, :]
    dc = np.clip(kcol - qcol, 1 - NA_WIN_W, NA_WIN_W - 1) + NA_WIN_W - 1
    wstart = np.clip(qcol - NA_WIN_W // 2, 0, GRID_W - NA_WIN_W)
    in_win = (kcol >= wstart) & (kcol < wstart + NA_WIN_W)
    sel = (np.arange(2 * NA_WIN_W - 1)[:, None] == dc.reshape(1, -1)).astype(np.float32)
    cols = jnp.einsum("lhrd,dn->lhrn", rpb, jnp.asarray(sel), precision=lax.Precision.HIGHEST)
    cols = jnp.where(in_win.reshape(-1), cols, NEG_INF).reshape(DEPTH, H_C, 2 * NA_WIN_H - 1, GRID_W, GRID_W)
    outside = jnp.full((DEPTH, H_C, GRID_W, GRID_W), NEG_INF, F32)
    variants = []
    for dr0, off in NA_VARIANTS:
        span_rows = [cols[:, :, dr0 + i - off] if 0 <= i - off < NA_WIN_H else outside for i in range(NA_SPAN_ROWS)]
        variants.append(jnp.stack(span_rows, 3).reshape(DEPTH, H_C, GRID_W, NA_SPAN_ROWS * GRID_W))
    tab = jnp.stack(variants, 2).reshape(DEPTH, H_C // 2, 2, len(NA_VARIANTS), GRID_W, NA_SPAN_ROWS * GRID_W)
    return tab.transpose(0, 1, 3, 2, 4, 5).reshape(DEPTH, H_C // 2, len(NA_VARIANTS), 2 * GRID_W, NA_SPAN_ROWS * GRID_W)


def _dispatch_plan(route, counts):
    nblk = 2 * T_ALL // MOE_BLK + N_EXPERTS
    cnt = counts[0, :N_EXPERTS].astype(jnp.int32)
    padded = (cnt + MOE_BLK - 1) // MOE_BLK * MOE_BLK
    p_end = jnp.cumsum(padded)
    p_start = p_end - padded
    e = route[:, 0:2].astype(jnp.int32)
    hot = e[..., None] == jnp.arange(N_EXPERTS, dtype=jnp.int32)
    pos = jnp.sum(jnp.where(hot, p_start, 0), -1) + route[:, 4:6].astype(jnp.int32)
    pos2 = pos.T.reshape(-1)
    blk_first = jnp.arange(nblk, dtype=jnp.int32) * MOE_BLK
    blk_e = jnp.minimum(jnp.sum((p_end[None, :] <= blk_first[:, None]).astype(jnp.int32), axis=1), N_EXPERTS - 1)
    nused = p_end[-1:] // MOE_BLK
    return blk_e, nused, pos2, p_end, nblk


def kernel(x_prompt, x_sample, state_a_C, state_a_n, state_a_m, cache_b_k, cache_b_v, cache_c_k, cache_c_v, c, c_ctx, w_in, b_a_i, b_a_f, w_a_hnorm, b_sink, rpb, w_out, w_ada, b_ada, ln_g, ln_b, w_router_grp, w_router_exp, w_e_gate, w_e_up, w_e_down):
    w_in_p = _prep_w_in(w_in)
    w_out_p = _prep_w_out(w_out)
    w_r = jnp.pad(jnp.concatenate([w_router_exp, w_router_grp], -1),
                  ((0, 0), (0, 0), (0, LANES - N_EXPERTS - N_GROUPS)))
    wn_p = jnp.pad(w_a_hnorm.reshape(DEPTH, H_A, 1, DH_A), ((0, 0), (0, 0), (0, 0), (0, DP_A - DH_A)))
    gate_b = jnp.concatenate([b_a_i, b_a_f], 1).transpose(0, 2, 1)
    gate_b = jnp.pad(gate_b.reshape(DEPTH, H_A // HP_A, 1, 4 * HP_A), ((0, 0), (0, 0), (0, 0), (0, LANES - 4 * HP_A)))
    sink_p = jnp.pad(b_sink, ((0, 0), (0, 8 - H_B))).reshape(DEPTH, 1, 8)
    cos_t, sin_t = _rope_tables()
    na_bias = _na_bias_tables(rpb)
    cb_k = cache_b_k.reshape(B_LAT, DEPTH, PAST_LEN, D_KVB)
    cb_v = cache_b_v.reshape(B_LAT, DEPTH, PAST_LEN, D_KVB)
    cc_k = cache_c_k.reshape(B_LAT, DEPTH, PAST_LEN, D_C)
    cc_v = cache_c_v.reshape(B_LAT, DEPTH, PAST_LEN, D_C)
    pad_c = ((0, 0), (0, 0), (0, 0), (0, 0), (0, DP_A - DH_A), (0, DP_A - DH_A))
    st_ct = jnp.swapaxes(jnp.pad(state_a_C, pad_c), -1, -2)
    st_nr = jnp.broadcast_to(jnp.pad(state_a_n, pad_c[:-1])[..., None], st_ct.shape)
    st_s = jnp.concatenate([st_ct, st_nr], -1)
    st_m = state_a_m[..., None, None]
    z_s = jnp.zeros((B_CTX, 2, H_A, DP_A, 2 * DP_A), F32)
    z_m = jnp.zeros((B_CTX, 2, H_A, 1, 1), F32)

    cvec = jnp.concatenate([c, c_ctx[None, :], jnp.zeros((3, D_MODEL), F32)], 0)
    mod = _ada_call(cvec, w_ada, b_ada)
    tile_row = np.concatenate([np.full(T_CTX // TM, B_LAT), np.repeat(np.arange(B_LAT), N_LAT // TM)])
    mod_t = mod[:, tile_row].reshape(DEPTH, N_TILES, 6, 1, D_MODEL)

    x = jnp.concatenate([x_prompt.reshape(T_CTX, D_MODEL), x_sample.reshape(T_LAT, D_MODEL)], 0)
    cs_, ns_, ms_, kbs, vbs, kcs, vcs = [], [], [], [], [], [], []
    for l in range(DEPTH):
        za, bq, bk, bv, cq, ck, cv, zg = _inproj_call(x, mod_t, l, w_in_p)
        ya_c, s_l, m_l = _mlstm_call(za, zg, gate_b[l], wn_p[l], z_s, z_m, nb=B_CTX, seq=L_CTX, row_blk0=0)
        yb_c, yc_c = _ctx_attn_call(bq, bk, bv, cq, ck, cv, sink_p[l])
        ya_l, _, _ = _mlstm_call(za, zg, gate_b[l], wn_p[l], st_s[:, l], st_m[:, l],
                                 nb=B_LAT, seq=N_LAT, row_blk0=T_CTX // N_LAT)
        yb_l = _win_attn_call(bq, bk, bv, cb_k, cb_v, cos_t, sin_t, sink_p[l], l)
        yc_l = _na_attn_call(cq, ck, cv, cc_k, cc_v, na_bias, l)
        x1, h2, route, counts = _outproj_call((ya_c, yb_c, yc_c), (ya_l, yb_l, yc_l), x, mod_t, l,
                                              ln_g[l, 0:1], ln_b[l, 0:1], w_out_p, w_r[l])
        blk_e, nused, pos2, p_end, nblk = _dispatch_plan(route, counts)
        xs = _dispatch_call(pos2, p_end, h2, nblk)
        eo = _expert_call(blk_e, nused, xs, w_e_gate, w_e_up, w_e_down, l, nblk)
        x = _combine_call(pos2, eo, x1, route, mod_t, l, ln_g[l, 1:2], ln_b[l, 1:2])
        cs_.append(jnp.swapaxes(s_l[..., :DH_A, :DH_A], -1, -2))
        ns_.append(s_l[..., :DH_A, DP_A])
        ms_.append(m_l.reshape(B_CTX, 2, H_A))
        kbs.append(bk[:T_CTX].reshape(B_CTX, L_CTX, KV_B, HD))
        vbs.append(bv[:T_CTX].reshape(B_CTX, L_CTX, KV_B, HD))
        kcs.append(ck[:T_CTX].reshape(B_CTX, L_CTX, H_C, HD))
        vcs.append(cv[:T_CTX].reshape(B_CTX, L_CTX, H_C, HD))
    y_prompt = x[:T_CTX].reshape(B_CTX, L_CTX, D_MODEL)
    y_sample = x[T_CTX:].reshape(B_LAT, N_LAT, D_MODEL)
    return (y_prompt, y_sample, jnp.stack(cs_, 1), jnp.stack(ns_, 1), jnp.stack(ms_, 1),
            jnp.stack(kbs, 1), jnp.stack(vbs, 1), jnp.stack(kcs, 1), jnp.stack(vcs, 1))
```
